```python
import numpy as np
import jax, jax.numpy as jnp
from jax import lax

D_MODEL = 1024
BATCH = 32
SEQ = 2048
DEPTH = 1

GLA_HEADS = 4
GLA_DV = D_MODEL // 2 // GLA_HEADS
GLA_DK = GLA_DV // 2
GLA_LOWRANK = 16
GLA_TAU = 16.0
GLA_CHUNK = 64
NSA_HEADS = 8
NSA_KV_GROUPS = 2
NSA_HPG = NSA_HEADS // NSA_KV_GROUPS
NSA_DH = D_MODEL // 2 // NSA_HEADS
CMP_LEN = 32
CMP_STRIDE = 16
CMP_HIDDEN = 256
SEL_LEN = 64
SEL_TOPK = 16
WINDOW = 512
SEL_Q_CHUNK = 16
WIN_Q_BLOCK = 128
ROPE_THETA = 10000.0
N_EXPERTS = 256
TOP_K = 8
N_GROUPS = 8
TOPK_GROUPS = 4
D_EXPERT = 256
ROUTED_SCALE = 2.5
MOE_BLOCK = 256
DN_ALPHA = (2.0 * DEPTH) ** 0.25
DN_BETA = (8.0 * DEPTH) ** -0.25
LN_EPS = 1e-5

SPLITS = (
    GLA_HEADS * GLA_DK, GLA_HEADS * GLA_DK, GLA_HEADS * GLA_DV, GLA_LOWRANK, GLA_HEADS * GLA_DV,
    NSA_HEADS * NSA_DH,
    NSA_KV_GROUPS * NSA_DH, NSA_KV_GROUPS * NSA_DH,
    NSA_KV_GROUPS * NSA_DH, NSA_KV_GROUPS * NSA_DH,
    NSA_KV_GROUPS * NSA_DH, NSA_KV_GROUPS * NSA_DH,
    NSA_HEADS * 3,
)
D_IN = sum(SPLITS)

kernel_name = 'hybrid_gla_nsa_moe_deepnorm'


def layer_norm(x, g, b):
    xf = x.astype(jnp.float32)
    mu = jnp.mean(xf, axis=-1, keepdims=True)
    var = jnp.mean(jnp.square(xf - mu), axis=-1, keepdims=True)
    return ((xf - mu) * lax.rsqrt(var + LN_EPS) * g + b).astype(x.dtype)


def rope(x, positions):
    half = x.shape[-1] // 2
    inv = ROPE_THETA ** (-jnp.arange(half, dtype=jnp.float32) / half)
    ang = positions.astype(jnp.float32)[..., None] * inv
    ang = ang.reshape((ang.shape[0],) + (1,) * (x.ndim - 3) + ang.shape[1:])
    cos, sin = jnp.cos(ang), jnp.sin(ang)
    x1 = x[..., :half].astype(jnp.float32)
    x2 = x[..., half:].astype(jnp.float32)
    return jnp.concatenate([x1 * cos - x2 * sin, x2 * cos + x1 * sin], axis=-1).astype(x.dtype)


def masked_softmax(s, mask):
    p = jax.nn.softmax(jnp.where(mask, s.astype(jnp.float32), -1e30), axis=-1)
    return p * mask


def gla_mixer(q, k, v, a_lr, r, w_alpha2, b_alpha, norm_g):
    B, S, _ = q.shape
    H, C = GLA_HEADS, GLA_CHUNK
    nch = S // C
    f32 = jnp.float32

    def heads(t, d):
        return t.astype(f32).reshape(B, nch, C, H, d).transpose(1, 0, 3, 2, 4)

    qh = heads(q, GLA_DK) * GLA_DK ** -0.5
    kh = heads(k, GLA_DK)
    vh = heads(v, GLA_DV)
    log_a = jax.nn.log_sigmoid((a_lr @ w_alpha2 + b_alpha).astype(f32)) / GLA_TAU
    gh = heads(log_a, GLA_DK)
    causal = jnp.tril(jnp.ones((C, C), bool))

    def step(state, inp):
        qc, kc, vc, gc = inp
        b = jnp.cumsum(gc, axis=-2)
        o_inter = jnp.einsum('bhcd,bhde->bhce', qc * jnp.exp(b), state)
        diff = b[:, :, :, None, :] - b[:, :, None, :, :]
        decay = jnp.exp(jnp.where(causal[:, :, None], diff, -jnp.inf))
        att = jnp.einsum('bhid,bhjd,bhijd->bhij', qc, kc, decay)
        o = o_inter + jnp.einsum('bhij,bhje->bhie', att, vc)
        b_last = b[:, :, -1:, :]
        state = state * jnp.exp(b_last)[:, :, 0, :, None] + jnp.einsum(
            'bhcd,bhce->bhde', kc * jnp.exp(b_last - b), vc)
        return state, o

    state0 = jnp.zeros((B, H, GLA_DK, GLA_DV), f32)
    _, o = lax.scan(step, state0, (qh, kh, vh, gh))
    o = o.transpose(1, 0, 3, 2, 4).reshape(B, S, H, GLA_DV)
    o = o * lax.rsqrt(jnp.mean(o * o, axis=-1, keepdims=True) + 1e-6) * norm_g
    o = o.reshape(B, S, H * GLA_DV) * jax.nn.silu(r.astype(f32))
    return o.astype(q.dtype)


def nsa_mixer(q, k_c, v_c, k_s, v_s, k_w, v_w, gates, positions,
              cmp_pos_k, cmp_w1_k, cmp_b1_k, cmp_w2_k,
              cmp_pos_v, cmp_w1_v, cmp_b1_v, cmp_w2_v):
    B, S, _ = q.shape
    G, HPG, DH = NSA_KV_GROUPS, NSA_HPG, NSA_DH
    scale = DH ** -0.5
    tpos = jnp.arange(S)
    qh = rope(q.reshape(B, S, G, HPG, DH).transpose(0, 2, 3, 1, 4), positions)

    def kv(t):
        return t.reshape(B, S, G, DH).transpose(0, 2, 1, 3)

    n_cmp = (S - CMP_LEN) // CMP_STRIDE + 1
    blk_idx = jnp.arange(n_cmp)[:, None] * CMP_STRIDE + jnp.arange(CMP_LEN)[None, :]

    def compress(t, pos_emb, w1, b1, w2):
        blocks = t[:, :, blk_idx] + pos_emb
        hid = jax.nn.gelu(blocks.reshape(B, G, n_cmp, CMP_LEN * DH) @ w1 + b1)
        return hid @ w2

    kc = compress(rope(kv(k_c), positions), cmp_pos_k, cmp_w1_k, cmp_b1_k, cmp_w2_k)
    vc = compress(kv(v_c), cmp_pos_v, cmp_w1_v, cmp_b1_v, cmp_w2_v)
    s_cmp = jnp.einsum('bghsd,bgnd->bghsn', qh, kc) * scale
    cmp_mask = (jnp.arange(n_cmp) * CMP_STRIDE + CMP_LEN - 1)[None, :] <= tpos[:, None]
    p_cmp = masked_softmax(s_cmp, cmp_mask)
    o_cmp = jnp.einsum('bghsn,bgnd->bghsd', p_cmp.astype(vc.dtype), vc)

    n_sel = S // SEL_LEN
    n_top = min(SEL_TOPK, n_sel)
    ci = np.arange(n_cmp)[:, None] * CMP_STRIDE
    sj = np.arange(n_sel)[None, :] * SEL_LEN
    ov = np.clip(np.minimum(ci + CMP_LEN, sj + SEL_LEN) - np.maximum(ci, sj), 0, None) / CMP_LEN
    overlap = jnp.asarray(ov, jnp.float32)
    imp = jnp.einsum('bghsn,nj->bgsj', p_cmp, overlap)
    blk = jnp.arange(n_sel)[None, :]
    cur = (tpos // SEL_LEN)[:, None]
    causal_blk = blk <= cur
    forced = (blk == 0) | (blk == cur) | (blk == cur - 1)
    imp = jnp.where(causal_blk, jnp.where(forced, jnp.inf, imp), -jnp.inf)
    _, sel_idx = lax.top_k(imp, n_top)

    ks = rope(kv(k_s), positions).reshape(B, G, n_sel, SEL_LEN, DH)
    vs = kv(v_s).reshape(B, G, n_sel, SEL_LEN, DH)
    ncq = S // SEL_Q_CHUNK
    q_chunks = qh.reshape(B, G, HPG, ncq, SEL_Q_CHUNK, DH).transpose(3, 0, 1, 2, 4, 5)
    idx_chunks = sel_idx.reshape(B, G, ncq, SEL_Q_CHUNK, n_top).transpose(2, 0, 1, 3, 4)
    t_chunks = tpos.reshape(ncq, SEL_Q_CHUNK)
    bi = jnp.arange(B)[:, None, None, None]
    gi = jnp.arange(G)[None, :, None, None]

    def sel_chunk(inp):
        qc, idx, tc = inp
        kg = ks[bi, gi, idx]
        vg = vs[bi, gi, idx]
        s = jnp.einsum('bghcd,bgcnkd->bghcnk', qc, kg) * scale
        kpos = idx[..., None] * SEL_LEN + jnp.arange(SEL_LEN)
        mask = (kpos <= tc[:, None, None])[:, :, None]
        C = qc.shape[3]
        p = masked_softmax(s.reshape(B, G, HPG, C, n_top * SEL_LEN),
                           mask.reshape(B, G, 1, C, n_top * SEL_LEN))
        p = p.reshape(s.shape).astype(vg.dtype)
        return jnp.einsum('bghcnk,bgcnkd->bghcd', p, vg)

    o_sel = lax.map(sel_chunk, (q_chunks, idx_chunks, t_chunks))
    o_sel = o_sel.transpose(1, 2, 3, 0, 4, 5).reshape(B, G, HPG, S, DH)

    kw = jnp.pad(rope(kv(k_w), positions), ((0, 0), (0, 0), (WINDOW, 0), (0, 0)))
    vw = jnp.pad(kv(v_w), ((0, 0), (0, 0), (WINDOW, 0), (0, 0)))
    nqb = S // WIN_Q_BLOCK
    span = WIN_Q_BLOCK + WINDOW
    q_blocks = qh.reshape(B, G, HPG, nqb, WIN_Q_BLOCK, DH).transpose(3, 0, 1, 2, 4, 5)

    def win_block(inp):
        qb, i = inp
        start = i * WIN_Q_BLOCK
        kb = lax.dynamic_slice_in_dim(kw, start, span, axis=2)
        vb = lax.dynamic_slice_in_dim(vw, start, span, axis=2)
        s = jnp.einsum('bghcd,bgkd->bghck', qb, kb) * scale
        tq = start + jnp.arange(WIN_Q_BLOCK)
        kp = start - WINDOW + jnp.arange(span)
        mask = (kp[None, :] <= tq[:, None]) & (kp[None, :] > tq[:, None] - WINDOW) & (kp[None, :] >= 0)
        p = masked_softmax(s, mask).astype(vb.dtype)
        return jnp.einsum('bghck,bgkd->bghcd', p, vb)

    o_win = lax.map(win_block, (q_blocks, jnp.arange(nqb)))
    o_win = o_win.transpose(1, 2, 3, 0, 4, 5).reshape(B, G, HPG, S, DH)

    g = jax.nn.sigmoid(gates.astype(jnp.float32)).reshape(B, S, G, HPG, 3).transpose(0, 2, 3, 1, 4)
    o = g[..., 0:1] * o_cmp + g[..., 1:2] * o_sel + g[..., 2:3] * o_win
    return o.transpose(0, 3, 1, 2, 4).reshape(B, S, NSA_HEADS * DH).astype(q.dtype)


def moe(h, w_router, router_bias, w_gate, w_up, w_down, ws_gate, ws_up, ws_down):
    B, S, D = h.shape
    T = B * S
    f32 = jnp.float32
    xt = h.reshape(T, D)
    scores = jax.nn.sigmoid((xt @ w_router).astype(f32))
    biased = scores + router_bias
    grp = biased.reshape(T, N_GROUPS, N_EXPERTS // N_GROUPS)
    grp_score = lax.top_k(grp, 2)[0].sum(-1)
    _, gidx = lax.top_k(grp_score, TOPK_GROUPS)
    gmask = jnp.any(gidx[:, :, None] == jnp.arange(N_GROUPS), axis=1)
    emask = jnp.repeat(gmask, N_EXPERTS // N_GROUPS, axis=1)
    _, eidx = lax.top_k(jnp.where(emask, biased, -jnp.inf), TOP_K)
    wts = jnp.take_along_axis(scores, eidx, axis=1)
    wts = wts / jnp.sum(wts, axis=-1, keepdims=True) * ROUTED_SCALE

    P = T * TOP_K
    e_flat = eidx.reshape(P)
    tok_flat = jnp.arange(P, dtype=jnp.int32) // TOP_K
    order = jnp.argsort(e_flat, stable=True)
    e_sorted = e_flat[order]
    counts = jnp.zeros((N_EXPERTS,), jnp.int32).at[e_flat].add(1)
    padded = (counts + MOE_BLOCK - 1) // MOE_BLOCK * MOE_BLOCK
    pad_end = jnp.cumsum(padded)
    pad_start = pad_end - padded
    start = jnp.cumsum(counts) - counts
    dest = pad_start[e_sorted] + jnp.arange(P, dtype=jnp.int32) - start[e_sorted]
    n_blocks = -(-P // MOE_BLOCK) + N_EXPERTS
    buf_tok = jnp.zeros((n_blocks * MOE_BLOCK,), jnp.int32).at[dest].set(tok_flat[order])
    buf_w = jnp.zeros((n_blocks * MOE_BLOCK,), f32).at[dest].set(wts.reshape(P)[order])
    blk_exp = jnp.minimum(jnp.searchsorted(pad_end, jnp.arange(n_blocks, dtype=jnp.int32) * MOE_BLOCK,
                                           side='right'), N_EXPERTS - 1)

    def step(out, inp):
        tok, w, e = inp
        xb = xt[tok]
        y = (jax.nn.silu(xb @ w_gate[e]) * (xb @ w_up[e])) @ w_down[e]
        return out.at[tok].add((y * w[:, None]).astype(out.dtype)), None

    routed, _ = lax.scan(step, jnp.zeros_like(xt),
                         (buf_tok.reshape(n_blocks, MOE_BLOCK), buf_w.reshape(n_blocks, MOE_BLOCK), blk_exp))
    shared = (jax.nn.silu(xt @ ws_gate) * (xt @ ws_up)) @ ws_down
    return (routed + shared).reshape(B, S, D)


def setup_inputs(seed: int = 0) -> dict:
    key = jax.random.key(seed)
    ks = jax.random.split(key, 32)
    f32 = jnp.float32
    L = DEPTH

    def nrm(k, shape, scale):
        return jax.random.normal(k, shape, f32) * scale

    x = jax.random.normal(ks[0], (BATCH, SEQ, D_MODEL), f32)
    positions = (jax.random.randint(ks[1], (BATCH, 1), 0, 1024, dtype=jnp.int32)
                 + jnp.arange(SEQ, dtype=jnp.int32)[None, :])
    return {
        'x': x,
        'positions': positions,
        'w_in': nrm(ks[2], (L, D_MODEL, D_IN), D_MODEL ** -0.5),
        'w_alpha2': nrm(ks[3], (L, GLA_LOWRANK, GLA_HEADS * GLA_DK), GLA_LOWRANK ** -0.5),
        'b_alpha': nrm(ks[4], (L, GLA_HEADS * GLA_DK), 0.5),
        'gla_norm_g': 1.0 + nrm(ks[5], (L, GLA_DV), 0.02),
        'cmp_pos_k': nrm(ks[6], (L, CMP_LEN, NSA_DH), 0.1),
        'cmp_w1_k': nrm(ks[7], (L, CMP_LEN * NSA_DH, CMP_HIDDEN), (CMP_LEN * NSA_DH) ** -0.5),
        'cmp_b1_k': nrm(ks[8], (L, CMP_HIDDEN), 0.02),
        'cmp_w2_k': nrm(ks[9], (L, CMP_HIDDEN, NSA_DH), CMP_HIDDEN ** -0.5),
        'cmp_pos_v': nrm(ks[10], (L, CMP_LEN, NSA_DH), 0.1),
        'cmp_w1_v': nrm(ks[11], (L, CMP_LEN * NSA_DH, CMP_HIDDEN), (CMP_LEN * NSA_DH) ** -0.5),
        'cmp_b1_v': nrm(ks[12], (L, CMP_HIDDEN), 0.02),
        'cmp_w2_v': nrm(ks[13], (L, CMP_HIDDEN, NSA_DH), CMP_HIDDEN ** -0.5),
        'w_out': nrm(ks[14], (L, D_MODEL, D_MODEL), D_MODEL ** -0.5 * DN_BETA),
        'ln1_g': 1.0 + nrm(ks[15], (L, D_MODEL), 0.02),
        'ln1_b': nrm(ks[16], (L, D_MODEL), 0.02),
        'w_router': nrm(ks[17], (L, D_MODEL, N_EXPERTS), D_MODEL ** -0.5),
        'router_bias': nrm(ks[18], (L, N_EXPERTS), 0.01),
        'w_exp_gate': nrm(ks[19], (L, N_EXPERTS, D_MODEL, D_EXPERT), D_MODEL ** -0.5),
        'w_exp_up': nrm(ks[20], (L, N_EXPERTS, D_MODEL, D_EXPERT), D_MODEL ** -0.5),
        'w_exp_down': nrm(ks[21], (L, N_EXPERTS, D_EXPERT, D_MODEL), D_EXPERT ** -0.5 * DN_BETA),
        'w_sh_gate': nrm(ks[22], (L, D_MODEL, D_EXPERT), D_MODEL ** -0.5),
        'w_sh_up': nrm(ks[23], (L, D_MODEL, D_EXPERT), D_MODEL ** -0.5),
        'w_sh_down': nrm(ks[24], (L, D_EXPERT, D_MODEL), D_EXPERT ** -0.5 * DN_BETA),
        'ln2_g': 1.0 + nrm(ks[25], (L, D_MODEL), 0.02),
        'ln2_b': nrm(ks[26], (L, D_MODEL), 0.02),
    }


def reference(x, positions, w_in, w_alpha2, b_alpha, gla_norm_g,
              cmp_pos_k, cmp_w1_k, cmp_b1_k, cmp_w2_k,
              cmp_pos_v, cmp_w1_v, cmp_b1_v, cmp_w2_v,
              w_out, ln1_g, ln1_b, w_router, router_bias,
              w_exp_gate, w_exp_up, w_exp_down, w_sh_gate, w_sh_up, w_sh_down,
              ln2_g, ln2_b):
    split_at = np.cumsum(SPLITS)[:-1].tolist()
    h = x
    for l in range(DEPTH):
        proj = h @ w_in[l]
        (gq, gk, gv, ga, gr, nq, nkc, nvc, nks, nvs, nkw, nvw, ng) = jnp.split(proj, split_at, axis=-1)
        o_gla = gla_mixer(gq, gk, gv, ga, gr, w_alpha2[l], b_alpha[l], gla_norm_g[l])
        o_nsa = nsa_mixer(nq, nkc, nvc, nks, nvs, nkw, nvw, ng, positions,
                          cmp_pos_k[l], cmp_w1_k[l], cmp_b1_k[l], cmp_w2_k[l],
                          cmp_pos_v[l], cmp_w1_v[l], cmp_b1_v[l], cmp_w2_v[l])
        mix = jnp.concatenate([o_gla, o_nsa], axis=-1) @ w_out[l]
        h = layer_norm(DN_ALPHA * h + mix, ln1_g[l], ln1_b[l])
        ffn = moe(h, w_router[l], router_bias[l], w_exp_gate[l], w_exp_up[l], w_exp_down[l],
                  w_sh_gate[l], w_sh_up[l], w_sh_down[l])
        h = layer_norm(DN_ALPHA * h + ffn, ln2_g[l], ln2_b[l])
    return h
```

```python
import functools

import numpy as np
import jax
import jax.numpy as jnp
from jax import lax
from jax.experimental import pallas as pl
from jax.experimental.pallas import tpu as pltpu

D_MODEL = 1024
GLA_HEADS = 4
GLA_DV = 128
GLA_DK = 64
GLA_LOWRANK = 16
GLA_TAU = 16.0
GLA_CHUNK = 64
NSA_HEADS = 8
NSA_KV_GROUPS = 2
NSA_HPG = 4
NSA_DH = 64
CMP_LEN = 32
CMP_STRIDE = 16
CMP_HIDDEN = 256
SEL_LEN = 64
SEL_TOPK = 16
WINDOW = 512
ROPE_THETA = 10000.0
N_EXPERTS = 256
TOP_K = 8
N_GROUPS = 8
TOPK_GROUPS = 4
D_EXPERT = 256
ROUTED_SCALE = 2.5
DEPTH = 1
DN_ALPHA = (2.0 * DEPTH) ** 0.25
LN_EPS = 1e-5

MOE_BLOCK = 256
NEG = -1e30
F32 = jnp.float32
BF16 = jnp.bfloat16

_COLS = {}
_off = 0
for _name, _w in (("gq", 256), ("gk", 256), ("gv", 512), ("gr", 512), ("nq", 512),
                  ("nkc", 128), ("nvc", 128), ("nks", 128), ("nvs", 128), ("nkw", 128), ("nvw", 128),
                  ("ga", 128), ("ng", 128)):
    _COLS[_name] = (_off, _w)
    _off += _w
D_PROJ = _off
_SPLITS = (("gq", 256), ("gk", 256), ("gv", 512), ("ga", 16), ("gr", 512), ("nq", 512),
           ("nkc", 128), ("nvc", 128), ("nks", 128), ("nvs", 128), ("nkw", 128), ("nvw", 128), ("ng", 24))

VMEM_LIMIT = 56 * 1024 * 1024


def _cparams(*sem):
    return pltpu.CompilerParams(dimension_semantics=sem, vmem_limit_bytes=VMEM_LIMIT)


def _dot(a, b):
    return jnp.dot(a, b, preferred_element_type=F32)


def _dot_nt(a, b):
    return lax.dot_general(a, b, (((1,), (1,)), ((), ())), preferred_element_type=F32)


def _split3(x):
    hi = x.astype(BF16)
    r1 = x - hi.astype(F32)
    mid = r1.astype(BF16)
    lo = (r1 - mid.astype(F32)).astype(BF16)
    return hi, mid, lo


def _proj_kernel(x_ref, w_ref, o_ref):
    o_ref[...] = _dot(x_ref[...].astype(BF16), w_ref[...])


def _proj(x2, w):
    T, D = x2.shape
    N = w.shape[1]
    tm, tn = 512, 1024
    return pl.pallas_call(
        _proj_kernel,
        grid=(T // tm, N // tn),
        in_specs=[pl.BlockSpec((tm, D), lambda i, j: (i, 0)),
                  pl.BlockSpec((D, tn), lambda i, j: (0, j))],
        out_specs=pl.BlockSpec((tm, tn), lambda i, j: (i, j)),
        out_shape=jax.ShapeDtypeStruct((T, N), F32),
        compiler_params=_cparams("parallel", "arbitrary"),
        name="proj",
    )(x2, w)


def _gla_kernel(q_ref, k_ref, v_ref, r_ref, a_ref, wa_ref, ba_ref, ng_ref, o_ref, st_ref, *, nchunk):
    C = GLA_CHUNK
    HK = GLA_HEADS * GLA_DK

    @pl.when(pl.program_id(1) == 0)
    def _():
        st_ref[...] = jnp.zeros_like(st_ref)

    ri = lax.broadcasted_iota(jnp.int32, (C, C), 0)
    ci = lax.broadcasted_iota(jnp.int32, (C, C), 1)
    causal = ri >= ci
    tri = jnp.where(causal, 1.0, 0.0).astype(BF16)
    lane_head = lax.broadcasted_iota(jnp.int32, (1, HK), 1) // GLA_DK
    wa = wa_ref[...]
    ba = ba_ref[...]
    ng = ng_ref[...]

    def chunk(c, carry):
        rows = pl.ds(pl.multiple_of(c * C, C), C)
        q = q_ref[0, rows, :]
        k = k_ref[0, rows, :]
        v = v_ref[0, rows, :]
        r = r_ref[0, rows, :]
        a = a_ref[0, rows, :]
        z = _dot(a.astype(BF16), wa) + ba
        g = (jnp.minimum(z, 0.0) - jnp.log1p(jnp.exp(-jnp.abs(z)))) * (1.0 / GLA_TAU)
        g_hi, g_mid, g_lo = _split3(g)
        b = _dot(tri, g_hi) + _dot(tri, g_mid) + _dot(tri, g_lo)
        b_last = b[C - 1:C, :]
        qt = q * jnp.exp(b) * (GLA_DK ** -0.5)
        kt = (k * jnp.exp(-b)).astype(BF16)
        ks = (k * jnp.exp(b_last - b)).astype(BF16)
        st = st_ref[...]
        st_b = st.astype(BF16)
        new_st = st * jnp.exp(b_last)
        for h in range(GLA_HEADS):
            hm = lane_head == h
            qh = jnp.where(hm, qt, 0.0).astype(BF16)
            att = jnp.where(causal, _dot_nt(qh, kt), 0.0)
            vh = v[:, h * GLA_DV:(h + 1) * GLA_DV].astype(BF16)
            o = _dot_nt(qh, st_b) + _dot(att.astype(BF16), vh)
            new_st = new_st + jnp.where(hm, _dot(vh.T, ks), 0.0)
            o = o * lax.rsqrt(jnp.mean(o * o, axis=-1, keepdims=True) + 1e-6) * ng
            rh = r[:, h * GLA_DV:(h + 1) * GLA_DV]
            o = o * (rh * jax.nn.sigmoid(rh))
            o_ref[0, rows, h * GLA_DV:(h + 1) * GLA_DV] = o.astype(o_ref.dtype)
        st_ref[...] = new_st
        return carry

    lax.fori_loop(0, nchunk, chunk, 0)


def _gla(proj3, wa, ba, ng):
    B, S, _ = proj3.shape
    L = 512 if S % 512 == 0 else S
    nchunk = L // GLA_CHUNK

    def col(name, width):
        off = _COLS[name][0]
        assert off % width == 0
        return pl.BlockSpec((1, L, width), lambda b, j, o=off // width: (b, j, o))

    full2 = lambda shape: pl.BlockSpec(shape, lambda b, j: (0, 0))
    return pl.pallas_call(
        functools.partial(_gla_kernel, nchunk=nchunk),
        grid=(B, S // L),
        in_specs=[col("gq", 256), col("gk", 256), col("gv", 512), col("gr", 512), col("ga", 128),
                  full2((128, 256)), full2((1, 256)), full2((1, 128))],
        out_specs=pl.BlockSpec((1, L, 512), lambda b, j: (b, j, 0)),
        out_shape=jax.ShapeDtypeStruct((B, S, 512), BF16),
        scratch_shapes=[pltpu.VMEM((GLA_DV, GLA_HEADS * GLA_DK), F32)],
        compiler_params=_cparams("parallel", "arbitrary"),
        name="gla",
    )(proj3, proj3, proj3, proj3, proj3, wa, ba, ng)


def _rot_half(x):
    n = x.shape[-1]
    lane = lax.broadcasted_iota(jnp.int32, (1, n), 1)
    first = (lane % NSA_DH) < (NSA_DH // 2)
    return jnp.where(first, -pltpu.roll(x, n - NSA_DH // 2, 1), pltpu.roll(x, NSA_DH // 2, 1))


def _nsa_prep_kernel(pos_ref, inv_ref, q_ref, kc_ref, ks_ref, vs_ref, kw_ref, vw_ref,
                     qo_ref, kco_ref, kso_ref, vso_ref, kwo_ref, vwo_ref):
    ang = pos_ref[0] * inv_ref[...]
    cos = jnp.cos(ang)
    sin = jnp.sin(ang)
    cos4 = jnp.concatenate([cos] * 4, axis=1)
    sin4 = jnp.concatenate([sin] * 4, axis=1)

    q = q_ref[0]
    qr = q * cos4 + _rot_half(q) * sin4
    for h in range(NSA_HEADS):
        qo_ref[0, h] = qr[:, h * NSA_DH:(h + 1) * NSA_DH].astype(qo_ref.dtype)

    def rope128(x):
        return x * cos + _rot_half(x) * sin

    kco_ref[0] = rope128(kc_ref[0])
    ksr = rope128(ks_ref[0])
    kwr = rope128(kw_ref[0])
    vs = vs_ref[0]
    vw = vw_ref[0]
    for g in range(NSA_KV_GROUPS):
        sl = slice(g * NSA_DH, (g + 1) * NSA_DH)
        kso_ref[0, g] = ksr[:, sl].astype(kso_ref.dtype)
        kwo_ref[0, g] = kwr[:, sl].astype(kwo_ref.dtype)
        vso_ref[0, g] = vs[:, sl].astype(vso_ref.dtype)
        vwo_ref[0, g] = vw[:, sl].astype(vwo_ref.dtype)


def _nsa_prep(proj3, pos3, inv_row):
    B, S, _ = proj3.shape
    ts = 512 if S % 512 == 0 else S

    def col(name, width):
        off = _COLS[name][0]
        assert off % width == 0
        return pl.BlockSpec((1, ts, width), lambda b, j, o=off // width: (b, j, o))

    kv_spec = pl.BlockSpec((1, NSA_KV_GROUPS, ts, NSA_DH), lambda b, j: (b, 0, j, 0))
    kv_shape = jax.ShapeDtypeStruct((B, NSA_KV_GROUPS, S, NSA_DH), BF16)
    return pl.pallas_call(
        _nsa_prep_kernel,
        grid=(B, S // ts),
        in_specs=[pl.BlockSpec((1, ts, 1), lambda b, j: (b, j, 0)),
                  pl.BlockSpec((1, 128), lambda b, j: (0, 0)),
                  col("nq", 512), col("nkc", 128), col("nks", 128), col("nvs", 128),
                  col("nkw", 128), col("nvw", 128)],
        out_specs=[pl.BlockSpec((1, NSA_HEADS, ts, NSA_DH), lambda b, j: (b, 0, j, 0)),
                   pl.BlockSpec((1, ts, 128), lambda b, j: (b, j, 0)),
                   kv_spec, kv_spec, kv_spec, kv_spec],
        out_shape=[jax.ShapeDtypeStruct((B, NSA_HEADS, S, NSA_DH), BF16),
                   jax.ShapeDtypeStruct((B, S, 128), F32),
                   kv_shape, kv_shape, kv_shape, kv_shape],
        compiler_params=_cparams("parallel", "parallel"),
        name="nsa_prep",
    )(pos3, inv_row, proj3, proj3, proj3, proj3, proj3, proj3)


def _compress_kernel(k_ref, v_ref, pk_ref, w1k_ref, b1k_ref, w2k_ref, pv_ref, w1v_ref, b1v_ref, w2v_ref,
                     ko_ref, vo_ref):
    half = CMP_STRIDE * NSA_DH

    def run(x_ref, p_ref, w1_ref, b1_ref, w2_ref, o_ref):
        p = p_ref[...]
        for g in range(NSA_KV_GROUPS):
            x = x_ref[0, g]
            nr = x.shape[0]
            a = _dot((x + p[0:1]).astype(BF16), w1_ref[0:half, :])
            bb = _dot((x + p[1:2]).astype(BF16), w1_ref[half:2 * half, :])
            pre = a + pltpu.roll(bb, nr - 1, 0) + b1_ref[...]
            hid = jax.nn.gelu(pre)
            out = _dot(hid.astype(BF16), w2_ref[...])
            row = lax.broadcasted_iota(jnp.int32, out.shape, 0)
            o_ref[0, g] = jnp.where(row < nr - 1, out, 0.0)

    run(k_ref, pk_ref, w1k_ref, b1k_ref, w2k_ref, ko_ref)
    run(v_ref, pv_ref, w1v_ref, b1v_ref, w2v_ref, vo_ref)


def _compress(kc4, vc4, pk, w1k, b1k, w2k, pv, w1v, b1v, w2v):
    B, G, NR, W = kc4.shape
    x_spec = pl.BlockSpec((1, G, NR, W), lambda b: (b, 0, 0, 0))
    full = lambda a: pl.BlockSpec(a.shape, lambda b: (0,) * a.ndim)
    o_spec = pl.BlockSpec((1, G, NR, NSA_DH), lambda b: (b, 0, 0, 0))
    o_shape = jax.ShapeDtypeStruct((B, G, NR, NSA_DH), F32)
    params = (pk, w1k, b1k, w2k, pv, w1v, b1v, w2v)
    return pl.pallas_call(
        _compress_kernel,
        grid=(B,),
        in_specs=[x_spec, x_spec] + [full(a) for a in params],
        out_specs=[o_spec, o_spec],
        out_shape=[o_shape, o_shape],
        compiler_params=_cparams("parallel"),
        name="compress",
    )(kc4, vc4, *params)


def _nsa_attn_kernel(q_ref, kc_ref, vc_ref, ks_ref, vs_ref, kw_ref, vw_ref, gate_ref, ov_ref, o_ref,
                     *, tq, S, n_sel, n_top):
    g = pl.program_id(1)
    qi = pl.program_id(2)
    start = qi * tq
    H = NSA_HPG
    scale = NSA_DH ** -0.5
    qs = q_ref[0].reshape(H * tq, NSA_DH)
    t_col = start + lax.broadcasted_iota(jnp.int32, (tq, 1), 0)

    kc = kc_ref[0, 0].astype(BF16)
    vc = vc_ref[0, 0].astype(BF16)
    NR = kc.shape[0]
    s = _dot_nt(qs, kc) * scale
    ncol = lax.broadcasted_iota(jnp.int32, (1, NR), 1)
    cmask = (ncol < NR - 1) & (ncol * CMP_STRIDE + (CMP_LEN - 1) <= t_col)
    s3 = jnp.where(cmask[None], s.reshape(H, tq, NR), NEG)
    m = jnp.max(s3, axis=-1, keepdims=True)
    e = jnp.exp(s3 - m)
    p3 = e / jnp.sum(e, axis=-1, keepdims=True)
    p3 = jnp.where(cmask[None], p3, 0.0)
    o_cmp = _dot(p3.reshape(H * tq, NR).astype(BF16), vc)

    psum = jnp.sum(p3, axis=0)
    p_hi = psum.astype(BF16)
    p_lo = (psum - p_hi.astype(F32)).astype(BF16)
    ov = ov_ref[...]
    imp = _dot(p_hi, ov) + _dot(p_lo, ov)
    blk = lax.broadcasted_iota(jnp.int32, (1, n_sel), 1)
    cur = t_col // SEL_LEN
    causal_blk = blk <= cur
    forced = (blk == 0) | (blk == cur) | (blk == cur - 1)
    val = jnp.where(causal_blk, jnp.where(forced, jnp.inf, imp), -jnp.inf)
    rank = jnp.zeros((tq, n_sel), jnp.int32)
    for j in range(n_sel):
        cj = val[:, j:j + 1]
        beats = (cj > val) | ((cj == val) & (blk > j))
        rank = rank + beats.astype(jnp.int32)
    sel = jnp.where((rank < n_top) & causal_blk, 1.0, 0.0).astype(BF16)

    tk = 256
    bpc = tk // SEL_LEN
    n_chunks = (start + tq + tk - 1) // tk
    erow = lax.broadcasted_iota(jnp.int32, (n_sel, tk), 0)
    ecol = lax.broadcasted_iota(jnp.int32, (n_sel, tk), 1) // SEL_LEN
    kk = lax.broadcasted_iota(jnp.int32, (1, tk), 1)

    def sel_chunk(c, carry):
        m_i, l_i, acc = carry
        k0 = pl.multiple_of(c * tk, tk)
        kb = ks_ref[0, 0, pl.ds(k0, tk), :]
        vb = vs_ref[0, 0, pl.ds(k0, tk), :]
        expand = jnp.where(erow == ecol + c * bpc, 1.0, 0.0).astype(BF16)
        msk = (_dot(sel, expand) > 0.5) & (k0 + kk <= t_col)
        sc = _dot_nt(qs, kb) * scale
        sc3 = jnp.where(msk[None], sc.reshape(H, tq, tk), NEG)
        m_new = jnp.maximum(m_i, jnp.max(sc3, axis=-1, keepdims=True))
        pe = jnp.where(msk[None], jnp.exp(sc3 - m_new), 0.0)
        alpha = jnp.exp(m_i - m_new)
        l_new = alpha * l_i + jnp.sum(pe, axis=-1, keepdims=True)
        pv = _dot(pe.reshape(H * tq, tk).astype(BF16), vb)
        acc = alpha.reshape(H * tq, 1) * acc + pv
        return m_new, l_new, acc

    m0 = jnp.full((H, tq, 1), NEG, F32)
    l0 = jnp.zeros((H, tq, 1), F32)
    a0 = jnp.zeros((H * tq, NSA_DH), F32)
    _, l_f, acc_f = lax.fori_loop(0, n_chunks, sel_chunk, (m0, l0, a0))
    o_sel = acc_f / l_f.reshape(H * tq, 1)

    span = min(WINDOW + tq, S)
    ws = jnp.clip(start - WINDOW, 0, S - span)
    ws = pl.multiple_of(ws, tq)
    kwb = kw_ref[0, 0, pl.ds(ws, span), :]
    vwb = vw_ref[0, 0, pl.ds(ws, span), :]
    kp = ws + lax.broadcasted_iota(jnp.int32, (1, span), 1)
    wmask = (kp <= t_col) & (kp > t_col - WINDOW)
    sw = _dot_nt(qs, kwb) * scale
    sw3 = jnp.where(wmask[None], sw.reshape(H, tq, span), NEG)
    mw = jnp.max(sw3, axis=-1, keepdims=True)
    ew = jnp.where(wmask[None], jnp.exp(sw3 - mw), 0.0)
    pw = ew / jnp.sum(ew, axis=-1, keepdims=True)
    o_win = _dot(pw.reshape(H * tq, span).astype(BF16), vwb)

    gs = jax.nn.sigmoid(gate_ref[0])
    for g_static in range(NSA_KV_GROUPS):
        @pl.when(g == g_static)
        def _(g_static=g_static):
            for h in range(H):
                c0 = g_static * H * 3 + h * 3
                rs = slice(h * tq, (h + 1) * tq)
                o = (gs[:, c0:c0 + 1] * o_cmp[rs] + gs[:, c0 + 1:c0 + 2] * o_sel[rs]
                     + gs[:, c0 + 2:c0 + 3] * o_win[rs])
                o_ref[0, :, h * NSA_DH:(h + 1) * NSA_DH] = o.astype(o_ref.dtype)


def _nsa_attn(q_r, kcmp, vcmp, ks_r, vs_r, kw_r, vw_r, proj3, ov):
    B, _, S, _ = q_r.shape
    G, H = NSA_KV_GROUPS, NSA_HPG
    NR = kcmp.shape[2]
    tq = 128
    n_sel = S // SEL_LEN
    n_top = min(SEL_TOPK, n_sel)
    cmp_spec = pl.BlockSpec((1, 1, NR, NSA_DH), lambda b, g, i: (b, g, 0, 0))
    kv_spec = pl.BlockSpec((1, 1, S, NSA_DH), lambda b, g, i: (b, g, 0, 0))
    goff = _COLS["ng"][0] // 128
    return pl.pallas_call(
        functools.partial(_nsa_attn_kernel, tq=tq, S=S, n_sel=n_sel, n_top=n_top),
        grid=(B, G, S // tq),
        in_specs=[pl.BlockSpec((1, H, tq, NSA_DH), lambda b, g, i: (b, g, i, 0)),
                  cmp_spec, cmp_spec, kv_spec, kv_spec, kv_spec, kv_spec,
                  pl.BlockSpec((1, tq, 128), lambda b, g, i: (b, i, goff)),
                  pl.BlockSpec(ov.shape, lambda b, g, i: (0, 0))],
        out_specs=pl.BlockSpec((1, tq, H * NSA_DH), lambda b, g, i: (b, i, g)),
        out_shape=jax.ShapeDtypeStruct((B, S, NSA_HEADS * NSA_DH), BF16),
        compiler_params=_cparams("parallel", "parallel", "arbitrary"),
        name="nsa_attn",
    )(q_r, kcmp, vcmp, ks_r, vs_r, kw_r, vw_r, proj3, ov)


def _layer_norm(x, g, b):
    mu = jnp.mean(x, axis=-1, keepdims=True)
    xc = x - mu
    var = jnp.mean(xc * xc, axis=-1, keepdims=True)
    return xc * lax.rsqrt(var + LN_EPS) * g + b


def _outproj_kernel(x_ref, og_ref, on_ref, w_ref, g_ref, b_ref, o_ref):
    half = og_ref.shape[1]
    mix = _dot(og_ref[...], w_ref[0:half, :]) + _dot(on_ref[...], w_ref[half:, :])
    o_ref[...] = _layer_norm(DN_ALPHA * x_ref[...] + mix, g_ref[...], b_ref[...])


def _outproj(x2, og2, on2, w, g, b):
    T, D = x2.shape
    tm = 512
    row = lambda width: pl.BlockSpec((tm, width), lambda i: (i, 0))
    full = lambda a: pl.BlockSpec(a.shape, lambda i: (0, 0))
    return pl.pallas_call(
        _outproj_kernel,
        grid=(T // tm,),
        in_specs=[row(D), row(og2.shape[1]), row(on2.shape[1]), full(w), full(g), full(b)],
        out_specs=row(D),
        out_shape=jax.ShapeDtypeStruct((T, D), F32),
        compiler_params=_cparams("parallel"),
        name="outproj_ln",
    )(x2, og2, on2, w, g, b)


def _router_kernel(h_ref, wh_ref, wl_ref, bias_ref, eidx_ref, wts_ref, rank_ref, cnt_ref, carry_ref):
    @pl.when(pl.program_id(0) == 0)
    def _():
        carry_ref[...] = jnp.zeros_like(carry_ref)

    h = h_ref[...]
    tm = h.shape[0]
    E = N_EXPERTS
    h_hi = h.astype(BF16)
    h_lo = (h - h_hi.astype(F32)).astype(BF16)
    wh = wh_ref[...]
    logits = _dot(h_hi, wh) + _dot(h_lo, wh) + _dot(h_hi, wl_ref[...])
    scores = jax.nn.sigmoid(logits)
    biased = scores + bias_ref[...]
    lane = lax.broadcasted_iota(jnp.int32, (tm, E), 1)
    gid = lane // (E // N_GROUPS)
    ninf = -jnp.inf

    def row_max(x):
        return jnp.max(x, axis=-1, keepdims=True)

    def first_idx(x, mx):
        return jnp.min(jnp.where(x == mx, lane, E), axis=-1, keepdims=True)

    gscore = []
    for gi in range(N_GROUPS):
        mg = jnp.where(gid == gi, biased, ninf)
        m1 = row_max(mg)
        i1 = first_idx(mg, m1)
        m2 = row_max(jnp.where(lane == i1, ninf, mg))
        gscore.append(m1 + m2)
    emask = jnp.zeros((tm, E), jnp.bool_)
    for gi in range(N_GROUPS):
        rk = jnp.zeros((tm, 1), jnp.int32)
        for gj in range(N_GROUPS):
            if gj == gi:
                continue
            beats = (gscore[gj] > gscore[gi]) | ((gscore[gj] == gscore[gi]) & (gj < gi))
            rk = rk + beats.astype(jnp.int32)
        emask = emask | ((gid == gi) & (rk < TOPK_GROUPS))
    masked = jnp.where(emask, biased, ninf)

    onehots, wsel = [], []
    selm = jnp.zeros((tm, E), F32)
    for k in range(TOP_K):
        mx = row_max(masked)
        idx = first_idx(masked, mx)
        oh = lane == idx
        onehots.append(oh)
        wsel.append(jnp.sum(jnp.where(oh, scores, 0.0), axis=-1, keepdims=True))
        masked = jnp.where(oh, ninf, masked)
        selm = jnp.where(oh, 1.0, selm)
        eidx_ref[:, k:k + 1] = idx
    wsum = wsel[0]
    for k in range(1, TOP_K):
        wsum = wsum + wsel[k]
    for k in range(TOP_K):
        wts_ref[:, k:k + 1] = wsel[k] / wsum * ROUTED_SCALE

    ri = lax.broadcasted_iota(jnp.int32, (tm, tm), 0)
    ci = lax.broadcasted_iota(jnp.int32, (tm, tm), 1)
    ltri = jnp.where(ri > ci, 1.0, 0.0).astype(BF16)
    cum = _dot(ltri, selm.astype(BF16)) + carry_ref[...]
    for k in range(TOP_K):
        rk = jnp.sum(jnp.where(onehots[k], cum, 0.0), axis=-1, keepdims=True)
        rank_ref[:, k:k + 1] = rk.astype(jnp.int32)
    total = carry_ref[...] + jnp.sum(selm, axis=0, keepdims=True)
    carry_ref[...] = total
    cnt_ref[...] = total


def _router(h2, w_hi, w_lo, bias):
    T, D = h2.shape
    tm = 256
    full = lambda a: pl.BlockSpec(a.shape, lambda i: (0, 0))
    o8 = pl.BlockSpec((tm, TOP_K), lambda i: (i, 0))
    return pl.pallas_call(
        _router_kernel,
        grid=(T // tm,),
        in_specs=[pl.BlockSpec((tm, D), lambda i: (i, 0)), full(w_hi), full(w_lo), full(bias)],
        out_specs=[o8, o8, o8, pl.BlockSpec((1, N_EXPERTS), lambda i: (0, 0))],
        out_shape=[jax.ShapeDtypeStruct((T, TOP_K), jnp.int32),
                   jax.ShapeDtypeStruct((T, TOP_K), F32),
                   jax.ShapeDtypeStruct((T, TOP_K), jnp.int32),
                   jax.ShapeDtypeStruct((1, N_EXPERTS), F32)],
        scratch_shapes=[pltpu.VMEM((1, N_EXPERTS), F32)],
        compiler_params=_cparams("arbitrary"),
        name="router",
    )(h2, w_hi, w_lo, bias)


def _dispatch_kernel(dest_ref, h_hbm, xs_in_hbm, xs_hbm, sem, *, tm):
    del xs_in_hbm
    base = pl.program_id(0) * tm
    n = tm * TOP_K

    def copy(j):
        t = base + j // TOP_K
        d = dest_ref[0, 0, j]
        return pltpu.make_async_copy(h_hbm.at[pl.ds(t, 1)], xs_hbm.at[pl.ds(d, 1)], sem)

    def start(j, c):
        copy(j).start()
        return c

    def wait(j, c):
        copy(j).wait()
        return c

    lax.fori_loop(0, n, start, 0)
    lax.fori_loop(0, n, wait, 0)


def _dispatch(dest3, h2, xs_init, tm):
    T, D = h2.shape
    nt = T // tm
    return pl.pallas_call(
        functools.partial(_dispatch_kernel, tm=tm),
        grid=(nt,),
        in_specs=[pl.BlockSpec((1, 1, tm * TOP_K), lambda i: (i, 0, 0), memory_space=pltpu.SMEM),
                  pl.BlockSpec(memory_space=pl.ANY),
                  pl.BlockSpec(memory_space=pl.ANY)],
        out_specs=pl.BlockSpec(memory_space=pl.ANY),
        out_shape=jax.ShapeDtypeStruct(xs_init.shape, xs_init.dtype),
        scratch_shapes=[pltpu.SemaphoreType.DMA(())],
        input_output_aliases={2: 0},
        compiler_params=_cparams("arbitrary"),
        name="dispatch",
    )(dest3, h2, xs_init)


def _expert_kernel(bexp_ref, nused_ref, x_ref, wg_ref, wu_ref, wd_ref, y_ref):
    del bexp_ref

    @pl.when(pl.program_id(0) < nused_ref[0])
    def _():
        x = x_ref[...].astype(BF16)
        gate = _dot(x, wg_ref[0].astype(BF16))
        up = _dot(x, wu_ref[0].astype(BF16))
        act = (gate * jax.nn.sigmoid(gate) * up).astype(BF16)
        y_ref[...] = _dot(act, wd_ref[0].astype(BF16))


def _experts(blk_exp, n_used, xs, wg, wu, wd):
    NP, D = xs.shape
    nb = NP // MOE_BLOCK

    def xmap(i, bexp, nused):
        return (jnp.minimum(i, nused[0] - 1), 0)

    def wmap(i, bexp, nused):
        return (bexp[jnp.minimum(i, nused[0] - 1)], 0, 0)

    grid_spec = pltpu.PrefetchScalarGridSpec(
        num_scalar_prefetch=2,
        grid=(nb,),
        in_specs=[pl.BlockSpec((MOE_BLOCK, D), xmap),
                  pl.BlockSpec((1, D, D_EXPERT), wmap),
                  pl.BlockSpec((1, D, D_EXPERT), wmap),
                  pl.BlockSpec((1, D_EXPERT, D), wmap)],
        out_specs=pl.BlockSpec((MOE_BLOCK, D), xmap),
    )
    return pl.pallas_call(
        _expert_kernel,
        grid_spec=grid_spec,
        out_shape=jax.ShapeDtypeStruct((NP, D), F32),
        compiler_params=_cparams("arbitrary"),
        name="experts",
    )(blk_exp, n_used, xs, wg, wu, wd)


def _combine_kernel(dest_ref, h_ref, wts_ref, y_hbm, wsg_ref, wsu_ref, wsd_ref, g_ref, b_ref, o_ref,
                    buf_ref, sem, *, tm):
    n = tm * TOP_K

    def copy(j):
        r = j // TOP_K
        k = j % TOP_K
        d = dest_ref[0, 0, j]
        return pltpu.make_async_copy(y_hbm.at[pl.ds(d, 1)], buf_ref.at[k, pl.ds(r, 1)], sem)

    def start(j, c):
        copy(j).start()
        return c

    def wait(j, c):
        copy(j).wait()
        return c

    lax.fori_loop(0, n, start, 0)
    h = h_ref[...]
    hb = h.astype(BF16)
    gate = _dot(hb, wsg_ref[...])
    up = _dot(hb, wsu_ref[...])
    shared = _dot((gate * jax.nn.sigmoid(gate) * up).astype(BF16), wsd_ref[...])
    lax.fori_loop(0, n, wait, 0)
    wts = wts_ref[...]
    routed = wts[:, 0:1] * buf_ref[0]
    for k in range(1, TOP_K):
        routed = routed + wts[:, k:k + 1] * buf_ref[k]
    o_ref[...] = _layer_norm(DN_ALPHA * h + (routed + shared), g_ref[...], b_ref[...])


def _combine(dest3, h2, wts, ys, wsg, wsu, wsd, g, b, tm):
    T, D = h2.shape
    full = lambda a: pl.BlockSpec(a.shape, lambda i: (0, 0))
    return pl.pallas_call(
        functools.partial(_combine_kernel, tm=tm),
        grid=(T // tm,),
        in_specs=[pl.BlockSpec((1, 1, tm * TOP_K), lambda i: (i, 0, 0), memory_space=pltpu.SMEM),
                  pl.BlockSpec((tm, D), lambda i: (i, 0)),
                  pl.BlockSpec((tm, TOP_K), lambda i: (i, 0)),
                  pl.BlockSpec(memory_space=pl.ANY),
                  full(wsg), full(wsu), full(wsd), full(g), full(b)],
        out_specs=pl.BlockSpec((tm, D), lambda i: (i, 0)),
        out_shape=jax.ShapeDtypeStruct((T, D), F32),
        scratch_shapes=[pltpu.VMEM((TOP_K, tm, D), F32), pltpu.SemaphoreType.DMA(())],
        compiler_params=_cparams("arbitrary"),
        name="combine_ln",
    )(dest3, h2, wts, ys, wsg, wsu, wsd, g, b)


def _regroup_w_in(w_in):
    parts, off = {}, 0
    for name, width in _SPLITS:
        parts[name] = w_in[:, off:off + width]
        off += width
    cols = []
    for name, (_, width) in _COLS.items():
        p = parts[name]
        if p.shape[1] < width:
            p = jnp.pad(p, ((0, 0), (0, width - p.shape[1])))
        cols.append(p)
    return jnp.concatenate(cols, axis=1).astype(BF16)


def _overlap_matrix(S):
    nr = S // CMP_STRIDE
    n_sel = S // SEL_LEN
    ci = np.arange(nr)[:, None] * CMP_STRIDE
    sj = np.arange(n_sel)[None, :] * SEL_LEN
    ov = np.clip(np.minimum(ci + CMP_LEN, sj + SEL_LEN) - np.maximum(ci, sj), 0, None) / CMP_LEN
    ov[nr - 1] = 0.0
    return jnp.asarray(ov, BF16)


def _mixers(h, positions, w_in, w_alpha2, b_alpha, gla_norm_g,
            cmp_pos_k, cmp_w1_k, cmp_b1_k, cmp_w2_k, cmp_pos_v, cmp_w1_v, cmp_b1_v, cmp_w2_v):
    B, S, D = h.shape
    proj = _proj(h.reshape(B * S, D), _regroup_w_in(w_in)).reshape(B, S, D_PROJ)

    wa = jnp.pad(w_alpha2, ((0, 128 - GLA_LOWRANK), (0, 0))).astype(BF16)
    o_gla = _gla(proj, wa, b_alpha.reshape(1, -1), gla_norm_g.reshape(1, -1))

    half = NSA_DH // 2
    inv = ROPE_THETA ** (-np.arange(half, dtype=np.float32) / half)
    inv_row = jnp.asarray(np.tile(inv, 128 // half).reshape(1, 128), F32)
    pos3 = positions.astype(F32).reshape(B, S, 1)
    q_r, kc_r, ks_r, vs_r, kw_r, vw_r = _nsa_prep(proj, pos3, inv_row)

    def blocks16(t):
        return (t.reshape(B, S // CMP_STRIDE, CMP_STRIDE, NSA_KV_GROUPS, NSA_DH)
                .transpose(0, 3, 1, 2, 4).reshape(B, NSA_KV_GROUPS, S // CMP_STRIDE, CMP_STRIDE * NSA_DH))

    vc_off = _COLS["nvc"][0]
    kcmp, vcmp = _compress(
        blocks16(kc_r), blocks16(proj[:, :, vc_off:vc_off + 128]),
        cmp_pos_k.reshape(2, -1), cmp_w1_k.astype(BF16), cmp_b1_k.reshape(1, -1), cmp_w2_k.astype(BF16),
        cmp_pos_v.reshape(2, -1), cmp_w1_v.astype(BF16), cmp_b1_v.reshape(1, -1), cmp_w2_v.astype(BF16))
    o_nsa = _nsa_attn(q_r, kcmp, vcmp, ks_r, vs_r, kw_r, vw_r, proj, _overlap_matrix(S))
    return o_gla, o_nsa


def _moe_meta(counts, eidx, rank):
    counts = counts.reshape(-1).astype(jnp.int32)
    padded = (counts + MOE_BLOCK - 1) // MOE_BLOCK * MOE_BLOCK
    pad_end = jnp.cumsum(padded)
    pad_start = pad_end - padded
    dest = pad_start[eidx] + rank
    n_used = (pad_end[-1] // MOE_BLOCK).astype(jnp.int32).reshape(1)
    return dest, pad_end, n_used


def _moe_ln(h2, w_router, router_bias, w_gate, w_up, w_down, ws_gate, ws_up, ws_down, ln_g, ln_b):
    T, D = h2.shape
    w_hi = w_router.astype(BF16)
    w_lo = (w_router - w_hi.astype(F32)).astype(BF16)
    eidx, wts, rank, counts = _router(h2, w_hi, w_lo, router_bias.reshape(1, -1))
    dest, pad_end, n_used = _moe_meta(counts, eidx, rank)
    nb = -(-T * TOP_K // MOE_BLOCK) + N_EXPERTS
    blk_exp = jnp.minimum(jnp.searchsorted(pad_end, jnp.arange(nb, dtype=jnp.int32) * MOE_BLOCK, side="right"),
                          N_EXPERTS - 1).astype(jnp.int32)
    tm_d = 256
    xs = _dispatch(dest.reshape(T // tm_d, 1, tm_d * TOP_K), h2, jnp.zeros((nb * MOE_BLOCK, D), F32), tm_d)
    ys = _experts(blk_exp, n_used, xs, w_gate, w_up, w_down)
    tm_c = 128
    return _combine(dest.reshape(T // tm_c, 1, tm_c * TOP_K), h2, wts, ys,
                    ws_gate.astype(BF16), ws_up.astype(BF16), ws_down.astype(BF16),
                    ln_g.reshape(1, -1), ln_b.reshape(1, -1), tm_c)


def kernel(x, positions, w_in, w_alpha2, b_alpha, gla_norm_g, cmp_pos_k, cmp_w1_k, cmp_b1_k, cmp_w2_k, cmp_pos_v, cmp_w1_v, cmp_b1_v, cmp_w2_v, w_out, ln1_g, ln1_b, w_router, router_bias, w_exp_gate, w_exp_up, w_exp_down, w_sh_gate, w_sh_up, w_sh_down, ln2_g, ln2_b):
    B, S, D = x.shape
    h = x
    for l in range(w_in.shape[0]):
        o_gla, o_nsa = _mixers(h, positions, w_in[l], w_alpha2[l], b_alpha[l], gla_norm_g[l],
                               cmp_pos_k[l], cmp_w1_k[l], cmp_b1_k[l], cmp_w2_k[l],
                               cmp_pos_v[l], cmp_w1_v[l], cmp_b1_v[l], cmp_w2_v[l])
        h1 = _outproj(h.reshape(B * S, D), o_gla.reshape(B * S, -1), o_nsa.reshape(B * S, -1),
                      w_out[l].astype(BF16), ln1_g[l].reshape(1, -1), ln1_b[l].reshape(1, -1))
        h2 = _moe_ln(h1, w_router[l], router_bias[l], w_exp_gate[l], w_exp_up[l], w_exp_down[l],
                     w_sh_gate[l], w_sh_up[l], w_sh_down[l], ln2_g[l], ln2_b[l])
        h = h2.reshape(B, S, D)
    return h
```

```python
import functools

import numpy as np
import jax
import jax.numpy as jnp
from jax import lax
from jax.experimental import pallas as pl
from jax.experimental.pallas import tpu as pltpu
from jax.experimental.pallas import tpu_sc as plsc

D_MODEL = 1024
GLA_HEADS = 4
GLA_DV = 128
GLA_DK = 64
GLA_LOWRANK = 16
GLA_TAU = 16.0
GLA_CHUNK = 64
NSA_HEADS = 8
NSA_KV_GROUPS = 2
NSA_HPG = 4
NSA_DH = 64
CMP_LEN = 32
CMP_STRIDE = 16
CMP_HIDDEN = 256
SEL_LEN = 64
SEL_TOPK = 16
WINDOW = 512
ROPE_THETA = 10000.0
N_EXPERTS = 256
TOP_K = 8
N_GROUPS = 8
TOPK_GROUPS = 4
D_EXPERT = 256
ROUTED_SCALE = 2.5
DEPTH = 1
DN_ALPHA = (2.0 * DEPTH) ** 0.25
LN_EPS = 1e-5

MOE_BLOCK = 256
NEG = -1e30
F32 = jnp.float32
BF16 = jnp.bfloat16

_COLS = {}
_off = 0
for _name, _w in (("gq", 256), ("gk", 256), ("gv", 512), ("gr", 512), ("nq", 512),
                  ("nkc", 128), ("nvc", 128), ("nks", 128), ("nvs", 128), ("nkw", 128), ("nvw", 128),
                  ("ga", 128), ("ng", 128)):
    _COLS[_name] = (_off, _w)
    _off += _w
D_PROJ = _off
_SPLITS = (("gq", 256), ("gk", 256), ("gv", 512), ("ga", 16), ("gr", 512), ("nq", 512),
           ("nkc", 128), ("nvc", 128), ("nks", 128), ("nvs", 128), ("nkw", 128), ("nvw", 128), ("ng", 24))

VMEM_LIMIT = 56 * 1024 * 1024


def _cparams(*sem):
    return pltpu.CompilerParams(dimension_semantics=sem, vmem_limit_bytes=VMEM_LIMIT)


def _dot(a, b):
    return jnp.dot(a, b, preferred_element_type=F32)


def _dot_nt(a, b):
    return lax.dot_general(a, b, (((1,), (1,)), ((), ())), preferred_element_type=F32)


def _split3(x):
    hi = x.astype(BF16)
    r1 = x - hi.astype(F32)
    mid = r1.astype(BF16)
    lo = (r1 - mid.astype(F32)).astype(BF16)
    return hi, mid, lo


def _proj_kernel(x_ref, w_ref, o_ref):
    o_ref[...] = _dot(x_ref[...].astype(BF16), w_ref[...])


def _proj(x2, w):
    T, D = x2.shape
    N = w.shape[1]
    tm, tn = 512, 1024
    return pl.pallas_call(
        _proj_kernel,
        grid=(T // tm, N // tn),
        in_specs=[pl.BlockSpec((tm, D), lambda i, j: (i, 0)),
                  pl.BlockSpec((D, tn), lambda i, j: (0, j))],
        out_specs=pl.BlockSpec((tm, tn), lambda i, j: (i, j)),
        out_shape=jax.ShapeDtypeStruct((T, N), F32),
        compiler_params=_cparams("parallel", "arbitrary"),
        name="proj",
    )(x2, w)


def _gla_kernel(q_ref, k_ref, v_ref, r_ref, a_ref, wa_ref, ba_ref, ng_ref, o_ref, st_ref, *, nchunk):
    C = GLA_CHUNK
    HK = GLA_HEADS * GLA_DK

    @pl.when(pl.program_id(1) == 0)
    def _():
        st_ref[...] = jnp.zeros_like(st_ref)

    ri = lax.broadcasted_iota(jnp.int32, (C, C), 0)
    ci = lax.broadcasted_iota(jnp.int32, (C, C), 1)
    causal = ri >= ci
    tri = jnp.where(causal, 1.0, 0.0).astype(BF16)
    lane_head = lax.broadcasted_iota(jnp.int32, (1, HK), 1) // GLA_DK
    wa = wa_ref[...]
    ba = ba_ref[...]
    ng = ng_ref[...]

    def chunk(c, carry):
        rows = pl.ds(pl.multiple_of(c * C, C), C)
        q = q_ref[0, rows, :]
        k = k_ref[0, rows, :]
        v = v_ref[0, rows, :]
        r = r_ref[0, rows, :]
        a = a_ref[0, rows, :]
        z = _dot(a.astype(BF16), wa) + ba
        g = (jnp.minimum(z, 0.0) - jnp.log1p(jnp.exp(-jnp.abs(z)))) * (1.0 / GLA_TAU)
        g_hi, g_mid, g_lo = _split3(g)
        b = _dot(tri, g_hi) + _dot(tri, g_mid) + _dot(tri, g_lo)
        b_last = b[C - 1:C, :]
        qt = q * jnp.exp(b) * (GLA_DK ** -0.5)
        kt = (k * jnp.exp(-b)).astype(BF16)
        ks = (k * jnp.exp(b_last - b)).astype(BF16)
        st = st_ref[...]
        st_b = st.astype(BF16)
        new_st = st * jnp.exp(b_last)
        for h in range(GLA_HEADS):
            hm = lane_head == h
            qh = jnp.where(hm, qt, 0.0).astype(BF16)
            att = jnp.where(causal, _dot_nt(qh, kt), 0.0)
            vh = v[:, h * GLA_DV:(h + 1) * GLA_DV].astype(BF16)
            o = _dot_nt(qh, st_b) + _dot(att.astype(BF16), vh)
            new_st = new_st + jnp.where(hm, _dot(vh.T, ks), 0.0)
            o = o * lax.rsqrt(jnp.mean(o * o, axis=-1, keepdims=True) + 1e-6) * ng
            rh = r[:, h * GLA_DV:(h + 1) * GLA_DV]
            o = o * (rh * jax.nn.sigmoid(rh))
            o_ref[0, rows, h * GLA_DV:(h + 1) * GLA_DV] = o.astype(o_ref.dtype)
        st_ref[...] = new_st
        return carry

    lax.fori_loop(0, nchunk, chunk, 0)


def _gla(proj3, wa, ba, ng):
    B, S, _ = proj3.shape
    L = 512 if S % 512 == 0 else S
    nchunk = L // GLA_CHUNK

    def col(name, width):
        off = _COLS[name][0]
        assert off % width == 0
        return pl.BlockSpec((1, L, width), lambda b, j, o=off // width: (b, j, o))

    full2 = lambda shape: pl.BlockSpec(shape, lambda b, j: (0, 0))
    return pl.pallas_call(
        functools.partial(_gla_kernel, nchunk=nchunk),
        grid=(B, S // L),
        in_specs=[col("gq", 256), col("gk", 256), col("gv", 512), col("gr", 512), col("ga", 128),
                  full2((128, 256)), full2((1, 256)), full2((1, 128))],
        out_specs=pl.BlockSpec((1, L, 512), lambda b, j: (b, j, 0)),
        out_shape=jax.ShapeDtypeStruct((B, S, 512), BF16),
        scratch_shapes=[pltpu.VMEM((GLA_DV, GLA_HEADS * GLA_DK), F32)],
        compiler_params=_cparams("parallel", "arbitrary"),
        name="gla",
    )(proj3, proj3, proj3, proj3, proj3, wa, ba, ng)


def _rot_half(x):
    n = x.shape[-1]
    lane = lax.broadcasted_iota(jnp.int32, (1, n), 1)
    first = (lane % NSA_DH) < (NSA_DH // 2)
    return jnp.where(first, -pltpu.roll(x, n - NSA_DH // 2, 1), pltpu.roll(x, NSA_DH // 2, 1))


def _nsa_prep_kernel(pos_ref, inv_ref, q_ref, kc_ref, ks_ref, vs_ref, kw_ref, vw_ref,
                     qo_ref, kco_ref, kso_ref, vso_ref, kwo_ref, vwo_ref):
    ang = pos_ref[0] * inv_ref[...]
    cos = jnp.cos(ang)
    sin = jnp.sin(ang)
    cos4 = jnp.concatenate([cos] * 4, axis=1)
    sin4 = jnp.concatenate([sin] * 4, axis=1)

    q = q_ref[0]
    qr = q * cos4 + _rot_half(q) * sin4
    for h in range(NSA_HEADS):
        qo_ref[0, h] = qr[:, h * NSA_DH:(h + 1) * NSA_DH].astype(qo_ref.dtype)

    def rope128(x):
        return x * cos + _rot_half(x) * sin

    kco_ref[0] = rope128(kc_ref[0])
    ksr = rope128(ks_ref[0])
    kwr = rope128(kw_ref[0])
    vs = vs_ref[0]
    vw = vw_ref[0]
    for g in range(NSA_KV_GROUPS):
        sl = slice(g * NSA_DH, (g + 1) * NSA_DH)
        kso_ref[0, g] = ksr[:, sl].astype(kso_ref.dtype)
        kwo_ref[0, g] = kwr[:, sl].astype(kwo_ref.dtype)
        vso_ref[0, g] = vs[:, sl].astype(vso_ref.dtype)
        vwo_ref[0, g] = vw[:, sl].astype(vwo_ref.dtype)


def _nsa_prep(proj3, pos3, inv_row):
    B, S, _ = proj3.shape
    ts = 512 if S % 512 == 0 else S

    def col(name, width):
        off = _COLS[name][0]
        assert off % width == 0
        return pl.BlockSpec((1, ts, width), lambda b, j, o=off // width: (b, j, o))

    kv_spec = pl.BlockSpec((1, NSA_KV_GROUPS, ts, NSA_DH), lambda b, j: (b, 0, j, 0))
    kv_shape = jax.ShapeDtypeStruct((B, NSA_KV_GROUPS, S, NSA_DH), BF16)
    return pl.pallas_call(
        _nsa_prep_kernel,
        grid=(B, S // ts),
        in_specs=[pl.BlockSpec((1, ts, 1), lambda b, j: (b, j, 0)),
                  pl.BlockSpec((1, 128), lambda b, j: (0, 0)),
                  col("nq", 512), col("nkc", 128), col("nks", 128), col("nvs", 128),
                  col("nkw", 128), col("nvw", 128)],
        out_specs=[pl.BlockSpec((1, NSA_HEADS, ts, NSA_DH), lambda b, j: (b, 0, j, 0)),
                   pl.BlockSpec((1, ts, 128), lambda b, j: (b, j, 0)),
                   kv_spec, kv_spec, kv_spec, kv_spec],
        out_shape=[jax.ShapeDtypeStruct((B, NSA_HEADS, S, NSA_DH), BF16),
                   jax.ShapeDtypeStruct((B, S, 128), F32),
                   kv_shape, kv_shape, kv_shape, kv_shape],
        compiler_params=_cparams("parallel", "parallel"),
        name="nsa_prep",
    )(pos3, inv_row, proj3, proj3, proj3, proj3, proj3, proj3)


def _compress_kernel(k_ref, v_ref, pk_ref, w1k_ref, b1k_ref, w2k_ref, pv_ref, w1v_ref, b1v_ref, w2v_ref,
                     ko_ref, vo_ref):
    half = CMP_STRIDE * NSA_DH

    def run(x_ref, p_ref, w1_ref, b1_ref, w2_ref, o_ref):
        p = p_ref[...]
        for g in range(NSA_KV_GROUPS):
            x = x_ref[0, g]
            nr = x.shape[0]
            a = _dot((x + p[0:1]).astype(BF16), w1_ref[0:half, :])
            bb = _dot((x + p[1:2]).astype(BF16), w1_ref[half:2 * half, :])
            pre = a + pltpu.roll(bb, nr - 1, 0) + b1_ref[...]
            hid = jax.nn.gelu(pre)
            out = _dot(hid.astype(BF16), w2_ref[...])
            row = lax.broadcasted_iota(jnp.int32, out.shape, 0)
            o_ref[0, g] = jnp.where(row < nr - 1, out, 0.0)

    run(k_ref, pk_ref, w1k_ref, b1k_ref, w2k_ref, ko_ref)
    run(v_ref, pv_ref, w1v_ref, b1v_ref, w2v_ref, vo_ref)


def _compress(kc4, vc4, pk, w1k, b1k, w2k, pv, w1v, b1v, w2v):
    B, G, NR, W = kc4.shape
    x_spec = pl.BlockSpec((1, G, NR, W), lambda b: (b, 0, 0, 0))
    full = lambda a: pl.BlockSpec(a.shape, lambda b: (0,) * a.ndim)
    o_spec = pl.BlockSpec((1, G, NR, NSA_DH), lambda b: (b, 0, 0, 0))
    o_shape = jax.ShapeDtypeStruct((B, G, NR, NSA_DH), F32)
    params = (pk, w1k, b1k, w2k, pv, w1v, b1v, w2v)
    return pl.pallas_call(
        _compress_kernel,
        grid=(B,),
        in_specs=[x_spec, x_spec] + [full(a) for a in params],
        out_specs=[o_spec, o_spec],
        out_shape=[o_shape, o_shape],
        compiler_params=_cparams("parallel"),
        name="compress",
    )(kc4, vc4, *params)


def _nsa_attn_kernel(q_ref, kc_ref, vc_ref, ks_ref, vs_ref, kw_ref, vw_ref, gate_ref, ov_ref, o_ref,
                     *, tq, S, n_sel, n_top):
    g = pl.program_id(1)
    qi = pl.program_id(2)
    start = qi * tq
    H = NSA_HPG
    scale = NSA_DH ** -0.5
    qs = q_ref[0].reshape(H * tq, NSA_DH)
    t_col = start + lax.broadcasted_iota(jnp.int32, (tq, 1), 0)

    kc = kc_ref[0, 0].astype(BF16)
    vc = vc_ref[0, 0].astype(BF16)
    NR = kc.shape[0]
    s = _dot_nt(qs, kc) * scale
    ncol = lax.broadcasted_iota(jnp.int32, (1, NR), 1)
    cmask = (ncol < NR - 1) & (ncol * CMP_STRIDE + (CMP_LEN - 1) <= t_col)
    s3 = jnp.where(cmask[None], s.reshape(H, tq, NR), NEG)
    m = jnp.max(s3, axis=-1, keepdims=True)
    e = jnp.exp(s3 - m)
    p3 = e / jnp.sum(e, axis=-1, keepdims=True)
    p3 = jnp.where(cmask[None], p3, 0.0)
    o_cmp = _dot(p3.reshape(H * tq, NR).astype(BF16), vc)

    psum = jnp.sum(p3, axis=0)
    p_hi = psum.astype(BF16)
    p_lo = (psum - p_hi.astype(F32)).astype(BF16)
    ov = ov_ref[...]
    imp = _dot(p_hi, ov) + _dot(p_lo, ov)
    blk = lax.broadcasted_iota(jnp.int32, (1, n_sel), 1)
    cur = t_col // SEL_LEN
    causal_blk = blk <= cur
    forced = (blk == 0) | (blk == cur) | (blk == cur - 1)
    val = jnp.where(causal_blk, jnp.where(forced, jnp.inf, imp), -jnp.inf)
    rank = jnp.zeros((tq, n_sel), jnp.int32)
    for j in range(n_sel):
        cj = val[:, j:j + 1]
        beats = (cj > val) | ((cj == val) & (blk > j))
        rank = rank + beats.astype(jnp.int32)
    sel = jnp.where((rank < n_top) & causal_blk, 1.0, 0.0).astype(BF16)

    tk = 256
    bpc = tk // SEL_LEN
    n_chunks = (start + tq + tk - 1) // tk
    erow = lax.broadcasted_iota(jnp.int32, (n_sel, tk), 0)
    ecol = lax.broadcasted_iota(jnp.int32, (n_sel, tk), 1) // SEL_LEN
    kk = lax.broadcasted_iota(jnp.int32, (1, tk), 1)

    def sel_chunk(c, carry):
        m_i, l_i, acc = carry
        k0 = pl.multiple_of(c * tk, tk)
        kb = ks_ref[0, 0, pl.ds(k0, tk), :]
        vb = vs_ref[0, 0, pl.ds(k0, tk), :]
        expand = jnp.where(erow == ecol + c * bpc, 1.0, 0.0).astype(BF16)
        msk = (_dot(sel, expand) > 0.5) & (k0 + kk <= t_col)
        sc = _dot_nt(qs, kb) * scale
        sc3 = jnp.where(msk[None], sc.reshape(H, tq, tk), NEG)
        m_new = jnp.maximum(m_i, jnp.max(sc3, axis=-1, keepdims=True))
        pe = jnp.where(msk[None], jnp.exp(sc3 - m_new), 0.0)
        alpha = jnp.exp(m_i - m_new)
        l_new = alpha * l_i + jnp.sum(pe, axis=-1, keepdims=True)
        pv = _dot(pe.reshape(H * tq, tk).astype(BF16), vb)
        acc = alpha.reshape(H * tq, 1) * acc + pv
        return m_new, l_new, acc

    m0 = jnp.full((H, tq, 1), NEG, F32)
    l0 = jnp.zeros((H, tq, 1), F32)
    a0 = jnp.zeros((H * tq, NSA_DH), F32)
    _, l_f, acc_f = lax.fori_loop(0, n_chunks, sel_chunk, (m0, l0, a0))
    o_sel = acc_f / l_f.reshape(H * tq, 1)

    span = min(WINDOW + tq, S)
    ws = jnp.clip(start - WINDOW, 0, S - span)
    ws = pl.multiple_of(ws, tq)
    kwb = kw_ref[0, 0, pl.ds(ws, span), :]
    vwb = vw_ref[0, 0, pl.ds(ws, span), :]
    kp = ws + lax.broadcasted_iota(jnp.int32, (1, span), 1)
    wmask = (kp <= t_col) & (kp > t_col - WINDOW)
    sw = _dot_nt(qs, kwb) * scale
    sw3 = jnp.where(wmask[None], sw.reshape(H, tq, span), NEG)
    mw = jnp.max(sw3, axis=-1, keepdims=True)
    ew = jnp.where(wmask[None], jnp.exp(sw3 - mw), 0.0)
    pw = ew / jnp.sum(ew, axis=-1, keepdims=True)
    o_win = _dot(pw.reshape(H * tq, span).astype(BF16), vwb)

    gs = jax.nn.sigmoid(gate_ref[0])
    for g_static in range(NSA_KV_GROUPS):
        @pl.when(g == g_static)
        def _(g_static=g_static):
            for h in range(H):
                c0 = g_static * H * 3 + h * 3
                rs = slice(h * tq, (h + 1) * tq)
                o = (gs[:, c0:c0 + 1] * o_cmp[rs] + gs[:, c0 + 1:c0 + 2] * o_sel[rs]
                     + gs[:, c0 + 2:c0 + 3] * o_win[rs])
                o_ref[0, :, h * NSA_DH:(h + 1) * NSA_DH] = o.astype(o_ref.dtype)


def _nsa_attn(q_r, kcmp, vcmp, ks_r, vs_r, kw_r, vw_r, proj3, ov):
    B, _, S, _ = q_r.shape
    G, H = NSA_KV_GROUPS, NSA_HPG
    NR = kcmp.shape[2]
    tq = 128
    n_sel = S // SEL_LEN
    n_top = min(SEL_TOPK, n_sel)
    cmp_spec = pl.BlockSpec((1, 1, NR, NSA_DH), lambda b, g, i: (b, g, 0, 0))
    kv_spec = pl.BlockSpec((1, 1, S, NSA_DH), lambda b, g, i: (b, g, 0, 0))
    goff = _COLS["ng"][0] // 128
    return pl.pallas_call(
        functools.partial(_nsa_attn_kernel, tq=tq, S=S, n_sel=n_sel, n_top=n_top),
        grid=(B, G, S // tq),
        in_specs=[pl.BlockSpec((1, H, tq, NSA_DH), lambda b, g, i: (b, g, i, 0)),
                  cmp_spec, cmp_spec, kv_spec, kv_spec, kv_spec, kv_spec,
                  pl.BlockSpec((1, tq, 128), lambda b, g, i: (b, i, goff)),
                  pl.BlockSpec(ov.shape, lambda b, g, i: (0, 0))],
        out_specs=pl.BlockSpec((1, tq, H * NSA_DH), lambda b, g, i: (b, i, g)),
        out_shape=jax.ShapeDtypeStruct((B, S, NSA_HEADS * NSA_DH), BF16),
        compiler_params=_cparams("parallel", "parallel", "arbitrary"),
        name="nsa_attn",
    )(q_r, kcmp, vcmp, ks_r, vs_r, kw_r, vw_r, proj3, ov)


def _layer_norm(x, g, b):
    mu = jnp.mean(x, axis=-1, keepdims=True)
    xc = x - mu
    var = jnp.mean(xc * xc, axis=-1, keepdims=True)
    return xc * lax.rsqrt(var + LN_EPS) * g + b


def _outproj_kernel(x_ref, og_ref, on_ref, w_ref, g_ref, b_ref, o_ref):
    half = og_ref.shape[1]
    mix = _dot(og_ref[...], w_ref[0:half, :]) + _dot(on_ref[...], w_ref[half:, :])
    o_ref[...] = _layer_norm(DN_ALPHA * x_ref[...] + mix, g_ref[...], b_ref[...])


def _outproj(x2, og2, on2, w, g, b):
    T, D = x2.shape
    tm = 512
    row = lambda width: pl.BlockSpec((tm, width), lambda i: (i, 0))
    full = lambda a: pl.BlockSpec(a.shape, lambda i: (0, 0))
    return pl.pallas_call(
        _outproj_kernel,
        grid=(T // tm,),
        in_specs=[row(D), row(og2.shape[1]), row(on2.shape[1]), full(w), full(g), full(b)],
        out_specs=row(D),
        out_shape=jax.ShapeDtypeStruct((T, D), F32),
        compiler_params=_cparams("parallel"),
        name="outproj_ln",
    )(x2, og2, on2, w, g, b)


def _router_kernel(h_ref, wh_ref, wl_ref, bias_ref, eidx_ref, wts_ref, rank_ref, cnt_ref, carry_ref):
    @pl.when(pl.program_id(0) == 0)
    def _():
        carry_ref[...] = jnp.zeros_like(carry_ref)

    h = h_ref[...]
    tm = h.shape[0]
    E = N_EXPERTS
    h_hi = h.astype(BF16)
    h_lo = (h - h_hi.astype(F32)).astype(BF16)
    wh = wh_ref[...]
    logits = _dot(h_hi, wh) + _dot(h_lo, wh) + _dot(h_hi, wl_ref[...])
    scores = jax.nn.sigmoid(logits)
    biased = scores + bias_ref[...]
    lane = lax.broadcasted_iota(jnp.int32, (tm, E), 1)
    gid = lane // (E // N_GROUPS)
    ninf = -jnp.inf

    def row_max(x):
        return jnp.max(x, axis=-1, keepdims=True)

    def first_idx(x, mx):
        return jnp.min(jnp.where(x == mx, lane, E), axis=-1, keepdims=True)

    gscore = []
    for gi in range(N_GROUPS):
        mg = jnp.where(gid == gi, biased, ninf)
        m1 = row_max(mg)
        i1 = first_idx(mg, m1)
        m2 = row_max(jnp.where(lane == i1, ninf, mg))
        gscore.append(m1 + m2)
    emask = jnp.zeros((tm, E), jnp.bool_)
    for gi in range(N_GROUPS):
        rk = jnp.zeros((tm, 1), jnp.int32)
        for gj in range(N_GROUPS):
            if gj == gi:
                continue
            beats = (gscore[gj] > gscore[gi]) | ((gscore[gj] == gscore[gi]) & (gj < gi))
            rk = rk + beats.astype(jnp.int32)
        emask = emask | ((gid == gi) & (rk < TOPK_GROUPS))
    masked = jnp.where(emask, biased, ninf)

    onehots, wsel = [], []
    selm = jnp.zeros((tm, E), F32)
    for k in range(TOP_K):
        mx = row_max(masked)
        idx = first_idx(masked, mx)
        oh = lane == idx
        onehots.append(oh)
        wsel.append(jnp.sum(jnp.where(oh, scores, 0.0), axis=-1, keepdims=True))
        masked = jnp.where(oh, ninf, masked)
        selm = jnp.where(oh, 1.0, selm)
        eidx_ref[:, k:k + 1] = idx
    wsum = wsel[0]
    for k in range(1, TOP_K):
        wsum = wsum + wsel[k]
    for k in range(TOP_K):
        wts_ref[:, k:k + 1] = wsel[k] / wsum * ROUTED_SCALE

    ri = lax.broadcasted_iota(jnp.int32, (tm, tm), 0)
    ci = lax.broadcasted_iota(jnp.int32, (tm, tm), 1)
    ltri = jnp.where(ri > ci, 1.0, 0.0).astype(BF16)
    cum = _dot(ltri, selm.astype(BF16)) + carry_ref[...]
    for k in range(TOP_K):
        rk = jnp.sum(jnp.where(onehots[k], cum, 0.0), axis=-1, keepdims=True)
        rank_ref[:, k:k + 1] = rk.astype(jnp.int32)
    total = carry_ref[...] + jnp.sum(selm, axis=0, keepdims=True)
    carry_ref[...] = total
    cnt_ref[...] = total


def _router(h2, w_hi, w_lo, bias):
    T, D = h2.shape
    tm = 256
    full = lambda a: pl.BlockSpec(a.shape, lambda i: (0, 0))
    o8 = pl.BlockSpec((tm, TOP_K), lambda i: (i, 0))
    return pl.pallas_call(
        _router_kernel,
        grid=(T // tm,),
        in_specs=[pl.BlockSpec((tm, D), lambda i: (i, 0)), full(w_hi), full(w_lo), full(bias)],
        out_specs=[o8, o8, o8, pl.BlockSpec((1, N_EXPERTS), lambda i: (0, 0))],
        out_shape=[jax.ShapeDtypeStruct((T, TOP_K), jnp.int32),
                   jax.ShapeDtypeStruct((T, TOP_K), F32),
                   jax.ShapeDtypeStruct((T, TOP_K), jnp.int32),
                   jax.ShapeDtypeStruct((1, N_EXPERTS), F32)],
        scratch_shapes=[pltpu.VMEM((1, N_EXPERTS), F32)],
        compiler_params=_cparams("arbitrary"),
        name="router",
    )(h2, w_hi, w_lo, bias)


def _dest_kernel(eidx_ref, rank_ref, ps_ref, dest_ref):
    tm = eidx_ref.shape[0]
    lane = lax.broadcasted_iota(jnp.int32, (tm, N_EXPERTS), 1)
    ps = ps_ref[...]
    for k in range(TOP_K):
        start = jnp.sum(jnp.where(lane == eidx_ref[:, k:k + 1], ps, 0.0), axis=-1, keepdims=True)
        dest_ref[:, k:k + 1] = start.astype(jnp.int32) + rank_ref[:, k:k + 1]


def _dest(eidx, rank, pad_start_f):
    T = eidx.shape[0]
    tm = 1024 if T % 1024 == 0 else T
    o8 = pl.BlockSpec((tm, TOP_K), lambda i: (i, 0))
    return pl.pallas_call(
        _dest_kernel,
        grid=(T // tm,),
        in_specs=[o8, o8, pl.BlockSpec((1, N_EXPERTS), lambda i: (0, 0))],
        out_specs=o8,
        out_shape=jax.ShapeDtypeStruct((T, TOP_K), jnp.int32),
        compiler_params=_cparams("parallel"),
        name="dest",
    )(eidx, rank, pad_start_f)


SC_WINDOW = 128
SUB = D_MODEL // 128


def _sc_gather(table3, idx):
    v, sub, lanes = table3.shape
    n = idx.shape[0] * sub
    idx_rows = (idx[:, None] * sub + jnp.arange(sub, dtype=jnp.int32)[None, :]).reshape(1, n)
    mesh = plsc.VectorSubcoreMesh(core_axis_name="core", subcore_axis_name="subcore")

    @functools.partial(pl.kernel, out_type=jax.ShapeDtypeStruct((n, lanes), table3.dtype), mesh=mesh,
                       name="sc_row_gather")
    def gather(x_hbm, i_hbm, o_hbm):
        def body(i_vmem, o_vmem):
            pltpu.sync_copy(x_hbm.at[i_vmem.at[0]], o_vmem)

        pltpu.emit_pipeline(
            body,
            grid=(n // SC_WINDOW,),
            in_specs=[pl.BlockSpec((1, SC_WINDOW), lambda i: (0, i))],
            out_specs=[pl.BlockSpec((SC_WINDOW, lanes), lambda i: (i, 0))],
            core_axis_name=("core", "subcore"),
            dimension_semantics=(pltpu.PARALLEL,),
        )(i_hbm, o_hbm)

    return gather(table3.reshape(v * sub, lanes), idx_rows).reshape(n // sub, sub, lanes)


def _expert_kernel(bexp_ref, nused_ref, x_ref, wg_ref, wu_ref, wd_ref, y_ref):
    del bexp_ref

    @pl.when(pl.program_id(0) < nused_ref[0])
    def _():
        x = jnp.concatenate([x_ref[:, c, :] for c in range(SUB)], axis=1).astype(BF16)
        gate = _dot(x, wg_ref[0].astype(BF16))
        up = _dot(x, wu_ref[0].astype(BF16))
        act = (gate * jax.nn.sigmoid(gate) * up).astype(BF16)
        y = _dot(act, wd_ref[0].astype(BF16))
        for c in range(SUB):
            y_ref[:, c, :] = y[:, c * 128:(c + 1) * 128]


def _experts(blk_exp, n_used, xs3, wg, wu, wd):
    NP = xs3.shape[0]
    D = D_MODEL
    nb = NP // MOE_BLOCK

    def xmap(i, bexp, nused):
        return (jnp.minimum(i, nused[0] - 1), 0, 0)

    def wmap(i, bexp, nused):
        return (bexp[jnp.minimum(i, nused[0] - 1)], 0, 0)

    grid_spec = pltpu.PrefetchScalarGridSpec(
        num_scalar_prefetch=2,
        grid=(nb,),
        in_specs=[pl.BlockSpec((MOE_BLOCK, SUB, 128), xmap),
                  pl.BlockSpec((1, D, D_EXPERT), wmap),
                  pl.BlockSpec((1, D, D_EXPERT), wmap),
                  pl.BlockSpec((1, D_EXPERT, D), wmap)],
        out_specs=pl.BlockSpec((MOE_BLOCK, SUB, 128), xmap),
    )
    return pl.pallas_call(
        _expert_kernel,
        grid_spec=grid_spec,
        out_shape=jax.ShapeDtypeStruct((NP, SUB, 128), F32),
        compiler_params=_cparams("arbitrary"),
        name="experts",
    )(blk_exp, n_used, xs3, wg, wu, wd)


def _combine_kernel(h_ref, wts_ref, yg_ref, wsg_ref, wsu_ref, wsd_ref, g_ref, b_ref, o_ref):
    h = h_ref[...]
    hb = h.astype(BF16)
    gate = _dot(hb, wsg_ref[...])
    up = _dot(hb, wsu_ref[...])
    shared = _dot((gate * jax.nn.sigmoid(gate) * up).astype(BF16), wsd_ref[...])
    wts = wts_ref[...]
    cols = []
    for c in range(SUB):
        acc = wts[:, 0:1] * yg_ref[:, 0, c, :]
        for k in range(1, TOP_K):
            acc = acc + wts[:, k:k + 1] * yg_ref[:, k, c, :]
        cols.append(acc)
    routed = jnp.concatenate(cols, axis=1)
    o_ref[...] = _layer_norm(DN_ALPHA * h + (routed + shared), g_ref[...], b_ref[...])


def _combine(h2, wts, yg4, wsg, wsu, wsd, g, b):
    T, D = h2.shape
    tm = 128
    full = lambda a: pl.BlockSpec(a.shape, lambda i: (0, 0))
    return pl.pallas_call(
        _combine_kernel,
        grid=(T // tm,),
        in_specs=[pl.BlockSpec((tm, D), lambda i: (i, 0)),
                  pl.BlockSpec((tm, TOP_K), lambda i: (i, 0)),
                  pl.BlockSpec((tm, TOP_K, SUB, 128), lambda i: (i, 0, 0, 0)),
                  full(wsg), full(wsu), full(wsd), full(g), full(b)],
        out_specs=pl.BlockSpec((tm, D), lambda i: (i, 0)),
        out_shape=jax.ShapeDtypeStruct((T, D), F32),
        compiler_params=_cparams("parallel"),
        name="combine_ln",
    )(h2, wts, yg4, wsg, wsu, wsd, g, b)


def _regroup_w_in(w_in):
    parts, off = {}, 0
    for name, width in _SPLITS:
        parts[name] = w_in[:, off:off + width]
        off += width
    cols = []
    for name, (_, width) in _COLS.items():
        p = parts[name]
        if p.shape[1] < width:
            p = jnp.pad(p, ((0, 0), (0, width - p.shape[1])))
        cols.append(p)
    return jnp.concatenate(cols, axis=1).astype(BF16)


def _overlap_matrix(S):
    nr = S // CMP_STRIDE
    n_sel = S // SEL_LEN
    ci = np.arange(nr)[:, None] * CMP_STRIDE
    sj = np.arange(n_sel)[None, :] * SEL_LEN
    ov = np.clip(np.minimum(ci + CMP_LEN, sj + SEL_LEN) - np.maximum(ci, sj), 0, None) / CMP_LEN
    ov[nr - 1] = 0.0
    return jnp.asarray(ov, BF16)


def _mixers(h, positions, w_in, w_alpha2, b_alpha, gla_norm_g,
            cmp_pos_k, cmp_w1_k, cmp_b1_k, cmp_w2_k, cmp_pos_v, cmp_w1_v, cmp_b1_v, cmp_w2_v):
    B, S, D = h.shape
    proj = _proj(h.reshape(B * S, D), _regroup_w_in(w_in)).reshape(B, S, D_PROJ)

    wa = jnp.pad(w_alpha2, ((0, 128 - GLA_LOWRANK), (0, 0))).astype(BF16)
    o_gla = _gla(proj, wa, b_alpha.reshape(1, -1), gla_norm_g.reshape(1, -1))

    half = NSA_DH // 2
    inv = ROPE_THETA ** (-np.arange(half, dtype=np.float32) / half)
    inv_row = jnp.asarray(np.tile(inv, 128 // half).reshape(1, 128), F32)
    pos3 = positions.astype(F32).reshape(B, S, 1)
    q_r, kc_r, ks_r, vs_r, kw_r, vw_r = _nsa_prep(proj, pos3, inv_row)

    def blocks16(t):
        return (t.reshape(B, S // CMP_STRIDE, CMP_STRIDE, NSA_KV_GROUPS, NSA_DH)
                .transpose(0, 3, 1, 2, 4).reshape(B, NSA_KV_GROUPS, S // CMP_STRIDE, CMP_STRIDE * NSA_DH))

    vc_off = _COLS["nvc"][0]
    kcmp, vcmp = _compress(
        blocks16(kc_r), blocks16(proj[:, :, vc_off:vc_off + 128]),
        cmp_pos_k.reshape(2, -1), cmp_w1_k.astype(BF16), cmp_b1_k.reshape(1, -1), cmp_w2_k.astype(BF16),
        cmp_pos_v.reshape(2, -1), cmp_w1_v.astype(BF16), cmp_b1_v.reshape(1, -1), cmp_w2_v.astype(BF16))
    o_nsa = _nsa_attn(q_r, kcmp, vcmp, ks_r, vs_r, kw_r, vw_r, proj, _overlap_matrix(S))
    return o_gla, o_nsa


def _moe_ln(h2, w_router, router_bias, w_gate, w_up, w_down, ws_gate, ws_up, ws_down, ln_g, ln_b):
    T, D = h2.shape
    P = T * TOP_K
    w_hi = w_router.astype(BF16)
    w_lo = (w_router - w_hi.astype(F32)).astype(BF16)
    eidx, wts, rank, counts = _router(h2, w_hi, w_lo, router_bias.reshape(1, -1))

    counts = counts.reshape(-1).astype(jnp.int32)
    padded = (counts + MOE_BLOCK - 1) // MOE_BLOCK * MOE_BLOCK
    pad_end = jnp.cumsum(padded)
    pad_start = pad_end - padded
    nb = -(-P // MOE_BLOCK) + N_EXPERTS
    n_used = (pad_end[-1] // MOE_BLOCK).astype(jnp.int32).reshape(1)
    blk_start = jnp.arange(nb, dtype=jnp.int32) * MOE_BLOCK
    blk_exp = jnp.minimum(jnp.sum((pad_end[None, :] <= blk_start[:, None]).astype(jnp.int32), axis=1),
                          N_EXPERTS - 1)

    dest = _dest(eidx, rank, pad_start.astype(F32).reshape(1, -1)).reshape(P)
    tok_of_row = jnp.zeros((nb * MOE_BLOCK,), jnp.int32).at[dest].set(jnp.arange(P, dtype=jnp.int32) // TOP_K)

    xs3 = _sc_gather(h2.reshape(T, SUB, 128), tok_of_row)
    ys3 = _experts(blk_exp, n_used, xs3, w_gate, w_up, w_down)
    yg4 = _sc_gather(ys3, dest).reshape(T, TOP_K, SUB, 128)
    return _combine(h2, wts, yg4, ws_gate.astype(BF16), ws_up.astype(BF16), ws_down.astype(BF16),
                    ln_g.reshape(1, -1), ln_b.reshape(1, -1))


def kernel(x, positions, w_in, w_alpha2, b_alpha, gla_norm_g, cmp_pos_k, cmp_w1_k, cmp_b1_k, cmp_w2_k, cmp_pos_v, cmp_w1_v, cmp_b1_v, cmp_w2_v, w_out, ln1_g, ln1_b, w_router, router_bias, w_exp_gate, w_exp_up, w_exp_down, w_sh_gate, w_sh_up, w_sh_down, ln2_g, ln2_b):
    B, S, D = x.shape
    h = x
    for l in range(w_in.shape[0]):
        o_gla, o_nsa = _mixers(h, positions, w_in[l], w_alpha2[l], b_alpha[l], gla_norm_g[l],
                               cmp_pos_k[l], cmp_w1_k[l], cmp_b1_k[l], cmp_w2_k[l],
                               cmp_pos_v[l], cmp_w1_v[l], cmp_b1_v[l], cmp_w2_v[l])
        h1 = _outproj(h.reshape(B * S, D), o_gla.reshape(B * S, -1), o_nsa.reshape(B * S, -1),
                      w_out[l].astype(BF16), ln1_g[l].reshape(1, -1), ln1_b[l].reshape(1, -1))
        h2 = _moe_ln(h1, w_router[l], router_bias[l], w_exp_gate[l], w_exp_up[l], w_exp_down[l],
                     w_sh_gate[l], w_sh_up[l], w_sh_down[l], ln2_g[l], ln2_b[l])
        h = h2.reshape(B, S, D)
    return h
```

```python
import functools

import numpy as np
import jax
import jax.numpy as jnp
from jax import lax
from jax.experimental import pallas as pl
from jax.experimental.pallas import tpu as pltpu
from jax.experimental.pallas import tpu_sc as plsc

D_MODEL = 1024
GLA_HEADS = 4
GLA_DV = 128
GLA_DK = 64
GLA_LOWRANK = 16
GLA_TAU = 16.0
GLA_CHUNK = 64
NSA_HEADS = 8
NSA_KV_GROUPS = 2
NSA_HPG = 4
NSA_DH = 64
CMP_LEN = 32
CMP_STRIDE = 16
CMP_HIDDEN = 256
SEL_LEN = 64
SEL_TOPK = 16
WINDOW = 512
ROPE_THETA = 10000.0
N_EXPERTS = 256
TOP_K = 8
N_GROUPS = 8
TOPK_GROUPS = 4
D_EXPERT = 256
ROUTED_SCALE = 2.5
DEPTH = 1
DN_ALPHA = (2.0 * DEPTH) ** 0.25
LN_EPS = 1e-5

MOE_BLOCK = 256
NEG = -1e30
F32 = jnp.float32
BF16 = jnp.bfloat16

_COLS = {}
_off = 0
for _name, _w in (("gq", 256), ("gk", 256), ("gv", 512), ("gr", 512), ("nq", 512),
                  ("nkc", 128), ("nvc", 128), ("nks", 128), ("nvs", 128), ("nkw", 128), ("nvw", 128),
                  ("ga", 128), ("ng", 128)):
    _COLS[_name] = (_off, _w)
    _off += _w
D_PROJ = _off
_SPLITS = (("gq", 256), ("gk", 256), ("gv", 512), ("ga", 16), ("gr", 512), ("nq", 512),
           ("nkc", 128), ("nvc", 128), ("nks", 128), ("nvs", 128), ("nkw", 128), ("nvw", 128), ("ng", 24))

VMEM_LIMIT = 56 * 1024 * 1024


def _cparams(*sem):
    return pltpu.CompilerParams(dimension_semantics=sem, vmem_limit_bytes=VMEM_LIMIT)


def _dot(a, b):
    return jnp.dot(a, b, preferred_element_type=F32)


def _dot_nt(a, b):
    return lax.dot_general(a, b, (((1,), (1,)), ((), ())), preferred_element_type=F32)


def _split3(x):
    hi = x.astype(BF16)
    r1 = x - hi.astype(F32)
    mid = r1.astype(BF16)
    lo = (r1 - mid.astype(F32)).astype(BF16)
    return hi, mid, lo


def _proj_kernel(x_ref, w_ref, o_ref):
    o_ref[...] = _dot(x_ref[...].astype(BF16), w_ref[...])


def _proj(x2, w):
    T, D = x2.shape
    N = w.shape[1]
    tm, tn = 512, 1024
    return pl.pallas_call(
        _proj_kernel,
        grid=(T // tm, N // tn),
        in_specs=[pl.BlockSpec((tm, D), lambda i, j: (i, 0)),
                  pl.BlockSpec((D, tn), lambda i, j: (0, j))],
        out_specs=pl.BlockSpec((tm, tn), lambda i, j: (i, j)),
        out_shape=jax.ShapeDtypeStruct((T, N), F32),
        compiler_params=_cparams("parallel", "arbitrary"),
        name="proj",
    )(x2, w)


def _gla_kernel(q_ref, k_ref, v_ref, r_ref, a_ref, wa_ref, ba_ref, ng_ref, o_ref, st_ref, *, nchunk):
    C = GLA_CHUNK
    HK = GLA_HEADS * GLA_DK

    @pl.when(pl.program_id(1) == 0)
    def _():
        st_ref[...] = jnp.zeros_like(st_ref)

    ri = lax.broadcasted_iota(jnp.int32, (C, C), 0)
    ci = lax.broadcasted_iota(jnp.int32, (C, C), 1)
    causal = ri >= ci
    tri = jnp.where(causal, 1.0, 0.0).astype(BF16)
    lane_head = lax.broadcasted_iota(jnp.int32, (1, HK), 1) // GLA_DK
    wa = wa_ref[...]
    ba = ba_ref[...]
    ng = ng_ref[...]

    def chunk(c, carry):
        rows = pl.ds(pl.multiple_of(c * C, C), C)
        q = q_ref[0, rows, :]
        k = k_ref[0, rows, :]
        v = v_ref[0, rows, :]
        r = r_ref[0, rows, :]
        a = a_ref[0, rows, :]
        z = _dot(a.astype(BF16), wa) + ba
        g = (jnp.minimum(z, 0.0) - jnp.log1p(jnp.exp(-jnp.abs(z)))) * (1.0 / GLA_TAU)
        g_hi, g_mid, g_lo = _split3(g)
        b = _dot(tri, g_hi) + _dot(tri, g_mid) + _dot(tri, g_lo)
        b_last = b[C - 1:C, :]
        qt = q * jnp.exp(b) * (GLA_DK ** -0.5)
        kt = (k * jnp.exp(-b)).astype(BF16)
        ks = (k * jnp.exp(b_last - b)).astype(BF16)
        st = st_ref[...]
        st_b = st.astype(BF16)
        new_st = st * jnp.exp(b_last)
        for h in range(GLA_HEADS):
            hm = lane_head == h
            qh = jnp.where(hm, qt, 0.0).astype(BF16)
            att = jnp.where(causal, _dot_nt(qh, kt), 0.0)
            vh = v[:, h * GLA_DV:(h + 1) * GLA_DV].astype(BF16)
            o = _dot_nt(qh, st_b) + _dot(att.astype(BF16), vh)
            new_st = new_st + jnp.where(hm, _dot(vh.T, ks), 0.0)
            o = o * lax.rsqrt(jnp.mean(o * o, axis=-1, keepdims=True) + 1e-6) * ng
            rh = r[:, h * GLA_DV:(h + 1) * GLA_DV]
            o = o * (rh * jax.nn.sigmoid(rh))
            o_ref[0, rows, h * GLA_DV:(h + 1) * GLA_DV] = o.astype(o_ref.dtype)
        st_ref[...] = new_st
        return carry

    lax.fori_loop(0, nchunk, chunk, 0)


def _gla(proj3, wa, ba, ng):
    B, S, _ = proj3.shape
    L = 512 if S % 512 == 0 else S
    nchunk = L // GLA_CHUNK

    def col(name, width):
        off = _COLS[name][0]
        assert off % width == 0
        return pl.BlockSpec((1, L, width), lambda b, j, o=off // width: (b, j, o))

    full2 = lambda shape: pl.BlockSpec(shape, lambda b, j: (0, 0))
    return pl.pallas_call(
        functools.partial(_gla_kernel, nchunk=nchunk),
        grid=(B, S // L),
        in_specs=[col("gq", 256), col("gk", 256), col("gv", 512), col("gr", 512), col("ga", 128),
                  full2((128, 256)), full2((1, 256)), full2((1, 128))],
        out_specs=pl.BlockSpec((1, L, 512), lambda b, j: (b, j, 0)),
        out_shape=jax.ShapeDtypeStruct((B, S, 512), BF16),
        scratch_shapes=[pltpu.VMEM((GLA_DV, GLA_HEADS * GLA_DK), F32)],
        compiler_params=_cparams("parallel", "arbitrary"),
        name="gla",
    )(proj3, proj3, proj3, proj3, proj3, wa, ba, ng)


def _rot_half(x):
    n = x.shape[-1]
    lane = lax.broadcasted_iota(jnp.int32, (1, n), 1)
    first = (lane % NSA_DH) < (NSA_DH // 2)
    return jnp.where(first, -pltpu.roll(x, n - NSA_DH // 2, 1), pltpu.roll(x, NSA_DH // 2, 1))


def _nsa_prep_kernel(pos_ref, inv_ref, q_ref, kc_ref, ks_ref, vs_ref, kw_ref, vw_ref,
                     qo_ref, kco_ref, kso_ref, vso_ref, kwo_ref, vwo_ref):
    ang = pos_ref[0] * inv_ref[...]
    cos = jnp.cos(ang)
    sin = jnp.sin(ang)
    cos4 = jnp.concatenate([cos] * 4, axis=1)
    sin4 = jnp.concatenate([sin] * 4, axis=1)

    q = q_ref[0]
    qr = q * cos4 + _rot_half(q) * sin4
    for h in range(NSA_HEADS):
        qo_ref[0, h] = (qr[:, h * NSA_DH:(h + 1) * NSA_DH] * (NSA_DH ** -0.5)).astype(qo_ref.dtype)

    def rope128(x):
        return x * cos + _rot_half(x) * sin

    kco_ref[0] = rope128(kc_ref[0])
    ksr = rope128(ks_ref[0])
    kwr = rope128(kw_ref[0])
    vs = vs_ref[0]
    vw = vw_ref[0]
    for g in range(NSA_KV_GROUPS):
        sl = slice(g * NSA_DH, (g + 1) * NSA_DH)
        kso_ref[0, g] = ksr[:, sl].astype(kso_ref.dtype)
        kwo_ref[0, g] = kwr[:, sl].astype(kwo_ref.dtype)
        vso_ref[0, g] = vs[:, sl].astype(vso_ref.dtype)
        vwo_ref[0, g] = vw[:, sl].astype(vwo_ref.dtype)


def _nsa_prep(proj3, pos3, inv_row):
    B, S, _ = proj3.shape
    ts = 512 if S % 512 == 0 else S

    def col(name, width):
        off = _COLS[name][0]
        assert off % width == 0
        return pl.BlockSpec((1, ts, width), lambda b, j, o=off // width: (b, j, o))

    kv_spec = pl.BlockSpec((1, NSA_KV_GROUPS, ts, NSA_DH), lambda b, j: (b, 0, j, 0))
    kv_shape = jax.ShapeDtypeStruct((B, NSA_KV_GROUPS, S, NSA_DH), BF16)
    return pl.pallas_call(
        _nsa_prep_kernel,
        grid=(B, S // ts),
        in_specs=[pl.BlockSpec((1, ts, 1), lambda b, j: (b, j, 0)),
                  pl.BlockSpec((1, 128), lambda b, j: (0, 0)),
                  col("nq", 512), col("nkc", 128), col("nks", 128), col("nvs", 128),
                  col("nkw", 128), col("nvw", 128)],
        out_specs=[pl.BlockSpec((1, NSA_HEADS, ts, NSA_DH), lambda b, j: (b, 0, j, 0)),
                   pl.BlockSpec((1, ts, 128), lambda b, j: (b, j, 0)),
                   kv_spec, kv_spec, kv_spec, kv_spec],
        out_shape=[jax.ShapeDtypeStruct((B, NSA_HEADS, S, NSA_DH), BF16),
                   jax.ShapeDtypeStruct((B, S, 128), F32),
                   kv_shape, kv_shape, kv_shape, kv_shape],
        compiler_params=_cparams("parallel", "parallel"),
        name="nsa_prep",
    )(pos3, inv_row, proj3, proj3, proj3, proj3, proj3, proj3)


def _compress_kernel(k_ref, v_ref, pk_ref, w1k_ref, b1k_ref, w2k_ref, pv_ref, w1v_ref, b1v_ref, w2v_ref,
                     ko_ref, vo_ref):
    half = CMP_STRIDE * NSA_DH

    def run(x_ref, p_ref, w1_ref, b1_ref, w2_ref, o_ref):
        p = p_ref[...]
        for g in range(NSA_KV_GROUPS):
            x = x_ref[0, g]
            nr = x.shape[0]
            a = _dot((x + p[0:1]).astype(BF16), w1_ref[0:half, :])
            bb = _dot((x + p[1:2]).astype(BF16), w1_ref[half:2 * half, :])
            pre = a + pltpu.roll(bb, nr - 1, 0) + b1_ref[...]
            hid = jax.nn.gelu(pre)
            out = _dot(hid.astype(BF16), w2_ref[...])
            row = lax.broadcasted_iota(jnp.int32, out.shape, 0)
            o_ref[0, g] = jnp.where(row < nr - 1, out, 0.0)

    run(k_ref, pk_ref, w1k_ref, b1k_ref, w2k_ref, ko_ref)
    run(v_ref, pv_ref, w1v_ref, b1v_ref, w2v_ref, vo_ref)


def _compress(kc4, vc4, pk, w1k, b1k, w2k, pv, w1v, b1v, w2v):
    B, G, NR, W = kc4.shape
    x_spec = pl.BlockSpec((1, G, NR, W), lambda b: (b, 0, 0, 0))
    full = lambda a: pl.BlockSpec(a.shape, lambda b: (0,) * a.ndim)
    o_spec = pl.BlockSpec((1, G, NR, NSA_DH), lambda b: (b, 0, 0, 0))
    o_shape = jax.ShapeDtypeStruct((B, G, NR, NSA_DH), F32)
    params = (pk, w1k, b1k, w2k, pv, w1v, b1v, w2v)
    return pl.pallas_call(
        _compress_kernel,
        grid=(B,),
        in_specs=[x_spec, x_spec] + [full(a) for a in params],
        out_specs=[o_spec, o_spec],
        out_shape=[o_shape, o_shape],
        compiler_params=_cparams("parallel"),
        name="compress",
    )(kc4, vc4, *params)


def _nsa_attn_kernel(q_ref, kc_ref, vc_ref, ks_ref, vs_ref, kw_ref, vw_ref, gate_ref, ovt_ref, o_ref,
                     *, tq, S, n_sel, n_top):
    g = pl.program_id(1)
    qi = pl.program_id(2)
    start = qi * tq
    H = NSA_HPG
    qs = q_ref[0].reshape(H * tq, NSA_DH)
    t_col = start + lax.broadcasted_iota(jnp.int32, (tq, 1), 0)
    t_row = start + lax.broadcasted_iota(jnp.int32, (1, tq), 1)

    kc = kc_ref[0, 0].astype(BF16)
    vc = vc_ref[0, 0].astype(BF16)
    NR = kc.shape[0]
    s = _dot_nt(qs, kc)
    ncol = lax.broadcasted_iota(jnp.int32, (1, NR), 1)
    cmask = (ncol < NR - 1) & (ncol * CMP_STRIDE + (CMP_LEN - 1) <= t_col)
    s3 = jnp.where(cmask[None], s.reshape(H, tq, NR), NEG)
    m = jnp.max(s3, axis=-1, keepdims=True)
    e = jnp.exp(s3 - m)
    p3 = e / jnp.sum(e, axis=-1, keepdims=True)
    p3 = jnp.where(cmask[None], p3, 0.0)
    o_cmp = _dot(p3.reshape(H * tq, NR).astype(BF16), vc)

    psum = jnp.sum(p3, axis=0)
    p_hi = psum.astype(BF16)
    p_lo = (psum - p_hi.astype(F32)).astype(BF16)
    ovt = ovt_ref[...]
    imp = _dot_nt(ovt, p_hi) + _dot_nt(ovt, p_lo)
    blk = lax.broadcasted_iota(jnp.int32, (n_sel, 1), 0)
    cur = t_row // SEL_LEN
    causal_blk = blk <= cur
    forced = (blk == 0) | (blk == cur) | (blk == cur - 1)
    val = jnp.where(causal_blk, jnp.where(forced, jnp.inf, imp), -jnp.inf)
    rank = jnp.zeros((n_sel, tq), jnp.int32)
    for j in range(n_sel):
        vj = val[j:j + 1, :]
        beats = (vj > val) | ((vj == val) & (blk > j))
        rank = rank + beats.astype(jnp.int32)
    sel_t = jnp.where((rank < n_top) & causal_blk, 1.0, 0.0)
    sel_t = jnp.concatenate([sel_t, jnp.zeros((128 - n_sel, tq), F32)], axis=0)
    sel = sel_t.T.astype(BF16)

    tk = 256
    nsub = 2 if S % (2 * tk) == 0 else 1
    bpc = tk // SEL_LEN
    n_trips = (start + tq + nsub * tk - 1) // (nsub * tk)
    erow = lax.broadcasted_iota(jnp.int32, (128, tk), 0)
    ecol = lax.broadcasted_iota(jnp.int32, (128, tk), 1) // SEL_LEN
    kk = lax.broadcasted_iota(jnp.int32, (1, tk), 1)

    def sel_trip(c, carry):
        m_i, l_i, acc = carry
        scores, vals = [], []
        m_new = m_i
        for u in range(nsub):
            cu = c * nsub + u
            k0 = pl.multiple_of(cu * tk, tk)
            kb = ks_ref[0, 0, pl.ds(k0, tk), :]
            vals.append(vs_ref[0, 0, pl.ds(k0, tk), :])
            expand = jnp.where(erow == ecol + cu * bpc, 1.0, 0.0).astype(BF16)
            allowed = (_dot(sel, expand) > 0.5) & (k0 + kk <= t_col)
            bias = jnp.where(allowed, 0.0, NEG)
            sc3 = _dot_nt(qs, kb).reshape(H, tq, tk) + bias[None]
            scores.append(sc3)
            m_new = jnp.maximum(m_new, jnp.max(sc3, axis=-1, keepdims=True))
        alpha = jnp.exp(m_i - m_new)
        l_new = alpha * l_i
        acc = alpha.reshape(H * tq, 1) * acc
        for u in range(nsub):
            pe = jnp.exp(scores[u] - m_new)
            l_new = l_new + jnp.sum(pe, axis=-1, keepdims=True)
            acc = acc + _dot(pe.reshape(H * tq, tk).astype(BF16), vals[u])
        return m_new, l_new, acc

    m0 = jnp.full((H, tq, 1), NEG, F32)
    l0 = jnp.zeros((H, tq, 1), F32)
    a0 = jnp.zeros((H * tq, NSA_DH), F32)
    _, l_f, acc_f = lax.fori_loop(0, n_trips, sel_trip, (m0, l0, a0))
    o_sel = acc_f / l_f.reshape(H * tq, 1)

    span = min(WINDOW + tq, S)
    ws = jnp.clip(start - WINDOW, 0, S - span)
    ws = pl.multiple_of(ws, tq)
    kwb = kw_ref[0, 0, pl.ds(ws, span), :]
    vwb = vw_ref[0, 0, pl.ds(ws, span), :]
    kp = ws + lax.broadcasted_iota(jnp.int32, (1, span), 1)
    wbias = jnp.where((kp <= t_col) & (kp > t_col - WINDOW), 0.0, NEG)
    sw3 = _dot_nt(qs, kwb).reshape(H, tq, span) + wbias[None]
    mw = jnp.max(sw3, axis=-1, keepdims=True)
    ew = jnp.exp(sw3 - mw)
    pw = ew / jnp.sum(ew, axis=-1, keepdims=True)
    o_win = _dot(pw.reshape(H * tq, span).astype(BF16), vwb)

    gs = jax.nn.sigmoid(gate_ref[0])
    for g_static in range(NSA_KV_GROUPS):
        @pl.when(g == g_static)
        def _(g_static=g_static):
            for h in range(H):
                c0 = g_static * H * 3 + h * 3
                rs = slice(h * tq, (h + 1) * tq)
                o = (gs[:, c0:c0 + 1] * o_cmp[rs] + gs[:, c0 + 1:c0 + 2] * o_sel[rs]
                     + gs[:, c0 + 2:c0 + 3] * o_win[rs])
                o_ref[0, :, h * NSA_DH:(h + 1) * NSA_DH] = o.astype(o_ref.dtype)


def _nsa_attn(q_r, kcmp, vcmp, ks_r, vs_r, kw_r, vw_r, proj3, ov):
    B, _, S, _ = q_r.shape
    G, H = NSA_KV_GROUPS, NSA_HPG
    NR = kcmp.shape[2]
    tq = 256
    n_sel = S // SEL_LEN
    n_top = min(SEL_TOPK, n_sel)
    cmp_spec = pl.BlockSpec((1, 1, NR, NSA_DH), lambda b, g, i: (b, g, 0, 0))
    kv_spec = pl.BlockSpec((1, 1, S, NSA_DH), lambda b, g, i: (b, g, 0, 0))
    goff = _COLS["ng"][0] // 128
    return pl.pallas_call(
        functools.partial(_nsa_attn_kernel, tq=tq, S=S, n_sel=n_sel, n_top=n_top),
        grid=(B, G, S // tq),
        in_specs=[pl.BlockSpec((1, H, tq, NSA_DH), lambda b, g, i: (b, g, i, 0)),
                  cmp_spec, cmp_spec, kv_spec, kv_spec, kv_spec, kv_spec,
                  pl.BlockSpec((1, tq, 128), lambda b, g, i: (b, i, goff)),
                  pl.BlockSpec(ov.shape, lambda b, g, i: (0, 0))],
        out_specs=pl.BlockSpec((1, tq, H * NSA_DH), lambda b, g, i: (b, i, g)),
        out_shape=jax.ShapeDtypeStruct((B, S, NSA_HEADS * NSA_DH), BF16),
        compiler_params=_cparams("parallel", "parallel", "arbitrary"),
        name="nsa_attn",
    )(q_r, kcmp, vcmp, ks_r, vs_r, kw_r, vw_r, proj3, ov)


def _layer_norm(x, g, b):
    mu = jnp.mean(x, axis=-1, keepdims=True)
    xc = x - mu
    var = jnp.mean(xc * xc, axis=-1, keepdims=True)
    return xc * lax.rsqrt(var + LN_EPS) * g + b


SUB = D_MODEL // 128
ROWS8 = 8


def _tile_rows(t):
    return (t // ROWS8) * (SUB * ROWS8) + t % ROWS8


def _to_tiles(ref, val):
    rows = val.shape[0]
    for c in range(SUB):
        ref[:, c] = val[:, c * 128:(c + 1) * 128].reshape(rows // ROWS8, ROWS8, 128)


def _from_tiles(ref, lead=()):
    groups = ref.shape[len(lead)]
    return jnp.concatenate([ref[lead + (slice(None), c)].reshape(groups * ROWS8, 128) for c in range(SUB)], axis=1)


def _outproj_kernel(x_ref, og_ref, on_ref, w_ref, g_ref, b_ref, o_ref, ot_ref):
    half = og_ref.shape[1]
    mix = _dot(og_ref[...], w_ref[0:half, :]) + _dot(on_ref[...], w_ref[half:, :])
    h = _layer_norm(DN_ALPHA * x_ref[...] + mix, g_ref[...], b_ref[...])
    o_ref[...] = h
    _to_tiles(ot_ref, h)


def _outproj(x2, og2, on2, w, g, b):
    T, D = x2.shape
    tm = 512
    row = lambda width: pl.BlockSpec((tm, width), lambda i: (i, 0))
    full = lambda a: pl.BlockSpec(a.shape, lambda i: (0, 0))
    return pl.pallas_call(
        _outproj_kernel,
        grid=(T // tm,),
        in_specs=[row(D), row(og2.shape[1]), row(on2.shape[1]), full(w), full(g), full(b)],
        out_specs=[row(D), pl.BlockSpec((tm // ROWS8, SUB, ROWS8, 128), lambda i: (i, 0, 0, 0))],
        out_shape=[jax.ShapeDtypeStruct((T, D), F32),
                   jax.ShapeDtypeStruct((T // ROWS8, SUB, ROWS8, 128), F32)],
        compiler_params=_cparams("parallel"),
        name="outproj_ln",
    )(x2, og2, on2, w, g, b)


def _router_kernel(h_ref, wh_ref, wl_ref, bias_ref, eidx_ref, wts_ref, rank_ref, cnt_ref, carry_ref):
    @pl.when(pl.program_id(0) == 0)
    def _():
        carry_ref[...] = jnp.zeros_like(carry_ref)

    h = h_ref[...]
    tm = h.shape[0]
    E = N_EXPERTS
    h_hi = h.astype(BF16)
    h_lo = (h - h_hi.astype(F32)).astype(BF16)
    wh = wh_ref[...]
    logits = _dot(h_hi, wh) + _dot(h_lo, wh) + _dot(h_hi, wl_ref[...])
    scores = jax.nn.sigmoid(logits)
    biased = scores + bias_ref[...]
    lane = lax.broadcasted_iota(jnp.int32, (tm, E), 1)
    gid = lane // (E // N_GROUPS)
    ninf = -jnp.inf

    def row_max(x):
        return jnp.max(x, axis=-1, keepdims=True)

    def first_idx(x, mx):
        return jnp.min(jnp.where(x == mx, lane, E), axis=-1, keepdims=True)

    gscore = []
    for gi in range(N_GROUPS):
        mg = jnp.where(gid == gi, biased, ninf)
        m1 = row_max(mg)
        i1 = first_idx(mg, m1)
        m2 = row_max(jnp.where(lane == i1, ninf, mg))
        gscore.append(m1 + m2)
    emask = jnp.zeros((tm, E), jnp.bool_)
    for gi in range(N_GROUPS):
        rk = jnp.zeros((tm, 1), jnp.int32)
        for gj in range(N_GROUPS):
            if gj == gi:
                continue
            beats = (gscore[gj] > gscore[gi]) | ((gscore[gj] == gscore[gi]) & (gj < gi))
            rk = rk + beats.astype(jnp.int32)
        emask = emask | ((gid == gi) & (rk < TOPK_GROUPS))
    masked = jnp.where(emask, biased, ninf)

    onehots, wsel = [], []
    selm = jnp.zeros((tm, E), F32)
    for k in range(TOP_K):
        mx = row_max(masked)
        idx = first_idx(masked, mx)
        oh = lane == idx
        onehots.append(oh)
        wsel.append(jnp.sum(jnp.where(oh, scores, 0.0), axis=-1, keepdims=True))
        masked = jnp.where(oh, ninf, masked)
        selm = jnp.where(oh, 1.0, selm)
        eidx_ref[:, k:k + 1] = idx
    wsum = wsel[0]
    for k in range(1, TOP_K):
        wsum = wsum + wsel[k]
    for k in range(TOP_K):
        wts_ref[:, k:k + 1] = wsel[k] / wsum * ROUTED_SCALE

    ri = lax.broadcasted_iota(jnp.int32, (tm, tm), 0)
    ci = lax.broadcasted_iota(jnp.int32, (tm, tm), 1)
    ltri = jnp.where(ri > ci, 1.0, 0.0).astype(BF16)
    cum = _dot(ltri, selm.astype(BF16)) + carry_ref[...]
    for k in range(TOP_K):
        rk = jnp.sum(jnp.where(onehots[k], cum, 0.0), axis=-1, keepdims=True)
        rank_ref[:, k:k + 1] = rk.astype(jnp.int32)
    total = carry_ref[...] + jnp.sum(selm, axis=0, keepdims=True)
    carry_ref[...] = total
    cnt_ref[...] = total


def _router(h2, w_hi, w_lo, bias):
    T, D = h2.shape
    tm = 256
    full = lambda a: pl.BlockSpec(a.shape, lambda i: (0, 0))
    o8 = pl.BlockSpec((tm, TOP_K), lambda i: (i, 0))
    return pl.pallas_call(
        _router_kernel,
        grid=(T // tm,),
        in_specs=[pl.BlockSpec((tm, D), lambda i: (i, 0)), full(w_hi), full(w_lo), full(bias)],
        out_specs=[o8, o8, o8, pl.BlockSpec((1, N_EXPERTS), lambda i: (0, 0))],
        out_shape=[jax.ShapeDtypeStruct((T, TOP_K), jnp.int32),
                   jax.ShapeDtypeStruct((T, TOP_K), F32),
                   jax.ShapeDtypeStruct((T, TOP_K), jnp.int32),
                   jax.ShapeDtypeStruct((1, N_EXPERTS), F32)],
        scratch_shapes=[pltpu.VMEM((1, N_EXPERTS), F32)],
        compiler_params=_cparams("arbitrary"),
        name="router",
    )(h2, w_hi, w_lo, bias)


def _dest_kernel(eidx_ref, rank_ref, ps_ref, dest_ref):
    tm = eidx_ref.shape[0]
    lane = lax.broadcasted_iota(jnp.int32, (tm, N_EXPERTS), 1)
    ps = ps_ref[...]
    for k in range(TOP_K):
        start = jnp.sum(jnp.where(lane == eidx_ref[:, k:k + 1], ps, 0.0), axis=-1, keepdims=True)
        dest_ref[:, k:k + 1] = start.astype(jnp.int32) + rank_ref[:, k:k + 1]


def _dest(eidx, rank, pad_start_f):
    T = eidx.shape[0]
    tm = 1024 if T % 1024 == 0 else T
    o8 = pl.BlockSpec((tm, TOP_K), lambda i: (i, 0))
    return pl.pallas_call(
        _dest_kernel,
        grid=(T // tm,),
        in_specs=[o8, o8, pl.BlockSpec((1, N_EXPERTS), lambda i: (0, 0))],
        out_specs=o8,
        out_shape=jax.ShapeDtypeStruct((T, TOP_K), jnp.int32),
        compiler_params=_cparams("parallel"),
        name="dest",
    )(eidx, rank, pad_start_f)


SC_WINDOW = 128


def _sc_gather(table, idx):
    _, lanes = table.shape
    n = idx.shape[0]
    mesh = plsc.VectorSubcoreMesh(core_axis_name="core", subcore_axis_name="subcore")

    @functools.partial(pl.kernel, out_type=jax.ShapeDtypeStruct((n, lanes), table.dtype), mesh=mesh,
                       name="sc_row_gather")
    def gather(x_hbm, i_hbm, o_hbm):
        def body(i_vmem, o_vmem):
            pltpu.sync_copy(x_hbm.at[i_vmem.at[0]], o_vmem)

        pltpu.emit_pipeline(
            body,
            grid=(n // SC_WINDOW,),
            in_specs=[pl.BlockSpec((1, SC_WINDOW), lambda i: (0, i))],
            out_specs=[pl.BlockSpec((SC_WINDOW, lanes), lambda i: (i, 0))],
            core_axis_name=("core", "subcore"),
            dimension_semantics=(pltpu.PARALLEL,),
            trace_scopes=False,
        )(i_hbm, o_hbm)

    return gather(table, idx.reshape(1, n))


def _expert_kernel(bexp_ref, nused_ref, x_ref, wg_ref, wu_ref, wd_ref, y_ref):
    del bexp_ref

    @pl.when(pl.program_id(0) < nused_ref[0])
    def _():
        x = _from_tiles(x_ref).astype(BF16)
        gate = _dot(x, wg_ref[0].astype(BF16))
        up = _dot(x, wu_ref[0].astype(BF16))
        act = (gate * jax.nn.sigmoid(gate) * up).astype(BF16)
        _to_tiles(y_ref, _dot(act, wd_ref[0].astype(BF16)))


def _experts(blk_exp, n_used, xs_t, wg, wu, wd):
    NP = xs_t.shape[0] * ROWS8
    D = D_MODEL
    nb = NP // MOE_BLOCK
    blk = (MOE_BLOCK // ROWS8, SUB, ROWS8, 128)

    def xmap(i, bexp, nused):
        return (jnp.minimum(i, nused[0] - 1), 0, 0, 0)

    def wmap(i, bexp, nused):
        return (bexp[jnp.minimum(i, nused[0] - 1)], 0, 0)

    grid_spec = pltpu.PrefetchScalarGridSpec(
        num_scalar_prefetch=2,
        grid=(nb,),
        in_specs=[pl.BlockSpec(blk, xmap),
                  pl.BlockSpec((1, D, D_EXPERT), wmap),
                  pl.BlockSpec((1, D, D_EXPERT), wmap),
                  pl.BlockSpec((1, D_EXPERT, D), wmap)],
        out_specs=pl.BlockSpec(blk, xmap),
    )
    return pl.pallas_call(
        _expert_kernel,
        grid_spec=grid_spec,
        out_shape=jax.ShapeDtypeStruct(xs_t.shape, F32),
        compiler_params=_cparams("arbitrary"),
        name="experts",
    )(blk_exp, n_used, xs_t, wg, wu, wd)


def _combine_kernel(h_ref, wts_ref, yg_ref, wsg_ref, wsu_ref, wsd_ref, g_ref, b_ref, o_ref):
    h = h_ref[...]
    hb = h.astype(BF16)
    gate = _dot(hb, wsg_ref[...])
    up = _dot(hb, wsu_ref[...])
    shared = _dot((gate * jax.nn.sigmoid(gate) * up).astype(BF16), wsd_ref[...])
    wts = wts_ref[...]
    routed = wts[:, 0:1] * _from_tiles(yg_ref, (0,))
    for k in range(1, TOP_K):
        routed = routed + wts[:, k:k + 1] * _from_tiles(yg_ref, (k,))
    o_ref[...] = _layer_norm(DN_ALPHA * h + (routed + shared), g_ref[...], b_ref[...])


def _combine(h2, wts, yg_t, wsg, wsu, wsd, g, b):
    T, D = h2.shape
    tm = 128
    full = lambda a: pl.BlockSpec(a.shape, lambda i: (0, 0))
    return pl.pallas_call(
        _combine_kernel,
        grid=(T // tm,),
        in_specs=[pl.BlockSpec((tm, D), lambda i: (i, 0)),
                  pl.BlockSpec((tm, TOP_K), lambda i: (i, 0)),
                  pl.BlockSpec((TOP_K, tm // ROWS8, SUB, ROWS8, 128), lambda i: (0, i, 0, 0, 0)),
                  full(wsg), full(wsu), full(wsd), full(g), full(b)],
        out_specs=pl.BlockSpec((tm, D), lambda i: (i, 0)),
        out_shape=jax.ShapeDtypeStruct((T, D), F32),
        compiler_params=_cparams("parallel"),
        name="combine_ln",
    )(h2, wts, yg_t, wsg, wsu, wsd, g, b)


def _regroup_w_in(w_in):
    parts, off = {}, 0
    for name, width in _SPLITS:
        parts[name] = w_in[:, off:off + width]
        off += width
    cols = []
    for name, (_, width) in _COLS.items():
        p = parts[name]
        if p.shape[1] < width:
            p = jnp.pad(p, ((0, 0), (0, width - p.shape[1])))
        cols.append(p)
    return jnp.concatenate(cols, axis=1).astype(BF16)


def _overlap_matrix(S):
    nr = S // CMP_STRIDE
    n_sel = S // SEL_LEN
    ci = np.arange(nr)[:, None] * CMP_STRIDE
    sj = np.arange(n_sel)[None, :] * SEL_LEN
    ov = np.clip(np.minimum(ci + CMP_LEN, sj + SEL_LEN) - np.maximum(ci, sj), 0, None) / CMP_LEN
    ov[nr - 1] = 0.0
    return jnp.asarray(ov.T, BF16)


def _mixers(h, positions, w_in, w_alpha2, b_alpha, gla_norm_g,
            cmp_pos_k, cmp_w1_k, cmp_b1_k, cmp_w2_k, cmp_pos_v, cmp_w1_v, cmp_b1_v, cmp_w2_v):
    B, S, D = h.shape
    proj = _proj(h.reshape(B * S, D), _regroup_w_in(w_in)).reshape(B, S, D_PROJ)

    wa = jnp.pad(w_alpha2, ((0, 128 - GLA_LOWRANK), (0, 0))).astype(BF16)
    o_gla = _gla(proj, wa, b_alpha.reshape(1, -1), gla_norm_g.reshape(1, -1))

    half = NSA_DH // 2
    inv = ROPE_THETA ** (-np.arange(half, dtype=np.float32) / half)
    inv_row = jnp.asarray(np.tile(inv, 128 // half).reshape(1, 128), F32)
    pos3 = positions.astype(F32).reshape(B, S, 1)
    q_r, kc_r, ks_r, vs_r, kw_r, vw_r = _nsa_prep(proj, pos3, inv_row)

    def blocks16(t):
        return (t.reshape(B, S // CMP_STRIDE, CMP_STRIDE, NSA_KV_GROUPS, NSA_DH)
                .transpose(0, 3, 1, 2, 4).reshape(B, NSA_KV_GROUPS, S // CMP_STRIDE, CMP_STRIDE * NSA_DH))

    vc_off = _COLS["nvc"][0]
    kcmp, vcmp = _compress(
        blocks16(kc_r), blocks16(proj[:, :, vc_off:vc_off + 128]),
        cmp_pos_k.reshape(2, -1), cmp_w1_k.astype(BF16), cmp_b1_k.reshape(1, -1), cmp_w2_k.astype(BF16),
        cmp_pos_v.reshape(2, -1), cmp_w1_v.astype(BF16), cmp_b1_v.reshape(1, -1), cmp_w2_v.astype(BF16))
    o_nsa = _nsa_attn(q_r, kcmp, vcmp, ks_r, vs_r, kw_r, vw_r, proj, _overlap_matrix(S))
    return o_gla, o_nsa


def _moe_ln(h2, h_t, w_router, router_bias, w_gate, w_up, w_down, ws_gate, ws_up, ws_down, ln_g, ln_b):
    T, D = h2.shape
    P = T * TOP_K
    w_hi = w_router.astype(BF16)
    w_lo = (w_router - w_hi.astype(F32)).astype(BF16)
    eidx, wts, rank, counts = _router(h2, w_hi, w_lo, router_bias.reshape(1, -1))

    counts = counts.reshape(-1).astype(jnp.int32)
    padded = (counts + MOE_BLOCK - 1) // MOE_BLOCK * MOE_BLOCK
    pad_end = jnp.cumsum(padded)
    pad_start = pad_end - padded
    nb = -(-P // MOE_BLOCK) + N_EXPERTS
    n_used = (pad_end[-1] // MOE_BLOCK).astype(jnp.int32).reshape(1)
    blk_start = jnp.arange(nb, dtype=jnp.int32) * MOE_BLOCK
    blk_exp = jnp.minimum(jnp.sum((pad_end[None, :] <= blk_start[:, None]).astype(jnp.int32), axis=1),
                          N_EXPERTS - 1)

    NP = nb * MOE_BLOCK
    dest = _dest(eidx, rank, pad_start.astype(F32).reshape(1, -1))
    tok_of_row = (jnp.arange(NP, dtype=jnp.int32) % T).at[dest.reshape(P)].set(
        jnp.arange(P, dtype=jnp.int32) // TOP_K)
    col = (jnp.arange(SUB, dtype=jnp.int32) * ROWS8)

    src = _tile_rows(tok_of_row).reshape(NP // ROWS8, 1, ROWS8) + col[None, :, None]
    xs_t = _sc_gather(h_t.reshape(T * SUB, 128), src.reshape(-1)).reshape(NP // ROWS8, SUB, ROWS8, 128)
    ys_t = _experts(blk_exp, n_used, xs_t, w_gate, w_up, w_down)
    src = _tile_rows(dest.T).reshape(TOP_K, T // ROWS8, 1, ROWS8) + col[None, None, :, None]
    yg_t = _sc_gather(ys_t.reshape(NP * SUB, 128), src.reshape(-1)).reshape(TOP_K, T // ROWS8, SUB, ROWS8, 128)
    return _combine(h2, wts, yg_t, ws_gate.astype(BF16), ws_up.astype(BF16), ws_down.astype(BF16),
                    ln_g.reshape(1, -1), ln_b.reshape(1, -1))


def kernel(x, positions, w_in, w_alpha2, b_alpha, gla_norm_g, cmp_pos_k, cmp_w1_k, cmp_b1_k, cmp_w2_k, cmp_pos_v, cmp_w1_v, cmp_b1_v, cmp_w2_v, w_out, ln1_g, ln1_b, w_router, router_bias, w_exp_gate, w_exp_up, w_exp_down, w_sh_gate, w_sh_up, w_sh_down, ln2_g, ln2_b):
    B, S, D = x.shape
    h = x
    for l in range(w_in.shape[0]):
        o_gla, o_nsa = _mixers(h, positions, w_in[l], w_alpha2[l], b_alpha[l], gla_norm_g[l],
                               cmp_pos_k[l], cmp_w1_k[l], cmp_b1_k[l], cmp_w2_k[l],
                               cmp_pos_v[l], cmp_w1_v[l], cmp_b1_v[l], cmp_w2_v[l])
        h1, h1_t = _outproj(h.reshape(B * S, D), o_gla.reshape(B * S, -1), o_nsa.reshape(B * S, -1),
                            w_out[l].astype(BF16), ln1_g[l].reshape(1, -1), ln1_b[l].reshape(1, -1))
        h2 = _moe_ln(h1, h1_t, w_router[l], router_bias[l], w_exp_gate[l], w_exp_up[l], w_exp_down[l],
                     w_sh_gate[l], w_sh_up[l], w_sh_down[l], ln2_g[l], ln2_b[l])
        h = h2.reshape(B, S, D)
    return h
```

```python
import functools

import numpy as np
import jax
import jax.numpy as jnp
from jax import lax
from jax.experimental import pallas as pl
from jax.experimental.pallas import tpu as pltpu
from jax.experimental.pallas import tpu_sc as plsc

D_MODEL = 1024
GLA_HEADS = 4
GLA_DV = 128
GLA_DK = 64
GLA_LOWRANK = 16
GLA_TAU = 16.0
GLA_CHUNK = 64
NSA_HEADS = 8
NSA_KV_GROUPS = 2
NSA_HPG = 4
NSA_DH = 64
CMP_LEN = 32
CMP_STRIDE = 16
CMP_HIDDEN = 256
SEL_LEN = 64
SEL_TOPK = 16
WINDOW = 512
ROPE_THETA = 10000.0
N_EXPERTS = 256
TOP_K = 8
N_GROUPS = 8
TOPK_GROUPS = 4
D_EXPERT = 256
ROUTED_SCALE = 2.5
DEPTH = 1
DN_ALPHA = (2.0 * DEPTH) ** 0.25
LN_EPS = 1e-5

MOE_BLOCK = 256
BATCH_GROUPS = 2
NEG = -1e30
F32 = jnp.float32
BF16 = jnp.bfloat16

_COLS = {}
_off = 0
for _name, _w in (("gq", 256), ("gk", 256), ("gv", 512), ("gr", 512), ("nq", 512),
                  ("nkc", 128), ("nvc", 128), ("nks", 128), ("nvs", 128), ("nkw", 128), ("nvw", 128),
                  ("ga", 128), ("ng", 128)):
    _COLS[_name] = (_off, _w)
    _off += _w
D_PROJ = _off
_SPLITS = (("gq", 256), ("gk", 256), ("gv", 512), ("ga", 16), ("gr", 512), ("nq", 512),
           ("nkc", 128), ("nvc", 128), ("nks", 128), ("nvs", 128), ("nkw", 128), ("nvw", 128), ("ng", 24))

VMEM_LIMIT = 56 * 1024 * 1024


def _cparams(*sem):
    return pltpu.CompilerParams(dimension_semantics=sem, vmem_limit_bytes=VMEM_LIMIT)


def _dot(a, b):
    return jnp.dot(a, b, preferred_element_type=F32)


def _dot_nt(a, b):
    return lax.dot_general(a, b, (((1,), (1,)), ((), ())), preferred_element_type=F32)


def _split3(x):
    hi = x.astype(BF16)
    r1 = x - hi.astype(F32)
    mid = r1.astype(BF16)
    lo = (r1 - mid.astype(F32)).astype(BF16)
    return hi, mid, lo


def _proj_kernel(x_ref, w_ref, o_ref):
    o_ref[...] = _dot(x_ref[...].astype(BF16), w_ref[...])


def _proj(x2, w):
    T, D = x2.shape
    N = w.shape[1]
    tm, tn = 512, 1024
    return pl.pallas_call(
        _proj_kernel,
        grid=(T // tm, N // tn),
        in_specs=[pl.BlockSpec((tm, D), lambda i, j: (i, 0)),
                  pl.BlockSpec((D, tn), lambda i, j: (0, j))],
        out_specs=pl.BlockSpec((tm, tn), lambda i, j: (i, j)),
        out_shape=jax.ShapeDtypeStruct((T, N), F32),
        compiler_params=_cparams("parallel", "arbitrary"),
        name="proj",
    )(x2, w)


def _gla_kernel(q_ref, k_ref, v_ref, r_ref, a_ref, wa_ref, ba_ref, ng_ref, o_ref, st_ref, *, nchunk):
    C = GLA_CHUNK
    HK = GLA_HEADS * GLA_DK

    @pl.when(pl.program_id(1) == 0)
    def _():
        st_ref[...] = jnp.zeros_like(st_ref)

    ri = lax.broadcasted_iota(jnp.int32, (C, C), 0)
    ci = lax.broadcasted_iota(jnp.int32, (C, C), 1)
    causal = ri >= ci
    tri = jnp.where(causal, 1.0, 0.0).astype(BF16)
    lane_head = lax.broadcasted_iota(jnp.int32, (1, HK), 1) // GLA_DK
    wa = wa_ref[...]
    ba = ba_ref[...]
    ng = ng_ref[...]

    def chunk(c, carry):
        rows = pl.ds(pl.multiple_of(c * C, C), C)
        q = q_ref[0, rows, :]
        k = k_ref[0, rows, :]
        v = v_ref[0, rows, :]
        r = r_ref[0, rows, :]
        a = a_ref[0, rows, :]
        z = _dot(a.astype(BF16), wa) + ba
        g = (jnp.minimum(z, 0.0) - jnp.log1p(jnp.exp(-jnp.abs(z)))) * (1.0 / GLA_TAU)
        g_hi, g_mid, g_lo = _split3(g)
        b = _dot(tri, g_hi) + _dot(tri, g_mid) + _dot(tri, g_lo)
        b_last = b[C - 1:C, :]
        qt = q * jnp.exp(b) * (GLA_DK ** -0.5)
        kt = (k * jnp.exp(-b)).astype(BF16)
        ks = (k * jnp.exp(b_last - b)).astype(BF16)
        st = st_ref[...]
        st_b = st.astype(BF16)
        new_st = st * jnp.exp(b_last)
        for h in range(GLA_HEADS):
            hm = lane_head == h
            qh = jnp.where(hm, qt, 0.0).astype(BF16)
            att = jnp.where(causal, _dot_nt(qh, kt), 0.0)
            vh = v[:, h * GLA_DV:(h + 1) * GLA_DV].astype(BF16)
            o = _dot_nt(qh, st_b) + _dot(att.astype(BF16), vh)
            new_st = new_st + jnp.where(hm, _dot(vh.T, ks), 0.0)
            o = o * lax.rsqrt(jnp.mean(o * o, axis=-1, keepdims=True) + 1e-6) * ng
            rh = r[:, h * GLA_DV:(h + 1) * GLA_DV]
            o = o * (rh * jax.nn.sigmoid(rh))
            o_ref[0, rows, h * GLA_DV:(h + 1) * GLA_DV] = o.astype(o_ref.dtype)
        st_ref[...] = new_st
        return carry

    lax.fori_loop(0, nchunk, chunk, 0)


def _gla(proj3, wa, ba, ng):
    B, S, _ = proj3.shape
    L = 512 if S % 512 == 0 else S
    nchunk = L // GLA_CHUNK

    def col(name, width):
        off = _COLS[name][0]
        assert off % width == 0
        return pl.BlockSpec((1, L, width), lambda b, j, o=off // width: (b, j, o))

    full2 = lambda shape: pl.BlockSpec(shape, lambda b, j: (0, 0))
    return pl.pallas_call(
        functools.partial(_gla_kernel, nchunk=nchunk),
        grid=(B, S // L),
        in_specs=[col("gq", 256), col("gk", 256), col("gv", 512), col("gr", 512), col("ga", 128),
                  full2((128, 256)), full2((1, 256)), full2((1, 128))],
        out_specs=pl.BlockSpec((1, L, 512), lambda b, j: (b, j, 0)),
        out_shape=jax.ShapeDtypeStruct((B, S, 512), BF16),
        scratch_shapes=[pltpu.VMEM((GLA_DV, GLA_HEADS * GLA_DK), F32)],
        compiler_params=_cparams("parallel", "arbitrary"),
        name="gla",
    )(proj3, proj3, proj3, proj3, proj3, wa, ba, ng)


def _rot_half(x):
    n = x.shape[-1]
    lane = lax.broadcasted_iota(jnp.int32, (1, n), 1)
    first = (lane % NSA_DH) < (NSA_DH // 2)
    return jnp.where(first, -pltpu.roll(x, n - NSA_DH // 2, 1), pltpu.roll(x, NSA_DH // 2, 1))


def _nsa_prep_kernel(pos_ref, inv_ref, q_ref, kc_ref, ks_ref, vs_ref, kw_ref, vw_ref,
                     qo_ref, kco_ref, kso_ref, vso_ref, kwo_ref, vwo_ref):
    ang = pos_ref[0] * inv_ref[...]
    cos = jnp.cos(ang)
    sin = jnp.sin(ang)
    cos4 = jnp.concatenate([cos] * 4, axis=1)
    sin4 = jnp.concatenate([sin] * 4, axis=1)

    q = q_ref[0]
    qr = q * cos4 + _rot_half(q) * sin4
    for h in range(NSA_HEADS):
        qo_ref[0, h] = (qr[:, h * NSA_DH:(h + 1) * NSA_DH] * (NSA_DH ** -0.5)).astype(qo_ref.dtype)

    def rope128(x):
        return x * cos + _rot_half(x) * sin

    kco_ref[0] = rope128(kc_ref[0])
    ksr = rope128(ks_ref[0])
    kwr = rope128(kw_ref[0])
    vs = vs_ref[0]
    vw = vw_ref[0]
    for g in range(NSA_KV_GROUPS):
        sl = slice(g * NSA_DH, (g + 1) * NSA_DH)
        kso_ref[0, g] = ksr[:, sl].astype(kso_ref.dtype)
        kwo_ref[0, g] = kwr[:, sl].astype(kwo_ref.dtype)
        vso_ref[0, g] = vs[:, sl].astype(vso_ref.dtype)
        vwo_ref[0, g] = vw[:, sl].astype(vwo_ref.dtype)


def _nsa_prep(proj3, pos3, inv_row):
    B, S, _ = proj3.shape
    ts = 512 if S % 512 == 0 else S

    def col(name, width):
        off = _COLS[name][0]
        assert off % width == 0
        return pl.BlockSpec((1, ts, width), lambda b, j, o=off // width: (b, j, o))

    kv_spec = pl.BlockSpec((1, NSA_KV_GROUPS, ts, NSA_DH), lambda b, j: (b, 0, j, 0))
    kv_shape = jax.ShapeDtypeStruct((B, NSA_KV_GROUPS, S, NSA_DH), BF16)
    return pl.pallas_call(
        _nsa_prep_kernel,
        grid=(B, S // ts),
        in_specs=[pl.BlockSpec((1, ts, 1), lambda b, j: (b, j, 0)),
                  pl.BlockSpec((1, 128), lambda b, j: (0, 0)),
                  col("nq", 512), col("nkc", 128), col("nks", 128), col("nvs", 128),
                  col("nkw", 128), col("nvw", 128)],
        out_specs=[pl.BlockSpec((1, NSA_HEADS, ts, NSA_DH), lambda b, j: (b, 0, j, 0)),
                   pl.BlockSpec((1, ts, 128), lambda b, j: (b, j, 0)),
                   kv_spec, kv_spec, kv_spec, kv_spec],
        out_shape=[jax.ShapeDtypeStruct((B, NSA_HEADS, S, NSA_DH), BF16),
                   jax.ShapeDtypeStruct((B, S, 128), F32),
                   kv_shape, kv_shape, kv_shape, kv_shape],
        compiler_params=_cparams("parallel", "parallel"),
        name="nsa_prep",
    )(pos3, inv_row, proj3, proj3, proj3, proj3, proj3, proj3)


def _compress_kernel(k_ref, v_ref, pk_ref, w1k_ref, b1k_ref, w2k_ref, pv_ref, w1v_ref, b1v_ref, w2v_ref,
                     ko_ref, vo_ref):
    half = CMP_STRIDE * NSA_DH

    def run(x_ref, p_ref, w1_ref, b1_ref, w2_ref, o_ref):
        p = p_ref[...]
        for g in range(NSA_KV_GROUPS):
            x = x_ref[0, g]
            nr = x.shape[0]
            a = _dot((x + p[0:1]).astype(BF16), w1_ref[0:half, :])
            bb = _dot((x + p[1:2]).astype(BF16), w1_ref[half:2 * half, :])
            pre = a + pltpu.roll(bb, nr - 1, 0) + b1_ref[...]
            hid = jax.nn.gelu(pre)
            out = _dot(hid.astype(BF16), w2_ref[...])
            row = lax.broadcasted_iota(jnp.int32, out.shape, 0)
            o_ref[0, g] = jnp.where(row < nr - 1, out, 0.0)

    run(k_ref, pk_ref, w1k_ref, b1k_ref, w2k_ref, ko_ref)
    run(v_ref, pv_ref, w1v_ref, b1v_ref, w2v_ref, vo_ref)


def _compress(kc4, vc4, pk, w1k, b1k, w2k, pv, w1v, b1v, w2v):
    B, G, NR, W = kc4.shape
    x_spec = pl.BlockSpec((1, G, NR, W), lambda b: (b, 0, 0, 0))
    full = lambda a: pl.BlockSpec(a.shape, lambda b: (0,) * a.ndim)
    o_spec = pl.BlockSpec((1, G, NR, NSA_DH), lambda b: (b, 0, 0, 0))
    o_shape = jax.ShapeDtypeStruct((B, G, NR, NSA_DH), F32)
    params = (pk, w1k, b1k, w2k, pv, w1v, b1v, w2v)
    return pl.pallas_call(
        _compress_kernel,
        grid=(B,),
        in_specs=[x_spec, x_spec] + [full(a) for a in params],
        out_specs=[o_spec, o_spec],
        out_shape=[o_shape, o_shape],
        compiler_params=_cparams("parallel"),
        name="compress",
    )(kc4, vc4, *params)


def _nsa_attn_kernel(q_ref, kc_ref, vc_ref, ks_ref, vs_ref, kw_ref, vw_ref, gate_ref, ovt_ref, o_ref,
                     *, tq, S, n_sel, n_top):
    g = pl.program_id(1)
    qi = pl.program_id(2)
    start = qi * tq
    H = NSA_HPG
    qs = q_ref[0].reshape(H * tq, NSA_DH)
    t_col = start + lax.broadcasted_iota(jnp.int32, (tq, 1), 0)
    t_row = start + lax.broadcasted_iota(jnp.int32, (1, tq), 1)

    kc = kc_ref[0, 0].astype(BF16)
    vc = vc_ref[0, 0].astype(BF16)
    NR = kc.shape[0]
    s = _dot_nt(qs, kc)
    ncol = lax.broadcasted_iota(jnp.int32, (1, NR), 1)
    cmask = (ncol < NR - 1) & (ncol * CMP_STRIDE + (CMP_LEN - 1) <= t_col)
    s3 = jnp.where(cmask[None], s.reshape(H, tq, NR), NEG)
    m = jnp.max(s3, axis=-1, keepdims=True)
    e = jnp.exp(s3 - m)
    p3 = e / jnp.sum(e, axis=-1, keepdims=True)
    p3 = jnp.where(cmask[None], p3, 0.0)
    o_cmp = _dot(p3.reshape(H * tq, NR).astype(BF16), vc)

    psum = jnp.sum(p3, axis=0)
    p_hi = psum.astype(BF16)
    p_lo = (psum - p_hi.astype(F32)).astype(BF16)
    ovt = ovt_ref[...]
    imp = _dot_nt(ovt, p_hi) + _dot_nt(ovt, p_lo)
    blk = lax.broadcasted_iota(jnp.int32, (n_sel, 1), 0)
    cur = t_row // SEL_LEN
    causal_blk = blk <= cur
    forced = (blk == 0) | (blk == cur) | (blk == cur - 1)
    val = jnp.where(causal_blk, jnp.where(forced, jnp.inf, imp), -jnp.inf)
    rank = jnp.zeros((n_sel, tq), jnp.int32)
    for j in range(n_sel):
        vj = val[j:j + 1, :]
        beats = (vj > val) | ((vj == val) & (blk > j))
        rank = rank + beats.astype(jnp.int32)
    sel_t = jnp.where((rank < n_top) & causal_blk, 1.0, 0.0)
    sel_t = jnp.concatenate([sel_t, jnp.zeros((128 - n_sel, tq), F32)], axis=0)
    sel = sel_t.T.astype(BF16)

    tk = 256
    nsub = 2 if S % (2 * tk) == 0 else 1
    bpc = tk // SEL_LEN
    n_trips = (start + tq + nsub * tk - 1) // (nsub * tk)
    erow = lax.broadcasted_iota(jnp.int32, (128, tk), 0)
    ecol = lax.broadcasted_iota(jnp.int32, (128, tk), 1) // SEL_LEN
    kk = lax.broadcasted_iota(jnp.int32, (1, tk), 1)

    def sel_trip(c, carry):
        m_i, l_i, acc = carry
        scores, vals = [], []
        m_new = m_i
        for u in range(nsub):
            cu = c * nsub + u
            k0 = pl.multiple_of(cu * tk, tk)
            kb = ks_ref[0, 0, pl.ds(k0, tk), :]
            vals.append(vs_ref[0, 0, pl.ds(k0, tk), :])
            expand = jnp.where(erow == ecol + cu * bpc, 1.0, 0.0).astype(BF16)
            allowed = (_dot(sel, expand) > 0.5) & (k0 + kk <= t_col)
            bias = jnp.where(allowed, 0.0, NEG)
            sc3 = _dot_nt(qs, kb).reshape(H, tq, tk) + bias[None]
            scores.append(sc3)
            m_new = jnp.maximum(m_new, jnp.max(sc3, axis=-1, keepdims=True))
        alpha = jnp.exp(m_i - m_new)
        l_new = alpha * l_i
        acc = alpha.reshape(H * tq, 1) * acc
        for u in range(nsub):
            pe = jnp.exp(scores[u] - m_new)
            l_new = l_new + jnp.sum(pe, axis=-1, keepdims=True)
            acc = acc + _dot(pe.reshape(H * tq, tk).astype(BF16), vals[u])
        return m_new, l_new, acc

    m0 = jnp.full((H, tq, 1), NEG, F32)
    l0 = jnp.zeros((H, tq, 1), F32)
    a0 = jnp.zeros((H * tq, NSA_DH), F32)
    _, l_f, acc_f = lax.fori_loop(0, n_trips, sel_trip, (m0, l0, a0))
    o_sel = acc_f / l_f.reshape(H * tq, 1)

    span = min(WINDOW + tq, S)
    ws = jnp.clip(start - WINDOW, 0, S - span)
    ws = pl.multiple_of(ws, tq)
    kwb = kw_ref[0, 0, pl.ds(ws, span), :]
    vwb = vw_ref[0, 0, pl.ds(ws, span), :]
    kp = ws + lax.broadcasted_iota(jnp.int32, (1, span), 1)
    wbias = jnp.where((kp <= t_col) & (kp > t_col - WINDOW), 0.0, NEG)
    sw3 = _dot_nt(qs, kwb).reshape(H, tq, span) + wbias[None]
    mw = jnp.max(sw3, axis=-1, keepdims=True)
    ew = jnp.exp(sw3 - mw)
    pw = ew / jnp.sum(ew, axis=-1, keepdims=True)
    o_win = _dot(pw.reshape(H * tq, span).astype(BF16), vwb)

    gs = jax.nn.sigmoid(gate_ref[0])
    for g_static in range(NSA_KV_GROUPS):
        @pl.when(g == g_static)
        def _(g_static=g_static):
            for h in range(H):
                c0 = g_static * H * 3 + h * 3
                rs = slice(h * tq, (h + 1) * tq)
                o = (gs[:, c0:c0 + 1] * o_cmp[rs] + gs[:, c0 + 1:c0 + 2] * o_sel[rs]
                     + gs[:, c0 + 2:c0 + 3] * o_win[rs])
                o_ref[0, :, h * NSA_DH:(h + 1) * NSA_DH] = o.astype(o_ref.dtype)


def _nsa_attn(q_r, kcmp, vcmp, ks_r, vs_r, kw_r, vw_r, proj3, ov):
    B, _, S, _ = q_r.shape
    G, H = NSA_KV_GROUPS, NSA_HPG
    NR = kcmp.shape[2]
    tq = 256
    n_sel = S // SEL_LEN
    n_top = min(SEL_TOPK, n_sel)
    cmp_spec = pl.BlockSpec((1, 1, NR, NSA_DH), lambda b, g, i: (b, g, 0, 0))
    kv_spec = pl.BlockSpec((1, 1, S, NSA_DH), lambda b, g, i: (b, g, 0, 0))
    goff = _COLS["ng"][0] // 128
    return pl.pallas_call(
        functools.partial(_nsa_attn_kernel, tq=tq, S=S, n_sel=n_sel, n_top=n_top),
        grid=(B, G, S // tq),
        in_specs=[pl.BlockSpec((1, H, tq, NSA_DH), lambda b, g, i: (b, g, i, 0)),
                  cmp_spec, cmp_spec, kv_spec, kv_spec, kv_spec, kv_spec,
                  pl.BlockSpec((1, tq, 128), lambda b, g, i: (b, i, goff)),
                  pl.BlockSpec(ov.shape, lambda b, g, i: (0, 0))],
        out_specs=pl.BlockSpec((1, tq, H * NSA_DH), lambda b, g, i: (b, i, g)),
        out_shape=jax.ShapeDtypeStruct((B, S, NSA_HEADS * NSA_DH), BF16),
        compiler_params=_cparams("parallel", "parallel", "arbitrary"),
        name="nsa_attn",
    )(q_r, kcmp, vcmp, ks_r, vs_r, kw_r, vw_r, proj3, ov)


def _layer_norm(x, g, b):
    mu = jnp.mean(x, axis=-1, keepdims=True)
    xc = x - mu
    var = jnp.mean(xc * xc, axis=-1, keepdims=True)
    return xc * lax.rsqrt(var + LN_EPS) * g + b


SUB = D_MODEL // 128
ROWS8 = 8


def _tile_rows(t):
    return (t // ROWS8) * (SUB * ROWS8) + t % ROWS8


def _to_tiles(ref, val):
    rows = val.shape[0]
    for c in range(SUB):
        ref[:, c] = val[:, c * 128:(c + 1) * 128].reshape(rows // ROWS8, ROWS8, 128)


def _from_tiles(ref, lead=()):
    groups = ref.shape[len(lead)]
    return jnp.concatenate([ref[lead + (slice(None), c)].reshape(groups * ROWS8, 128) for c in range(SUB)], axis=1)


def _outproj_kernel(x_ref, og_ref, on_ref, w_ref, g_ref, b_ref, o_ref, ot_ref):
    half = og_ref.shape[1]
    mix = _dot(og_ref[...], w_ref[0:half, :]) + _dot(on_ref[...], w_ref[half:, :])
    h = _layer_norm(DN_ALPHA * x_ref[...] + mix, g_ref[...], b_ref[...])
    o_ref[...] = h
    _to_tiles(ot_ref, h)


def _outproj(x2, og2, on2, w, g, b):
    T, D = x2.shape
    tm = 512
    row = lambda width: pl.BlockSpec((tm, width), lambda i: (i, 0))
    full = lambda a: pl.BlockSpec(a.shape, lambda i: (0, 0))
    return pl.pallas_call(
        _outproj_kernel,
        grid=(T // tm,),
        in_specs=[row(D), row(og2.shape[1]), row(on2.shape[1]), full(w), full(g), full(b)],
        out_specs=[row(D), pl.BlockSpec((tm // ROWS8, SUB, ROWS8, 128), lambda i: (i, 0, 0, 0))],
        out_shape=[jax.ShapeDtypeStruct((T, D), F32),
                   jax.ShapeDtypeStruct((T // ROWS8, SUB, ROWS8, 128), F32)],
        compiler_params=_cparams("parallel"),
        name="outproj_ln",
    )(x2, og2, on2, w, g, b)


def _router_kernel(h_ref, wh_ref, wl_ref, bias_ref, eidx_ref, wts_ref, rank_ref, cnt_ref, carry_ref):
    @pl.when(pl.program_id(0) == 0)
    def _():
        carry_ref[...] = jnp.zeros_like(carry_ref)

    h = h_ref[...]
    tm = h.shape[0]
    E = N_EXPERTS
    h_hi = h.astype(BF16)
    h_lo = (h - h_hi.astype(F32)).astype(BF16)
    wh = wh_ref[...]
    logits = _dot(h_hi, wh) + _dot(h_lo, wh) + _dot(h_hi, wl_ref[...])
    scores = jax.nn.sigmoid(logits)
    biased = scores + bias_ref[...]
    lane = lax.broadcasted_iota(jnp.int32, (tm, E), 1)
    gid = lane // (E // N_GROUPS)
    ninf = -jnp.inf

    def row_max(x):
        return jnp.max(x, axis=-1, keepdims=True)

    def first_idx(x, mx):
        return jnp.min(jnp.where(x == mx, lane, E), axis=-1, keepdims=True)

    gscore = []
    for gi in range(N_GROUPS):
        mg = jnp.where(gid == gi, biased, ninf)
        m1 = row_max(mg)
        i1 = first_idx(mg, m1)
        m2 = row_max(jnp.where(lane == i1, ninf, mg))
        gscore.append(m1 + m2)
    emask = jnp.zeros((tm, E), jnp.bool_)
    for gi in range(N_GROUPS):
        rk = jnp.zeros((tm, 1), jnp.int32)
        for gj in range(N_GROUPS):
            if gj == gi:
                continue
            beats = (gscore[gj] > gscore[gi]) | ((gscore[gj] == gscore[gi]) & (gj < gi))
            rk = rk + beats.astype(jnp.int32)
        emask = emask | ((gid == gi) & (rk < TOPK_GROUPS))
    masked = jnp.where(emask, biased, ninf)

    onehots, wsel = [], []
    selm = jnp.zeros((tm, E), F32)
    for k in range(TOP_K):
        mx = row_max(masked)
        idx = first_idx(masked, mx)
        oh = lane == idx
        onehots.append(oh)
        wsel.append(jnp.sum(jnp.where(oh, scores, 0.0), axis=-1, keepdims=True))
        masked = jnp.where(oh, ninf, masked)
        selm = jnp.where(oh, 1.0, selm)
        eidx_ref[:, k:k + 1] = idx
    wsum = wsel[0]
    for k in range(1, TOP_K):
        wsum = wsum + wsel[k]
    for k in range(TOP_K):
        wts_ref[:, k:k + 1] = wsel[k] / wsum * ROUTED_SCALE

    ri = lax.broadcasted_iota(jnp.int32, (tm, tm), 0)
    ci = lax.broadcasted_iota(jnp.int32, (tm, tm), 1)
    ltri = jnp.where(ri > ci, 1.0, 0.0).astype(BF16)
    cum = _dot(ltri, selm.astype(BF16)) + carry_ref[...]
    for k in range(TOP_K):
        rk = jnp.sum(jnp.where(onehots[k], cum, 0.0), axis=-1, keepdims=True)
        rank_ref[:, k:k + 1] = rk.astype(jnp.int32)
    total = carry_ref[...] + jnp.sum(selm, axis=0, keepdims=True)
    carry_ref[...] = total
    cnt_ref[...] = total


def _router(h2, w_hi, w_lo, bias):
    T, D = h2.shape
    tm = 256
    full = lambda a: pl.BlockSpec(a.shape, lambda i: (0, 0))
    o8 = pl.BlockSpec((tm, TOP_K), lambda i: (i, 0))
    return pl.pallas_call(
        _router_kernel,
        grid=(T // tm,),
        in_specs=[pl.BlockSpec((tm, D), lambda i: (i, 0)), full(w_hi), full(w_lo), full(bias)],
        out_specs=[o8, o8, o8, pl.BlockSpec((1, N_EXPERTS), lambda i: (0, 0))],
        out_shape=[jax.ShapeDtypeStruct((T, TOP_K), jnp.int32),
                   jax.ShapeDtypeStruct((T, TOP_K), F32),
                   jax.ShapeDtypeStruct((T, TOP_K), jnp.int32),
                   jax.ShapeDtypeStruct((1, N_EXPERTS), F32)],
        scratch_shapes=[pltpu.VMEM((1, N_EXPERTS), F32)],
        compiler_params=_cparams("arbitrary"),
        name="router",
    )(h2, w_hi, w_lo, bias)


def _dest_kernel(eidx_ref, rank_ref, ps_ref, dest_ref):
    tm = eidx_ref.shape[0]
    lane = lax.broadcasted_iota(jnp.int32, (tm, N_EXPERTS), 1)
    ps = ps_ref[...]
    for k in range(TOP_K):
        start = jnp.sum(jnp.where(lane == eidx_ref[:, k:k + 1], ps, 0.0), axis=-1, keepdims=True)
        dest_ref[:, k:k + 1] = start.astype(jnp.int32) + rank_ref[:, k:k + 1]


def _dest(eidx, rank, pad_start_f):
    T = eidx.shape[0]
    tm = 1024 if T % 1024 == 0 else T
    o8 = pl.BlockSpec((tm, TOP_K), lambda i: (i, 0))
    return pl.pallas_call(
        _dest_kernel,
        grid=(T // tm,),
        in_specs=[o8, o8, pl.BlockSpec((1, N_EXPERTS), lambda i: (0, 0))],
        out_specs=o8,
        out_shape=jax.ShapeDtypeStruct((T, TOP_K), jnp.int32),
        compiler_params=_cparams("parallel"),
        name="dest",
    )(eidx, rank, pad_start_f)


SC_WINDOW = 128


def _sc_gather(table, idx):
    _, lanes = table.shape
    n = idx.shape[0]
    mesh = plsc.VectorSubcoreMesh(core_axis_name="core", subcore_axis_name="subcore")

    @functools.partial(pl.kernel, out_type=jax.ShapeDtypeStruct((n, lanes), table.dtype), mesh=mesh,
                       name="sc_row_gather")
    def gather(x_hbm, i_hbm, o_hbm):
        def body(i_vmem, o_vmem):
            pltpu.sync_copy(x_hbm.at[i_vmem.at[0]], o_vmem)

        pltpu.emit_pipeline(
            body,
            grid=(n // SC_WINDOW,),
            in_specs=[pl.BlockSpec((1, SC_WINDOW), lambda i: (0, i))],
            out_specs=[pl.BlockSpec((SC_WINDOW, lanes), lambda i: (i, 0))],
            core_axis_name=("core", "subcore"),
            dimension_semantics=(pltpu.PARALLEL,),
            trace_scopes=False,
        )(i_hbm, o_hbm)

    return gather(table, idx.reshape(1, n))


def _expert_kernel(bexp_ref, nused_ref, x_ref, wg_ref, wu_ref, wd_ref, y_ref):
    del bexp_ref

    @pl.when(pl.program_id(0) < nused_ref[0])
    def _():
        x = _from_tiles(x_ref).astype(BF16)
        gate = _dot(x, wg_ref[0].astype(BF16))
        up = _dot(x, wu_ref[0].astype(BF16))
        act = (gate * jax.nn.sigmoid(gate) * up).astype(BF16)
        _to_tiles(y_ref, _dot(act, wd_ref[0].astype(BF16)))


def _experts(blk_exp, n_used, xs_t, wg, wu, wd):
    NP = xs_t.shape[0] * ROWS8
    D = D_MODEL
    nb = NP // MOE_BLOCK
    blk = (MOE_BLOCK // ROWS8, SUB, ROWS8, 128)

    def xmap(i, bexp, nused):
        return (jnp.minimum(i, nused[0] - 1), 0, 0, 0)

    def wmap(i, bexp, nused):
        return (bexp[jnp.minimum(i, nused[0] - 1)], 0, 0)

    grid_spec = pltpu.PrefetchScalarGridSpec(
        num_scalar_prefetch=2,
        grid=(nb,),
        in_specs=[pl.BlockSpec(blk, xmap),
                  pl.BlockSpec((1, D, D_EXPERT), wmap),
                  pl.BlockSpec((1, D, D_EXPERT), wmap),
                  pl.BlockSpec((1, D_EXPERT, D), wmap)],
        out_specs=pl.BlockSpec(blk, xmap),
    )
    return pl.pallas_call(
        _expert_kernel,
        grid_spec=grid_spec,
        out_shape=jax.ShapeDtypeStruct(xs_t.shape, F32),
        compiler_params=_cparams("arbitrary"),
        name="experts",
    )(blk_exp, n_used, xs_t, wg, wu, wd)


def _combine_kernel(h_ref, wts_ref, yg_ref, wsg_ref, wsu_ref, wsd_ref, g_ref, b_ref, o_ref):
    h = h_ref[...]
    hb = h.astype(BF16)
    gate = _dot(hb, wsg_ref[...])
    up = _dot(hb, wsu_ref[...])
    shared = _dot((gate * jax.nn.sigmoid(gate) * up).astype(BF16), wsd_ref[...])
    wts = wts_ref[...]
    routed = wts[:, 0:1] * _from_tiles(yg_ref, (0,))
    for k in range(1, TOP_K):
        routed = routed + wts[:, k:k + 1] * _from_tiles(yg_ref, (k,))
    o_ref[...] = _layer_norm(DN_ALPHA * h + (routed + shared), g_ref[...], b_ref[...])


def _combine(h2, wts, yg_t, wsg, wsu, wsd, g, b):
    T, D = h2.shape
    tm = 128
    full = lambda a: pl.BlockSpec(a.shape, lambda i: (0, 0))
    return pl.pallas_call(
        _combine_kernel,
        grid=(T // tm,),
        in_specs=[pl.BlockSpec((tm, D), lambda i: (i, 0)),
                  pl.BlockSpec((tm, TOP_K), lambda i: (i, 0)),
                  pl.BlockSpec((TOP_K, tm // ROWS8, SUB, ROWS8, 128), lambda i: (0, i, 0, 0, 0)),
                  full(wsg), full(wsu), full(wsd), full(g), full(b)],
        out_specs=pl.BlockSpec((tm, D), lambda i: (i, 0)),
        out_shape=jax.ShapeDtypeStruct((T, D), F32),
        compiler_params=_cparams("parallel"),
        name="combine_ln",
    )(h2, wts, yg_t, wsg, wsu, wsd, g, b)


def _regroup_w_in(w_in):
    parts, off = {}, 0
    for name, width in _SPLITS:
        parts[name] = w_in[:, off:off + width]
        off += width
    cols = []
    for name, (_, width) in _COLS.items():
        p = parts[name]
        if p.shape[1] < width:
            p = jnp.pad(p, ((0, 0), (0, width - p.shape[1])))
        cols.append(p)
    return jnp.concatenate(cols, axis=1).astype(BF16)


def _overlap_matrix(S):
    nr = S // CMP_STRIDE
    n_sel = S // SEL_LEN
    ci = np.arange(nr)[:, None] * CMP_STRIDE
    sj = np.arange(n_sel)[None, :] * SEL_LEN
    ov = np.clip(np.minimum(ci + CMP_LEN, sj + SEL_LEN) - np.maximum(ci, sj), 0, None) / CMP_LEN
    ov[nr - 1] = 0.0
    return jnp.asarray(ov.T, BF16)


def _mixers(h, positions, w_in, w_alpha2, b_alpha, gla_norm_g,
            cmp_pos_k, cmp_w1_k, cmp_b1_k, cmp_w2_k, cmp_pos_v, cmp_w1_v, cmp_b1_v, cmp_w2_v):
    B, S, D = h.shape
    proj = _proj(h.reshape(B * S, D), _regroup_w_in(w_in)).reshape(B, S, D_PROJ)

    wa = jnp.pad(w_alpha2, ((0, 128 - GLA_LOWRANK), (0, 0))).astype(BF16)
    o_gla = _gla(proj, wa, b_alpha.reshape(1, -1), gla_norm_g.reshape(1, -1))

    half = NSA_DH // 2
    inv = ROPE_THETA ** (-np.arange(half, dtype=np.float32) / half)
    inv_row = jnp.asarray(np.tile(inv, 128 // half).reshape(1, 128), F32)
    pos3 = positions.astype(F32).reshape(B, S, 1)
    q_r, kc_r, ks_r, vs_r, kw_r, vw_r = _nsa_prep(proj, pos3, inv_row)

    def blocks16(t):
        return (t.reshape(B, S // CMP_STRIDE, CMP_STRIDE, NSA_KV_GROUPS, NSA_DH)
                .transpose(0, 3, 1, 2, 4).reshape(B, NSA_KV_GROUPS, S // CMP_STRIDE, CMP_STRIDE * NSA_DH))

    vc_off = _COLS["nvc"][0]
    kcmp, vcmp = _compress(
        blocks16(kc_r), blocks16(proj[:, :, vc_off:vc_off + 128]),
        cmp_pos_k.reshape(2, -1), cmp_w1_k.astype(BF16), cmp_b1_k.reshape(1, -1), cmp_w2_k.astype(BF16),
        cmp_pos_v.reshape(2, -1), cmp_w1_v.astype(BF16), cmp_b1_v.reshape(1, -1), cmp_w2_v.astype(BF16))
    o_nsa = _nsa_attn(q_r, kcmp, vcmp, ks_r, vs_r, kw_r, vw_r, proj, _overlap_matrix(S))
    return o_gla, o_nsa


def _moe_ln(h2, h_t, w_router, router_bias, w_gate, w_up, w_down, ws_gate, ws_up, ws_down, ln_g, ln_b):
    T, D = h2.shape
    P = T * TOP_K
    w_hi = w_router.astype(BF16)
    w_lo = (w_router - w_hi.astype(F32)).astype(BF16)
    eidx, wts, rank, counts = _router(h2, w_hi, w_lo, router_bias.reshape(1, -1))

    counts = counts.reshape(-1).astype(jnp.int32)
    padded = (counts + MOE_BLOCK - 1) // MOE_BLOCK * MOE_BLOCK
    pad_end = jnp.cumsum(padded)
    pad_start = pad_end - padded
    nb = -(-P // MOE_BLOCK) + N_EXPERTS
    n_used = (pad_end[-1] // MOE_BLOCK).astype(jnp.int32).reshape(1)
    blk_start = jnp.arange(nb, dtype=jnp.int32) * MOE_BLOCK
    blk_exp = jnp.minimum(jnp.sum((pad_end[None, :] <= blk_start[:, None]).astype(jnp.int32), axis=1),
                          N_EXPERTS - 1)

    NP = nb * MOE_BLOCK
    dest = _dest(eidx, rank, pad_start.astype(F32).reshape(1, -1))
    tok_of_row = (jnp.arange(NP, dtype=jnp.int32) % T).at[dest.reshape(P)].set(
        jnp.arange(P, dtype=jnp.int32) // TOP_K)
    col = (jnp.arange(SUB, dtype=jnp.int32) * ROWS8)

    src = _tile_rows(tok_of_row).reshape(NP // ROWS8, 1, ROWS8) + col[None, :, None]
    xs_t = _sc_gather(h_t.reshape(T * SUB, 128), src.reshape(-1)).reshape(NP // ROWS8, SUB, ROWS8, 128)
    ys_t = _experts(blk_exp, n_used, xs_t, w_gate, w_up, w_down)
    src = _tile_rows(dest.T).reshape(TOP_K, T // ROWS8, 1, ROWS8) + col[None, None, :, None]
    yg_t = _sc_gather(ys_t.reshape(NP * SUB, 128), src.reshape(-1)).reshape(TOP_K, T // ROWS8, SUB, ROWS8, 128)
    return _combine(h2, wts, yg_t, ws_gate.astype(BF16), ws_up.astype(BF16), ws_down.astype(BF16),
                    ln_g.reshape(1, -1), ln_b.reshape(1, -1))


def kernel(x, positions, w_in, w_alpha2, b_alpha, gla_norm_g, cmp_pos_k, cmp_w1_k, cmp_b1_k, cmp_w2_k, cmp_pos_v, cmp_w1_v, cmp_b1_v, cmp_w2_v, w_out, ln1_g, ln1_b, w_router, router_bias, w_exp_gate, w_exp_up, w_exp_down, w_sh_gate, w_sh_up, w_sh_down, ln2_g, ln2_b):
    def layer(h, pos, l):
        B, S, D = h.shape
        o_gla, o_nsa = _mixers(h, pos, w_in[l], w_alpha2[l], b_alpha[l], gla_norm_g[l],
                               cmp_pos_k[l], cmp_w1_k[l], cmp_b1_k[l], cmp_w2_k[l],
                               cmp_pos_v[l], cmp_w1_v[l], cmp_b1_v[l], cmp_w2_v[l])
        h1, h1_t = _outproj(h.reshape(B * S, D), o_gla.reshape(B * S, -1), o_nsa.reshape(B * S, -1),
                            w_out[l].astype(BF16), ln1_g[l].reshape(1, -1), ln1_b[l].reshape(1, -1))
        h2 = _moe_ln(h1, h1_t, w_router[l], router_bias[l], w_exp_gate[l], w_exp_up[l], w_exp_down[l],
                     w_sh_gate[l], w_sh_up[l], w_sh_down[l], ln2_g[l], ln2_b[l])
        return h2.reshape(B, S, D)

    n_groups = BATCH_GROUPS if x.shape[0] % BATCH_GROUPS == 0 else 1
    hs = jnp.split(x, n_groups, axis=0)
    ps = jnp.split(positions, n_groups, axis=0)
    for l in range(w_in.shape[0]):
        hs = [layer(h, p, l) for h, p in zip(hs, ps)]
    return jnp.concatenate(hs, axis=0)
```

```python
import functools

import numpy as np
import jax
import jax.numpy as jnp
from jax import lax
from jax.experimental import pallas as pl
from jax.experimental.pallas import tpu as pltpu
from jax.experimental.pallas import tpu_sc as plsc

D_MODEL = 1024
GLA_HEADS = 4
GLA_DV = 128
GLA_DK = 64
GLA_LOWRANK = 16
GLA_TAU = 16.0
GLA_CHUNK = 64
NSA_HEADS = 8
NSA_KV_GROUPS = 2
NSA_HPG = 4
NSA_DH = 64
CMP_LEN = 32
CMP_STRIDE = 16
CMP_HIDDEN = 256
SEL_LEN = 64
SEL_TOPK = 16
WINDOW = 512
ROPE_THETA = 10000.0
N_EXPERTS = 256
TOP_K = 8
N_GROUPS = 8
TOPK_GROUPS = 4
D_EXPERT = 256
ROUTED_SCALE = 2.5
DEPTH = 1
DN_ALPHA = (2.0 * DEPTH) ** 0.25
LN_EPS = 1e-5

MOE_BLOCK = 256
BATCH_GROUPS = 2
NEG = -1e30
F32 = jnp.float32
BF16 = jnp.bfloat16

_COLS = {}
_off = 0
for _name, _w in (("gq", 256), ("gk", 256), ("gv", 512), ("gr", 512), ("nq", 512),
                  ("nkc", 128), ("nvc", 128), ("nks", 128), ("nvs", 128), ("nkw", 128), ("nvw", 128),
                  ("ga", 128), ("ng", 128)):
    _COLS[_name] = (_off, _w)
    _off += _w
D_PROJ = _off
_SPLITS = (("gq", 256), ("gk", 256), ("gv", 512), ("ga", 16), ("gr", 512), ("nq", 512),
           ("nkc", 128), ("nvc", 128), ("nks", 128), ("nvs", 128), ("nkw", 128), ("nvw", 128), ("ng", 24))

VMEM_LIMIT = 56 * 1024 * 1024


def _cparams(*sem):
    return pltpu.CompilerParams(dimension_semantics=sem, vmem_limit_bytes=VMEM_LIMIT)


def _dot(a, b):
    return jnp.dot(a, b, preferred_element_type=F32)


def _dot_nt(a, b):
    return lax.dot_general(a, b, (((1,), (1,)), ((), ())), preferred_element_type=F32)


def _split3(x):
    hi = x.astype(BF16)
    r1 = x - hi.astype(F32)
    mid = r1.astype(BF16)
    lo = (r1 - mid.astype(F32)).astype(BF16)
    return hi, mid, lo


def _proj_kernel(x_ref, w_ref, o_ref):
    o_ref[...] = _dot(x_ref[...].astype(BF16), w_ref[...])


def _proj(x2, w):
    T, D = x2.shape
    N = w.shape[1]
    tm, tn = 512, 1024
    return pl.pallas_call(
        _proj_kernel,
        grid=(T // tm, N // tn),
        in_specs=[pl.BlockSpec((tm, D), lambda i, j: (i, 0)),
                  pl.BlockSpec((D, tn), lambda i, j: (0, j))],
        out_specs=pl.BlockSpec((tm, tn), lambda i, j: (i, j)),
        out_shape=jax.ShapeDtypeStruct((T, N), F32),
        compiler_params=_cparams("parallel", "arbitrary"),
        name="proj",
    )(x2, w)


def _gla_kernel(q_ref, k_ref, v_ref, r_ref, a_ref, wa_ref, ba_ref, ng_ref, o_ref, st_ref, *, nchunk):
    C = GLA_CHUNK
    HK = GLA_HEADS * GLA_DK

    @pl.when(pl.program_id(1) == 0)
    def _():
        st_ref[...] = jnp.zeros_like(st_ref)

    ri = lax.broadcasted_iota(jnp.int32, (C, C), 0)
    ci = lax.broadcasted_iota(jnp.int32, (C, C), 1)
    causal = ri >= ci
    tri = jnp.where(causal, 1.0, 0.0).astype(BF16)
    lane_head = lax.broadcasted_iota(jnp.int32, (1, HK), 1) // GLA_DK
    wa = wa_ref[...]
    ba = ba_ref[...]
    ng = ng_ref[...]

    def chunk(c, carry):
        rows = pl.ds(pl.multiple_of(c * C, C), C)
        q = q_ref[0, rows, :]
        k = k_ref[0, rows, :]
        v = v_ref[0, rows, :]
        r = r_ref[0, rows, :]
        a = a_ref[0, rows, :]
        z = _dot(a.astype(BF16), wa) + ba
        g = (jnp.minimum(z, 0.0) - jnp.log1p(jnp.exp(-jnp.abs(z)))) * (1.0 / GLA_TAU)
        g_hi, g_mid, g_lo = _split3(g)
        b = _dot(tri, g_hi) + _dot(tri, g_mid) + _dot(tri, g_lo)
        b_last = b[C - 1:C, :]
        qt = q * jnp.exp(b) * (GLA_DK ** -0.5)
        kt = (k * jnp.exp(-b)).astype(BF16)
        ks = (k * jnp.exp(b_last - b)).astype(BF16)
        st = st_ref[...]
        st_b = st.astype(BF16)
        new_st = st * jnp.exp(b_last)
        for h in range(GLA_HEADS):
            hm = lane_head == h
            qh = jnp.where(hm, qt, 0.0).astype(BF16)
            att = jnp.where(causal, _dot_nt(qh, kt), 0.0)
            vh = v[:, h * GLA_DV:(h + 1) * GLA_DV].astype(BF16)
            o = _dot_nt(qh, st_b) + _dot(att.astype(BF16), vh)
            new_st = new_st + jnp.where(hm, _dot(vh.T, ks), 0.0)
            o = o * lax.rsqrt(jnp.mean(o * o, axis=-1, keepdims=True) + 1e-6) * ng
            rh = r[:, h * GLA_DV:(h + 1) * GLA_DV]
            o = o * (rh * jax.nn.sigmoid(rh))
            o_ref[0, rows, h * GLA_DV:(h + 1) * GLA_DV] = o.astype(o_ref.dtype)
        st_ref[...] = new_st
        return carry

    lax.fori_loop(0, nchunk, chunk, 0)


def _gla(proj3, wa, ba, ng):
    B, S, _ = proj3.shape
    L = 512 if S % 512 == 0 else S
    nchunk = L // GLA_CHUNK

    def col(name, width):
        off = _COLS[name][0]
        assert off % width == 0
        return pl.BlockSpec((1, L, width), lambda b, j, o=off // width: (b, j, o))

    full2 = lambda shape: pl.BlockSpec(shape, lambda b, j: (0, 0))
    return pl.pallas_call(
        functools.partial(_gla_kernel, nchunk=nchunk),
        grid=(B, S // L),
        in_specs=[col("gq", 256), col("gk", 256), col("gv", 512), col("gr", 512), col("ga", 128),
                  full2((128, 256)), full2((1, 256)), full2((1, 128))],
        out_specs=pl.BlockSpec((1, L, 512), lambda b, j: (b, j, 0)),
        out_shape=jax.ShapeDtypeStruct((B, S, 512), BF16),
        scratch_shapes=[pltpu.VMEM((GLA_DV, GLA_HEADS * GLA_DK), F32)],
        compiler_params=_cparams("parallel", "arbitrary"),
        name="gla",
    )(proj3, proj3, proj3, proj3, proj3, wa, ba, ng)


def _rot_half(x):
    n = x.shape[-1]
    lane = lax.broadcasted_iota(jnp.int32, (1, n), 1)
    first = (lane % NSA_DH) < (NSA_DH // 2)
    return jnp.where(first, -pltpu.roll(x, n - NSA_DH // 2, 1), pltpu.roll(x, NSA_DH // 2, 1))


def _nsa_prep_kernel(pos_ref, inv_ref, q_ref, kc_ref, ks_ref, vs_ref, kw_ref, vw_ref,
                     qo_ref, kco_ref, kso_ref, vso_ref, kwo_ref, vwo_ref):
    ang = pos_ref[0] * inv_ref[...]
    cos = jnp.cos(ang)
    sin = jnp.sin(ang)
    cos4 = jnp.concatenate([cos] * 4, axis=1)
    sin4 = jnp.concatenate([sin] * 4, axis=1)

    q = q_ref[0]
    qr = q * cos4 + _rot_half(q) * sin4
    for h in range(NSA_HEADS):
        qo_ref[0, h] = (qr[:, h * NSA_DH:(h + 1) * NSA_DH] * (NSA_DH ** -0.5)).astype(qo_ref.dtype)

    def rope128(x):
        return x * cos + _rot_half(x) * sin

    kco_ref[0] = rope128(kc_ref[0])
    ksr = rope128(ks_ref[0])
    kwr = rope128(kw_ref[0])
    vs = vs_ref[0]
    vw = vw_ref[0]
    for g in range(NSA_KV_GROUPS):
        sl = slice(g * NSA_DH, (g + 1) * NSA_DH)
        kso_ref[0, g] = ksr[:, sl].astype(kso_ref.dtype)
        kwo_ref[0, g] = kwr[:, sl].astype(kwo_ref.dtype)
        vso_ref[0, g] = vs[:, sl].astype(vso_ref.dtype)
        vwo_ref[0, g] = vw[:, sl].astype(vwo_ref.dtype)


def _nsa_prep(proj3, pos3, inv_row):
    B, S, _ = proj3.shape
    ts = 512 if S % 512 == 0 else S

    def col(name, width):
        off = _COLS[name][0]
        assert off % width == 0
        return pl.BlockSpec((1, ts, width), lambda b, j, o=off // width: (b, j, o))

    kv_spec = pl.BlockSpec((1, NSA_KV_GROUPS, ts, NSA_DH), lambda b, j: (b, 0, j, 0))
    kv_shape = jax.ShapeDtypeStruct((B, NSA_KV_GROUPS, S, NSA_DH), BF16)
    return pl.pallas_call(
        _nsa_prep_kernel,
        grid=(B, S // ts),
        in_specs=[pl.BlockSpec((1, ts, 1), lambda b, j: (b, j, 0)),
                  pl.BlockSpec((1, 128), lambda b, j: (0, 0)),
                  col("nq", 512), col("nkc", 128), col("nks", 128), col("nvs", 128),
                  col("nkw", 128), col("nvw", 128)],
        out_specs=[pl.BlockSpec((1, NSA_HEADS, ts, NSA_DH), lambda b, j: (b, 0, j, 0)),
                   pl.BlockSpec((1, ts, 128), lambda b, j: (b, j, 0)),
                   kv_spec, kv_spec, kv_spec, kv_spec],
        out_shape=[jax.ShapeDtypeStruct((B, NSA_HEADS, S, NSA_DH), BF16),
                   jax.ShapeDtypeStruct((B, S, 128), F32),
                   kv_shape, kv_shape, kv_shape, kv_shape],
        compiler_params=_cparams("parallel", "parallel"),
        name="nsa_prep",
    )(pos3, inv_row, proj3, proj3, proj3, proj3, proj3, proj3)


def _compress_kernel(k_ref, v_ref, pk_ref, w1k_ref, b1k_ref, w2k_ref, pv_ref, w1v_ref, b1v_ref, w2v_ref,
                     ko_ref, vo_ref):
    half = CMP_STRIDE * NSA_DH

    def run(x_ref, p_ref, w1_ref, b1_ref, w2_ref, o_ref):
        p = p_ref[...]
        for g in range(NSA_KV_GROUPS):
            x = x_ref[0, g]
            nr = x.shape[0]
            a = _dot((x + p[0:1]).astype(BF16), w1_ref[0:half, :])
            bb = _dot((x + p[1:2]).astype(BF16), w1_ref[half:2 * half, :])
            pre = a + pltpu.roll(bb, nr - 1, 0) + b1_ref[...]
            hid = jax.nn.gelu(pre)
            out = _dot(hid.astype(BF16), w2_ref[...])
            row = lax.broadcasted_iota(jnp.int32, out.shape, 0)
            o_ref[0, g] = jnp.where(row < nr - 1, out, 0.0)

    run(k_ref, pk_ref, w1k_ref, b1k_ref, w2k_ref, ko_ref)
    run(v_ref, pv_ref, w1v_ref, b1v_ref, w2v_ref, vo_ref)


def _compress(kc4, vc4, pk, w1k, b1k, w2k, pv, w1v, b1v, w2v):
    B, G, NR, W = kc4.shape
    x_spec = pl.BlockSpec((1, G, NR, W), lambda b: (b, 0, 0, 0))
    full = lambda a: pl.BlockSpec(a.shape, lambda b: (0,) * a.ndim)
    o_spec = pl.BlockSpec((1, G, NR, NSA_DH), lambda b: (b, 0, 0, 0))
    o_shape = jax.ShapeDtypeStruct((B, G, NR, NSA_DH), F32)
    params = (pk, w1k, b1k, w2k, pv, w1v, b1v, w2v)
    return pl.pallas_call(
        _compress_kernel,
        grid=(B,),
        in_specs=[x_spec, x_spec] + [full(a) for a in params],
        out_specs=[o_spec, o_spec],
        out_shape=[o_shape, o_shape],
        compiler_params=_cparams("parallel"),
        name="compress",
    )(kc4, vc4, *params)


def _nsa_attn_kernel(q_ref, kc_ref, vc_ref, ks_ref, vs_ref, kw_ref, vw_ref, gate_ref, ovt_ref, o_ref,
                     *, tq, S, n_sel, n_top):
    g = pl.program_id(1)
    qi = pl.program_id(2)
    start = qi * tq
    H = NSA_HPG
    qs = q_ref[0].reshape(H * tq, NSA_DH)
    t_col = start + lax.broadcasted_iota(jnp.int32, (tq, 1), 0)
    t_row = start + lax.broadcasted_iota(jnp.int32, (1, tq), 1)

    kc = kc_ref[0, 0].astype(BF16)
    vc = vc_ref[0, 0].astype(BF16)
    NR = kc.shape[0]
    s = _dot_nt(qs, kc)
    ncol = lax.broadcasted_iota(jnp.int32, (1, NR), 1)
    cmask = (ncol < NR - 1) & (ncol * CMP_STRIDE + (CMP_LEN - 1) <= t_col)
    s3 = jnp.where(cmask[None], s.reshape(H, tq, NR), NEG)
    m = jnp.max(s3, axis=-1, keepdims=True)
    e = jnp.exp(s3 - m)
    p3 = e / jnp.sum(e, axis=-1, keepdims=True)
    p3 = jnp.where(cmask[None], p3, 0.0)
    o_cmp = _dot(p3.reshape(H * tq, NR).astype(BF16), vc)

    psum = jnp.sum(p3, axis=0)
    p_hi = psum.astype(BF16)
    p_lo = (psum - p_hi.astype(F32)).astype(BF16)
    ovt = ovt_ref[...]
    imp = _dot_nt(ovt, p_hi) + _dot_nt(ovt, p_lo)
    blk = lax.broadcasted_iota(jnp.int32, (n_sel, 1), 0)
    cur = t_row // SEL_LEN
    causal_blk = blk <= cur
    forced = (blk == 0) | (blk == cur) | (blk == cur - 1)
    val = jnp.where(causal_blk, jnp.where(forced, jnp.inf, imp), -jnp.inf)
    rank = jnp.zeros((n_sel, tq), jnp.int32)
    for j in range(n_sel):
        vj = val[j:j + 1, :]
        beats = (vj > val) | ((vj == val) & (blk > j))
        rank = rank + beats.astype(jnp.int32)
    sel_t = jnp.where((rank < n_top) & causal_blk, 1.0, 0.0)
    sel_t = jnp.concatenate([sel_t, jnp.zeros((128 - n_sel, tq), F32)], axis=0)
    sel = sel_t.T.astype(BF16)

    tk = 256
    nsub = 2 if S % (2 * tk) == 0 else 1
    bpc = tk // SEL_LEN
    n_trips = (start + tq + nsub * tk - 1) // (nsub * tk)
    erow = lax.broadcasted_iota(jnp.int32, (128, tk), 0)
    ecol = lax.broadcasted_iota(jnp.int32, (128, tk), 1) // SEL_LEN
    kk = lax.broadcasted_iota(jnp.int32, (1, tk), 1)

    def sel_trip(c, carry):
        m_i, l_i, acc = carry
        scores, vals = [], []
        m_new = m_i
        for u in range(nsub):
            cu = c * nsub + u
            k0 = pl.multiple_of(cu * tk, tk)
            kb = ks_ref[0, 0, pl.ds(k0, tk), :]
            vals.append(vs_ref[0, 0, pl.ds(k0, tk), :])
            expand = jnp.where(erow == ecol + cu * bpc, 1.0, 0.0).astype(BF16)
            allowed = (_dot(sel, expand) > 0.5) & (k0 + kk <= t_col)
            bias = jnp.where(allowed, 0.0, NEG)
            sc3 = _dot_nt(qs, kb).reshape(H, tq, tk) + bias[None]
            scores.append(sc3)
            m_new = jnp.maximum(m_new, jnp.max(sc3, axis=-1, keepdims=True))
        alpha = jnp.exp(m_i - m_new)
        l_new = alpha * l_i
        acc = alpha.reshape(H * tq, 1) * acc
        for u in range(nsub):
            pe = jnp.exp(scores[u] - m_new)
            l_new = l_new + jnp.sum(pe, axis=-1, keepdims=True)
            acc = acc + _dot(pe.reshape(H * tq, tk).astype(BF16), vals[u])
        return m_new, l_new, acc

    m0 = jnp.full((H, tq, 1), NEG, F32)
    l0 = jnp.zeros((H, tq, 1), F32)
    a0 = jnp.zeros((H * tq, NSA_DH), F32)
    _, l_f, acc_f = lax.fori_loop(0, n_trips, sel_trip, (m0, l0, a0))
    o_sel = acc_f / l_f.reshape(H * tq, 1)

    span = min(WINDOW + tq, S)
    ws = jnp.clip(start - WINDOW, 0, S - span)
    ws = pl.multiple_of(ws, tq)
    kwb = kw_ref[0, 0, pl.ds(ws, span), :]
    vwb = vw_ref[0, 0, pl.ds(ws, span), :]
    kp = ws + lax.broadcasted_iota(jnp.int32, (1, span), 1)
    wbias = jnp.where((kp <= t_col) & (kp > t_col - WINDOW), 0.0, NEG)
    sw3 = _dot_nt(qs, kwb).reshape(H, tq, span) + wbias[None]
    mw = jnp.max(sw3, axis=-1, keepdims=True)
    ew = jnp.exp(sw3 - mw)
    pw = ew / jnp.sum(ew, axis=-1, keepdims=True)
    o_win = _dot(pw.reshape(H * tq, span).astype(BF16), vwb)

    gs = jax.nn.sigmoid(gate_ref[0])
    for g_static in range(NSA_KV_GROUPS):
        @pl.when(g == g_static)
        def _(g_static=g_static):
            for h in range(H):
                c0 = g_static * H * 3 + h * 3
                rs = slice(h * tq, (h + 1) * tq)
                o = (gs[:, c0:c0 + 1] * o_cmp[rs] + gs[:, c0 + 1:c0 + 2] * o_sel[rs]
                     + gs[:, c0 + 2:c0 + 3] * o_win[rs])
                o_ref[0, :, h * NSA_DH:(h + 1) * NSA_DH] = o.astype(o_ref.dtype)


def _nsa_attn(q_r, kcmp, vcmp, ks_r, vs_r, kw_r, vw_r, proj3, ov):
    B, _, S, _ = q_r.shape
    G, H = NSA_KV_GROUPS, NSA_HPG
    NR = kcmp.shape[2]
    tq = 256
    n_sel = S // SEL_LEN
    n_top = min(SEL_TOPK, n_sel)
    cmp_spec = pl.BlockSpec((1, 1, NR, NSA_DH), lambda b, g, i: (b, g, 0, 0))
    kv_spec = pl.BlockSpec((1, 1, S, NSA_DH), lambda b, g, i: (b, g, 0, 0))
    goff = _COLS["ng"][0] // 128
    return pl.pallas_call(
        functools.partial(_nsa_attn_kernel, tq=tq, S=S, n_sel=n_sel, n_top=n_top),
        grid=(B, G, S // tq),
        in_specs=[pl.BlockSpec((1, H, tq, NSA_DH), lambda b, g, i: (b, g, i, 0)),
                  cmp_spec, cmp_spec, kv_spec, kv_spec, kv_spec, kv_spec,
                  pl.BlockSpec((1, tq, 128), lambda b, g, i: (b, i, goff)),
                  pl.BlockSpec(ov.shape, lambda b, g, i: (0, 0))],
        out_specs=pl.BlockSpec((1, tq, H * NSA_DH), lambda b, g, i: (b, i, g)),
        out_shape=jax.ShapeDtypeStruct((B, S, NSA_HEADS * NSA_DH), BF16),
        compiler_params=_cparams("parallel", "parallel", "arbitrary"),
        name="nsa_attn",
    )(q_r, kcmp, vcmp, ks_r, vs_r, kw_r, vw_r, proj3, ov)


def _layer_norm(x, g, b):
    mu = jnp.mean(x, axis=-1, keepdims=True)
    xc = x - mu
    var = jnp.mean(xc * xc, axis=-1, keepdims=True)
    return xc * lax.rsqrt(var + LN_EPS) * g + b


SUB = D_MODEL // 2 // 128
ROWS8 = 8


def _pack_bf16_pairs(v):
    half = v.shape[1] // 2
    bits = pltpu.bitcast(v.astype(BF16).astype(F32), jnp.uint32)
    return pltpu.bitcast((bits[:, :half] >> 16) | bits[:, half:], jnp.int32)


def _unpack_bf16_pairs(p):
    u = pltpu.bitcast(p, jnp.uint32)
    return pltpu.bitcast(u << 16, F32), pltpu.bitcast(u & jnp.uint32(0xFFFF0000), F32)


def _tile_rows(t):
    return (t // ROWS8) * (SUB * ROWS8) + t % ROWS8


def _to_tiles(ref, val):
    rows = val.shape[0]
    for c in range(SUB):
        ref[:, c] = val[:, c * 128:(c + 1) * 128].reshape(rows // ROWS8, ROWS8, 128)


def _from_tiles(ref, lead=()):
    groups = ref.shape[len(lead)]
    return jnp.concatenate([ref[lead + (slice(None), c)].reshape(groups * ROWS8, 128) for c in range(SUB)], axis=1)


def _outproj_kernel(x_ref, og_ref, on_ref, w_ref, g_ref, b_ref, o_ref, ot_ref):
    half = og_ref.shape[1]
    mix = _dot(og_ref[...], w_ref[0:half, :]) + _dot(on_ref[...], w_ref[half:, :])
    h = _layer_norm(DN_ALPHA * x_ref[...] + mix, g_ref[...], b_ref[...])
    o_ref[...] = h
    _to_tiles(ot_ref, _pack_bf16_pairs(h))


def _outproj(x2, og2, on2, w, g, b):
    T, D = x2.shape
    tm = 512
    row = lambda width: pl.BlockSpec((tm, width), lambda i: (i, 0))
    full = lambda a: pl.BlockSpec(a.shape, lambda i: (0, 0))
    return pl.pallas_call(
        _outproj_kernel,
        grid=(T // tm,),
        in_specs=[row(D), row(og2.shape[1]), row(on2.shape[1]), full(w), full(g), full(b)],
        out_specs=[row(D), pl.BlockSpec((tm // ROWS8, SUB, ROWS8, 128), lambda i: (i, 0, 0, 0))],
        out_shape=[jax.ShapeDtypeStruct((T, D), F32),
                   jax.ShapeDtypeStruct((T // ROWS8, SUB, ROWS8, 128), jnp.int32)],
        compiler_params=_cparams("parallel"),
        name="outproj_ln",
    )(x2, og2, on2, w, g, b)


def _router_kernel(h_ref, wh_ref, wl_ref, bias_ref, eidx_ref, wts_ref, rank_ref, cnt_ref, carry_ref):
    @pl.when(pl.program_id(0) == 0)
    def _():
        carry_ref[...] = jnp.zeros_like(carry_ref)

    h = h_ref[...]
    tm = h.shape[0]
    E = N_EXPERTS
    h_hi = h.astype(BF16)
    h_lo = (h - h_hi.astype(F32)).astype(BF16)
    wh = wh_ref[...]
    logits = _dot(h_hi, wh) + _dot(h_lo, wh) + _dot(h_hi, wl_ref[...])
    scores = jax.nn.sigmoid(logits)
    biased = scores + bias_ref[...]
    lane_i = lax.broadcasted_iota(jnp.int32, (tm, E), 1)
    gid = lane_i // (E // N_GROUPS)
    lane = lane_i.astype(F32)
    ninf = -jnp.inf

    def row_max(x):
        return jnp.max(x, axis=-1, keepdims=True)

    def first_idx(x, mx):
        return jnp.min(jnp.where(x == mx, lane, float(E)), axis=-1, keepdims=True)

    gscore = []
    for gi in range(N_GROUPS):
        mg = jnp.where(gid == gi, biased, ninf)
        m1 = row_max(mg)
        i1 = first_idx(mg, m1)
        m2 = row_max(jnp.where(lane == i1, ninf, mg))
        gscore.append(m1 + m2)
    emask = jnp.zeros((tm, E), jnp.bool_)
    for gi in range(N_GROUPS):
        rk = jnp.zeros((tm, 1), jnp.int32)
        for gj in range(N_GROUPS):
            if gj == gi:
                continue
            beats = (gscore[gj] > gscore[gi]) | ((gscore[gj] == gscore[gi]) & (gj < gi))
            rk = rk + beats.astype(jnp.int32)
        emask = emask | ((gid == gi) & (rk < TOPK_GROUPS))
    masked = jnp.where(emask, biased, ninf)

    onehots, wsel = [], []
    selm = jnp.zeros((tm, E), F32)
    for k in range(TOP_K):
        mx = row_max(masked)
        idx = first_idx(masked, mx)
        oh = lane == idx
        onehots.append(oh)
        wsel.append(jnp.sum(jnp.where(oh, scores, 0.0), axis=-1, keepdims=True))
        masked = jnp.where(oh, ninf, masked)
        selm = jnp.where(oh, 1.0, selm)
        eidx_ref[:, k:k + 1] = idx.astype(jnp.int32)
    wsum = wsel[0]
    for k in range(1, TOP_K):
        wsum = wsum + wsel[k]
    for k in range(TOP_K):
        wts_ref[:, k:k + 1] = wsel[k] / wsum * ROUTED_SCALE

    ri = lax.broadcasted_iota(jnp.int32, (tm, tm), 0)
    ci = lax.broadcasted_iota(jnp.int32, (tm, tm), 1)
    ltri = jnp.where(ri > ci, 1.0, 0.0).astype(BF16)
    cum = _dot(ltri, selm.astype(BF16)) + carry_ref[...]
    for k in range(TOP_K):
        rk = jnp.sum(jnp.where(onehots[k], cum, 0.0), axis=-1, keepdims=True)
        rank_ref[:, k:k + 1] = rk.astype(jnp.int32)
    total = carry_ref[...] + jnp.sum(selm, axis=0, keepdims=True)
    carry_ref[...] = total
    cnt_ref[...] = total


def _router(h2, w_hi, w_lo, bias):
    T, D = h2.shape
    tm = 256
    full = lambda a: pl.BlockSpec(a.shape, lambda i: (0, 0))
    o8 = pl.BlockSpec((tm, TOP_K), lambda i: (i, 0))
    return pl.pallas_call(
        _router_kernel,
        grid=(T // tm,),
        in_specs=[pl.BlockSpec((tm, D), lambda i: (i, 0)), full(w_hi), full(w_lo), full(bias)],
        out_specs=[o8, o8, o8, pl.BlockSpec((1, N_EXPERTS), lambda i: (0, 0))],
        out_shape=[jax.ShapeDtypeStruct((T, TOP_K), jnp.int32),
                   jax.ShapeDtypeStruct((T, TOP_K), F32),
                   jax.ShapeDtypeStruct((T, TOP_K), jnp.int32),
                   jax.ShapeDtypeStruct((1, N_EXPERTS), F32)],
        scratch_shapes=[pltpu.VMEM((1, N_EXPERTS), F32)],
        compiler_params=_cparams("arbitrary"),
        name="router",
    )(h2, w_hi, w_lo, bias)


def _dest_kernel(eidx_ref, rank_ref, ps_ref, dest_ref):
    tm = eidx_ref.shape[0]
    lane = lax.broadcasted_iota(jnp.int32, (tm, N_EXPERTS), 1)
    ps = ps_ref[...]
    for k in range(TOP_K):
        start = jnp.sum(jnp.where(lane == eidx_ref[:, k:k + 1], ps, 0.0), axis=-1, keepdims=True)
        dest_ref[:, k:k + 1] = start.astype(jnp.int32) + rank_ref[:, k:k + 1]


def _dest(eidx, rank, pad_start_f):
    T = eidx.shape[0]
    tm = 1024 if T % 1024 == 0 else T
    o8 = pl.BlockSpec((tm, TOP_K), lambda i: (i, 0))
    return pl.pallas_call(
        _dest_kernel,
        grid=(T // tm,),
        in_specs=[o8, o8, pl.BlockSpec((1, N_EXPERTS), lambda i: (0, 0))],
        out_specs=o8,
        out_shape=jax.ShapeDtypeStruct((T, TOP_K), jnp.int32),
        compiler_params=_cparams("parallel"),
        name="dest",
    )(eidx, rank, pad_start_f)


SC_WINDOW = 128


def _sc_gather(table, idx):
    _, lanes = table.shape
    n = idx.shape[0]
    mesh = plsc.VectorSubcoreMesh(core_axis_name="core", subcore_axis_name="subcore")

    @functools.partial(pl.kernel, out_type=jax.ShapeDtypeStruct((n, lanes), table.dtype), mesh=mesh,
                       name="sc_row_gather")
    def gather(x_hbm, i_hbm, o_hbm):
        def body(i_vmem, o_vmem):
            pltpu.sync_copy(x_hbm.at[i_vmem.at[0]], o_vmem)

        pltpu.emit_pipeline(
            body,
            grid=(n // SC_WINDOW,),
            in_specs=[pl.BlockSpec((1, SC_WINDOW), lambda i: (0, i))],
            out_specs=[pl.BlockSpec((SC_WINDOW, lanes), lambda i: (i, 0))],
            core_axis_name=("core", "subcore"),
            dimension_semantics=(pltpu.PARALLEL,),
            trace_scopes=False,
        )(i_hbm, o_hbm)

    return gather(table, idx.reshape(1, n))


def _expert_kernel(bexp_ref, nused_ref, x_ref, wg_ref, wu_ref, wd_ref, y_ref):
    del bexp_ref

    @pl.when(pl.program_id(0) < nused_ref[0])
    def _():
        x = jnp.concatenate(_unpack_bf16_pairs(_from_tiles(x_ref)), axis=1).astype(BF16)
        gate = _dot(x, wg_ref[0].astype(BF16))
        up = _dot(x, wu_ref[0].astype(BF16))
        act = (gate * jax.nn.sigmoid(gate) * up).astype(BF16)
        _to_tiles(y_ref, _pack_bf16_pairs(_dot(act, wd_ref[0].astype(BF16))))


def _experts(blk_exp, n_used, xs_t, wg, wu, wd):
    NP = xs_t.shape[0] * ROWS8
    D = D_MODEL
    nb = NP // MOE_BLOCK
    blk = (MOE_BLOCK // ROWS8, SUB, ROWS8, 128)

    def xmap(i, bexp, nused):
        return (jnp.minimum(i, nused[0] - 1), 0, 0, 0)

    def wmap(i, bexp, nused):
        return (bexp[jnp.minimum(i, nused[0] - 1)], 0, 0)

    grid_spec = pltpu.PrefetchScalarGridSpec(
        num_scalar_prefetch=2,
        grid=(nb,),
        in_specs=[pl.BlockSpec(blk, xmap),
                  pl.BlockSpec((1, D, D_EXPERT), wmap),
                  pl.BlockSpec((1, D, D_EXPERT), wmap),
                  pl.BlockSpec((1, D_EXPERT, D), wmap)],
        out_specs=pl.BlockSpec(blk, xmap),
    )
    return pl.pallas_call(
        _expert_kernel,
        grid_spec=grid_spec,
        out_shape=jax.ShapeDtypeStruct(xs_t.shape, jnp.int32),
        compiler_params=_cparams("arbitrary"),
        name="experts",
    )(blk_exp, n_used, xs_t, wg, wu, wd)


def _combine_kernel(h_ref, wts_ref, yg_ref, wsg_ref, wsu_ref, wsd_ref, g_ref, b_ref, o_ref):
    h = h_ref[...]
    hb = h.astype(BF16)
    gate = _dot(hb, wsg_ref[...])
    up = _dot(hb, wsu_ref[...])
    shared = _dot((gate * jax.nn.sigmoid(gate) * up).astype(BF16), wsd_ref[...])
    wts = wts_ref[...]
    lo, hi = _unpack_bf16_pairs(_from_tiles(yg_ref, (0,)))
    r_lo, r_hi = wts[:, 0:1] * lo, wts[:, 0:1] * hi
    for k in range(1, TOP_K):
        lo, hi = _unpack_bf16_pairs(_from_tiles(yg_ref, (k,)))
        r_lo, r_hi = r_lo + wts[:, k:k + 1] * lo, r_hi + wts[:, k:k + 1] * hi
    routed = jnp.concatenate([r_lo, r_hi], axis=1)
    o_ref[...] = _layer_norm(DN_ALPHA * h + (routed + shared), g_ref[...], b_ref[...])


def _combine(h2, wts, yg_t, wsg, wsu, wsd, g, b):
    T, D = h2.shape
    tm = 128
    full = lambda a: pl.BlockSpec(a.shape, lambda i: (0, 0))
    return pl.pallas_call(
        _combine_kernel,
        grid=(T // tm,),
        in_specs=[pl.BlockSpec((tm, D), lambda i: (i, 0)),
                  pl.BlockSpec((tm, TOP_K), lambda i: (i, 0)),
                  pl.BlockSpec((TOP_K, tm // ROWS8, SUB, ROWS8, 128), lambda i: (0, i, 0, 0, 0)),
                  full(wsg), full(wsu), full(wsd), full(g), full(b)],
        out_specs=pl.BlockSpec((tm, D), lambda i: (i, 0)),
        out_shape=jax.ShapeDtypeStruct((T, D), F32),
        compiler_params=_cparams("parallel"),
        name="combine_ln",
    )(h2, wts, yg_t, wsg, wsu, wsd, g, b)


def _regroup_w_in(w_in):
    parts, off = {}, 0
    for name, width in _SPLITS:
        parts[name] = w_in[:, off:off + width]
        off += width
    cols = []
    for name, (_, width) in _COLS.items():
        p = parts[name]
        if p.shape[1] < width:
            p = jnp.pad(p, ((0, 0), (0, width - p.shape[1])))
        cols.append(p)
    return jnp.concatenate(cols, axis=1).astype(BF16)


def _overlap_matrix(S):
    nr = S // CMP_STRIDE
    n_sel = S // SEL_LEN
    ci = np.arange(nr)[:, None] * CMP_STRIDE
    sj = np.arange(n_sel)[None, :] * SEL_LEN
    ov = np.clip(np.minimum(ci + CMP_LEN, sj + SEL_LEN) - np.maximum(ci, sj), 0, None) / CMP_LEN
    ov[nr - 1] = 0.0
    return jnp.asarray(ov.T, BF16)


def _mixers(h, positions, w_in, w_alpha2, b_alpha, gla_norm_g,
            cmp_pos_k, cmp_w1_k, cmp_b1_k, cmp_w2_k, cmp_pos_v, cmp_w1_v, cmp_b1_v, cmp_w2_v):
    B, S, D = h.shape
    proj = _proj(h.reshape(B * S, D), _regroup_w_in(w_in)).reshape(B, S, D_PROJ)

    wa = jnp.pad(w_alpha2, ((0, 128 - GLA_LOWRANK), (0, 0))).astype(BF16)
    o_gla = _gla(proj, wa, b_alpha.reshape(1, -1), gla_norm_g.reshape(1, -1))

    half = NSA_DH // 2
    inv = ROPE_THETA ** (-np.arange(half, dtype=np.float32) / half)
    inv_row = jnp.asarray(np.tile(inv, 128 // half).reshape(1, 128), F32)
    pos3 = positions.astype(F32).reshape(B, S, 1)
    q_r, kc_r, ks_r, vs_r, kw_r, vw_r = _nsa_prep(proj, pos3, inv_row)

    def blocks16(t):
        return (t.reshape(B, S // CMP_STRIDE, CMP_STRIDE, NSA_KV_GROUPS, NSA_DH)
                .transpose(0, 3, 1, 2, 4).reshape(B, NSA_KV_GROUPS, S // CMP_STRIDE, CMP_STRIDE * NSA_DH))

    vc_off = _COLS["nvc"][0]
    kcmp, vcmp = _compress(
        blocks16(kc_r), blocks16(proj[:, :, vc_off:vc_off + 128]),
        cmp_pos_k.reshape(2, -1), cmp_w1_k.astype(BF16), cmp_b1_k.reshape(1, -1), cmp_w2_k.astype(BF16),
        cmp_pos_v.reshape(2, -1), cmp_w1_v.astype(BF16), cmp_b1_v.reshape(1, -1), cmp_w2_v.astype(BF16))
    o_nsa = _nsa_attn(q_r, kcmp, vcmp, ks_r, vs_r, kw_r, vw_r, proj, _overlap_matrix(S))
    return o_gla, o_nsa


def _moe_ln(h2, h_t, w_router, router_bias, w_gate, w_up, w_down, ws_gate, ws_up, ws_down, ln_g, ln_b):
    T, D = h2.shape
    P = T * TOP_K
    w_hi = w_router.astype(BF16)
    w_lo = (w_router - w_hi.astype(F32)).astype(BF16)
    eidx, wts, rank, counts = _router(h2, w_hi, w_lo, router_bias.reshape(1, -1))

    counts = counts.reshape(-1).astype(jnp.int32)
    padded = (counts + MOE_BLOCK - 1) // MOE_BLOCK * MOE_BLOCK
    pad_end = jnp.cumsum(padded)
    pad_start = pad_end - padded
    nb = -(-P // MOE_BLOCK) + N_EXPERTS
    n_used = (pad_end[-1] // MOE_BLOCK).astype(jnp.int32).reshape(1)
    blk_start = jnp.arange(nb, dtype=jnp.int32) * MOE_BLOCK
    blk_exp = jnp.minimum(jnp.sum((pad_end[None, :] <= blk_start[:, None]).astype(jnp.int32), axis=1),
                          N_EXPERTS - 1)

    NP = nb * MOE_BLOCK
    dest = _dest(eidx, rank, pad_start.astype(F32).reshape(1, -1))
    tok_of_row = (jnp.arange(NP, dtype=jnp.int32) % T).at[dest.reshape(P)].set(
        jnp.arange(P, dtype=jnp.int32) // TOP_K)
    col = (jnp.arange(SUB, dtype=jnp.int32) * ROWS8)

    src = _tile_rows(tok_of_row).reshape(NP // ROWS8, 1, ROWS8) + col[None, :, None]
    xs_t = _sc_gather(h_t.reshape(T * SUB, 128), src.reshape(-1)).reshape(NP // ROWS8, SUB, ROWS8, 128)
    ys_t = _experts(blk_exp, n_used, xs_t, w_gate, w_up, w_down)
    src = _tile_rows(dest.T).reshape(TOP_K, T // ROWS8, 1, ROWS8) + col[None, None, :, None]
    yg_t = _sc_gather(ys_t.reshape(NP * SUB, 128), src.reshape(-1)).reshape(TOP_K, T // ROWS8, SUB, ROWS8, 128)
    return _combine(h2, wts, yg_t, ws_gate.astype(BF16), ws_up.astype(BF16), ws_down.astype(BF16),
                    ln_g.reshape(1, -1), ln_b.reshape(1, -1))


def kernel(x, positions, w_in, w_alpha2, b_alpha, gla_norm_g, cmp_pos_k, cmp_w1_k, cmp_b1_k, cmp_w2_k, cmp_pos_v, cmp_w1_v, cmp_b1_v, cmp_w2_v, w_out, ln1_g, ln1_b, w_router, router_bias, w_exp_gate, w_exp_up, w_exp_down, w_sh_gate, w_sh_up, w_sh_down, ln2_g, ln2_b):
    def layer(h, pos, l):
        B, S, D = h.shape
        o_gla, o_nsa = _mixers(h, pos, w_in[l], w_alpha2[l], b_alpha[l], gla_norm_g[l],
                               cmp_pos_k[l], cmp_w1_k[l], cmp_b1_k[l], cmp_w2_k[l],
                               cmp_pos_v[l], cmp_w1_v[l], cmp_b1_v[l], cmp_w2_v[l])
        h1, h1_t = _outproj(h.reshape(B * S, D), o_gla.reshape(B * S, -1), o_nsa.reshape(B * S, -1),
                            w_out[l].astype(BF16), ln1_g[l].reshape(1, -1), ln1_b[l].reshape(1, -1))
        h2 = _moe_ln(h1, h1_t, w_router[l], router_bias[l], w_exp_gate[l], w_exp_up[l], w_exp_down[l],
                     w_sh_gate[l], w_sh_up[l], w_sh_down[l], ln2_g[l], ln2_b[l])
        return h2.reshape(B, S, D)

    n_groups = BATCH_GROUPS if x.shape[0] % BATCH_GROUPS == 0 else 1
    hs = jnp.split(x, n_groups, axis=0)
    ps = jnp.split(positions, n_groups, axis=0)
    for l in range(w_in.shape[0]):
        hs = [layer(h, p, l) for h, p in zip(hs, ps)]
    return jnp.concatenate(hs, axis=0)
```

```python
import functools

import numpy as np
import jax
import jax.numpy as jnp
from jax import lax
from jax.experimental import pallas as pl
from jax.experimental.pallas import tpu as pltpu
from jax.experimental.pallas import tpu_sc as plsc

D_MODEL = 1024
GLA_HEADS = 4
GLA_DV = 128
GLA_DK = 64
GLA_LOWRANK = 16
GLA_TAU = 16.0
GLA_CHUNK = 64
NSA_HEADS = 8
NSA_KV_GROUPS = 2
NSA_HPG = 4
NSA_DH = 64
CMP_LEN = 32
CMP_STRIDE = 16
CMP_HIDDEN = 256
SEL_LEN = 64
SEL_TOPK = 16
WINDOW = 512
ROPE_THETA = 10000.0
N_EXPERTS = 256
TOP_K = 8
N_GROUPS = 8
TOPK_GROUPS = 4
D_EXPERT = 256
ROUTED_SCALE = 2.5
DEPTH = 1
DN_ALPHA = (2.0 * DEPTH) ** 0.25
LN_EPS = 1e-5

MOE_BLOCK = 256
BATCH_GROUPS = 2
NEG = -1e30
F32 = jnp.float32
BF16 = jnp.bfloat16

_COLS = {}
_off = 0
for _name, _w in (("gq", 256), ("gk", 256), ("gv", 512), ("gr", 512), ("nq", 512),
                  ("nkc", 128), ("nvc", 128), ("nks", 128), ("nvs", 128), ("nkw", 128), ("nvw", 128),
                  ("ga", 128), ("ng", 128)):
    _COLS[_name] = (_off, _w)
    _off += _w
D_PROJ = _off
_SPLITS = (("gq", 256), ("gk", 256), ("gv", 512), ("ga", 16), ("gr", 512), ("nq", 512),
           ("nkc", 128), ("nvc", 128), ("nks", 128), ("nvs", 128), ("nkw", 128), ("nvw", 128), ("ng", 24))

VMEM_LIMIT = 56 * 1024 * 1024


def _cparams(*sem):
    return pltpu.CompilerParams(dimension_semantics=sem, vmem_limit_bytes=VMEM_LIMIT)


def _dot(a, b):
    return jnp.dot(a, b, preferred_element_type=F32)


def _dot_nt(a, b):
    return lax.dot_general(a, b, (((1,), (1,)), ((), ())), preferred_element_type=F32)


def _split3(x):
    hi = x.astype(BF16)
    r1 = x - hi.astype(F32)
    mid = r1.astype(BF16)
    lo = (r1 - mid.astype(F32)).astype(BF16)
    return hi, mid, lo


def _proj_kernel(x_ref, w_ref, o_ref):
    o_ref[...] = _dot(x_ref[...].astype(BF16), w_ref[...])


def _proj(x2, w):
    T, D = x2.shape
    N = w.shape[1]
    tm, tn = 512, 1024
    return pl.pallas_call(
        _proj_kernel,
        grid=(T // tm, N // tn),
        in_specs=[pl.BlockSpec((tm, D), lambda i, j: (i, 0)),
                  pl.BlockSpec((D, tn), lambda i, j: (0, j))],
        out_specs=pl.BlockSpec((tm, tn), lambda i, j: (i, j)),
        out_shape=jax.ShapeDtypeStruct((T, N), F32),
        compiler_params=_cparams("parallel", "arbitrary"),
        name="proj",
    )(x2, w)


def _gla_kernel(q_ref, k_ref, v_ref, r_ref, a_ref, wa_ref, ba_ref, ng_ref, o_ref, st_ref, *, nchunk):
    C = GLA_CHUNK
    HK = GLA_HEADS * GLA_DK

    @pl.when(pl.program_id(1) == 0)
    def _():
        st_ref[...] = jnp.zeros_like(st_ref)

    ri = lax.broadcasted_iota(jnp.int32, (C, C), 0)
    ci = lax.broadcasted_iota(jnp.int32, (C, C), 1)
    causal = ri >= ci
    tri = jnp.where(causal, 1.0, 0.0).astype(BF16)
    lane_head = lax.broadcasted_iota(jnp.int32, (1, HK), 1) // GLA_DK
    wa = wa_ref[...]
    ba = ba_ref[...]
    ng = ng_ref[...]

    def chunk(c, carry):
        rows = pl.ds(pl.multiple_of(c * C, C), C)
        q = q_ref[0, rows, :]
        k = k_ref[0, rows, :]
        v = v_ref[0, rows, :]
        r = r_ref[0, rows, :]
        a = a_ref[0, rows, :]
        z = _dot(a.astype(BF16), wa) + ba
        g = (jnp.minimum(z, 0.0) - jnp.log1p(jnp.exp(-jnp.abs(z)))) * (1.0 / GLA_TAU)
        g_hi, g_mid, g_lo = _split3(g)
        b = _dot(tri, g_hi) + _dot(tri, g_mid) + _dot(tri, g_lo)
        b_last = b[C - 1:C, :]
        qt = q * jnp.exp(b) * (GLA_DK ** -0.5)
        kt = (k * jnp.exp(-b)).astype(BF16)
        ks = (k * jnp.exp(b_last - b)).astype(BF16)
        st = st_ref[...]
        st_b = st.astype(BF16)
        new_st = st * jnp.exp(b_last)
        for h in range(GLA_HEADS):
            hm = lane_head == h
            qh = jnp.where(hm, qt, 0.0).astype(BF16)
            att = jnp.where(causal, _dot_nt(qh, kt), 0.0)
            vh = v[:, h * GLA_DV:(h + 1) * GLA_DV].astype(BF16)
            o = _dot_nt(qh, st_b) + _dot(att.astype(BF16), vh)
            new_st = new_st + jnp.where(hm, _dot(vh.T, ks), 0.0)
            o = o * lax.rsqrt(jnp.mean(o * o, axis=-1, keepdims=True) + 1e-6) * ng
            rh = r[:, h * GLA_DV:(h + 1) * GLA_DV]
            o = o * (rh * jax.nn.sigmoid(rh))
            o_ref[0, rows, h * GLA_DV:(h + 1) * GLA_DV] = o.astype(o_ref.dtype)
        st_ref[...] = new_st
        return carry

    lax.fori_loop(0, nchunk, chunk, 0)


def _gla(proj3, wa, ba, ng):
    B, S, _ = proj3.shape
    L = 512 if S % 512 == 0 else S
    nchunk = L // GLA_CHUNK

    def col(name, width):
        off = _COLS[name][0]
        assert off % width == 0
        return pl.BlockSpec((1, L, width), lambda b, j, o=off // width: (b, j, o))

    full2 = lambda shape: pl.BlockSpec(shape, lambda b, j: (0, 0))
    return pl.pallas_call(
        functools.partial(_gla_kernel, nchunk=nchunk),
        grid=(B, S // L),
        in_specs=[col("gq", 256), col("gk", 256), col("gv", 512), col("gr", 512), col("ga", 128),
                  full2((128, 256)), full2((1, 256)), full2((1, 128))],
        out_specs=pl.BlockSpec((1, L, 512), lambda b, j: (b, j, 0)),
        out_shape=jax.ShapeDtypeStruct((B, S, 512), BF16),
        scratch_shapes=[pltpu.VMEM((GLA_DV, GLA_HEADS * GLA_DK), F32)],
        compiler_params=_cparams("parallel", "arbitrary"),
        name="gla",
    )(proj3, proj3, proj3, proj3, proj3, wa, ba, ng)


def _rot_half(x):
    n = x.shape[-1]
    lane = lax.broadcasted_iota(jnp.int32, (1, n), 1)
    first = (lane % NSA_DH) < (NSA_DH // 2)
    return jnp.where(first, -pltpu.roll(x, n - NSA_DH // 2, 1), pltpu.roll(x, NSA_DH // 2, 1))


def _nsa_prep_kernel(pos_ref, inv_ref, q_ref, kc_ref, ks_ref, vs_ref, kw_ref, vw_ref,
                     qo_ref, kco_ref, kso_ref, vso_ref, kwo_ref, vwo_ref):
    ang = pos_ref[0] * inv_ref[...]
    cos = jnp.cos(ang)
    sin = jnp.sin(ang)
    cos4 = jnp.concatenate([cos] * 4, axis=1)
    sin4 = jnp.concatenate([sin] * 4, axis=1)

    q = q_ref[0]
    qr = q * cos4 + _rot_half(q) * sin4
    for h in range(NSA_HEADS):
        qo_ref[0, h] = (qr[:, h * NSA_DH:(h + 1) * NSA_DH] * (NSA_DH ** -0.5)).astype(qo_ref.dtype)

    def rope128(x):
        return x * cos + _rot_half(x) * sin

    kco_ref[0] = rope128(kc_ref[0])
    ksr = rope128(ks_ref[0])
    kwr = rope128(kw_ref[0])
    vs = vs_ref[0]
    vw = vw_ref[0]
    for g in range(NSA_KV_GROUPS):
        sl = slice(g * NSA_DH, (g + 1) * NSA_DH)
        kso_ref[0, g] = ksr[:, sl].astype(kso_ref.dtype)
        kwo_ref[0, g] = kwr[:, sl].astype(kwo_ref.dtype)
        vso_ref[0, g] = vs[:, sl].astype(vso_ref.dtype)
        vwo_ref[0, g] = vw[:, sl].astype(vwo_ref.dtype)


def _nsa_prep(proj3, pos3, inv_row):
    B, S, _ = proj3.shape
    ts = 512 if S % 512 == 0 else S

    def col(name, width):
        off = _COLS[name][0]
        assert off % width == 0
        return pl.BlockSpec((1, ts, width), lambda b, j, o=off // width: (b, j, o))

    kv_spec = pl.BlockSpec((1, NSA_KV_GROUPS, ts, NSA_DH), lambda b, j: (b, 0, j, 0))
    kv_shape = jax.ShapeDtypeStruct((B, NSA_KV_GROUPS, S, NSA_DH), BF16)
    return pl.pallas_call(
        _nsa_prep_kernel,
        grid=(B, S // ts),
        in_specs=[pl.BlockSpec((1, ts, 1), lambda b, j: (b, j, 0)),
                  pl.BlockSpec((1, 128), lambda b, j: (0, 0)),
                  col("nq", 512), col("nkc", 128), col("nks", 128), col("nvs", 128),
                  col("nkw", 128), col("nvw", 128)],
        out_specs=[pl.BlockSpec((1, NSA_HEADS, ts, NSA_DH), lambda b, j: (b, 0, j, 0)),
                   pl.BlockSpec((1, ts, 128), lambda b, j: (b, j, 0)),
                   kv_spec, kv_spec, kv_spec, kv_spec],
        out_shape=[jax.ShapeDtypeStruct((B, NSA_HEADS, S, NSA_DH), BF16),
                   jax.ShapeDtypeStruct((B, S, 128), F32),
                   kv_shape, kv_shape, kv_shape, kv_shape],
        compiler_params=_cparams("parallel", "parallel"),
        name="nsa_prep",
    )(pos3, inv_row, proj3, proj3, proj3, proj3, proj3, proj3)


def _compress_kernel(k_ref, v_ref, pk_ref, w1k_ref, b1k_ref, w2k_ref, pv_ref, w1v_ref, b1v_ref, w2v_ref,
                     ko_ref, vo_ref):
    half = CMP_STRIDE * NSA_DH

    def run(x_ref, p_ref, w1_ref, b1_ref, w2_ref, o_ref):
        p = p_ref[...]
        for g in range(NSA_KV_GROUPS):
            x = x_ref[0, g]
            nr = x.shape[0]
            a = _dot((x + p[0:1]).astype(BF16), w1_ref[0:half, :])
            bb = _dot((x + p[1:2]).astype(BF16), w1_ref[half:2 * half, :])
            pre = a + pltpu.roll(bb, nr - 1, 0) + b1_ref[...]
            hid = jax.nn.gelu(pre)
            out = _dot(hid.astype(BF16), w2_ref[...])
            row = lax.broadcasted_iota(jnp.int32, out.shape, 0)
            o_ref[0, g] = jnp.where(row < nr - 1, out, 0.0)

    run(k_ref, pk_ref, w1k_ref, b1k_ref, w2k_ref, ko_ref)
    run(v_ref, pv_ref, w1v_ref, b1v_ref, w2v_ref, vo_ref)


def _compress(kc4, vc4, pk, w1k, b1k, w2k, pv, w1v, b1v, w2v):
    B, G, NR, W = kc4.shape
    x_spec = pl.BlockSpec((1, G, NR, W), lambda b: (b, 0, 0, 0))
    full = lambda a: pl.BlockSpec(a.shape, lambda b: (0,) * a.ndim)
    o_spec = pl.BlockSpec((1, G, NR, NSA_DH), lambda b: (b, 0, 0, 0))
    o_shape = jax.ShapeDtypeStruct((B, G, NR, NSA_DH), F32)
    params = (pk, w1k, b1k, w2k, pv, w1v, b1v, w2v)
    return pl.pallas_call(
        _compress_kernel,
        grid=(B,),
        in_specs=[x_spec, x_spec] + [full(a) for a in params],
        out_specs=[o_spec, o_spec],
        out_shape=[o_shape, o_shape],
        compiler_params=_cparams("parallel"),
        name="compress",
    )(kc4, vc4, *params)


def _nsa_attn_kernel(q_ref, kc_ref, vc_ref, ks_ref, vs_ref, kw_ref, vw_ref, gate_ref, ovt_ref, o_ref,
                     *, tq, S, n_sel, n_top):
    g = pl.program_id(1)
    qi = pl.program_id(2)
    start = qi * tq
    H = NSA_HPG
    qs = q_ref[0].reshape(H * tq, NSA_DH)
    t_col = start + lax.broadcasted_iota(jnp.int32, (tq, 1), 0)
    t_row = start + lax.broadcasted_iota(jnp.int32, (1, tq), 1)

    kc = kc_ref[0, 0].astype(BF16)
    vc = vc_ref[0, 0].astype(BF16)
    NR = kc.shape[0]
    s = _dot_nt(qs, kc)
    ncol = lax.broadcasted_iota(jnp.int32, (1, NR), 1)
    cmask = (ncol < NR - 1) & (ncol * CMP_STRIDE + (CMP_LEN - 1) <= t_col)
    s3 = jnp.where(cmask[None], s.reshape(H, tq, NR), NEG)
    m = jnp.max(s3, axis=-1, keepdims=True)
    e = jnp.exp(s3 - m)
    p3 = e / jnp.sum(e, axis=-1, keepdims=True)
    p3 = jnp.where(cmask[None], p3, 0.0)
    o_cmp = _dot(p3.reshape(H * tq, NR).astype(BF16), vc)

    psum = jnp.sum(p3, axis=0)
    p_hi = psum.astype(BF16)
    p_lo = (psum - p_hi.astype(F32)).astype(BF16)
    ovt = ovt_ref[...]
    imp = _dot_nt(ovt, p_hi) + _dot_nt(ovt, p_lo)
    blk = lax.broadcasted_iota(jnp.int32, (n_sel, 1), 0)
    cur = t_row // SEL_LEN
    causal_blk = blk <= cur
    forced = (blk == 0) | (blk == cur) | (blk == cur - 1)
    val = jnp.where(causal_blk, jnp.where(forced, jnp.inf, imp), -jnp.inf)
    rank = jnp.zeros((n_sel, tq), jnp.int32)
    for j in range(n_sel):
        vj = val[j:j + 1, :]
        beats = (vj > val) | ((vj == val) & (blk > j))
        rank = rank + beats.astype(jnp.int32)
    sel_t = jnp.where((rank < n_top) & causal_blk, 1.0, 0.0)
    sel_t = jnp.concatenate([sel_t, jnp.zeros((128 - n_sel, tq), F32)], axis=0)
    sel = sel_t.T.astype(BF16)

    tk = 256
    nsub = 2 if S % (2 * tk) == 0 else 1
    bpc = tk // SEL_LEN
    n_trips = (start + tq + nsub * tk - 1) // (nsub * tk)
    erow = lax.broadcasted_iota(jnp.int32, (128, tk), 0)
    ecol = lax.broadcasted_iota(jnp.int32, (128, tk), 1) // SEL_LEN
    kk = lax.broadcasted_iota(jnp.int32, (1, tk), 1)

    def sel_trip(c, carry):
        m_i, l_i, acc = carry
        scores, vals = [], []
        m_new = m_i
        for u in range(nsub):
            cu = c * nsub + u
            k0 = pl.multiple_of(cu * tk, tk)
            kb = ks_ref[0, 0, pl.ds(k0, tk), :]
            vals.append(vs_ref[0, 0, pl.ds(k0, tk), :])
            expand = jnp.where(erow == ecol + cu * bpc, 1.0, 0.0).astype(BF16)
            allowed = (_dot(sel, expand) > 0.5) & (k0 + kk <= t_col)
            bias = jnp.where(allowed, 0.0, NEG)
            sc3 = _dot_nt(qs, kb).reshape(H, tq, tk) + bias[None]
            scores.append(sc3)
            m_new = jnp.maximum(m_new, jnp.max(sc3, axis=-1, keepdims=True))
        alpha = jnp.exp(m_i - m_new)
        l_new = alpha * l_i
        acc = alpha.reshape(H * tq, 1) * acc
        for u in range(nsub):
            pe = jnp.exp(scores[u] - m_new)
            l_new = l_new + jnp.sum(pe, axis=-1, keepdims=True)
            acc = acc + _dot(pe.reshape(H * tq, tk).astype(BF16), vals[u])
        return m_new, l_new, acc

    m0 = jnp.full((H, tq, 1), NEG, F32)
    l0 = jnp.zeros((H, tq, 1), F32)
    a0 = jnp.zeros((H * tq, NSA_DH), F32)
    _, l_f, acc_f = lax.fori_loop(0, n_trips, sel_trip, (m0, l0, a0))
    o_sel = acc_f / l_f.reshape(H * tq, 1)

    span = min(WINDOW + tq, S)
    ws = jnp.clip(start - WINDOW, 0, S - span)
    ws = pl.multiple_of(ws, tq)
    kwb = kw_ref[0, 0, pl.ds(ws, span), :]
    vwb = vw_ref[0, 0, pl.ds(ws, span), :]
    kp = ws + lax.broadcasted_iota(jnp.int32, (1, span), 1)
    wbias = jnp.where((kp <= t_col) & (kp > t_col - WINDOW), 0.0, NEG)
    sw3 = _dot_nt(qs, kwb).reshape(H, tq, span) + wbias[None]
    mw = jnp.max(sw3, axis=-1, keepdims=True)
    ew = jnp.exp(sw3 - mw)
    pw = ew / jnp.sum(ew, axis=-1, keepdims=True)
    o_win = _dot(pw.reshape(H * tq, span).astype(BF16), vwb)

    gs = jax.nn.sigmoid(gate_ref[0])
    for g_static in range(NSA_KV_GROUPS):
        @pl.when(g == g_static)
        def _(g_static=g_static):
            for h in range(H):
                c0 = g_static * H * 3 + h * 3
                rs = slice(h * tq, (h + 1) * tq)
                o = (gs[:, c0:c0 + 1] * o_cmp[rs] + gs[:, c0 + 1:c0 + 2] * o_sel[rs]
                     + gs[:, c0 + 2:c0 + 3] * o_win[rs])
                o_ref[0, :, h * NSA_DH:(h + 1) * NSA_DH] = o.astype(o_ref.dtype)


def _nsa_attn(q_r, kcmp, vcmp, ks_r, vs_r, kw_r, vw_r, proj3, ov):
    B, _, S, _ = q_r.shape
    G, H = NSA_KV_GROUPS, NSA_HPG
    NR = kcmp.shape[2]
    tq = 256
    n_sel = S // SEL_LEN
    n_top = min(SEL_TOPK, n_sel)
    cmp_spec = pl.BlockSpec((1, 1, NR, NSA_DH), lambda b, g, i: (b, g, 0, 0))
    kv_spec = pl.BlockSpec((1, 1, S, NSA_DH), lambda b, g, i: (b, g, 0, 0))
    goff = _COLS["ng"][0] // 128
    return pl.pallas_call(
        functools.partial(_nsa_attn_kernel, tq=tq, S=S, n_sel=n_sel, n_top=n_top),
        grid=(B, G, S // tq),
        in_specs=[pl.BlockSpec((1, H, tq, NSA_DH), lambda b, g, i: (b, g, i, 0)),
                  cmp_spec, cmp_spec, kv_spec, kv_spec, kv_spec, kv_spec,
                  pl.BlockSpec((1, tq, 128), lambda b, g, i: (b, i, goff)),
                  pl.BlockSpec(ov.shape, lambda b, g, i: (0, 0))],
        out_specs=pl.BlockSpec((1, tq, H * NSA_DH), lambda b, g, i: (b, i, g)),
        out_shape=jax.ShapeDtypeStruct((B, S, NSA_HEADS * NSA_DH), BF16),
        compiler_params=_cparams("parallel", "parallel", "arbitrary"),
        name="nsa_attn",
    )(q_r, kcmp, vcmp, ks_r, vs_r, kw_r, vw_r, proj3, ov)


def _layer_norm(x, g, b):
    mu = jnp.mean(x, axis=-1, keepdims=True)
    xc = x - mu
    var = jnp.mean(xc * xc, axis=-1, keepdims=True)
    return xc * lax.rsqrt(var + LN_EPS) * g + b


SUB = D_MODEL // 2 // 128
ROWS8 = 8


def _pack_bf16_pairs(v):
    half = v.shape[1] // 2
    bits = pltpu.bitcast(v.astype(BF16).astype(F32), jnp.uint32)
    return pltpu.bitcast((bits[:, :half] >> 16) | bits[:, half:], jnp.int32)


def _unpack_bf16_pairs(p):
    u = pltpu.bitcast(p, jnp.uint32)
    return pltpu.bitcast(u << 16, F32), pltpu.bitcast(u & jnp.uint32(0xFFFF0000), F32)


def _tile_rows(t):
    return (t // ROWS8) * (SUB * ROWS8) + t % ROWS8


def _to_tiles(ref, val):
    rows = val.shape[0]
    for c in range(SUB):
        ref[:, c] = val[:, c * 128:(c + 1) * 128].reshape(rows // ROWS8, ROWS8, 128)


def _from_tiles(ref, lead=()):
    groups = ref.shape[len(lead)]
    return jnp.concatenate([ref[lead + (slice(None), c)].reshape(groups * ROWS8, 128) for c in range(SUB)], axis=1)


def _outproj_kernel(x_ref, og_ref, on_ref, w_ref, g_ref, b_ref, o_ref, ot_ref):
    half = og_ref.shape[1]
    mix = _dot(og_ref[...], w_ref[0:half, :]) + _dot(on_ref[...], w_ref[half:, :])
    h = _layer_norm(DN_ALPHA * x_ref[...] + mix, g_ref[...], b_ref[...])
    o_ref[...] = h
    _to_tiles(ot_ref, _pack_bf16_pairs(h))


def _outproj(x2, og2, on2, w, g, b):
    T, D = x2.shape
    tm = 512
    row = lambda width: pl.BlockSpec((tm, width), lambda i: (i, 0))
    full = lambda a: pl.BlockSpec(a.shape, lambda i: (0, 0))
    return pl.pallas_call(
        _outproj_kernel,
        grid=(T // tm,),
        in_specs=[row(D), row(og2.shape[1]), row(on2.shape[1]), full(w), full(g), full(b)],
        out_specs=[row(D), pl.BlockSpec((tm // ROWS8, SUB, ROWS8, 128), lambda i: (i, 0, 0, 0))],
        out_shape=[jax.ShapeDtypeStruct((T, D), F32),
                   jax.ShapeDtypeStruct((T // ROWS8, SUB, ROWS8, 128), jnp.int32)],
        compiler_params=_cparams("parallel"),
        name="outproj_ln",
    )(x2, og2, on2, w, g, b)


def _router_kernel(h_ref, wh_ref, wl_ref, bias_ref, eidx_ref, wts_ref, rank_ref, cnt_ref, carry_ref):
    @pl.when(pl.program_id(0) == 0)
    def _():
        carry_ref[...] = jnp.zeros_like(carry_ref)

    h = h_ref[...]
    tm = h.shape[0]
    E = N_EXPERTS
    h_hi = h.astype(BF16)
    h_lo = (h - h_hi.astype(F32)).astype(BF16)
    wh = wh_ref[...]
    logits = _dot(h_hi, wh) + _dot(h_lo, wh) + _dot(h_hi, wl_ref[...])
    scores = jax.nn.sigmoid(logits)
    biased = scores + bias_ref[...]
    lane_i = lax.broadcasted_iota(jnp.int32, (tm, E), 1)
    gid = lane_i // (E // N_GROUPS)
    lane = lane_i.astype(F32)
    ninf = -jnp.inf

    def row_max(x):
        return jnp.max(x, axis=-1, keepdims=True)

    def first_idx(x, mx):
        return jnp.min(jnp.where(x == mx, lane, float(E)), axis=-1, keepdims=True)

    gscore = []
    for gi in range(N_GROUPS):
        mg = jnp.where(gid == gi, biased, ninf)
        m1 = row_max(mg)
        i1 = first_idx(mg, m1)
        m2 = row_max(jnp.where(lane == i1, ninf, mg))
        gscore.append(m1 + m2)
    emask = jnp.zeros((tm, E), jnp.bool_)
    for gi in range(N_GROUPS):
        rk = jnp.zeros((tm, 1), jnp.int32)
        for gj in range(N_GROUPS):
            if gj == gi:
                continue
            beats = (gscore[gj] > gscore[gi]) | ((gscore[gj] == gscore[gi]) & (gj < gi))
            rk = rk + beats.astype(jnp.int32)
        emask = emask | ((gid == gi) & (rk < TOPK_GROUPS))
    masked = jnp.where(emask, biased, ninf)

    onehots, wsel = [], []
    selm = jnp.zeros((tm, E), F32)
    for k in range(TOP_K):
        mx = row_max(masked)
        idx = first_idx(masked, mx)
        oh = lane == idx
        onehots.append(oh)
        wsel.append(jnp.sum(jnp.where(oh, scores, 0.0), axis=-1, keepdims=True))
        masked = jnp.where(oh, ninf, masked)
        selm = jnp.where(oh, 1.0, selm)
        eidx_ref[:, k:k + 1] = idx.astype(jnp.int32)
    wsum = wsel[0]
    for k in range(1, TOP_K):
        wsum = wsum + wsel[k]
    for k in range(TOP_K):
        wts_ref[:, k:k + 1] = wsel[k] / wsum * ROUTED_SCALE

    ri = lax.broadcasted_iota(jnp.int32, (tm, tm), 0)
    ci = lax.broadcasted_iota(jnp.int32, (tm, tm), 1)
    ltri = jnp.where(ri > ci, 1.0, 0.0).astype(BF16)
    cum = _dot(ltri, selm.astype(BF16)) + carry_ref[...]
    for k in range(TOP_K):
        rk = jnp.sum(jnp.where(onehots[k], cum, 0.0), axis=-1, keepdims=True)
        rank_ref[:, k:k + 1] = rk.astype(jnp.int32)
    total = carry_ref[...] + jnp.sum(selm, axis=0, keepdims=True)
    carry_ref[...] = total
    cnt_ref[...] = total


def _router(h2, w_hi, w_lo, bias):
    T, D = h2.shape
    tm = 256
    full = lambda a: pl.BlockSpec(a.shape, lambda i: (0, 0))
    o8 = pl.BlockSpec((tm, TOP_K), lambda i: (i, 0))
    return pl.pallas_call(
        _router_kernel,
        grid=(T // tm,),
        in_specs=[pl.BlockSpec((tm, D), lambda i: (i, 0)), full(w_hi), full(w_lo), full(bias)],
        out_specs=[o8, o8, o8, pl.BlockSpec((1, N_EXPERTS), lambda i: (0, 0))],
        out_shape=[jax.ShapeDtypeStruct((T, TOP_K), jnp.int32),
                   jax.ShapeDtypeStruct((T, TOP_K), F32),
                   jax.ShapeDtypeStruct((T, TOP_K), jnp.int32),
                   jax.ShapeDtypeStruct((1, N_EXPERTS), F32)],
        scratch_shapes=[pltpu.VMEM((1, N_EXPERTS), F32)],
        compiler_params=_cparams("arbitrary"),
        name="router",
    )(h2, w_hi, w_lo, bias)


def _dest_kernel(eidx_ref, rank_ref, ps_ref, dest_ref):
    tm = eidx_ref.shape[0]
    lane = lax.broadcasted_iota(jnp.int32, (tm, N_EXPERTS), 1)
    ps = ps_ref[...]
    for k in range(TOP_K):
        start = jnp.sum(jnp.where(lane == eidx_ref[:, k:k + 1], ps, 0.0), axis=-1, keepdims=True)
        dest_ref[:, k:k + 1] = start.astype(jnp.int32) + rank_ref[:, k:k + 1]


def _dest(eidx, rank, pad_start_f):
    T = eidx.shape[0]
    tm = 1024 if T % 1024 == 0 else T
    o8 = pl.BlockSpec((tm, TOP_K), lambda i: (i, 0))
    return pl.pallas_call(
        _dest_kernel,
        grid=(T // tm,),
        in_specs=[o8, o8, pl.BlockSpec((1, N_EXPERTS), lambda i: (0, 0))],
        out_specs=o8,
        out_shape=jax.ShapeDtypeStruct((T, TOP_K), jnp.int32),
        compiler_params=_cparams("parallel"),
        name="dest",
    )(eidx, rank, pad_start_f)


SC_WINDOW = 128


def _sc_gather(table, idx):
    _, lanes = table.shape
    n = idx.shape[0]
    mesh = plsc.VectorSubcoreMesh(core_axis_name="core", subcore_axis_name="subcore")

    @functools.partial(pl.kernel, out_type=jax.ShapeDtypeStruct((n, lanes), table.dtype), mesh=mesh,
                       name="sc_row_gather")
    def gather(x_hbm, i_hbm, o_hbm):
        def body(i_vmem, o_vmem):
            pltpu.sync_copy(x_hbm.at[i_vmem.at[0]], o_vmem)

        pltpu.emit_pipeline(
            body,
            grid=(n // SC_WINDOW,),
            in_specs=[pl.BlockSpec((1, SC_WINDOW), lambda i: (0, i))],
            out_specs=[pl.BlockSpec((SC_WINDOW, lanes), lambda i: (i, 0))],
            core_axis_name=("core", "subcore"),
            dimension_semantics=(pltpu.PARALLEL,),
            trace_scopes=False,
        )(i_hbm, o_hbm)

    return gather(table, idx.reshape(1, n))


def _sc_scatter(src, idx, n_out):
    rows, lanes = src.shape
    n_idx = idx.shape[0]
    mesh = plsc.VectorSubcoreMesh(core_axis_name="core", subcore_axis_name="subcore")

    @functools.partial(pl.kernel, out_type=jax.ShapeDtypeStruct((n_out, lanes), src.dtype), mesh=mesh,
                       name="sc_row_scatter")
    def scatter(x_hbm, i_hbm, o_hbm):
        def body(x_vmem, *i_vmems):
            for i_vmem in i_vmems:
                pltpu.sync_copy(x_vmem, o_hbm.at[i_vmem.at[0]])

        pltpu.emit_pipeline(
            body,
            grid=(rows // SC_WINDOW,),
            in_specs=[pl.BlockSpec((SC_WINDOW, lanes), lambda i: (i, 0))]
                     + [pl.BlockSpec((1, SC_WINDOW), lambda i, j=j: (j, i)) for j in range(n_idx)],
            out_specs=[],
            core_axis_name=("core", "subcore"),
            dimension_semantics=(pltpu.PARALLEL,),
            trace_scopes=False,
        )(x_hbm, *([i_hbm] * n_idx))

    return scatter(src, idx)


def _expert_kernel(bexp_ref, nused_ref, x_ref, wg_ref, wu_ref, wd_ref, y_ref):
    del bexp_ref

    @pl.when(pl.program_id(0) < nused_ref[0])
    def _():
        x = jnp.concatenate(_unpack_bf16_pairs(_from_tiles(x_ref)), axis=1).astype(BF16)
        gate = _dot(x, wg_ref[0].astype(BF16))
        up = _dot(x, wu_ref[0].astype(BF16))
        act = (gate * jax.nn.sigmoid(gate) * up).astype(BF16)
        _to_tiles(y_ref, _pack_bf16_pairs(_dot(act, wd_ref[0].astype(BF16))))


def _experts(blk_exp, n_used, xs_t, wg, wu, wd):
    NP = xs_t.shape[0] * ROWS8
    D = D_MODEL
    nb = NP // MOE_BLOCK
    blk = (MOE_BLOCK // ROWS8, SUB, ROWS8, 128)

    def xmap(i, bexp, nused):
        return (jnp.minimum(i, nused[0] - 1), 0, 0, 0)

    def wmap(i, bexp, nused):
        return (bexp[jnp.minimum(i, nused[0] - 1)], 0, 0)

    grid_spec = pltpu.PrefetchScalarGridSpec(
        num_scalar_prefetch=2,
        grid=(nb,),
        in_specs=[pl.BlockSpec(blk, xmap),
                  pl.BlockSpec((1, D, D_EXPERT), wmap),
                  pl.BlockSpec((1, D, D_EXPERT), wmap),
                  pl.BlockSpec((1, D_EXPERT, D), wmap)],
        out_specs=pl.BlockSpec(blk, xmap),
    )
    return pl.pallas_call(
        _expert_kernel,
        grid_spec=grid_spec,
        out_shape=jax.ShapeDtypeStruct(xs_t.shape, jnp.int32),
        compiler_params=_cparams("arbitrary"),
        name="experts",
    )(blk_exp, n_used, xs_t, wg, wu, wd)


def _combine_kernel(h_ref, wts_ref, yg_ref, wsg_ref, wsu_ref, wsd_ref, g_ref, b_ref, o_ref):
    h = h_ref[...]
    hb = h.astype(BF16)
    gate = _dot(hb, wsg_ref[...])
    up = _dot(hb, wsu_ref[...])
    shared = _dot((gate * jax.nn.sigmoid(gate) * up).astype(BF16), wsd_ref[...])
    wts = wts_ref[...]
    lo, hi = _unpack_bf16_pairs(_from_tiles(yg_ref, (0,)))
    r_lo, r_hi = wts[:, 0:1] * lo, wts[:, 0:1] * hi
    for k in range(1, TOP_K):
        lo, hi = _unpack_bf16_pairs(_from_tiles(yg_ref, (k,)))
        r_lo, r_hi = r_lo + wts[:, k:k + 1] * lo, r_hi + wts[:, k:k + 1] * hi
    routed = jnp.concatenate([r_lo, r_hi], axis=1)
    o_ref[...] = _layer_norm(DN_ALPHA * h + (routed + shared), g_ref[...], b_ref[...])


def _combine(h2, wts, yg_t, wsg, wsu, wsd, g, b):
    T, D = h2.shape
    tm = 128
    full = lambda a: pl.BlockSpec(a.shape, lambda i: (0, 0))
    return pl.pallas_call(
        _combine_kernel,
        grid=(T // tm,),
        in_specs=[pl.BlockSpec((tm, D), lambda i: (i, 0)),
                  pl.BlockSpec((tm, TOP_K), lambda i: (i, 0)),
                  pl.BlockSpec((TOP_K, tm // ROWS8, SUB, ROWS8, 128), lambda i: (0, i, 0, 0, 0)),
                  full(wsg), full(wsu), full(wsd), full(g), full(b)],
        out_specs=pl.BlockSpec((tm, D), lambda i: (i, 0)),
        out_shape=jax.ShapeDtypeStruct((T, D), F32),
        compiler_params=_cparams("parallel"),
        name="combine_ln",
    )(h2, wts, yg_t, wsg, wsu, wsd, g, b)


def _regroup_w_in(w_in):
    parts, off = {}, 0
    for name, width in _SPLITS:
        parts[name] = w_in[:, off:off + width]
        off += width
    cols = []
    for name, (_, width) in _COLS.items():
        p = parts[name]
        if p.shape[1] < width:
            p = jnp.pad(p, ((0, 0), (0, width - p.shape[1])))
        cols.append(p)
    return jnp.concatenate(cols, axis=1).astype(BF16)


def _overlap_matrix(S):
    nr = S // CMP_STRIDE
    n_sel = S // SEL_LEN
    ci = np.arange(nr)[:, None] * CMP_STRIDE
    sj = np.arange(n_sel)[None, :] * SEL_LEN
    ov = np.clip(np.minimum(ci + CMP_LEN, sj + SEL_LEN) - np.maximum(ci, sj), 0, None) / CMP_LEN
    ov[nr - 1] = 0.0
    return jnp.asarray(ov.T, BF16)


def _mixers(h, positions, w_in, w_alpha2, b_alpha, gla_norm_g,
            cmp_pos_k, cmp_w1_k, cmp_b1_k, cmp_w2_k, cmp_pos_v, cmp_w1_v, cmp_b1_v, cmp_w2_v):
    B, S, D = h.shape
    proj = _proj(h.reshape(B * S, D), _regroup_w_in(w_in)).reshape(B, S, D_PROJ)

    wa = jnp.pad(w_alpha2, ((0, 128 - GLA_LOWRANK), (0, 0))).astype(BF16)
    o_gla = _gla(proj, wa, b_alpha.reshape(1, -1), gla_norm_g.reshape(1, -1))

    half = NSA_DH // 2
    inv = ROPE_THETA ** (-np.arange(half, dtype=np.float32) / half)
    inv_row = jnp.asarray(np.tile(inv, 128 // half).reshape(1, 128), F32)
    pos3 = positions.astype(F32).reshape(B, S, 1)
    q_r, kc_r, ks_r, vs_r, kw_r, vw_r = _nsa_prep(proj, pos3, inv_row)

    def blocks16(t):
        return (t.reshape(B, S // CMP_STRIDE, CMP_STRIDE, NSA_KV_GROUPS, NSA_DH)
                .transpose(0, 3, 1, 2, 4).reshape(B, NSA_KV_GROUPS, S // CMP_STRIDE, CMP_STRIDE * NSA_DH))

    vc_off = _COLS["nvc"][0]
    kcmp, vcmp = _compress(
        blocks16(kc_r), blocks16(proj[:, :, vc_off:vc_off + 128]),
        cmp_pos_k.reshape(2, -1), cmp_w1_k.astype(BF16), cmp_b1_k.reshape(1, -1), cmp_w2_k.astype(BF16),
        cmp_pos_v.reshape(2, -1), cmp_w1_v.astype(BF16), cmp_b1_v.reshape(1, -1), cmp_w2_v.astype(BF16))
    o_nsa = _nsa_attn(q_r, kcmp, vcmp, ks_r, vs_r, kw_r, vw_r, proj, _overlap_matrix(S))
    return o_gla, o_nsa


def _moe_ln(h2, h_t, w_router, router_bias, w_gate, w_up, w_down, ws_gate, ws_up, ws_down, ln_g, ln_b):
    T, D = h2.shape
    P = T * TOP_K
    w_hi = w_router.astype(BF16)
    w_lo = (w_router - w_hi.astype(F32)).astype(BF16)
    eidx, wts, rank, counts = _router(h2, w_hi, w_lo, router_bias.reshape(1, -1))

    counts = counts.reshape(-1).astype(jnp.int32)
    padded = (counts + MOE_BLOCK - 1) // MOE_BLOCK * MOE_BLOCK
    pad_end = jnp.cumsum(padded)
    pad_start = pad_end - padded
    nb = -(-P // MOE_BLOCK) + N_EXPERTS
    n_used = (pad_end[-1] // MOE_BLOCK).astype(jnp.int32).reshape(1)
    blk_start = jnp.arange(nb, dtype=jnp.int32) * MOE_BLOCK
    blk_exp = jnp.minimum(jnp.sum((pad_end[None, :] <= blk_start[:, None]).astype(jnp.int32), axis=1),
                          N_EXPERTS - 1)

    NP = nb * MOE_BLOCK
    dest = _dest(eidx, rank, pad_start.astype(F32).reshape(1, -1))
    col = (jnp.arange(SUB, dtype=jnp.int32) * ROWS8)

    n_pad = NP - P
    assert n_pad % T == 0
    pad_cnt = padded - counts
    pad_hi = jnp.cumsum(pad_cnt)
    pad_lo = pad_hi - pad_cnt
    j = jnp.arange(n_pad, dtype=jnp.int32)[:, None]
    owner = (pad_lo[None, :] <= j) & (j < pad_hi[None, :])
    in_expert = jnp.sum(jnp.where(owner, (pad_start + counts - pad_lo)[None, :] + j, 0), axis=1)
    pad_rows = jnp.where(j[:, 0] < pad_hi[-1], in_expert, pad_end[-1] + j[:, 0] - pad_hi[-1])

    rows_all = jnp.concatenate([dest.T, pad_rows.reshape(n_pad // T, T)], axis=0)
    dst = _tile_rows(rows_all).reshape(-1, T // ROWS8, 1, ROWS8) + col[None, None, :, None]
    xs_t = _sc_scatter(h_t.reshape(T * SUB, 128), dst.reshape(-1, T * SUB), NP * SUB)
    xs_t = xs_t.reshape(NP // ROWS8, SUB, ROWS8, 128)
    ys_t = _experts(blk_exp, n_used, xs_t, w_gate, w_up, w_down)
    src = _tile_rows(dest.T).reshape(TOP_K, T // ROWS8, 1, ROWS8) + col[None, None, :, None]
    yg_t = _sc_gather(ys_t.reshape(NP * SUB, 128), src.reshape(-1)).reshape(TOP_K, T // ROWS8, SUB, ROWS8, 128)
    return _combine(h2, wts, yg_t, ws_gate.astype(BF16), ws_up.astype(BF16), ws_down.astype(BF16),
                    ln_g.reshape(1, -1), ln_b.reshape(1, -1))


def kernel(x, positions, w_in, w_alpha2, b_alpha, gla_norm_g, cmp_pos_k, cmp_w1_k, cmp_b1_k, cmp_w2_k, cmp_pos_v, cmp_w1_v, cmp_b1_v, cmp_w2_v, w_out, ln1_g, ln1_b, w_router, router_bias, w_exp_gate, w_exp_up, w_exp_down, w_sh_gate, w_sh_up, w_sh_down, ln2_g, ln2_b):
    def layer(h, pos, l):
        B, S, D = h.shape
        o_gla, o_nsa = _mixers(h, pos, w_in[l], w_alpha2[l], b_alpha[l], gla_norm_g[l],
                               cmp_pos_k[l], cmp_w1_k[l], cmp_b1_k[l], cmp_w2_k[l],
                               cmp_pos_v[l], cmp_w1_v[l], cmp_b1_v[l], cmp_w2_v[l])
        h1, h1_t = _outproj(h.reshape(B * S, D), o_gla.reshape(B * S, -1), o_nsa.reshape(B * S, -1),
                            w_out[l].astype(BF16), ln1_g[l].reshape(1, -1), ln1_b[l].reshape(1, -1))
        h2 = _moe_ln(h1, h1_t, w_router[l], router_bias[l], w_exp_gate[l], w_exp_up[l], w_exp_down[l],
                     w_sh_gate[l], w_sh_up[l], w_sh_down[l], ln2_g[l], ln2_b[l])
        return h2.reshape(B, S, D)

    n_groups = BATCH_GROUPS if x.shape[0] % BATCH_GROUPS == 0 else 1
    hs = jnp.split(x, n_groups, axis=0)
    ps = jnp.split(positions, n_groups, axis=0)
    for l in range(w_in.shape[0]):
        hs = [layer(h, p, l) for h, p in zip(hs, ps)]
    return jnp.concatenate(hs, axis=0)
```

```python
import functools

import numpy as np
import jax
import jax.numpy as jnp
from jax import lax
from jax.experimental import pallas as pl
from jax.experimental.pallas import tpu as pltpu
from jax.experimental.pallas import tpu_sc as plsc

D_MODEL = 1024
GLA_HEADS = 4
GLA_DV = 128
GLA_DK = 64
GLA_LOWRANK = 16
GLA_TAU = 16.0
GLA_CHUNK = 64
NSA_HEADS = 8
NSA_KV_GROUPS = 2
NSA_HPG = 4
NSA_DH = 64
CMP_LEN = 32
CMP_STRIDE = 16
CMP_HIDDEN = 256
SEL_LEN = 64
SEL_TOPK = 16
WINDOW = 512
ROPE_THETA = 10000.0
N_EXPERTS = 256
TOP_K = 8
N_GROUPS = 8
TOPK_GROUPS = 4
D_EXPERT = 256
ROUTED_SCALE = 2.5
DEPTH = 1
DN_ALPHA = (2.0 * DEPTH) ** 0.25
LN_EPS = 1e-5

MOE_BLOCK = 256
BATCH_GROUPS = 2
NEG = -1e30
F32 = jnp.float32
BF16 = jnp.bfloat16

_COLS = {}
_off = 0
for _name, _w in (("gq", 256), ("gk", 256), ("gv", 512), ("gr", 512), ("nq", 512),
                  ("nkc", 128), ("nvc", 128), ("nks", 128), ("nvs", 128), ("nkw", 128), ("nvw", 128),
                  ("ga", 128), ("ng", 128)):
    _COLS[_name] = (_off, _w)
    _off += _w
D_PROJ = _off
_SPLITS = (("gq", 256), ("gk", 256), ("gv", 512), ("ga", 16), ("gr", 512), ("nq", 512),
           ("nkc", 128), ("nvc", 128), ("nks", 128), ("nvs", 128), ("nkw", 128), ("nvw", 128), ("ng", 24))

VMEM_LIMIT = 56 * 1024 * 1024


def _cparams(*sem):
    return pltpu.CompilerParams(dimension_semantics=sem, vmem_limit_bytes=VMEM_LIMIT)


def _dot(a, b):
    return jnp.dot(a, b, preferred_element_type=F32)


def _dot_nt(a, b):
    return lax.dot_general(a, b, (((1,), (1,)), ((), ())), preferred_element_type=F32)


def _split3(x):
    hi = x.astype(BF16)
    r1 = x - hi.astype(F32)
    mid = r1.astype(BF16)
    lo = (r1 - mid.astype(F32)).astype(BF16)
    return hi, mid, lo


def _proj_kernel(x_ref, w_ref, o_ref):
    o_ref[...] = _dot(x_ref[...].astype(BF16), w_ref[...])


def _proj(x2, w):
    T, D = x2.shape
    N = w.shape[1]
    tm, tn = 512, 1024
    return pl.pallas_call(
        _proj_kernel,
        grid=(T // tm, N // tn),
        in_specs=[pl.BlockSpec((tm, D), lambda i, j: (i, 0)),
                  pl.BlockSpec((D, tn), lambda i, j: (0, j))],
        out_specs=pl.BlockSpec((tm, tn), lambda i, j: (i, j)),
        out_shape=jax.ShapeDtypeStruct((T, N), F32),
        compiler_params=_cparams("parallel", "arbitrary"),
        name="proj",
    )(x2, w)


def _gla_kernel(q_ref, k_ref, v_ref, r_ref, a_ref, wa_ref, ba_ref, ng_ref, o_ref, st_ref, *, nchunk):
    C = GLA_CHUNK
    HK = GLA_HEADS * GLA_DK

    @pl.when(pl.program_id(1) == 0)
    def _():
        st_ref[...] = jnp.zeros_like(st_ref)

    ri = lax.broadcasted_iota(jnp.int32, (C, C), 0)
    ci = lax.broadcasted_iota(jnp.int32, (C, C), 1)
    causal = ri >= ci
    tri = jnp.where(causal, 1.0, 0.0).astype(BF16)
    lane_head = lax.broadcasted_iota(jnp.int32, (1, HK), 1) // GLA_DK
    wa = wa_ref[...]
    ba = ba_ref[...]
    ng = ng_ref[...]

    def chunk(c, carry):
        rows = pl.ds(pl.multiple_of(c * C, C), C)
        q = q_ref[0, rows, :]
        k = k_ref[0, rows, :]
        v = v_ref[0, rows, :]
        r = r_ref[0, rows, :]
        a = a_ref[0, rows, :]
        z = _dot(a.astype(BF16), wa) + ba
        g = (jnp.minimum(z, 0.0) - jnp.log1p(jnp.exp(-jnp.abs(z)))) * (1.0 / GLA_TAU)
        g_hi, g_mid, g_lo = _split3(g)
        b = _dot(tri, g_hi) + _dot(tri, g_mid) + _dot(tri, g_lo)
        b_last = b[C - 1:C, :]
        qt = q * jnp.exp(b) * (GLA_DK ** -0.5)
        kt = (k * jnp.exp(-b)).astype(BF16)
        ks = (k * jnp.exp(b_last - b)).astype(BF16)
        st = st_ref[...]
        st_b = st.astype(BF16)
        new_st = st * jnp.exp(b_last)
        for h in range(GLA_HEADS):
            hm = lane_head == h
            qh = jnp.where(hm, qt, 0.0).astype(BF16)
            att = jnp.where(causal, _dot_nt(qh, kt), 0.0)
            vh = v[:, h * GLA_DV:(h + 1) * GLA_DV].astype(BF16)
            o = _dot_nt(qh, st_b) + _dot(att.astype(BF16), vh)
            new_st = new_st + jnp.where(hm, _dot(vh.T, ks), 0.0)
            o = o * lax.rsqrt(jnp.mean(o * o, axis=-1, keepdims=True) + 1e-6) * ng
            rh = r[:, h * GLA_DV:(h + 1) * GLA_DV]
            o = o * (rh * jax.nn.sigmoid(rh))
            o_ref[0, rows, h * GLA_DV:(h + 1) * GLA_DV] = o.astype(o_ref.dtype)
        st_ref[...] = new_st
        return carry

    lax.fori_loop(0, nchunk, chunk, 0)


def _gla(proj3, wa, ba, ng):
    B, S, _ = proj3.shape
    L = 512 if S % 512 == 0 else S
    nchunk = L // GLA_CHUNK

    def col(name, width):
        off = _COLS[name][0]
        assert off % width == 0
        return pl.BlockSpec((1, L, width), lambda b, j, o=off // width: (b, j, o))

    full2 = lambda shape: pl.BlockSpec(shape, lambda b, j: (0, 0))
    return pl.pallas_call(
        functools.partial(_gla_kernel, nchunk=nchunk),
        grid=(B, S // L),
        in_specs=[col("gq", 256), col("gk", 256), col("gv", 512), col("gr", 512), col("ga", 128),
                  full2((128, 256)), full2((1, 256)), full2((1, 128))],
        out_specs=pl.BlockSpec((1, L, 512), lambda b, j: (b, j, 0)),
        out_shape=jax.ShapeDtypeStruct((B, S, 512), BF16),
        scratch_shapes=[pltpu.VMEM((GLA_DV, GLA_HEADS * GLA_DK), F32)],
        compiler_params=_cparams("parallel", "arbitrary"),
        name="gla",
    )(proj3, proj3, proj3, proj3, proj3, wa, ba, ng)


def _rot_half(x):
    n = x.shape[-1]
    lane = lax.broadcasted_iota(jnp.int32, (1, n), 1)
    first = (lane % NSA_DH) < (NSA_DH // 2)
    return jnp.where(first, -pltpu.roll(x, n - NSA_DH // 2, 1), pltpu.roll(x, NSA_DH // 2, 1))


def _nsa_prep_kernel(pos_ref, inv_ref, q_ref, kc_ref, ks_ref, vs_ref, kw_ref, vw_ref,
                     qo_ref, kco_ref, kso_ref, vso_ref, kwo_ref, vwo_ref):
    ang = pos_ref[0] * inv_ref[...]
    cos = jnp.cos(ang)
    sin = jnp.sin(ang)
    cos4 = jnp.concatenate([cos] * 4, axis=1)
    sin4 = jnp.concatenate([sin] * 4, axis=1)

    q = q_ref[0]
    qr = q * cos4 + _rot_half(q) * sin4
    for h in range(NSA_HEADS):
        qo_ref[0, h] = (qr[:, h * NSA_DH:(h + 1) * NSA_DH] * (NSA_DH ** -0.5)).astype(qo_ref.dtype)

    def rope128(x):
        return x * cos + _rot_half(x) * sin

    kco_ref[0] = rope128(kc_ref[0])
    ksr = rope128(ks_ref[0])
    kwr = rope128(kw_ref[0])
    vs = vs_ref[0]
    vw = vw_ref[0]
    ones_col = jnp.where(lax.broadcasted_iota(jnp.int32, (vs.shape[0], NSA_DH), 1) == 0, 1.0, 0.0)
    for g in range(NSA_KV_GROUPS):
        sl = slice(g * NSA_DH, (g + 1) * NSA_DH)
        kso_ref[0, g] = ksr[:, sl].astype(kso_ref.dtype)
        kwo_ref[0, g] = kwr[:, sl].astype(kwo_ref.dtype)
        vso_ref[0, g] = jnp.concatenate([vs[:, sl], ones_col], axis=1).astype(vso_ref.dtype)
        vwo_ref[0, g] = jnp.concatenate([vw[:, sl], ones_col], axis=1).astype(vwo_ref.dtype)


def _nsa_prep(proj3, pos3, inv_row):
    B, S, _ = proj3.shape
    ts = 512 if S % 512 == 0 else S

    def col(name, width):
        off = _COLS[name][0]
        assert off % width == 0
        return pl.BlockSpec((1, ts, width), lambda b, j, o=off // width: (b, j, o))

    kv_spec = pl.BlockSpec((1, NSA_KV_GROUPS, ts, NSA_DH), lambda b, j: (b, 0, j, 0))
    kv_shape = jax.ShapeDtypeStruct((B, NSA_KV_GROUPS, S, NSA_DH), BF16)
    vx_spec = pl.BlockSpec((1, NSA_KV_GROUPS, ts, 2 * NSA_DH), lambda b, j: (b, 0, j, 0))
    vx_shape = jax.ShapeDtypeStruct((B, NSA_KV_GROUPS, S, 2 * NSA_DH), BF16)
    return pl.pallas_call(
        _nsa_prep_kernel,
        grid=(B, S // ts),
        in_specs=[pl.BlockSpec((1, ts, 1), lambda b, j: (b, j, 0)),
                  pl.BlockSpec((1, 128), lambda b, j: (0, 0)),
                  col("nq", 512), col("nkc", 128), col("nks", 128), col("nvs", 128),
                  col("nkw", 128), col("nvw", 128)],
        out_specs=[pl.BlockSpec((1, NSA_HEADS, ts, NSA_DH), lambda b, j: (b, 0, j, 0)),
                   pl.BlockSpec((1, ts, 128), lambda b, j: (b, j, 0)),
                   kv_spec, vx_spec, kv_spec, vx_spec],
        out_shape=[jax.ShapeDtypeStruct((B, NSA_HEADS, S, NSA_DH), BF16),
                   jax.ShapeDtypeStruct((B, S, 128), F32),
                   kv_shape, vx_shape, kv_shape, vx_shape],
        compiler_params=_cparams("parallel", "parallel"),
        name="nsa_prep",
    )(pos3, inv_row, proj3, proj3, proj3, proj3, proj3, proj3)


def _compress_kernel(k_ref, v_ref, pk_ref, w1k_ref, b1k_ref, w2k_ref, pv_ref, w1v_ref, b1v_ref, w2v_ref,
                     ko_ref, vo_ref):
    half = CMP_STRIDE * NSA_DH

    def run(x_ref, p_ref, w1_ref, b1_ref, w2_ref, o_ref):
        p = p_ref[...]
        for g in range(NSA_KV_GROUPS):
            x = x_ref[0, g]
            nr = x.shape[0]
            a = _dot((x + p[0:1]).astype(BF16), w1_ref[0:half, :])
            bb = _dot((x + p[1:2]).astype(BF16), w1_ref[half:2 * half, :])
            pre = a + pltpu.roll(bb, nr - 1, 0) + b1_ref[...]
            hid = jax.nn.gelu(pre)
            out = _dot(hid.astype(BF16), w2_ref[...])
            row = lax.broadcasted_iota(jnp.int32, out.shape, 0)
            o_ref[0, g] = jnp.where(row < nr - 1, out, 0.0)

    run(k_ref, pk_ref, w1k_ref, b1k_ref, w2k_ref, ko_ref)
    run(v_ref, pv_ref, w1v_ref, b1v_ref, w2v_ref, vo_ref)


def _compress(kc4, vc4, pk, w1k, b1k, w2k, pv, w1v, b1v, w2v):
    B, G, NR, W = kc4.shape
    x_spec = pl.BlockSpec((1, G, NR, W), lambda b: (b, 0, 0, 0))
    full = lambda a: pl.BlockSpec(a.shape, lambda b: (0,) * a.ndim)
    o_spec = pl.BlockSpec((1, G, NR, NSA_DH), lambda b: (b, 0, 0, 0))
    o_shape = jax.ShapeDtypeStruct((B, G, NR, NSA_DH), F32)
    params = (pk, w1k, b1k, w2k, pv, w1v, b1v, w2v)
    return pl.pallas_call(
        _compress_kernel,
        grid=(B,),
        in_specs=[x_spec, x_spec] + [full(a) for a in params],
        out_specs=[o_spec, o_spec],
        out_shape=[o_shape, o_shape],
        compiler_params=_cparams("parallel"),
        name="compress",
    )(kc4, vc4, *params)


def _nsa_attn_kernel(q_ref, kc_ref, vc_ref, ks_ref, vs_ref, kw_ref, vw_ref, gate_ref, ovt_ref, o_ref,
                     *, tq, S, n_sel, n_top):
    g = pl.program_id(1)
    qi = pl.program_id(2)
    start = qi * tq
    H = NSA_HPG
    qs = q_ref[0].reshape(H * tq, NSA_DH)
    t_col = start + lax.broadcasted_iota(jnp.int32, (tq, 1), 0)
    t_row = start + lax.broadcasted_iota(jnp.int32, (1, tq), 1)

    kc = kc_ref[0, 0].astype(BF16)
    vc = vc_ref[0, 0].astype(BF16)
    NR = kc.shape[0]
    s = _dot_nt(qs, kc)
    ncol = lax.broadcasted_iota(jnp.int32, (1, NR), 1)
    cmask = (ncol < NR - 1) & (ncol * CMP_STRIDE + (CMP_LEN - 1) <= t_col)
    s3 = jnp.where(cmask[None], s.reshape(H, tq, NR), NEG)
    m = jnp.max(s3, axis=-1, keepdims=True)
    e = jnp.exp(s3 - m)
    p3 = e / jnp.sum(e, axis=-1, keepdims=True)
    p3 = jnp.where(cmask[None], p3, 0.0)
    o_cmp = _dot(p3.reshape(H * tq, NR).astype(BF16), vc)

    psum = jnp.sum(p3, axis=0)
    p_hi = psum.astype(BF16)
    p_lo = (psum - p_hi.astype(F32)).astype(BF16)
    ovt = ovt_ref[...]
    imp = _dot_nt(ovt, p_hi) + _dot_nt(ovt, p_lo)
    blk = lax.broadcasted_iota(jnp.int32, (n_sel, 1), 0)
    cur = t_row // SEL_LEN
    causal_blk = blk <= cur
    forced = (blk == 0) | (blk == cur) | (blk == cur - 1)
    val = jnp.where(causal_blk, jnp.where(forced, jnp.inf, imp), -jnp.inf)
    rank = jnp.zeros((n_sel, tq), jnp.int32)
    for j in range(n_sel):
        vj = val[j:j + 1, :]
        beats = (vj > val) | ((vj == val) & (blk > j))
        rank = rank + beats.astype(jnp.int32)
    sel_t = jnp.where((rank < n_top) & causal_blk, 1.0, 0.0)
    sel_t = jnp.concatenate([sel_t, jnp.zeros((128 - n_sel, tq), F32)], axis=0)
    sel = sel_t.T.astype(BF16)

    tk = 256
    nsub = 2 if S % (2 * tk) == 0 else 1
    bpc = tk // SEL_LEN
    n_trips = (start + tq + nsub * tk - 1) // (nsub * tk)
    erow = lax.broadcasted_iota(jnp.int32, (128, tk), 0)
    ecol = lax.broadcasted_iota(jnp.int32, (128, tk), 1) // SEL_LEN
    kk = lax.broadcasted_iota(jnp.int32, (1, tk), 1)

    def sel_trip(c, carry):
        m_i, acc = carry
        scores, vals = [], []
        m_new = m_i
        for u in range(nsub):
            cu = c * nsub + u
            k0 = pl.multiple_of(cu * tk, tk)
            kb = ks_ref[0, 0, pl.ds(k0, tk), :]
            vals.append(vs_ref[0, 0, pl.ds(k0, tk), :])
            expand = jnp.where(erow == ecol + cu * bpc, 1.0, 0.0).astype(BF16)
            allowed = (_dot(sel, expand) > 0.5) & (k0 + kk <= t_col)
            bias = jnp.where(allowed, 0.0, NEG)
            sc3 = _dot_nt(qs, kb).reshape(H, tq, tk) + bias[None]
            scores.append(sc3)
            m_new = jnp.maximum(m_new, jnp.max(sc3, axis=-1, keepdims=True))
        acc = jnp.exp(m_i - m_new).reshape(H * tq, 1) * acc
        for u in range(nsub):
            pe = jnp.exp((scores[u] - m_new).astype(BF16))
            acc = acc + _dot(pe.reshape(H * tq, tk), vals[u])
        return m_new, acc

    m0 = jnp.full((H, tq, 1), NEG, F32)
    a0 = jnp.zeros((H * tq, 2 * NSA_DH), F32)
    _, acc_f = lax.fori_loop(0, n_trips, sel_trip, (m0, a0))
    o_sel = acc_f[:, :NSA_DH] / acc_f[:, NSA_DH:NSA_DH + 1]

    span = min(WINDOW + tq, S)
    ws = jnp.clip(start - WINDOW, 0, S - span)
    ws = pl.multiple_of(ws, tq)
    kwb = kw_ref[0, 0, pl.ds(ws, span), :]
    vwb = vw_ref[0, 0, pl.ds(ws, span), :]
    kp = ws + lax.broadcasted_iota(jnp.int32, (1, span), 1)
    wbias = jnp.where((kp <= t_col) & (kp > t_col - WINDOW), 0.0, NEG)
    sw3 = _dot_nt(qs, kwb).reshape(H, tq, span) + wbias[None]
    mw = jnp.max(sw3, axis=-1, keepdims=True)
    ew = jnp.exp((sw3 - mw).astype(BF16))
    aw = _dot(ew.reshape(H * tq, span), vwb)
    o_win = aw[:, :NSA_DH] / aw[:, NSA_DH:NSA_DH + 1]

    gs = jax.nn.sigmoid(gate_ref[0])
    for g_static in range(NSA_KV_GROUPS):
        @pl.when(g == g_static)
        def _(g_static=g_static):
            for h in range(H):
                c0 = g_static * H * 3 + h * 3
                rs = slice(h * tq, (h + 1) * tq)
                o = (gs[:, c0:c0 + 1] * o_cmp[rs] + gs[:, c0 + 1:c0 + 2] * o_sel[rs]
                     + gs[:, c0 + 2:c0 + 3] * o_win[rs])
                o_ref[0, :, h * NSA_DH:(h + 1) * NSA_DH] = o.astype(o_ref.dtype)


def _nsa_attn(q_r, kcmp, vcmp, ks_r, vs_r, kw_r, vw_r, proj3, ov):
    B, _, S, _ = q_r.shape
    G, H = NSA_KV_GROUPS, NSA_HPG
    NR = kcmp.shape[2]
    tq = 256
    n_sel = S // SEL_LEN
    n_top = min(SEL_TOPK, n_sel)
    cmp_spec = pl.BlockSpec((1, 1, NR, NSA_DH), lambda b, g, i: (b, g, 0, 0))
    kv_spec = pl.BlockSpec((1, 1, S, NSA_DH), lambda b, g, i: (b, g, 0, 0))
    vx_spec = pl.BlockSpec((1, 1, S, 2 * NSA_DH), lambda b, g, i: (b, g, 0, 0))
    goff = _COLS["ng"][0] // 128
    return pl.pallas_call(
        functools.partial(_nsa_attn_kernel, tq=tq, S=S, n_sel=n_sel, n_top=n_top),
        grid=(B, G, S // tq),
        in_specs=[pl.BlockSpec((1, H, tq, NSA_DH), lambda b, g, i: (b, g, i, 0)),
                  cmp_spec, cmp_spec, kv_spec, vx_spec, kv_spec, vx_spec,
                  pl.BlockSpec((1, tq, 128), lambda b, g, i: (b, i, goff)),
                  pl.BlockSpec(ov.shape, lambda b, g, i: (0, 0))],
        out_specs=pl.BlockSpec((1, tq, H * NSA_DH), lambda b, g, i: (b, i, g)),
        out_shape=jax.ShapeDtypeStruct((B, S, NSA_HEADS * NSA_DH), BF16),
        compiler_params=_cparams("parallel", "parallel", "arbitrary"),
        name="nsa_attn",
    )(q_r, kcmp, vcmp, ks_r, vs_r, kw_r, vw_r, proj3, ov)


def _layer_norm(x, g, b):
    mu = jnp.mean(x, axis=-1, keepdims=True)
    xc = x - mu
    var = jnp.mean(xc * xc, axis=-1, keepdims=True)
    return xc * lax.rsqrt(var + LN_EPS) * g + b


SUB = D_MODEL // 2 // 128
ROWS8 = 8


def _pack_bf16_pairs(v):
    half = v.shape[1] // 2
    bits = pltpu.bitcast(v.astype(BF16).astype(F32), jnp.uint32)
    return pltpu.bitcast((bits[:, :half] >> 16) | bits[:, half:], jnp.int32)


def _unpack_bf16_pairs(p):
    u = pltpu.bitcast(p, jnp.uint32)
    return pltpu.bitcast(u << 16, F32), pltpu.bitcast(u & jnp.uint32(0xFFFF0000), F32)


def _tile_rows(t):
    return (t // ROWS8) * (SUB * ROWS8) + t % ROWS8


def _to_tiles(ref, val):
    rows = val.shape[0]
    for c in range(SUB):
        ref[:, c] = val[:, c * 128:(c + 1) * 128].reshape(rows // ROWS8, ROWS8, 128)


def _from_tiles(ref, lead=()):
    groups = ref.shape[len(lead)]
    return jnp.concatenate([ref[lead + (slice(None), c)].reshape(groups * ROWS8, 128) for c in range(SUB)], axis=1)


def _outproj_kernel(x_ref, og_ref, on_ref, w_ref, g_ref, b_ref, o_ref, ot_ref):
    half = og_ref.shape[1]
    mix = _dot(og_ref[...], w_ref[0:half, :]) + _dot(on_ref[...], w_ref[half:, :])
    h = _layer_norm(DN_ALPHA * x_ref[...] + mix, g_ref[...], b_ref[...])
    o_ref[...] = h
    _to_tiles(ot_ref, _pack_bf16_pairs(h))


def _outproj(x2, og2, on2, w, g, b):
    T, D = x2.shape
    tm = 512
    row = lambda width: pl.BlockSpec((tm, width), lambda i: (i, 0))
    full = lambda a: pl.BlockSpec(a.shape, lambda i: (0, 0))
    return pl.pallas_call(
        _outproj_kernel,
        grid=(T // tm,),
        in_specs=[row(D), row(og2.shape[1]), row(on2.shape[1]), full(w), full(g), full(b)],
        out_specs=[row(D), pl.BlockSpec((tm // ROWS8, SUB, ROWS8, 128), lambda i: (i, 0, 0, 0))],
        out_shape=[jax.ShapeDtypeStruct((T, D), F32),
                   jax.ShapeDtypeStruct((T // ROWS8, SUB, ROWS8, 128), jnp.int32)],
        compiler_params=_cparams("parallel"),
        name="outproj_ln",
    )(x2, og2, on2, w, g, b)


def _router_kernel(h_ref, wh_ref, wl_ref, bias_ref, eidx_ref, wts_ref, rank_ref, cnt_ref, carry_ref):
    @pl.when(pl.program_id(0) == 0)
    def _():
        carry_ref[...] = jnp.zeros_like(carry_ref)

    h = h_ref[...]
    tm = h.shape[0]
    E = N_EXPERTS
    h_hi = h.astype(BF16)
    h_lo = (h - h_hi.astype(F32)).astype(BF16)
    wh = wh_ref[...]
    logits = _dot(h_hi, wh) + _dot(h_lo, wh) + _dot(h_hi, wl_ref[...])
    scores = jax.nn.sigmoid(logits)
    biased = scores + bias_ref[...]
    lane_i = lax.broadcasted_iota(jnp.int32, (tm, E), 1)
    gid = lane_i // (E // N_GROUPS)
    lane = lane_i.astype(F32)
    ninf = -jnp.inf

    def row_max(x):
        return jnp.max(x, axis=-1, keepdims=True)

    def first_idx(x, mx):
        return jnp.min(jnp.where(x == mx, lane, float(E)), axis=-1, keepdims=True)

    gscore = []
    for gi in range(N_GROUPS):
        mg = jnp.where(gid == gi, biased, ninf)
        m1 = row_max(mg)
        i1 = first_idx(mg, m1)
        m2 = row_max(jnp.where(lane == i1, ninf, mg))
        gscore.append(m1 + m2)
    emask = jnp.zeros((tm, E), jnp.bool_)
    for gi in range(N_GROUPS):
        rk = jnp.zeros((tm, 1), jnp.int32)
        for gj in range(N_GROUPS):
            if gj == gi:
                continue
            beats = (gscore[gj] > gscore[gi]) | ((gscore[gj] == gscore[gi]) & (gj < gi))
            rk = rk + beats.astype(jnp.int32)
        emask = emask | ((gid == gi) & (rk < TOPK_GROUPS))
    masked = jnp.where(emask, biased, ninf)

    onehots, wsel = [], []
    selm = jnp.zeros((tm, E), F32)
    for k in range(TOP_K):
        mx = row_max(masked)
        idx = first_idx(masked, mx)
        oh = lane == idx
        onehots.append(oh)
        wsel.append(jnp.sum(jnp.where(oh, scores, 0.0), axis=-1, keepdims=True))
        masked = jnp.where(oh, ninf, masked)
        selm = jnp.where(oh, 1.0, selm)
        eidx_ref[:, k:k + 1] = idx.astype(jnp.int32)
    wsum = wsel[0]
    for k in range(1, TOP_K):
        wsum = wsum + wsel[k]
    for k in range(TOP_K):
        wts_ref[:, k:k + 1] = wsel[k] / wsum * ROUTED_SCALE

    ri = lax.broadcasted_iota(jnp.int32, (tm, tm), 0)
    ci = lax.broadcasted_iota(jnp.int32, (tm, tm), 1)
    ltri = jnp.where(ri > ci, 1.0, 0.0).astype(BF16)
    cum = _dot(ltri, selm.astype(BF16)) + carry_ref[...]
    for k in range(TOP_K):
        rk = jnp.sum(jnp.where(onehots[k], cum, 0.0), axis=-1, keepdims=True)
        rank_ref[:, k:k + 1] = rk.astype(jnp.int32)
    total = carry_ref[...] + jnp.sum(selm, axis=0, keepdims=True)
    carry_ref[...] = total
    cnt_ref[...] = total


def _router(h2, w_hi, w_lo, bias):
    T, D = h2.shape
    tm = 256
    full = lambda a: pl.BlockSpec(a.shape, lambda i: (0, 0))
    o8 = pl.BlockSpec((tm, TOP_K), lambda i: (i, 0))
    return pl.pallas_call(
        _router_kernel,
        grid=(T // tm,),
        in_specs=[pl.BlockSpec((tm, D), lambda i: (i, 0)), full(w_hi), full(w_lo), full(bias)],
        out_specs=[o8, o8, o8, pl.BlockSpec((1, N_EXPERTS), lambda i: (0, 0))],
        out_shape=[jax.ShapeDtypeStruct((T, TOP_K), jnp.int32),
                   jax.ShapeDtypeStruct((T, TOP_K), F32),
                   jax.ShapeDtypeStruct((T, TOP_K), jnp.int32),
                   jax.ShapeDtypeStruct((1, N_EXPERTS), F32)],
        scratch_shapes=[pltpu.VMEM((1, N_EXPERTS), F32)],
        compiler_params=_cparams("arbitrary"),
        name="router",
    )(h2, w_hi, w_lo, bias)


def _dest_kernel(eidx_ref, rank_ref, ps_ref, dest_ref):
    tm = eidx_ref.shape[0]
    lane = lax.broadcasted_iota(jnp.int32, (tm, N_EXPERTS), 1)
    ps = ps_ref[...]
    for k in range(TOP_K):
        start = jnp.sum(jnp.where(lane == eidx_ref[:, k:k + 1], ps, 0.0), axis=-1, keepdims=True)
        dest_ref[:, k:k + 1] = start.astype(jnp.int32) + rank_ref[:, k:k + 1]


def _dest(eidx, rank, pad_start_f):
    T = eidx.shape[0]
    tm = 1024 if T % 1024 == 0 else T
    o8 = pl.BlockSpec((tm, TOP_K), lambda i: (i, 0))
    return pl.pallas_call(
        _dest_kernel,
        grid=(T // tm,),
        in_specs=[o8, o8, pl.BlockSpec((1, N_EXPERTS), lambda i: (0, 0))],
        out_specs=o8,
        out_shape=jax.ShapeDtypeStruct((T, TOP_K), jnp.int32),
        compiler_params=_cparams("parallel"),
        name="dest",
    )(eidx, rank, pad_start_f)


SC_WINDOW = 128


def _sc_gather(table, idx):
    _, lanes = table.shape
    n = idx.shape[0]
    mesh = plsc.VectorSubcoreMesh(core_axis_name="core", subcore_axis_name="subcore")

    @functools.partial(pl.kernel, out_type=jax.ShapeDtypeStruct((n, lanes), table.dtype), mesh=mesh,
                       name="sc_row_gather")
    def gather(x_hbm, i_hbm, o_hbm):
        def body(i_vmem, o_vmem):
            pltpu.sync_copy(x_hbm.at[i_vmem.at[0]], o_vmem)

        pltpu.emit_pipeline(
            body,
            grid=(n // SC_WINDOW,),
            in_specs=[pl.BlockSpec((1, SC_WINDOW), lambda i: (0, i))],
            out_specs=[pl.BlockSpec((SC_WINDOW, lanes), lambda i: (i, 0))],
            core_axis_name=("core", "subcore"),
            dimension_semantics=(pltpu.PARALLEL,),
            trace_scopes=False,
        )(i_hbm, o_hbm)

    return gather(table, idx.reshape(1, n))


def _sc_scatter(src, idx, n_out):
    rows, lanes = src.shape
    n_idx = idx.shape[0]
    mesh = plsc.VectorSubcoreMesh(core_axis_name="core", subcore_axis_name="subcore")

    @functools.partial(pl.kernel, out_type=jax.ShapeDtypeStruct((n_out, lanes), src.dtype), mesh=mesh,
                       name="sc_row_scatter")
    def scatter(x_hbm, i_hbm, o_hbm):
        def body(x_vmem, *i_vmems):
            for i_vmem in i_vmems:
                pltpu.sync_copy(x_vmem, o_hbm.at[i_vmem.at[0]])

        pltpu.emit_pipeline(
            body,
            grid=(rows // SC_WINDOW,),
            in_specs=[pl.BlockSpec((SC_WINDOW, lanes), lambda i: (i, 0))]
                     + [pl.BlockSpec((1, SC_WINDOW), lambda i, j=j: (j, i)) for j in range(n_idx)],
            out_specs=[],
            core_axis_name=("core", "subcore"),
            dimension_semantics=(pltpu.PARALLEL,),
            trace_scopes=False,
        )(x_hbm, *([i_hbm] * n_idx))

    return scatter(src, idx)


def _expert_kernel(bexp_ref, nused_ref, x_ref, wg_ref, wu_ref, wd_ref, y_ref, wg_b, wu_b, wd_b):
    i = pl.program_id(0)

    @pl.when(i < nused_ref[0])
    def _():
        @pl.when((i == 0) | (bexp_ref[i] != bexp_ref[jnp.maximum(i - 1, 0)]))
        def _():
            wg_b[...] = wg_ref[0].astype(BF16)
            wu_b[...] = wu_ref[0].astype(BF16)
            wd_b[...] = wd_ref[0].astype(BF16)

        x = jnp.concatenate(_unpack_bf16_pairs(_from_tiles(x_ref)), axis=1).astype(BF16)
        gate = _dot(x, wg_b[...])
        up = _dot(x, wu_b[...])
        act = (gate * jax.nn.sigmoid(gate) * up).astype(BF16)
        _to_tiles(y_ref, _pack_bf16_pairs(_dot(act, wd_b[...])))


def _experts(blk_exp, n_used, xs_t, wg, wu, wd):
    NP = xs_t.shape[0] * ROWS8
    D = D_MODEL
    nb = NP // MOE_BLOCK
    blk = (MOE_BLOCK // ROWS8, SUB, ROWS8, 128)

    def xmap(i, bexp, nused):
        return (jnp.minimum(i, nused[0] - 1), 0, 0, 0)

    def wmap(i, bexp, nused):
        return (bexp[jnp.minimum(i, nused[0] - 1)], 0, 0)

    grid_spec = pltpu.PrefetchScalarGridSpec(
        num_scalar_prefetch=2,
        grid=(nb,),
        in_specs=[pl.BlockSpec(blk, xmap),
                  pl.BlockSpec((1, D, D_EXPERT), wmap),
                  pl.BlockSpec((1, D, D_EXPERT), wmap),
                  pl.BlockSpec((1, D_EXPERT, D), wmap)],
        out_specs=pl.BlockSpec(blk, xmap),
        scratch_shapes=[pltpu.VMEM((D, D_EXPERT), BF16), pltpu.VMEM((D, D_EXPERT), BF16),
                        pltpu.VMEM((D_EXPERT, D), BF16)],
    )
    return pl.pallas_call(
        _expert_kernel,
        grid_spec=grid_spec,
        out_shape=jax.ShapeDtypeStruct(xs_t.shape, jnp.int32),
        compiler_params=_cparams("arbitrary"),
        name="experts",
    )(blk_exp, n_used, xs_t, wg, wu, wd)


def _combine_kernel(h_ref, wts_ref, yg_ref, wsg_ref, wsu_ref, wsd_ref, g_ref, b_ref, o_ref):
    h = h_ref[...]
    hb = h.astype(BF16)
    gate = _dot(hb, wsg_ref[...])
    up = _dot(hb, wsu_ref[...])
    shared = _dot((gate * jax.nn.sigmoid(gate) * up).astype(BF16), wsd_ref[...])
    wts = wts_ref[...]
    lo, hi = _unpack_bf16_pairs(_from_tiles(yg_ref, (0,)))
    r_lo, r_hi = wts[:, 0:1] * lo, wts[:, 0:1] * hi
    for k in range(1, TOP_K):
        lo, hi = _unpack_bf16_pairs(_from_tiles(yg_ref, (k,)))
        r_lo, r_hi = r_lo + wts[:, k:k + 1] * lo, r_hi + wts[:, k:k + 1] * hi
    routed = jnp.concatenate([r_lo, r_hi], axis=1)
    o_ref[...] = _layer_norm(DN_ALPHA * h + (routed + shared), g_ref[...], b_ref[...])


def _combine(h2, wts, yg_t, wsg, wsu, wsd, g, b):
    T, D = h2.shape
    tm = 128
    full = lambda a: pl.BlockSpec(a.shape, lambda i: (0, 0))
    return pl.pallas_call(
        _combine_kernel,
        grid=(T // tm,),
        in_specs=[pl.BlockSpec((tm, D), lambda i: (i, 0)),
                  pl.BlockSpec((tm, TOP_K), lambda i: (i, 0)),
                  pl.BlockSpec((TOP_K, tm // ROWS8, SUB, ROWS8, 128), lambda i: (0, i, 0, 0, 0)),
                  full(wsg), full(wsu), full(wsd), full(g), full(b)],
        out_specs=pl.BlockSpec((tm, D), lambda i: (i, 0)),
        out_shape=jax.ShapeDtypeStruct((T, D), F32),
        compiler_params=_cparams("parallel"),
        name="combine_ln",
    )(h2, wts, yg_t, wsg, wsu, wsd, g, b)


def _regroup_w_in(w_in):
    parts, off = {}, 0
    for name, width in _SPLITS:
        parts[name] = w_in[:, off:off + width]
        off += width
    cols = []
    for name, (_, width) in _COLS.items():
        p = parts[name]
        if p.shape[1] < width:
            p = jnp.pad(p, ((0, 0), (0, width - p.shape[1])))
        cols.append(p)
    return jnp.concatenate(cols, axis=1).astype(BF16)


def _overlap_matrix(S):
    nr = S // CMP_STRIDE
    n_sel = S // SEL_LEN
    ci = np.arange(nr)[:, None] * CMP_STRIDE
    sj = np.arange(n_sel)[None, :] * SEL_LEN
    ov = np.clip(np.minimum(ci + CMP_LEN, sj + SEL_LEN) - np.maximum(ci, sj), 0, None) / CMP_LEN
    ov[nr - 1] = 0.0
    return jnp.asarray(ov.T, BF16)


def _mixers(h, positions, w_in, w_alpha2, b_alpha, gla_norm_g,
            cmp_pos_k, cmp_w1_k, cmp_b1_k, cmp_w2_k, cmp_pos_v, cmp_w1_v, cmp_b1_v, cmp_w2_v):
    B, S, D = h.shape
    proj = _proj(h.reshape(B * S, D), _regroup_w_in(w_in)).reshape(B, S, D_PROJ)

    wa = jnp.pad(w_alpha2, ((0, 128 - GLA_LOWRANK), (0, 0))).astype(BF16)
    o_gla = _gla(proj, wa, b_alpha.reshape(1, -1), gla_norm_g.reshape(1, -1))

    half = NSA_DH // 2
    inv = ROPE_THETA ** (-np.arange(half, dtype=np.float32) / half)
    inv_row = jnp.asarray(np.tile(inv, 128 // half).reshape(1, 128), F32)
    pos3 = positions.astype(F32).reshape(B, S, 1)
    q_r, kc_r, ks_r, vs_r, kw_r, vw_r = _nsa_prep(proj, pos3, inv_row)

    def blocks16(t):
        return (t.reshape(B, S // CMP_STRIDE, CMP_STRIDE, NSA_KV_GROUPS, NSA_DH)
                .transpose(0, 3, 1, 2, 4).reshape(B, NSA_KV_GROUPS, S // CMP_STRIDE, CMP_STRIDE * NSA_DH))

    vc_off = _COLS["nvc"][0]
    kcmp, vcmp = _compress(
        blocks16(kc_r), blocks16(proj[:, :, vc_off:vc_off + 128]),
        cmp_pos_k.reshape(2, -1), cmp_w1_k.astype(BF16), cmp_b1_k.reshape(1, -1), cmp_w2_k.astype(BF16),
        cmp_pos_v.reshape(2, -1), cmp_w1_v.astype(BF16), cmp_b1_v.reshape(1, -1), cmp_w2_v.astype(BF16))
    o_nsa = _nsa_attn(q_r, kcmp, vcmp, ks_r, vs_r, kw_r, vw_r, proj, _overlap_matrix(S))
    return o_gla, o_nsa


def _moe_ln(h2, h_t, w_router, router_bias, w_gate, w_up, w_down, ws_gate, ws_up, ws_down, ln_g, ln_b):
    T, D = h2.shape
    P = T * TOP_K
    w_hi = w_router.astype(BF16)
    w_lo = (w_router - w_hi.astype(F32)).astype(BF16)
    eidx, wts, rank, counts = _router(h2, w_hi, w_lo, router_bias.reshape(1, -1))

    counts = counts.reshape(-1).astype(jnp.int32)
    padded = (counts + MOE_BLOCK - 1) // MOE_BLOCK * MOE_BLOCK
    pad_end = jnp.cumsum(padded)
    pad_start = pad_end - padded
    nb = -(-P // MOE_BLOCK) + N_EXPERTS
    n_used = (pad_end[-1] // MOE_BLOCK).astype(jnp.int32).reshape(1)
    blk_start = jnp.arange(nb, dtype=jnp.int32) * MOE_BLOCK
    blk_exp = jnp.minimum(jnp.sum((pad_end[None, :] <= blk_start[:, None]).astype(jnp.int32), axis=1),
                          N_EXPERTS - 1)

    NP = nb * MOE_BLOCK
    dest = _dest(eidx, rank, pad_start.astype(F32).reshape(1, -1))
    col = (jnp.arange(SUB, dtype=jnp.int32) * ROWS8)

    n_pad = NP - P
    assert n_pad % T == 0
    pad_cnt = padded - counts
    pad_hi = jnp.cumsum(pad_cnt)
    pad_lo = pad_hi - pad_cnt
    j = jnp.arange(n_pad, dtype=jnp.int32)[:, None]
    owner = (pad_lo[None, :] <= j) & (j < pad_hi[None, :])
    in_expert = jnp.sum(jnp.where(owner, (pad_start + counts - pad_lo)[None, :] + j, 0), axis=1)
    pad_rows = jnp.where(j[:, 0] < pad_hi[-1], in_expert, pad_end[-1] + j[:, 0] - pad_hi[-1])

    rows_all = jnp.concatenate([dest.T, pad_rows.reshape(n_pad // T, T)], axis=0)
    dst = _tile_rows(rows_all).reshape(-1, T // ROWS8, 1, ROWS8) + col[None, None, :, None]
    xs_t = _sc_scatter(h_t.reshape(T * SUB, 128), dst.reshape(-1, T * SUB), NP * SUB)
    xs_t = xs_t.reshape(NP // ROWS8, SUB, ROWS8, 128)
    ys_t = _experts(blk_exp, n_used, xs_t, w_gate, w_up, w_down)
    src = _tile_rows(dest.T).reshape(TOP_K, T // ROWS8, 1, ROWS8) + col[None, None, :, None]
    yg_t = _sc_gather(ys_t.reshape(NP * SUB, 128), src.reshape(-1)).reshape(TOP_K, T // ROWS8, SUB, ROWS8, 128)
    return _combine(h2, wts, yg_t, ws_gate.astype(BF16), ws_up.astype(BF16), ws_down.astype(BF16),
                    ln_g.reshape(1, -1), ln_b.reshape(1, -1))


def kernel(x, positions, w_in, w_alpha2, b_alpha, gla_norm_g, cmp_pos_k, cmp_w1_k, cmp_b1_k, cmp_w2_k, cmp_pos_v, cmp_w1_v, cmp_b1_v, cmp_w2_v, w_out, ln1_g, ln1_b, w_router, router_bias, w_exp_gate, w_exp_up, w_exp_down, w_sh_gate, w_sh_up, w_sh_down, ln2_g, ln2_b):
    def layer(h, pos, l):
        B, S, D = h.shape
        o_gla, o_nsa = _mixers(h, pos, w_in[l], w_alpha2[l], b_alpha[l], gla_norm_g[l],
                               cmp_pos_k[l], cmp_w1_k[l], cmp_b1_k[l], cmp_w2_k[l],
                               cmp_pos_v[l], cmp_w1_v[l], cmp_b1_v[l], cmp_w2_v[l])
        h1, h1_t = _outproj(h.reshape(B * S, D), o_gla.reshape(B * S, -1), o_nsa.reshape(B * S, -1),
                            w_out[l].astype(BF16), ln1_g[l].reshape(1, -1), ln1_b[l].reshape(1, -1))
        h2 = _moe_ln(h1, h1_t, w_router[l], router_bias[l], w_exp_gate[l], w_exp_up[l], w_exp_down[l],
                     w_sh_gate[l], w_sh_up[l], w_sh_down[l], ln2_g[l], ln2_b[l])
        return h2.reshape(B, S, D)

    n_groups = BATCH_GROUPS if x.shape[0] % BATCH_GROUPS == 0 else 1
    hs = jnp.split(x, n_groups, axis=0)
    ps = jnp.split(positions, n_groups, axis=0)
    for l in range(w_in.shape[0]):
        hs = [layer(h, p, l) for h, p in zip(hs, ps)]
    return jnp.concatenate(hs, axis=0)
```

```python
import functools

import numpy as np
import jax
import jax.numpy as jnp
from jax import lax
from jax.experimental import pallas as pl
from jax.experimental.pallas import tpu as pltpu
from jax.experimental.pallas import tpu_sc as plsc

D_MODEL = 1024
GLA_HEADS = 4
GLA_DV = 128
GLA_DK = 64
GLA_LOWRANK = 16
GLA_TAU = 16.0
GLA_CHUNK = 64
NSA_HEADS = 8
NSA_KV_GROUPS = 2
NSA_HPG = 4
NSA_DH = 64
CMP_LEN = 32
CMP_STRIDE = 16
CMP_HIDDEN = 256
SEL_LEN = 64
SEL_TOPK = 16
WINDOW = 512
ROPE_THETA = 10000.0
N_EXPERTS = 256
TOP_K = 8
N_GROUPS = 8
TOPK_GROUPS = 4
D_EXPERT = 256
ROUTED_SCALE = 2.5
DEPTH = 1
DN_ALPHA = (2.0 * DEPTH) ** 0.25
LN_EPS = 1e-5

MOE_BLOCK = 512
BATCH_GROUPS = 2
NEG = -1e30
F32 = jnp.float32
BF16 = jnp.bfloat16

_COLS = {}
_off = 0
for _name, _w in (("gq", 256), ("gk", 256), ("gv", 512), ("gr", 512), ("nq", 512),
                  ("nkc", 128), ("nvc", 128), ("nks", 128), ("nvs", 128), ("nkw", 128), ("nvw", 128),
                  ("ga", 128), ("ng", 128)):
    _COLS[_name] = (_off, _w)
    _off += _w
D_PROJ = _off
_SPLITS = (("gq", 256), ("gk", 256), ("gv", 512), ("ga", 16), ("gr", 512), ("nq", 512),
           ("nkc", 128), ("nvc", 128), ("nks", 128), ("nvs", 128), ("nkw", 128), ("nvw", 128), ("ng", 24))

VMEM_LIMIT = 56 * 1024 * 1024


def _cparams(*sem):
    return pltpu.CompilerParams(dimension_semantics=sem, vmem_limit_bytes=VMEM_LIMIT)


def _dot(a, b):
    return jnp.dot(a, b, preferred_element_type=F32)


def _dot_nt(a, b):
    return lax.dot_general(a, b, (((1,), (1,)), ((), ())), preferred_element_type=F32)


def _split3(x):
    hi = x.astype(BF16)
    r1 = x - hi.astype(F32)
    mid = r1.astype(BF16)
    lo = (r1 - mid.astype(F32)).astype(BF16)
    return hi, mid, lo


def _proj_kernel(x_ref, w_ref, o_ref):
    o_ref[...] = _dot(x_ref[...].astype(BF16), w_ref[...])


def _proj(x2, w):
    T, D = x2.shape
    N = w.shape[1]
    tm, tn = 512, 1024
    return pl.pallas_call(
        _proj_kernel,
        grid=(T // tm, N // tn),
        in_specs=[pl.BlockSpec((tm, D), lambda i, j: (i, 0)),
                  pl.BlockSpec((D, tn), lambda i, j: (0, j))],
        out_specs=pl.BlockSpec((tm, tn), lambda i, j: (i, j)),
        out_shape=jax.ShapeDtypeStruct((T, N), F32),
        compiler_params=_cparams("parallel", "arbitrary"),
        name="proj",
    )(x2, w)


def _gla_kernel(q_ref, k_ref, v_ref, r_ref, a_ref, wa_ref, ba_ref, ng_ref, o_ref, st_ref, *, nchunk):
    C = GLA_CHUNK
    HK = GLA_HEADS * GLA_DK

    @pl.when(pl.program_id(1) == 0)
    def _():
        st_ref[...] = jnp.zeros_like(st_ref)

    ri = lax.broadcasted_iota(jnp.int32, (C, C), 0)
    ci = lax.broadcasted_iota(jnp.int32, (C, C), 1)
    causal = ri >= ci
    tri = jnp.where(causal, 1.0, 0.0).astype(BF16)
    lane_head = lax.broadcasted_iota(jnp.int32, (1, HK), 1) // GLA_DK
    wa = wa_ref[...]
    ba = ba_ref[...]
    ng = ng_ref[...]

    def chunk(c, carry):
        for bb in range(q_ref.shape[0]):
            chunk_one(c, bb)
        return carry

    def chunk_one(c, bb):
        rows = pl.ds(pl.multiple_of(c * C, C), C)
        q = q_ref[bb, rows, :]
        k = k_ref[bb, rows, :]
        v = v_ref[bb, rows, :]
        r = r_ref[bb, rows, :]
        a = a_ref[bb, rows, :]
        z = _dot(a.astype(BF16), wa) + ba
        g = (jnp.minimum(z, 0.0) - jnp.log1p(jnp.exp(-jnp.abs(z)))) * (1.0 / GLA_TAU)
        g_hi, g_mid, g_lo = _split3(g)
        b = _dot(tri, g_hi) + _dot(tri, g_mid) + _dot(tri, g_lo)
        b_last = b[C - 1:C, :]
        qt = q * jnp.exp(b) * (GLA_DK ** -0.5)
        kt = (k * jnp.exp(-b)).astype(BF16)
        ks = (k * jnp.exp(b_last - b)).astype(BF16)
        st = st_ref[bb]
        st_b = st.astype(BF16)
        new_st = st * jnp.exp(b_last)
        for h in range(GLA_HEADS):
            hm = lane_head == h
            qh = jnp.where(hm, qt, 0.0).astype(BF16)
            att = jnp.where(causal, _dot_nt(qh, kt), 0.0)
            vh = v[:, h * GLA_DV:(h + 1) * GLA_DV].astype(BF16)
            o = _dot_nt(qh, st_b) + _dot(att.astype(BF16), vh)
            new_st = new_st + jnp.where(hm, _dot(vh.T, ks), 0.0)
            o = o * lax.rsqrt(jnp.mean(o * o, axis=-1, keepdims=True) + 1e-6) * ng
            rh = r[:, h * GLA_DV:(h + 1) * GLA_DV]
            o = o * (rh * jax.nn.sigmoid(rh))
            o_ref[bb, rows, h * GLA_DV:(h + 1) * GLA_DV] = o.astype(o_ref.dtype)
        st_ref[bb] = new_st

    lax.fori_loop(0, nchunk, chunk, 0)


def _gla(proj3, wa, ba, ng):
    B, S, _ = proj3.shape
    L = 512 if S % 512 == 0 else S
    nchunk = L // GLA_CHUNK

    def col(name, width):
        off = _COLS[name][0]
        assert off % width == 0
        return pl.BlockSpec((nbat, L, width), lambda b, j, o=off // width: (b, j, o))

    nbat = 2 if B % 2 == 0 else 1
    full2 = lambda shape: pl.BlockSpec(shape, lambda b, j: (0, 0))
    return pl.pallas_call(
        functools.partial(_gla_kernel, nchunk=nchunk),
        grid=(B // nbat, S // L),
        in_specs=[col("gq", 256), col("gk", 256), col("gv", 512), col("gr", 512), col("ga", 128),
                  full2((128, 256)), full2((1, 256)), full2((1, 128))],
        out_specs=pl.BlockSpec((nbat, L, 512), lambda b, j: (b, j, 0)),
        out_shape=jax.ShapeDtypeStruct((B, S, 512), BF16),
        scratch_shapes=[pltpu.VMEM((nbat, GLA_DV, GLA_HEADS * GLA_DK), F32)],
        compiler_params=_cparams("parallel", "arbitrary"),
        name="gla",
    )(proj3, proj3, proj3, proj3, proj3, wa, ba, ng)


def _rot_half(x):
    n = x.shape[-1]
    lane = lax.broadcasted_iota(jnp.int32, (1, n), 1)
    first = (lane % NSA_DH) < (NSA_DH // 2)
    return jnp.where(first, -pltpu.roll(x, n - NSA_DH // 2, 1), pltpu.roll(x, NSA_DH // 2, 1))


def _nsa_prep_kernel(pos_ref, inv_ref, q_ref, kc_ref, ks_ref, vs_ref, kw_ref, vw_ref,
                     qo_ref, kco_ref, kso_ref, vso_ref, kwo_ref, vwo_ref):
    ang = pos_ref[0] * inv_ref[...]
    cos = jnp.cos(ang)
    sin = jnp.sin(ang)
    cos4 = jnp.concatenate([cos] * 4, axis=1)
    sin4 = jnp.concatenate([sin] * 4, axis=1)

    q = q_ref[0]
    qr = q * cos4 + _rot_half(q) * sin4
    for h in range(NSA_HEADS):
        qo_ref[0, h] = (qr[:, h * NSA_DH:(h + 1) * NSA_DH] * (NSA_DH ** -0.5)).astype(qo_ref.dtype)

    def rope128(x):
        return x * cos + _rot_half(x) * sin

    kco_ref[0] = rope128(kc_ref[0])
    ksr = rope128(ks_ref[0])
    kwr = rope128(kw_ref[0])
    vs = vs_ref[0]
    vw = vw_ref[0]
    ones_col = jnp.where(lax.broadcasted_iota(jnp.int32, (vs.shape[0], NSA_DH), 1) == 0, 1.0, 0.0)
    for g in range(NSA_KV_GROUPS):
        sl = slice(g * NSA_DH, (g + 1) * NSA_DH)
        kso_ref[0, g] = ksr[:, sl].astype(kso_ref.dtype)
        kwo_ref[0, g] = kwr[:, sl].astype(kwo_ref.dtype)
        vso_ref[0, g] = jnp.concatenate([vs[:, sl], ones_col], axis=1).astype(vso_ref.dtype)
        vwo_ref[0, g] = jnp.concatenate([vw[:, sl], ones_col], axis=1).astype(vwo_ref.dtype)


def _nsa_prep(proj3, pos3, inv_row):
    B, S, _ = proj3.shape
    ts = 512 if S % 512 == 0 else S

    def col(name, width):
        off = _COLS[name][0]
        assert off % width == 0
        return pl.BlockSpec((1, ts, width), lambda b, j, o=off // width: (b, j, o))

    kv_spec = pl.BlockSpec((1, NSA_KV_GROUPS, ts, NSA_DH), lambda b, j: (b, 0, j, 0))
    kv_shape = jax.ShapeDtypeStruct((B, NSA_KV_GROUPS, S, NSA_DH), BF16)
    vx_spec = pl.BlockSpec((1, NSA_KV_GROUPS, ts, 2 * NSA_DH), lambda b, j: (b, 0, j, 0))
    vx_shape = jax.ShapeDtypeStruct((B, NSA_KV_GROUPS, S, 2 * NSA_DH), BF16)
    return pl.pallas_call(
        _nsa_prep_kernel,
        grid=(B, S // ts),
        in_specs=[pl.BlockSpec((1, ts, 1), lambda b, j: (b, j, 0)),
                  pl.BlockSpec((1, 128), lambda b, j: (0, 0)),
                  col("nq", 512), col("nkc", 128), col("nks", 128), col("nvs", 128),
                  col("nkw", 128), col("nvw", 128)],
        out_specs=[pl.BlockSpec((1, NSA_HEADS, ts, NSA_DH), lambda b, j: (b, 0, j, 0)),
                   pl.BlockSpec((1, ts, 128), lambda b, j: (b, j, 0)),
                   kv_spec, vx_spec, kv_spec, vx_spec],
        out_shape=[jax.ShapeDtypeStruct((B, NSA_HEADS, S, NSA_DH), BF16),
                   jax.ShapeDtypeStruct((B, S, 128), F32),
                   kv_shape, vx_shape, kv_shape, vx_shape],
        compiler_params=_cparams("parallel", "parallel"),
        name="nsa_prep",
    )(pos3, inv_row, proj3, proj3, proj3, proj3, proj3, proj3)


def _compress_kernel(k_ref, v_ref, pk_ref, w1k_ref, b1k_ref, w2k_ref, pv_ref, w1v_ref, b1v_ref, w2v_ref,
                     ko_ref, vo_ref):
    half = CMP_STRIDE * NSA_DH

    def run(x_ref, p_ref, w1_ref, b1_ref, w2_ref, o_ref):
        p = p_ref[...]
        for g in range(NSA_KV_GROUPS):
            x = x_ref[0, g]
            nr = x.shape[0]
            a = _dot((x + p[0:1]).astype(BF16), w1_ref[0:half, :])
            bb = _dot((x + p[1:2]).astype(BF16), w1_ref[half:2 * half, :])
            pre = a + pltpu.roll(bb, nr - 1, 0) + b1_ref[...]
            hid = jax.nn.gelu(pre)
            out = _dot(hid.astype(BF16), w2_ref[...])
            row = lax.broadcasted_iota(jnp.int32, out.shape, 0)
            o_ref[0, g] = jnp.where(row < nr - 1, out, 0.0)

    run(k_ref, pk_ref, w1k_ref, b1k_ref, w2k_ref, ko_ref)
    run(v_ref, pv_ref, w1v_ref, b1v_ref, w2v_ref, vo_ref)


def _compress(kc4, vc4, pk, w1k, b1k, w2k, pv, w1v, b1v, w2v):
    B, G, NR, W = kc4.shape
    x_spec = pl.BlockSpec((1, G, NR, W), lambda b: (b, 0, 0, 0))
    full = lambda a: pl.BlockSpec(a.shape, lambda b: (0,) * a.ndim)
    o_spec = pl.BlockSpec((1, G, NR, NSA_DH), lambda b: (b, 0, 0, 0))
    o_shape = jax.ShapeDtypeStruct((B, G, NR, NSA_DH), F32)
    params = (pk, w1k, b1k, w2k, pv, w1v, b1v, w2v)
    return pl.pallas_call(
        _compress_kernel,
        grid=(B,),
        in_specs=[x_spec, x_spec] + [full(a) for a in params],
        out_specs=[o_spec, o_spec],
        out_shape=[o_shape, o_shape],
        compiler_params=_cparams("parallel"),
        name="compress",
    )(kc4, vc4, *params)


def _nsa_attn_kernel(q_ref, kc_ref, vc_ref, ks_ref, vs_ref, kw_ref, vw_ref, gate_ref, ovt_ref, o_ref,
                     *, tq, S, n_sel, n_top):
    g = pl.program_id(1)
    qi = pl.program_id(2)
    start = qi * tq
    H = NSA_HPG
    qs = q_ref[0].reshape(H * tq, NSA_DH)
    t_col = start + lax.broadcasted_iota(jnp.int32, (tq, 1), 0)
    t_row = start + lax.broadcasted_iota(jnp.int32, (1, tq), 1)

    kc = kc_ref[0, 0].astype(BF16)
    vc = vc_ref[0, 0].astype(BF16)
    NR = kc.shape[0]
    s = _dot_nt(qs, kc)
    ncol = lax.broadcasted_iota(jnp.int32, (1, NR), 1)
    cmask = (ncol < NR - 1) & (ncol * CMP_STRIDE + (CMP_LEN - 1) <= t_col)
    s3 = jnp.where(cmask[None], s.reshape(H, tq, NR), NEG)
    m = jnp.max(s3, axis=-1, keepdims=True)
    e = jnp.exp(s3 - m)
    p3 = e / jnp.sum(e, axis=-1, keepdims=True)
    p3 = jnp.where(cmask[None], p3, 0.0)
    o_cmp = _dot(p3.reshape(H * tq, NR).astype(BF16), vc)

    psum = jnp.sum(p3, axis=0)
    p_hi = psum.astype(BF16)
    p_lo = (psum - p_hi.astype(F32)).astype(BF16)
    ovt = ovt_ref[...]
    imp = _dot_nt(ovt, p_hi) + _dot_nt(ovt, p_lo)
    blk = lax.broadcasted_iota(jnp.int32, (n_sel, 1), 0)
    cur = t_row // SEL_LEN
    causal_blk = blk <= cur
    forced = (blk == 0) | (blk == cur) | (blk == cur - 1)
    val = jnp.where(causal_blk, jnp.where(forced, jnp.inf, imp), -jnp.inf)
    rank = jnp.zeros((n_sel, tq), jnp.int32)
    for j in range(n_sel):
        vj = val[j:j + 1, :]
        beats = (vj > val) | ((vj == val) & (blk > j))
        rank = rank + beats.astype(jnp.int32)
    sel_t = jnp.where((rank < n_top) & causal_blk, 1.0, 0.0)
    sel_t = jnp.concatenate([sel_t, jnp.zeros((128 - n_sel, tq), F32)], axis=0)
    sel = sel_t.T.astype(BF16)

    tk = 256
    nsub = 2 if S % (2 * tk) == 0 else 1
    bpc = tk // SEL_LEN
    n_trips = (start + tq + nsub * tk - 1) // (nsub * tk)
    erow = lax.broadcasted_iota(jnp.int32, (128, tk), 0)
    ecol = lax.broadcasted_iota(jnp.int32, (128, tk), 1) // SEL_LEN
    kk = lax.broadcasted_iota(jnp.int32, (1, tk), 1)

    def sel_trip(c, carry):
        m_i, acc = carry
        scores, vals = [], []
        m_new = m_i
        for u in range(nsub):
            cu = c * nsub + u
            k0 = pl.multiple_of(cu * tk, tk)
            kb = ks_ref[0, 0, pl.ds(k0, tk), :]
            vals.append(vs_ref[0, 0, pl.ds(k0, tk), :])
            expand = jnp.where(erow == ecol + cu * bpc, 1.0, 0.0).astype(BF16)
            allowed = (_dot(sel, expand) > 0.5) & (k0 + kk <= t_col)
            bias = jnp.where(allowed, 0.0, NEG)
            sc3 = _dot_nt(qs, kb).reshape(H, tq, tk) + bias[None]
            scores.append(sc3)
            m_new = jnp.maximum(m_new, jnp.max(sc3, axis=-1, keepdims=True))
        acc = jnp.exp(m_i - m_new).reshape(H * tq, 1) * acc
        for u in range(nsub):
            pe = jnp.exp((scores[u] - m_new).astype(BF16))
            acc = acc + _dot(pe.reshape(H * tq, tk), vals[u])
        return m_new, acc

    m0 = jnp.full((H, tq, 1), NEG, F32)
    a0 = jnp.zeros((H * tq, 2 * NSA_DH), F32)
    _, acc_f = lax.fori_loop(0, n_trips, sel_trip, (m0, a0))
    o_sel = acc_f[:, :NSA_DH] / acc_f[:, NSA_DH:NSA_DH + 1]

    span = min(WINDOW + tq, S)
    ws = jnp.clip(start - WINDOW, 0, S - span)
    ws = pl.multiple_of(ws, tq)
    kwb = kw_ref[0, 0, pl.ds(ws, span), :]
    vwb = vw_ref[0, 0, pl.ds(ws, span), :]
    kp = ws + lax.broadcasted_iota(jnp.int32, (1, span), 1)
    wbias = jnp.where((kp <= t_col) & (kp > t_col - WINDOW), 0.0, NEG)
    sw3 = _dot_nt(qs, kwb).reshape(H, tq, span) + wbias[None]
    mw = jnp.max(sw3, axis=-1, keepdims=True)
    ew = jnp.exp((sw3 - mw).astype(BF16))
    aw = _dot(ew.reshape(H * tq, span), vwb)
    o_win = aw[:, :NSA_DH] / aw[:, NSA_DH:NSA_DH + 1]

    gs = jax.nn.sigmoid(gate_ref[0])
    for g_static in range(NSA_KV_GROUPS):
        @pl.when(g == g_static)
        def _(g_static=g_static):
            for h in range(H):
                c0 = g_static * H * 3 + h * 3
                rs = slice(h * tq, (h + 1) * tq)
                o = (gs[:, c0:c0 + 1] * o_cmp[rs] + gs[:, c0 + 1:c0 + 2] * o_sel[rs]
                     + gs[:, c0 + 2:c0 + 3] * o_win[rs])
                o_ref[0, :, h * NSA_DH:(h + 1) * NSA_DH] = o.astype(o_ref.dtype)


def _nsa_attn(q_r, kcmp, vcmp, ks_r, vs_r, kw_r, vw_r, proj3, ov):
    B, _, S, _ = q_r.shape
    G, H = NSA_KV_GROUPS, NSA_HPG
    NR = kcmp.shape[2]
    tq = 256
    n_sel = S // SEL_LEN
    n_top = min(SEL_TOPK, n_sel)
    cmp_spec = pl.BlockSpec((1, 1, NR, NSA_DH), lambda b, g, i: (b, g, 0, 0))
    kv_spec = pl.BlockSpec((1, 1, S, NSA_DH), lambda b, g, i: (b, g, 0, 0))
    vx_spec = pl.BlockSpec((1, 1, S, 2 * NSA_DH), lambda b, g, i: (b, g, 0, 0))
    goff = _COLS["ng"][0] // 128
    return pl.pallas_call(
        functools.partial(_nsa_attn_kernel, tq=tq, S=S, n_sel=n_sel, n_top=n_top),
        grid=(B, G, S // tq),
        in_specs=[pl.BlockSpec((1, H, tq, NSA_DH), lambda b, g, i: (b, g, i, 0)),
                  cmp_spec, cmp_spec, kv_spec, vx_spec, kv_spec, vx_spec,
                  pl.BlockSpec((1, tq, 128), lambda b, g, i: (b, i, goff)),
                  pl.BlockSpec(ov.shape, lambda b, g, i: (0, 0))],
        out_specs=pl.BlockSpec((1, tq, H * NSA_DH), lambda b, g, i: (b, i, g)),
        out_shape=jax.ShapeDtypeStruct((B, S, NSA_HEADS * NSA_DH), BF16),
        compiler_params=_cparams("parallel", "parallel", "arbitrary"),
        name="nsa_attn",
    )(q_r, kcmp, vcmp, ks_r, vs_r, kw_r, vw_r, proj3, ov)


def _layer_norm(x, g, b):
    mu = jnp.mean(x, axis=-1, keepdims=True)
    xc = x - mu
    var = jnp.mean(xc * xc, axis=-1, keepdims=True)
    return xc * lax.rsqrt(var + LN_EPS) * g + b


SUB = D_MODEL // 2 // 128
ROWS8 = 8


def _pack_bf16_pairs(v):
    half = v.shape[1] // 2
    bits = pltpu.bitcast(v.astype(BF16).astype(F32), jnp.uint32)
    return pltpu.bitcast((bits[:, :half] >> 16) | bits[:, half:], jnp.int32)


def _unpack_bf16_pairs(p):
    u = pltpu.bitcast(p, jnp.uint32)
    return pltpu.bitcast(u << 16, F32), pltpu.bitcast(u & jnp.uint32(0xFFFF0000), F32)


def _tile_rows(t):
    return (t // ROWS8) * (SUB * ROWS8) + t % ROWS8


def _to_tiles(ref, val):
    rows = val.shape[0]
    for c in range(SUB):
        ref[:, c] = val[:, c * 128:(c + 1) * 128].reshape(rows // ROWS8, ROWS8, 128)


def _from_tiles(ref, lead=()):
    groups = ref.shape[len(lead)]
    return jnp.concatenate([ref[lead + (slice(None), c)].reshape(groups * ROWS8, 128) for c in range(SUB)], axis=1)


def _outproj_kernel(x_ref, og_ref, on_ref, w_ref, g_ref, b_ref, o_ref, ot_ref):
    half = og_ref.shape[1]
    mix = _dot(og_ref[...], w_ref[0:half, :]) + _dot(on_ref[...], w_ref[half:, :])
    h = _layer_norm(DN_ALPHA * x_ref[...] + mix, g_ref[...], b_ref[...])
    o_ref[...] = h
    _to_tiles(ot_ref, _pack_bf16_pairs(h))


def _outproj(x2, og2, on2, w, g, b):
    T, D = x2.shape
    tm = 512
    row = lambda width: pl.BlockSpec((tm, width), lambda i: (i, 0))
    full = lambda a: pl.BlockSpec(a.shape, lambda i: (0, 0))
    return pl.pallas_call(
        _outproj_kernel,
        grid=(T // tm,),
        in_specs=[row(D), row(og2.shape[1]), row(on2.shape[1]), full(w), full(g), full(b)],
        out_specs=[row(D), pl.BlockSpec((tm // ROWS8, SUB, ROWS8, 128), lambda i: (i, 0, 0, 0))],
        out_shape=[jax.ShapeDtypeStruct((T, D), F32),
                   jax.ShapeDtypeStruct((T // ROWS8, SUB, ROWS8, 128), jnp.int32)],
        compiler_params=_cparams("parallel"),
        name="outproj_ln",
    )(x2, og2, on2, w, g, b)


def _router_kernel(h_ref, wh_ref, wl_ref, bias_ref, eidx_ref, wts_ref, rank_ref, cnt_ref, carry_ref):
    @pl.when(pl.program_id(0) == 0)
    def _():
        carry_ref[...] = jnp.zeros_like(carry_ref)

    h = h_ref[...]
    tm = h.shape[0]
    E = N_EXPERTS
    h_hi = h.astype(BF16)
    h_lo = (h - h_hi.astype(F32)).astype(BF16)
    wh = wh_ref[...]
    logits = _dot(h_hi, wh) + _dot(h_lo, wh) + _dot(h_hi, wl_ref[...])
    scores = jax.nn.sigmoid(logits)
    biased = scores + bias_ref[...]
    lane_i = lax.broadcasted_iota(jnp.int32, (tm, E), 1)
    gid = lane_i // (E // N_GROUPS)
    lane = lane_i.astype(F32)
    ninf = -jnp.inf

    def row_max(x):
        return jnp.max(x, axis=-1, keepdims=True)

    def first_idx(x, mx):
        return jnp.min(jnp.where(x == mx, lane, float(E)), axis=-1, keepdims=True)

    gscore = []
    for gi in range(N_GROUPS):
        mg = jnp.where(gid == gi, biased, ninf)
        m1 = row_max(mg)
        i1 = first_idx(mg, m1)
        m2 = row_max(jnp.where(lane == i1, ninf, mg))
        gscore.append(m1 + m2)
    emask = jnp.zeros((tm, E), jnp.bool_)
    for gi in range(N_GROUPS):
        rk = jnp.zeros((tm, 1), jnp.int32)
        for gj in range(N_GROUPS):
            if gj == gi:
                continue
            beats = (gscore[gj] > gscore[gi]) | ((gscore[gj] == gscore[gi]) & (gj < gi))
            rk = rk + beats.astype(jnp.int32)
        emask = emask | ((gid == gi) & (rk < TOPK_GROUPS))
    masked = jnp.where(emask, biased, ninf)

    onehots, wsel = [], []
    selm = jnp.zeros((tm, E), F32)
    for k in range(TOP_K):
        mx = row_max(masked)
        idx = first_idx(masked, mx)
        oh = lane == idx
        onehots.append(oh)
        wsel.append(jnp.sum(jnp.where(oh, scores, 0.0), axis=-1, keepdims=True))
        masked = jnp.where(oh, ninf, masked)
        selm = jnp.where(oh, 1.0, selm)
        eidx_ref[:, k:k + 1] = idx.astype(jnp.int32)
    wsum = wsel[0]
    for k in range(1, TOP_K):
        wsum = wsum + wsel[k]
    for k in range(TOP_K):
        wts_ref[:, k:k + 1] = wsel[k] / wsum * ROUTED_SCALE

    ri = lax.broadcasted_iota(jnp.int32, (tm, tm), 0)
    ci = lax.broadcasted_iota(jnp.int32, (tm, tm), 1)
    ltri = jnp.where(ri > ci, 1.0, 0.0).astype(BF16)
    cum = _dot(ltri, selm.astype(BF16)) + carry_ref[...]
    for k in range(TOP_K):
        rk = jnp.sum(jnp.where(onehots[k], cum, 0.0), axis=-1, keepdims=True)
        rank_ref[:, k:k + 1] = rk.astype(jnp.int32)
    total = carry_ref[...] + jnp.sum(selm, axis=0, keepdims=True)
    carry_ref[...] = total
    cnt_ref[...] = total


def _router(h2, w_hi, w_lo, bias):
    T, D = h2.shape
    tm = 256
    full = lambda a: pl.BlockSpec(a.shape, lambda i: (0, 0))
    o8 = pl.BlockSpec((tm, TOP_K), lambda i: (i, 0))
    return pl.pallas_call(
        _router_kernel,
        grid=(T // tm,),
        in_specs=[pl.BlockSpec((tm, D), lambda i: (i, 0)), full(w_hi), full(w_lo), full(bias)],
        out_specs=[o8, o8, o8, pl.BlockSpec((1, N_EXPERTS), lambda i: (0, 0))],
        out_shape=[jax.ShapeDtypeStruct((T, TOP_K), jnp.int32),
                   jax.ShapeDtypeStruct((T, TOP_K), F32),
                   jax.ShapeDtypeStruct((T, TOP_K), jnp.int32),
                   jax.ShapeDtypeStruct((1, N_EXPERTS), F32)],
        scratch_shapes=[pltpu.VMEM((1, N_EXPERTS), F32)],
        compiler_params=_cparams("arbitrary"),
        name="router",
    )(h2, w_hi, w_lo, bias)


def _dest_kernel(eidx_ref, rank_ref, ps_ref, dest_ref):
    tm = eidx_ref.shape[0]
    lane = lax.broadcasted_iota(jnp.int32, (tm, N_EXPERTS), 1)
    ps = ps_ref[...]
    for k in range(TOP_K):
        start = jnp.sum(jnp.where(lane == eidx_ref[:, k:k + 1], ps, 0.0), axis=-1, keepdims=True)
        dest_ref[:, k:k + 1] = start.astype(jnp.int32) + rank_ref[:, k:k + 1]


def _dest(eidx, rank, pad_start_f):
    T = eidx.shape[0]
    tm = 1024 if T % 1024 == 0 else T
    o8 = pl.BlockSpec((tm, TOP_K), lambda i: (i, 0))
    return pl.pallas_call(
        _dest_kernel,
        grid=(T // tm,),
        in_specs=[o8, o8, pl.BlockSpec((1, N_EXPERTS), lambda i: (0, 0))],
        out_specs=o8,
        out_shape=jax.ShapeDtypeStruct((T, TOP_K), jnp.int32),
        compiler_params=_cparams("parallel"),
        name="dest",
    )(eidx, rank, pad_start_f)


SC_WINDOW = 128


def _sc_gather(table, idx):
    _, lanes = table.shape
    n = idx.shape[0]
    mesh = plsc.VectorSubcoreMesh(core_axis_name="core", subcore_axis_name="subcore")

    @functools.partial(pl.kernel, out_type=jax.ShapeDtypeStruct((n, lanes), table.dtype), mesh=mesh,
                       name="sc_row_gather")
    def gather(x_hbm, i_hbm, o_hbm):
        def body(i_vmem, o_vmem):
            pltpu.sync_copy(x_hbm.at[i_vmem.at[0]], o_vmem)

        pltpu.emit_pipeline(
            body,
            grid=(n // SC_WINDOW,),
            in_specs=[pl.BlockSpec((1, SC_WINDOW), lambda i: (0, i))],
            out_specs=[pl.BlockSpec((SC_WINDOW, lanes), lambda i: (i, 0))],
            core_axis_name=("core", "subcore"),
            dimension_semantics=(pltpu.PARALLEL,),
            trace_scopes=False,
        )(i_hbm, o_hbm)

    return gather(table, idx.reshape(1, n))


def _sc_scatter(src, idx, n_out):
    rows, lanes = src.shape
    n_idx = idx.shape[0]
    mesh = plsc.VectorSubcoreMesh(core_axis_name="core", subcore_axis_name="subcore")

    @functools.partial(pl.kernel, out_type=jax.ShapeDtypeStruct((n_out, lanes), src.dtype), mesh=mesh,
                       name="sc_row_scatter")
    def scatter(x_hbm, i_hbm, o_hbm):
        def body(x_vmem, *i_vmems):
            for i_vmem in i_vmems:
                pltpu.sync_copy(x_vmem, o_hbm.at[i_vmem.at[0]])

        pltpu.emit_pipeline(
            body,
            grid=(rows // SC_WINDOW,),
            in_specs=[pl.BlockSpec((SC_WINDOW, lanes), lambda i: (i, 0))]
                     + [pl.BlockSpec((1, SC_WINDOW), lambda i, j=j: (j, i)) for j in range(n_idx)],
            out_specs=[],
            core_axis_name=("core", "subcore"),
            dimension_semantics=(pltpu.PARALLEL,),
            trace_scopes=False,
        )(x_hbm, *([i_hbm] * n_idx))

    return scatter(src, idx)


def _expert_kernel(bexp_ref, nused_ref, x_ref, wg_ref, wu_ref, wd_ref, y_ref, wg_b, wu_b, wd_b):
    i = pl.program_id(0)

    @pl.when(i < nused_ref[0])
    def _():
        @pl.when((i == 0) | (bexp_ref[i] != bexp_ref[jnp.maximum(i - 1, 0)]))
        def _():
            wg_b[...] = wg_ref[0].astype(BF16)
            wu_b[...] = wu_ref[0].astype(BF16)
            wd_b[...] = wd_ref[0].astype(BF16)

        x = jnp.concatenate(_unpack_bf16_pairs(_from_tiles(x_ref)), axis=1).astype(BF16)
        gate = _dot(x, wg_b[...])
        up = _dot(x, wu_b[...])
        act = (gate * jax.nn.sigmoid(gate) * up).astype(BF16)
        _to_tiles(y_ref, _pack_bf16_pairs(_dot(act, wd_b[...])))


def _experts(blk_exp, n_used, xs_t, wg, wu, wd):
    NP = xs_t.shape[0] * ROWS8
    D = D_MODEL
    nb = NP // MOE_BLOCK
    blk = (MOE_BLOCK // ROWS8, SUB, ROWS8, 128)

    def xmap(i, bexp, nused):
        return (jnp.minimum(i, nused[0] - 1), 0, 0, 0)

    def wmap(i, bexp, nused):
        return (bexp[jnp.minimum(i, nused[0] - 1)], 0, 0)

    grid_spec = pltpu.PrefetchScalarGridSpec(
        num_scalar_prefetch=2,
        grid=(nb,),
        in_specs=[pl.BlockSpec(blk, xmap),
                  pl.BlockSpec((1, D, D_EXPERT), wmap),
                  pl.BlockSpec((1, D, D_EXPERT), wmap),
                  pl.BlockSpec((1, D_EXPERT, D), wmap)],
        out_specs=pl.BlockSpec(blk, xmap),
        scratch_shapes=[pltpu.VMEM((D, D_EXPERT), BF16), pltpu.VMEM((D, D_EXPERT), BF16),
                        pltpu.VMEM((D_EXPERT, D), BF16)],
    )
    return pl.pallas_call(
        _expert_kernel,
        grid_spec=grid_spec,
        out_shape=jax.ShapeDtypeStruct(xs_t.shape, jnp.int32),
        compiler_params=_cparams("arbitrary"),
        name="experts",
    )(blk_exp, n_used, xs_t, wg, wu, wd)


def _combine_kernel(h_ref, wts_ref, yg_ref, wsg_ref, wsu_ref, wsd_ref, g_ref, b_ref, o_ref):
    h = h_ref[...]
    hb = h.astype(BF16)
    gate = _dot(hb, wsg_ref[...])
    up = _dot(hb, wsu_ref[...])
    shared = _dot((gate * jax.nn.sigmoid(gate) * up).astype(BF16), wsd_ref[...])
    wts = wts_ref[...]
    lo, hi = _unpack_bf16_pairs(_from_tiles(yg_ref, (0,)))
    r_lo, r_hi = wts[:, 0:1] * lo, wts[:, 0:1] * hi
    for k in range(1, TOP_K):
        lo, hi = _unpack_bf16_pairs(_from_tiles(yg_ref, (k,)))
        r_lo, r_hi = r_lo + wts[:, k:k + 1] * lo, r_hi + wts[:, k:k + 1] * hi
    routed = jnp.concatenate([r_lo, r_hi], axis=1)
    o_ref[...] = _layer_norm(DN_ALPHA * h + (routed + shared), g_ref[...], b_ref[...])


def _combine(h2, wts, yg_t, wsg, wsu, wsd, g, b):
    T, D = h2.shape
    tm = 128
    full = lambda a: pl.BlockSpec(a.shape, lambda i: (0, 0))
    return pl.pallas_call(
        _combine_kernel,
        grid=(T // tm,),
        in_specs=[pl.BlockSpec((tm, D), lambda i: (i, 0)),
                  pl.BlockSpec((tm, TOP_K), lambda i: (i, 0)),
                  pl.BlockSpec((TOP_K, tm // ROWS8, SUB, ROWS8, 128), lambda i: (0, i, 0, 0, 0)),
                  full(wsg), full(wsu), full(wsd), full(g), full(b)],
        out_specs=pl.BlockSpec((tm, D), lambda i: (i, 0)),
        out_shape=jax.ShapeDtypeStruct((T, D), F32),
        compiler_params=_cparams("parallel"),
        name="combine_ln",
    )(h2, wts, yg_t, wsg, wsu, wsd, g, b)


def _regroup_w_in(w_in):
    parts, off = {}, 0
    for name, width in _SPLITS:
        parts[name] = w_in[:, off:off + width]
        off += width
    cols = []
    for name, (_, width) in _COLS.items():
        p = parts[name]
        if p.shape[1] < width:
            p = jnp.pad(p, ((0, 0), (0, width - p.shape[1])))
        cols.append(p)
    return jnp.concatenate(cols, axis=1).astype(BF16)


def _overlap_matrix(S):
    nr = S // CMP_STRIDE
    n_sel = S // SEL_LEN
    ci = np.arange(nr)[:, None] * CMP_STRIDE
    sj = np.arange(n_sel)[None, :] * SEL_LEN
    ov = np.clip(np.minimum(ci + CMP_LEN, sj + SEL_LEN) - np.maximum(ci, sj), 0, None) / CMP_LEN
    ov[nr - 1] = 0.0
    return jnp.asarray(ov.T, BF16)


def _mixers(h, positions, w_in, w_alpha2, b_alpha, gla_norm_g,
            cmp_pos_k, cmp_w1_k, cmp_b1_k, cmp_w2_k, cmp_pos_v, cmp_w1_v, cmp_b1_v, cmp_w2_v):
    B, S, D = h.shape
    proj = _proj(h.reshape(B * S, D), _regroup_w_in(w_in)).reshape(B, S, D_PROJ)

    wa = jnp.pad(w_alpha2, ((0, 128 - GLA_LOWRANK), (0, 0))).astype(BF16)
    o_gla = _gla(proj, wa, b_alpha.reshape(1, -1), gla_norm_g.reshape(1, -1))

    half = NSA_DH // 2
    inv = ROPE_THETA ** (-np.arange(half, dtype=np.float32) / half)
    inv_row = jnp.asarray(np.tile(inv, 128 // half).reshape(1, 128), F32)
    pos3 = positions.astype(F32).reshape(B, S, 1)
    q_r, kc_r, ks_r, vs_r, kw_r, vw_r = _nsa_prep(proj, pos3, inv_row)

    def blocks16(t):
        return (t.reshape(B, S // CMP_STRIDE, CMP_STRIDE, NSA_KV_GROUPS, NSA_DH)
                .transpose(0, 3, 1, 2, 4).reshape(B, NSA_KV_GROUPS, S // CMP_STRIDE, CMP_STRIDE * NSA_DH))

    vc_off = _COLS["nvc"][0]
    kcmp, vcmp = _compress(
        blocks16(kc_r), blocks16(proj[:, :, vc_off:vc_off + 128]),
        cmp_pos_k.reshape(2, -1), cmp_w1_k.astype(BF16), cmp_b1_k.reshape(1, -1), cmp_w2_k.astype(BF16),
        cmp_pos_v.reshape(2, -1), cmp_w1_v.astype(BF16), cmp_b1_v.reshape(1, -1), cmp_w2_v.astype(BF16))
    o_nsa = _nsa_attn(q_r, kcmp, vcmp, ks_r, vs_r, kw_r, vw_r, proj, _overlap_matrix(S))
    return o_gla, o_nsa


def _moe_ln(h2, h_t, w_router, router_bias, w_gate, w_up, w_down, ws_gate, ws_up, ws_down, ln_g, ln_b):
    T, D = h2.shape
    P = T * TOP_K
    w_hi = w_router.astype(BF16)
    w_lo = (w_router - w_hi.astype(F32)).astype(BF16)
    eidx, wts, rank, counts = _router(h2, w_hi, w_lo, router_bias.reshape(1, -1))

    counts = counts.reshape(-1).astype(jnp.int32)
    padded = (counts + MOE_BLOCK - 1) // MOE_BLOCK * MOE_BLOCK
    pad_end = jnp.cumsum(padded)
    pad_start = pad_end - padded
    nb = -(-P // MOE_BLOCK) + N_EXPERTS
    n_used = (pad_end[-1] // MOE_BLOCK).astype(jnp.int32).reshape(1)
    blk_start = jnp.arange(nb, dtype=jnp.int32) * MOE_BLOCK
    blk_exp = jnp.minimum(jnp.sum((pad_end[None, :] <= blk_start[:, None]).astype(jnp.int32), axis=1),
                          N_EXPERTS - 1)

    NP = nb * MOE_BLOCK
    dest = _dest(eidx, rank, pad_start.astype(F32).reshape(1, -1))
    col = (jnp.arange(SUB, dtype=jnp.int32) * ROWS8)

    n_pad = NP - P
    assert n_pad % T == 0
    pad_cnt = padded - counts
    pad_hi = jnp.cumsum(pad_cnt)
    pad_lo = pad_hi - pad_cnt
    j = jnp.arange(n_pad, dtype=jnp.int32)[:, None]
    owner = (pad_lo[None, :] <= j) & (j < pad_hi[None, :])
    in_expert = jnp.sum(jnp.where(owner, (pad_start + counts - pad_lo)[None, :] + j, 0), axis=1)
    pad_rows = jnp.where(j[:, 0] < pad_hi[-1], in_expert, pad_end[-1] + j[:, 0] - pad_hi[-1])

    rows_all = jnp.concatenate([dest.T, pad_rows.reshape(n_pad // T, T)], axis=0)
    dst = _tile_rows(rows_all).reshape(-1, T // ROWS8, 1, ROWS8) + col[None, None, :, None]
    xs_t = _sc_scatter(h_t.reshape(T * SUB, 128), dst.reshape(-1, T * SUB), NP * SUB)
    xs_t = xs_t.reshape(NP // ROWS8, SUB, ROWS8, 128)
    ys_t = _experts(blk_exp, n_used, xs_t, w_gate, w_up, w_down)
    src = _tile_rows(dest.T).reshape(TOP_K, T // ROWS8, 1, ROWS8) + col[None, None, :, None]
    yg_t = _sc_gather(ys_t.reshape(NP * SUB, 128), src.reshape(-1)).reshape(TOP_K, T // ROWS8, SUB, ROWS8, 128)
    return _combine(h2, wts, yg_t, ws_gate.astype(BF16), ws_up.astype(BF16), ws_down.astype(BF16),
                    ln_g.reshape(1, -1), ln_b.reshape(1, -1))


def kernel(x, positions, w_in, w_alpha2, b_alpha, gla_norm_g, cmp_pos_k, cmp_w1_k, cmp_b1_k, cmp_w2_k, cmp_pos_v, cmp_w1_v, cmp_b1_v, cmp_w2_v, w_out, ln1_g, ln1_b, w_router, router_bias, w_exp_gate, w_exp_up, w_exp_down, w_sh_gate, w_sh_up, w_sh_down, ln2_g, ln2_b):
    def layer(h, pos, l):
        B, S, D = h.shape
        o_gla, o_nsa = _mixers(h, pos, w_in[l], w_alpha2[l], b_alpha[l], gla_norm_g[l],
                               cmp_pos_k[l], cmp_w1_k[l], cmp_b1_k[l], cmp_w2_k[l],
                               cmp_pos_v[l], cmp_w1_v[l], cmp_b1_v[l], cmp_w2_v[l])
        h1, h1_t = _outproj(h.reshape(B * S, D), o_gla.reshape(B * S, -1), o_nsa.reshape(B * S, -1),
                            w_out[l].astype(BF16), ln1_g[l].reshape(1, -1), ln1_b[l].reshape(1, -1))
        h2 = _moe_ln(h1, h1_t, w_router[l], router_bias[l], w_exp_gate[l], w_exp_up[l], w_exp_down[l],
                     w_sh_gate[l], w_sh_up[l], w_sh_down[l], ln2_g[l], ln2_b[l])
        return h2.reshape(B, S, D)

    n_groups = BATCH_GROUPS if x.shape[0] % BATCH_GROUPS == 0 else 1
    hs = jnp.split(x, n_groups, axis=0)
    ps = jnp.split(positions, n_groups, axis=0)
    for l in range(w_in.shape[0]):
        hs = [layer(h, p, l) for h, p in zip(hs, ps)]
    return jnp.concatenate(hs, axis=0)
```

```python
import functools

import numpy as np
import jax
import jax.numpy as jnp
from jax import lax
from jax.experimental import pallas as pl
from jax.experimental.pallas import tpu as pltpu
from jax.experimental.pallas import tpu_sc as plsc

D_MODEL = 1024
GLA_HEADS = 4
GLA_DV = 128
GLA_DK = 64
GLA_LOWRANK = 16
GLA_TAU = 16.0
GLA_CHUNK = 64
NSA_HEADS = 8
NSA_KV_GROUPS = 2
NSA_HPG = 4
NSA_DH = 64
CMP_LEN = 32
CMP_STRIDE = 16
CMP_HIDDEN = 256
SEL_LEN = 64
SEL_TOPK = 16
WINDOW = 512
ROPE_THETA = 10000.0
N_EXPERTS = 256
TOP_K = 8
N_GROUPS = 8
TOPK_GROUPS = 4
D_EXPERT = 256
ROUTED_SCALE = 2.5
DEPTH = 1
DN_ALPHA = (2.0 * DEPTH) ** 0.25
LN_EPS = 1e-5

MOE_BLOCK = 512
BATCH_GROUPS = 2
NEG = -1e30
F32 = jnp.float32
BF16 = jnp.bfloat16

_COLS = {}
_off = 0
for _name, _w in (("gq", 256), ("gk", 256), ("gv", 512), ("gr", 512), ("nq", 512),
                  ("nkc", 128), ("nvc", 128), ("nks", 128), ("nvs", 128), ("nkw", 128), ("nvw", 128),
                  ("ga", 128), ("ng", 128)):
    _COLS[_name] = (_off, _w)
    _off += _w
D_PROJ = _off
_SPLITS = (("gq", 256), ("gk", 256), ("gv", 512), ("ga", 16), ("gr", 512), ("nq", 512),
           ("nkc", 128), ("nvc", 128), ("nks", 128), ("nvs", 128), ("nkw", 128), ("nvw", 128), ("ng", 24))

VMEM_LIMIT = 56 * 1024 * 1024


def _cparams(*sem):
    return pltpu.CompilerParams(dimension_semantics=sem, vmem_limit_bytes=VMEM_LIMIT)


def _dot(a, b):
    return jnp.dot(a, b, preferred_element_type=F32)


def _dot_nt(a, b):
    return lax.dot_general(a, b, (((1,), (1,)), ((), ())), preferred_element_type=F32)


def _split3(x):
    hi = x.astype(BF16)
    r1 = x - hi.astype(F32)
    mid = r1.astype(BF16)
    lo = (r1 - mid.astype(F32)).astype(BF16)
    return hi, mid, lo


def _proj_kernel(x_ref, w_ref, o_ref):
    o_ref[...] = _dot(x_ref[...].astype(BF16), w_ref[...])


def _proj(x2, w):
    T, D = x2.shape
    N = w.shape[1]
    tm, tn = 512, N
    return pl.pallas_call(
        _proj_kernel,
        grid=(T // tm, N // tn),
        in_specs=[pl.BlockSpec((tm, D), lambda i, j: (i, 0)),
                  pl.BlockSpec((D, tn), lambda i, j: (0, j))],
        out_specs=pl.BlockSpec((tm, tn), lambda i, j: (i, j)),
        out_shape=jax.ShapeDtypeStruct((T, N), F32),
        compiler_params=_cparams("parallel", "arbitrary"),
        name="proj",
    )(x2, w)


def _gla_kernel(q_ref, k_ref, v_ref, r_ref, a_ref, wa_ref, ba_ref, ng_ref, o_ref, st_ref, *, nchunk):
    C = GLA_CHUNK
    HK = GLA_HEADS * GLA_DK

    @pl.when(pl.program_id(1) == 0)
    def _():
        st_ref[...] = jnp.zeros_like(st_ref)

    ri = lax.broadcasted_iota(jnp.int32, (C, C), 0)
    ci = lax.broadcasted_iota(jnp.int32, (C, C), 1)
    causal = ri >= ci
    tri = jnp.where(causal, 1.0, 0.0).astype(BF16)
    lane_head = lax.broadcasted_iota(jnp.int32, (1, HK), 1) // GLA_DK
    wa = wa_ref[...]
    ba = ba_ref[...]
    ng = ng_ref[...]

    def chunk(c, carry):
        for bb in range(q_ref.shape[0]):
            chunk_one(c, bb)
        return carry

    def chunk_one(c, bb):
        rows = pl.ds(pl.multiple_of(c * C, C), C)
        q = q_ref[bb, rows, :]
        k = k_ref[bb, rows, :]
        v = v_ref[bb, rows, :]
        r = r_ref[bb, rows, :]
        a = a_ref[bb, rows, :]
        z = _dot(a.astype(BF16), wa) + ba
        g = (jnp.minimum(z, 0.0) - jnp.log1p(jnp.exp(-jnp.abs(z)))) * (1.0 / GLA_TAU)
        g_hi, g_mid, g_lo = _split3(g)
        b = _dot(tri, g_hi) + _dot(tri, g_mid) + _dot(tri, g_lo)
        b_last = b[C - 1:C, :]
        qt = q * jnp.exp(b) * (GLA_DK ** -0.5)
        kt = (k * jnp.exp(-b)).astype(BF16)
        ks = (k * jnp.exp(b_last - b)).astype(BF16)
        st = st_ref[bb]
        st_b = st.astype(BF16)
        new_st = st * jnp.exp(b_last)
        for h in range(GLA_HEADS):
            hm = lane_head == h
            qh = jnp.where(hm, qt, 0.0).astype(BF16)
            att = jnp.where(causal, _dot_nt(qh, kt), 0.0)
            vh = v[:, h * GLA_DV:(h + 1) * GLA_DV].astype(BF16)
            o = _dot_nt(qh, st_b) + _dot(att.astype(BF16), vh)
            new_st = new_st + jnp.where(hm, _dot(vh.T, ks), 0.0)
            o = o * lax.rsqrt(jnp.mean(o * o, axis=-1, keepdims=True) + 1e-6) * ng
            rh = r[:, h * GLA_DV:(h + 1) * GLA_DV]
            o = o * (rh * jax.nn.sigmoid(rh))
            o_ref[bb, rows, h * GLA_DV:(h + 1) * GLA_DV] = o.astype(o_ref.dtype)
        st_ref[bb] = new_st

    lax.fori_loop(0, nchunk, chunk, 0)


def _gla(proj3, wa, ba, ng):
    B, S, _ = proj3.shape
    L = 512 if S % 512 == 0 else S
    nchunk = L // GLA_CHUNK

    def col(name, width):
        off = _COLS[name][0]
        assert off % width == 0
        return pl.BlockSpec((nbat, L, width), lambda b, j, o=off // width: (b, j, o))

    nbat = 2 if B % 2 == 0 else 1
    full2 = lambda shape: pl.BlockSpec(shape, lambda b, j: (0, 0))
    return pl.pallas_call(
        functools.partial(_gla_kernel, nchunk=nchunk),
        grid=(B // nbat, S // L),
        in_specs=[col("gq", 256), col("gk", 256), col("gv", 512), col("gr", 512), col("ga", 128),
                  full2((128, 256)), full2((1, 256)), full2((1, 128))],
        out_specs=pl.BlockSpec((nbat, L, 512), lambda b, j: (b, j, 0)),
        out_shape=jax.ShapeDtypeStruct((B, S, 512), BF16),
        scratch_shapes=[pltpu.VMEM((nbat, GLA_DV, GLA_HEADS * GLA_DK), F32)],
        compiler_params=_cparams("parallel", "arbitrary"),
        name="gla",
    )(proj3, proj3, proj3, proj3, proj3, wa, ba, ng)


def _rot_half(x):
    n = x.shape[-1]
    lane = lax.broadcasted_iota(jnp.int32, (1, n), 1)
    first = (lane % NSA_DH) < (NSA_DH // 2)
    return jnp.where(first, -pltpu.roll(x, n - NSA_DH // 2, 1), pltpu.roll(x, NSA_DH // 2, 1))


def _nsa_prep_kernel(pos_ref, inv_ref, q_ref, kc_ref, ks_ref, vs_ref, kw_ref, vw_ref,
                     qo_ref, kco_ref, kso_ref, vso_ref, kwo_ref, vwo_ref):
    ang = pos_ref[0] * inv_ref[...]
    cos = jnp.cos(ang)
    sin = jnp.sin(ang)
    cos4 = jnp.concatenate([cos] * 4, axis=1)
    sin4 = jnp.concatenate([sin] * 4, axis=1)

    q = q_ref[0]
    qr = q * cos4 + _rot_half(q) * sin4
    for h in range(NSA_HEADS):
        qo_ref[0, h] = (qr[:, h * NSA_DH:(h + 1) * NSA_DH] * (NSA_DH ** -0.5)).astype(qo_ref.dtype)

    def rope128(x):
        return x * cos + _rot_half(x) * sin

    kco_ref[0] = rope128(kc_ref[0])
    ksr = rope128(ks_ref[0])
    kwr = rope128(kw_ref[0])
    vs = vs_ref[0]
    vw = vw_ref[0]
    ones_col = jnp.where(lax.broadcasted_iota(jnp.int32, (vs.shape[0], NSA_DH), 1) == 0, 1.0, 0.0)
    for g in range(NSA_KV_GROUPS):
        sl = slice(g * NSA_DH, (g + 1) * NSA_DH)
        kso_ref[0, g] = ksr[:, sl].astype(kso_ref.dtype)
        kwo_ref[0, g] = kwr[:, sl].astype(kwo_ref.dtype)
        vso_ref[0, g] = jnp.concatenate([vs[:, sl], ones_col], axis=1).astype(vso_ref.dtype)
        vwo_ref[0, g] = jnp.concatenate([vw[:, sl], ones_col], axis=1).astype(vwo_ref.dtype)


def _nsa_prep(proj3, pos3, inv_row):
    B, S, _ = proj3.shape
    ts = 512 if S % 512 == 0 else S

    def col(name, width):
        off = _COLS[name][0]
        assert off % width == 0
        return pl.BlockSpec((1, ts, width), lambda b, j, o=off // width: (b, j, o))

    kv_spec = pl.BlockSpec((1, NSA_KV_GROUPS, ts, NSA_DH), lambda b, j: (b, 0, j, 0))
    kv_shape = jax.ShapeDtypeStruct((B, NSA_KV_GROUPS, S, NSA_DH), BF16)
    vx_spec = pl.BlockSpec((1, NSA_KV_GROUPS, ts, 2 * NSA_DH), lambda b, j: (b, 0, j, 0))
    vx_shape = jax.ShapeDtypeStruct((B, NSA_KV_GROUPS, S, 2 * NSA_DH), BF16)
    return pl.pallas_call(
        _nsa_prep_kernel,
        grid=(B, S // ts),
        in_specs=[pl.BlockSpec((1, ts, 1), lambda b, j: (b, j, 0)),
                  pl.BlockSpec((1, 128), lambda b, j: (0, 0)),
                  col("nq", 512), col("nkc", 128), col("nks", 128), col("nvs", 128),
                  col("nkw", 128), col("nvw", 128)],
        out_specs=[pl.BlockSpec((1, NSA_HEADS, ts, NSA_DH), lambda b, j: (b, 0, j, 0)),
                   pl.BlockSpec((1, ts, 128), lambda b, j: (b, j, 0)),
                   kv_spec, vx_spec, kv_spec, vx_spec],
        out_shape=[jax.ShapeDtypeStruct((B, NSA_HEADS, S, NSA_DH), BF16),
                   jax.ShapeDtypeStruct((B, S, 128), F32),
                   kv_shape, vx_shape, kv_shape, vx_shape],
        compiler_params=_cparams("parallel", "parallel"),
        name="nsa_prep",
    )(pos3, inv_row, proj3, proj3, proj3, proj3, proj3, proj3)


def _compress_kernel(k_ref, v_ref, pk_ref, w1k_ref, b1k_ref, w2k_ref, pv_ref, w1v_ref, b1v_ref, w2v_ref,
                     ko_ref, vo_ref):
    nr = k_ref.shape[1] // CMP_STRIDE
    hidden = w2k_ref.shape[0]

    def run(x_ref, p_ref, w1_ref, b1_ref, w2_ref, o_ref):
        first = jnp.zeros((nr, NSA_KV_GROUPS * hidden), F32)
        second = jnp.zeros((nr, NSA_KV_GROUPS * hidden), F32)
        for m in range(CMP_STRIDE):
            xm = x_ref[0, pl.ds(m, nr, stride=CMP_STRIDE), :]
            first = first + _dot((xm + p_ref[m:m + 1, :]).astype(BF16), w1_ref[m])
            second = second + _dot((xm + p_ref[CMP_STRIDE + m:CMP_STRIDE + m + 1, :]).astype(BF16),
                                   w1_ref[CMP_STRIDE + m])
        pre = first + pltpu.roll(second, nr - 1, 0) + b1_ref[...]
        hid = jax.nn.gelu(pre)
        for g in range(NSA_KV_GROUPS):
            out = _dot(hid[:, g * hidden:(g + 1) * hidden].astype(BF16), w2_ref[...])
            row = lax.broadcasted_iota(jnp.int32, out.shape, 0)
            o_ref[0, g] = jnp.where(row < nr - 1, out, 0.0)

    run(k_ref, pk_ref, w1k_ref, b1k_ref, w2k_ref, ko_ref)
    run(v_ref, pv_ref, w1v_ref, b1v_ref, w2v_ref, vo_ref)


def _compress_params(pos, w1, b1, w2):
    L, dh = pos.shape
    hidden = w1.shape[1]
    w = w1.reshape(L, dh, hidden).astype(BF16)
    z = jnp.zeros_like(w)
    w_bd = jnp.concatenate([jnp.concatenate([w, z], axis=2), jnp.concatenate([z, w], axis=2)], axis=1)
    return (jnp.tile(pos, (1, NSA_KV_GROUPS)), w_bd, jnp.tile(b1.reshape(1, -1), (1, NSA_KV_GROUPS)),
            w2.astype(BF16))


def _compress(kc_r, proj3, k_params, v_params):
    B, S, W = kc_r.shape
    NR = S // CMP_STRIDE
    full = lambda a: pl.BlockSpec(a.shape, lambda b: (0,) * a.ndim)
    o_spec = pl.BlockSpec((1, NSA_KV_GROUPS, NR, NSA_DH), lambda b: (b, 0, 0, 0))
    o_shape = jax.ShapeDtypeStruct((B, NSA_KV_GROUPS, NR, NSA_DH), F32)
    params = tuple(k_params) + tuple(v_params)
    return pl.pallas_call(
        _compress_kernel,
        grid=(B,),
        in_specs=[pl.BlockSpec((1, S, W), lambda b: (b, 0, 0)),
                  pl.BlockSpec((1, S, W), lambda b, o=_COLS["nvc"][0] // W: (b, 0, o))]
                 + [full(a) for a in params],
        out_specs=[o_spec, o_spec],
        out_shape=[o_shape, o_shape],
        compiler_params=_cparams("parallel"),
        name="compress",
    )(kc_r, proj3, *params)


def _nsa_attn_kernel(q_ref, kc_ref, vc_ref, ks_ref, vs_ref, kw_ref, vw_ref, gate_ref, ovt_ref, o_ref,
                     *, tq, S, n_sel, n_top):
    g = pl.program_id(1)
    qi = pl.program_id(2)
    start = qi * tq
    H = NSA_HPG
    qs = q_ref[0].reshape(H * tq, NSA_DH)
    t_col = start + lax.broadcasted_iota(jnp.int32, (tq, 1), 0)
    t_row = start + lax.broadcasted_iota(jnp.int32, (1, tq), 1)

    kc = kc_ref[0, 0].astype(BF16)
    vc = vc_ref[0, 0].astype(BF16)
    NR = kc.shape[0]
    s = _dot_nt(qs, kc)
    ncol = lax.broadcasted_iota(jnp.int32, (1, NR), 1)
    cmask = (ncol < NR - 1) & (ncol * CMP_STRIDE + (CMP_LEN - 1) <= t_col)
    s3 = jnp.where(cmask[None], s.reshape(H, tq, NR), NEG)
    m = jnp.max(s3, axis=-1, keepdims=True)
    e = jnp.exp(s3 - m)
    p3 = e / jnp.sum(e, axis=-1, keepdims=True)
    p3 = jnp.where(cmask[None], p3, 0.0)
    o_cmp = _dot(p3.reshape(H * tq, NR).astype(BF16), vc)

    psum = jnp.sum(p3, axis=0)
    p_hi = psum.astype(BF16)
    p_lo = (psum - p_hi.astype(F32)).astype(BF16)
    ovt = ovt_ref[...]
    imp = _dot_nt(ovt, p_hi) + _dot_nt(ovt, p_lo)
    blk = lax.broadcasted_iota(jnp.int32, (n_sel, 1), 0)
    cur = t_row // SEL_LEN
    causal_blk = blk <= cur
    forced = (blk == 0) | (blk == cur) | (blk == cur - 1)
    val = jnp.where(causal_blk, jnp.where(forced, jnp.inf, imp), -jnp.inf)
    rank = jnp.zeros((n_sel, tq), jnp.int32)
    for j in range(n_sel):
        vj = val[j:j + 1, :]
        beats = (vj > val) | ((vj == val) & (blk > j))
        rank = rank + beats.astype(jnp.int32)
    sel_t = jnp.where((rank < n_top) & causal_blk, 1.0, 0.0)
    sel_t = jnp.concatenate([sel_t, jnp.zeros((128 - n_sel, tq), F32)], axis=0)
    sel = sel_t.T.astype(BF16)

    tk = 256
    nsub = 2 if S % (2 * tk) == 0 else 1
    bpc = tk // SEL_LEN
    n_trips = (start + tq + nsub * tk - 1) // (nsub * tk)
    erow = lax.broadcasted_iota(jnp.int32, (128, tk), 0)
    ecol = lax.broadcasted_iota(jnp.int32, (128, tk), 1) // SEL_LEN
    kk = lax.broadcasted_iota(jnp.int32, (1, tk), 1)

    def sel_trip(c, carry):
        m_i, acc = carry
        scores, vals = [], []
        m_new = m_i
        for u in range(nsub):
            cu = c * nsub + u
            k0 = pl.multiple_of(cu * tk, tk)
            kb = ks_ref[0, 0, pl.ds(k0, tk), :]
            vals.append(vs_ref[0, 0, pl.ds(k0, tk), :])
            expand = jnp.where(erow == ecol + cu * bpc, 1.0, 0.0).astype(BF16)
            allowed = (_dot(sel, expand) > 0.5) & (k0 + kk <= t_col)
            bias = jnp.where(allowed, 0.0, NEG)
            sc3 = _dot_nt(qs, kb).reshape(H, tq, tk) + bias[None]
            scores.append(sc3)
            m_new = jnp.maximum(m_new, jnp.max(sc3, axis=-1, keepdims=True))
        acc = jnp.exp(m_i - m_new).reshape(H * tq, 1) * acc
        for u in range(nsub):
            pe = jnp.exp((scores[u] - m_new).astype(BF16))
            acc = acc + _dot(pe.reshape(H * tq, tk), vals[u])
        return m_new, acc

    m0 = jnp.full((H, tq, 1), NEG, F32)
    a0 = jnp.zeros((H * tq, 2 * NSA_DH), F32)
    _, acc_f = lax.fori_loop(0, n_trips, sel_trip, (m0, a0))
    o_sel = acc_f[:, :NSA_DH] / acc_f[:, NSA_DH:NSA_DH + 1]

    span = min(WINDOW + tq, S)
    ws = jnp.clip(start - WINDOW, 0, S - span)
    ws = pl.multiple_of(ws, tq)
    kwb = kw_ref[0, 0, pl.ds(ws, span), :]
    vwb = vw_ref[0, 0, pl.ds(ws, span), :]
    kp = ws + lax.broadcasted_iota(jnp.int32, (1, span), 1)
    wbias = jnp.where((kp <= t_col) & (kp > t_col - WINDOW), 0.0, NEG)
    sw3 = _dot_nt(qs, kwb).reshape(H, tq, span) + wbias[None]
    mw = jnp.max(sw3, axis=-1, keepdims=True)
    ew = jnp.exp((sw3 - mw).astype(BF16))
    aw = _dot(ew.reshape(H * tq, span), vwb)
    o_win = aw[:, :NSA_DH] / aw[:, NSA_DH:NSA_DH + 1]

    gs = jax.nn.sigmoid(gate_ref[0])
    for g_static in range(NSA_KV_GROUPS):
        @pl.when(g == g_static)
        def _(g_static=g_static):
            for h in range(H):
                c0 = g_static * H * 3 + h * 3
                rs = slice(h * tq, (h + 1) * tq)
                o = (gs[:, c0:c0 + 1] * o_cmp[rs] + gs[:, c0 + 1:c0 + 2] * o_sel[rs]
                     + gs[:, c0 + 2:c0 + 3] * o_win[rs])
                o_ref[0, :, h * NSA_DH:(h + 1) * NSA_DH] = o.astype(o_ref.dtype)


def _nsa_attn(q_r, kcmp, vcmp, ks_r, vs_r, kw_r, vw_r, proj3, ov):
    B, _, S, _ = q_r.shape
    G, H = NSA_KV_GROUPS, NSA_HPG
    NR = kcmp.shape[2]
    tq = 256
    n_sel = S // SEL_LEN
    n_top = min(SEL_TOPK, n_sel)
    cmp_spec = pl.BlockSpec((1, 1, NR, NSA_DH), lambda b, g, i: (b, g, 0, 0))
    kv_spec = pl.BlockSpec((1, 1, S, NSA_DH), lambda b, g, i: (b, g, 0, 0))
    vx_spec = pl.BlockSpec((1, 1, S, 2 * NSA_DH), lambda b, g, i: (b, g, 0, 0))
    goff = _COLS["ng"][0] // 128
    return pl.pallas_call(
        functools.partial(_nsa_attn_kernel, tq=tq, S=S, n_sel=n_sel, n_top=n_top),
        grid=(B, G, S // tq),
        in_specs=[pl.BlockSpec((1, H, tq, NSA_DH), lambda b, g, i: (b, g, i, 0)),
                  cmp_spec, cmp_spec, kv_spec, vx_spec, kv_spec, vx_spec,
                  pl.BlockSpec((1, tq, 128), lambda b, g, i: (b, i, goff)),
                  pl.BlockSpec(ov.shape, lambda b, g, i: (0, 0))],
        out_specs=pl.BlockSpec((1, tq, H * NSA_DH), lambda b, g, i: (b, i, g)),
        out_shape=jax.ShapeDtypeStruct((B, S, NSA_HEADS * NSA_DH), BF16),
        compiler_params=_cparams("parallel", "parallel", "arbitrary"),
        name="nsa_attn",
    )(q_r, kcmp, vcmp, ks_r, vs_r, kw_r, vw_r, proj3, ov)


def _layer_norm(x, g, b):
    mu = jnp.mean(x, axis=-1, keepdims=True)
    xc = x - mu
    var = jnp.mean(xc * xc, axis=-1, keepdims=True)
    return xc * lax.rsqrt(var + LN_EPS) * g + b


SUB = D_MODEL // 2 // 128
ROWS8 = 8


def _pack_bf16_pairs(v):
    half = v.shape[1] // 2
    bits = pltpu.bitcast(v.astype(BF16).astype(F32), jnp.uint32)
    return pltpu.bitcast((bits[:, :half] >> 16) | bits[:, half:], jnp.int32)


def _unpack_bf16_pairs(p):
    u = pltpu.bitcast(p, jnp.uint32)
    return pltpu.bitcast(u << 16, F32), pltpu.bitcast(u & jnp.uint32(0xFFFF0000), F32)


def _tile_rows(t):
    return (t // ROWS8) * (SUB * ROWS8) + t % ROWS8


def _to_tiles(ref, val):
    rows = val.shape[0]
    for c in range(SUB):
        ref[:, c] = val[:, c * 128:(c + 1) * 128].reshape(rows // ROWS8, ROWS8, 128)


def _from_tiles(ref, lead=()):
    groups = ref.shape[len(lead)]
    return jnp.concatenate([ref[lead + (slice(None), c)].reshape(groups * ROWS8, 128) for c in range(SUB)], axis=1)


def _outproj_kernel(x_ref, og_ref, on_ref, w_ref, g_ref, b_ref, o_ref, ot_ref):
    half = og_ref.shape[1]
    mix = _dot(og_ref[...], w_ref[0:half, :]) + _dot(on_ref[...], w_ref[half:, :])
    h = _layer_norm(DN_ALPHA * x_ref[...] + mix, g_ref[...], b_ref[...])
    o_ref[...] = h
    _to_tiles(ot_ref, _pack_bf16_pairs(h))


def _outproj(x2, og2, on2, w, g, b):
    T, D = x2.shape
    tm = 512
    row = lambda width: pl.BlockSpec((tm, width), lambda i: (i, 0))
    full = lambda a: pl.BlockSpec(a.shape, lambda i: (0, 0))
    return pl.pallas_call(
        _outproj_kernel,
        grid=(T // tm,),
        in_specs=[row(D), row(og2.shape[1]), row(on2.shape[1]), full(w), full(g), full(b)],
        out_specs=[row(D), pl.BlockSpec((tm // ROWS8, SUB, ROWS8, 128), lambda i: (i, 0, 0, 0))],
        out_shape=[jax.ShapeDtypeStruct((T, D), F32),
                   jax.ShapeDtypeStruct((T // ROWS8, SUB, ROWS8, 128), jnp.int32)],
        compiler_params=_cparams("parallel"),
        name="outproj_ln",
    )(x2, og2, on2, w, g, b)


def _router_kernel(h_ref, wh_ref, wl_ref, bias_ref, eidx_ref, wts_ref, rank_ref, cnt_ref, carry_ref):
    @pl.when(pl.program_id(0) == 0)
    def _():
        carry_ref[...] = jnp.zeros_like(carry_ref)

    h = h_ref[...]
    tm = h.shape[0]
    E = N_EXPERTS
    h_hi = h.astype(BF16)
    h_lo = (h - h_hi.astype(F32)).astype(BF16)
    wh = wh_ref[...]
    logits = _dot(h_hi, wh) + _dot(h_lo, wh) + _dot(h_hi, wl_ref[...])
    scores = jax.nn.sigmoid(logits)
    biased = scores + bias_ref[...]
    lane_i = lax.broadcasted_iota(jnp.int32, (tm, E), 1)
    gid = lane_i // (E // N_GROUPS)
    lane = lane_i.astype(F32)
    ninf = -jnp.inf

    def row_max(x):
        return jnp.max(x, axis=-1, keepdims=True)

    def first_idx(x, mx):
        return jnp.min(jnp.where(x == mx, lane, float(E)), axis=-1, keepdims=True)

    gscore = []
    for gi in range(N_GROUPS):
        mg = jnp.where(gid == gi, biased, ninf)
        m1 = row_max(mg)
        i1 = first_idx(mg, m1)
        m2 = row_max(jnp.where(lane == i1, ninf, mg))
        gscore.append(m1 + m2)
    emask = jnp.zeros((tm, E), jnp.bool_)
    for gi in range(N_GROUPS):
        rk = jnp.zeros((tm, 1), jnp.int32)
        for gj in range(N_GROUPS):
            if gj == gi:
                continue
            beats = (gscore[gj] > gscore[gi]) | ((gscore[gj] == gscore[gi]) & (gj < gi))
            rk = rk + beats.astype(jnp.int32)
        emask = emask | ((gid == gi) & (rk < TOPK_GROUPS))
    masked = jnp.where(emask, biased, ninf)

    onehots, wsel = [], []
    selm = jnp.zeros((tm, E), F32)
    for k in range(TOP_K):
        mx = row_max(masked)
        idx = first_idx(masked, mx)
        oh = lane == idx
        onehots.append(oh)
        wsel.append(jnp.sum(jnp.where(oh, scores, 0.0), axis=-1, keepdims=True))
        masked = jnp.where(oh, ninf, masked)
        selm = jnp.where(oh, 1.0, selm)
        eidx_ref[:, k:k + 1] = idx.astype(jnp.int32)
    wsum = wsel[0]
    for k in range(1, TOP_K):
        wsum = wsum + wsel[k]
    for k in range(TOP_K):
        wts_ref[:, k:k + 1] = wsel[k] / wsum * ROUTED_SCALE

    ri = lax.broadcasted_iota(jnp.int32, (tm, tm), 0)
    ci = lax.broadcasted_iota(jnp.int32, (tm, tm), 1)
    ltri = jnp.where(ri > ci, 1.0, 0.0).astype(BF16)
    cum = _dot(ltri, selm.astype(BF16)) + carry_ref[...]
    for k in range(TOP_K):
        rk = jnp.sum(jnp.where(onehots[k], cum, 0.0), axis=-1, keepdims=True)
        rank_ref[:, k:k + 1] = rk.astype(jnp.int32)
    total = carry_ref[...] + jnp.sum(selm, axis=0, keepdims=True)
    carry_ref[...] = total
    cnt_ref[...] = total


def _router(h2, w_hi, w_lo, bias):
    T, D = h2.shape
    tm = 256
    full = lambda a: pl.BlockSpec(a.shape, lambda i: (0, 0))
    o8 = pl.BlockSpec((tm, TOP_K), lambda i: (i, 0))
    return pl.pallas_call(
        _router_kernel,
        grid=(T // tm,),
        in_specs=[pl.BlockSpec((tm, D), lambda i: (i, 0)), full(w_hi), full(w_lo), full(bias)],
        out_specs=[o8, o8, o8, pl.BlockSpec((1, N_EXPERTS), lambda i: (0, 0))],
        out_shape=[jax.ShapeDtypeStruct((T, TOP_K), jnp.int32),
                   jax.ShapeDtypeStruct((T, TOP_K), F32),
                   jax.ShapeDtypeStruct((T, TOP_K), jnp.int32),
                   jax.ShapeDtypeStruct((1, N_EXPERTS), F32)],
        scratch_shapes=[pltpu.VMEM((1, N_EXPERTS), F32)],
        compiler_params=_cparams("arbitrary"),
        name="router",
    )(h2, w_hi, w_lo, bias)


def _dest_kernel(eidx_ref, rank_ref, ps_ref, dest_ref):
    tm = eidx_ref.shape[0]
    lane = lax.broadcasted_iota(jnp.int32, (tm, N_EXPERTS), 1)
    ps = ps_ref[...]
    for k in range(TOP_K):
        start = jnp.sum(jnp.where(lane == eidx_ref[:, k:k + 1], ps, 0.0), axis=-1, keepdims=True)
        dest_ref[:, k:k + 1] = start.astype(jnp.int32) + rank_ref[:, k:k + 1]


def _dest(eidx, rank, pad_start_f):
    T = eidx.shape[0]
    tm = 1024 if T % 1024 == 0 else T
    o8 = pl.BlockSpec((tm, TOP_K), lambda i: (i, 0))
    return pl.pallas_call(
        _dest_kernel,
        grid=(T // tm,),
        in_specs=[o8, o8, pl.BlockSpec((1, N_EXPERTS), lambda i: (0, 0))],
        out_specs=o8,
        out_shape=jax.ShapeDtypeStruct((T, TOP_K), jnp.int32),
        compiler_params=_cparams("parallel"),
        name="dest",
    )(eidx, rank, pad_start_f)


SC_WINDOW = 128


def _sc_gather(table, idx):
    _, lanes = table.shape
    n = idx.shape[0]
    mesh = plsc.VectorSubcoreMesh(core_axis_name="core", subcore_axis_name="subcore")

    @functools.partial(pl.kernel, out_type=jax.ShapeDtypeStruct((n, lanes), table.dtype), mesh=mesh,
                       name="sc_row_gather")
    def gather(x_hbm, i_hbm, o_hbm):
        def body(i_vmem, o_vmem):
            pltpu.sync_copy(x_hbm.at[i_vmem.at[0]], o_vmem)

        pltpu.emit_pipeline(
            body,
            grid=(n // SC_WINDOW,),
            in_specs=[pl.BlockSpec((1, SC_WINDOW), lambda i: (0, i))],
            out_specs=[pl.BlockSpec((SC_WINDOW, lanes), lambda i: (i, 0))],
            core_axis_name=("core", "subcore"),
            dimension_semantics=(pltpu.PARALLEL,),
            trace_scopes=False,
        )(i_hbm, o_hbm)

    return gather(table, idx.reshape(1, n))


def _sc_scatter(src, idx, n_out):
    rows, lanes = src.shape
    n_idx = idx.shape[0]
    mesh = plsc.VectorSubcoreMesh(core_axis_name="core", subcore_axis_name="subcore")

    @functools.partial(pl.kernel, out_type=jax.ShapeDtypeStruct((n_out, lanes), src.dtype), mesh=mesh,
                       name="sc_row_scatter")
    def scatter(x_hbm, i_hbm, o_hbm):
        def body(x_vmem, *i_vmems):
            for i_vmem in i_vmems:
                pltpu.sync_copy(x_vmem, o_hbm.at[i_vmem.at[0]])

        pltpu.emit_pipeline(
            body,
            grid=(rows // SC_WINDOW,),
            in_specs=[pl.BlockSpec((SC_WINDOW, lanes), lambda i: (i, 0))]
                     + [pl.BlockSpec((1, SC_WINDOW), lambda i, j=j: (j, i)) for j in range(n_idx)],
            out_specs=[],
            core_axis_name=("core", "subcore"),
            dimension_semantics=(pltpu.PARALLEL,),
            trace_scopes=False,
        )(x_hbm, *([i_hbm] * n_idx))

    return scatter(src, idx)


def _expert_kernel(bexp_ref, nused_ref, x_ref, wg_ref, wu_ref, wd_ref, y_ref):
    del bexp_ref

    @pl.when(pl.program_id(0) < nused_ref[0])
    def _():
        x = jnp.concatenate(_unpack_bf16_pairs(_from_tiles(x_ref)), axis=1).astype(BF16)
        gate = _dot(x, wg_ref[0])
        up = _dot(x, wu_ref[0])
        act = (gate * jax.nn.sigmoid(gate) * up).astype(BF16)
        _to_tiles(y_ref, _pack_bf16_pairs(_dot(act, wd_ref[0])))


def _experts(blk_exp, n_used, xs_t, wg, wu, wd):
    NP = xs_t.shape[0] * ROWS8
    D = D_MODEL
    nb = NP // MOE_BLOCK
    blk = (MOE_BLOCK // ROWS8, SUB, ROWS8, 128)

    def xmap(i, bexp, nused):
        return (jnp.minimum(i, nused[0] - 1), 0, 0, 0)

    def wmap(i, bexp, nused):
        return (bexp[jnp.minimum(i, nused[0] - 1)], 0, 0)

    grid_spec = pltpu.PrefetchScalarGridSpec(
        num_scalar_prefetch=2,
        grid=(nb,),
        in_specs=[pl.BlockSpec(blk, xmap),
                  pl.BlockSpec((1, D, D_EXPERT), wmap),
                  pl.BlockSpec((1, D, D_EXPERT), wmap),
                  pl.BlockSpec((1, D_EXPERT, D), wmap)],
        out_specs=pl.BlockSpec(blk, xmap),
    )
    return pl.pallas_call(
        _expert_kernel,
        grid_spec=grid_spec,
        out_shape=jax.ShapeDtypeStruct(xs_t.shape, jnp.int32),
        compiler_params=_cparams("arbitrary"),
        name="experts",
    )(blk_exp, n_used, xs_t, wg, wu, wd)


def _combine_kernel(h_ref, wts_ref, yg_ref, wsg_ref, wsu_ref, wsd_ref, g_ref, b_ref, o_ref):
    h = h_ref[...]
    hb = h.astype(BF16)
    gate = _dot(hb, wsg_ref[...])
    up = _dot(hb, wsu_ref[...])
    shared = _dot((gate * jax.nn.sigmoid(gate) * up).astype(BF16), wsd_ref[...])
    wts = wts_ref[...]
    lo, hi = _unpack_bf16_pairs(_from_tiles(yg_ref, (0,)))
    r_lo, r_hi = wts[:, 0:1] * lo, wts[:, 0:1] * hi
    for k in range(1, TOP_K):
        lo, hi = _unpack_bf16_pairs(_from_tiles(yg_ref, (k,)))
        r_lo, r_hi = r_lo + wts[:, k:k + 1] * lo, r_hi + wts[:, k:k + 1] * hi
    routed = jnp.concatenate([r_lo, r_hi], axis=1)
    o_ref[...] = _layer_norm(DN_ALPHA * h + (routed + shared), g_ref[...], b_ref[...])


def _combine(h2, wts, yg_t, wsg, wsu, wsd, g, b):
    T, D = h2.shape
    tm = 128
    full = lambda a: pl.BlockSpec(a.shape, lambda i: (0, 0))
    return pl.pallas_call(
        _combine_kernel,
        grid=(T // tm,),
        in_specs=[pl.BlockSpec((tm, D), lambda i: (i, 0)),
                  pl.BlockSpec((tm, TOP_K), lambda i: (i, 0)),
                  pl.BlockSpec((TOP_K, tm // ROWS8, SUB, ROWS8, 128), lambda i: (0, i, 0, 0, 0)),
                  full(wsg), full(wsu), full(wsd), full(g), full(b)],
        out_specs=pl.BlockSpec((tm, D), lambda i: (i, 0)),
        out_shape=jax.ShapeDtypeStruct((T, D), F32),
        compiler_params=_cparams("parallel"),
        name="combine_ln",
    )(h2, wts, yg_t, wsg, wsu, wsd, g, b)


def _regroup_w_in(w_in):
    parts, off = {}, 0
    for name, width in _SPLITS:
        parts[name] = w_in[:, off:off + width]
        off += width
    cols = []
    for name, (_, width) in _COLS.items():
        p = parts[name]
        if p.shape[1] < width:
            p = jnp.pad(p, ((0, 0), (0, width - p.shape[1])))
        cols.append(p)
    return jnp.concatenate(cols, axis=1).astype(BF16)


def _overlap_matrix(S):
    nr = S // CMP_STRIDE
    n_sel = S // SEL_LEN
    ci = np.arange(nr)[:, None] * CMP_STRIDE
    sj = np.arange(n_sel)[None, :] * SEL_LEN
    ov = np.clip(np.minimum(ci + CMP_LEN, sj + SEL_LEN) - np.maximum(ci, sj), 0, None) / CMP_LEN
    ov[nr - 1] = 0.0
    return jnp.asarray(ov.T, BF16)


def _mixers(h, positions, w_in, w_alpha2, b_alpha, gla_norm_g,
            cmp_pos_k, cmp_w1_k, cmp_b1_k, cmp_w2_k, cmp_pos_v, cmp_w1_v, cmp_b1_v, cmp_w2_v):
    B, S, D = h.shape
    proj = _proj(h.reshape(B * S, D), _regroup_w_in(w_in)).reshape(B, S, D_PROJ)

    wa = jnp.pad(w_alpha2, ((0, 128 - GLA_LOWRANK), (0, 0))).astype(BF16)
    o_gla = _gla(proj, wa, b_alpha.reshape(1, -1), gla_norm_g.reshape(1, -1))

    half = NSA_DH // 2
    inv = ROPE_THETA ** (-np.arange(half, dtype=np.float32) / half)
    inv_row = jnp.asarray(np.tile(inv, 128 // half).reshape(1, 128), F32)
    pos3 = positions.astype(F32).reshape(B, S, 1)
    q_r, kc_r, ks_r, vs_r, kw_r, vw_r = _nsa_prep(proj, pos3, inv_row)

    kcmp, vcmp = _compress(kc_r, proj,
                           _compress_params(cmp_pos_k, cmp_w1_k, cmp_b1_k, cmp_w2_k),
                           _compress_params(cmp_pos_v, cmp_w1_v, cmp_b1_v, cmp_w2_v))
    o_nsa = _nsa_attn(q_r, kcmp, vcmp, ks_r, vs_r, kw_r, vw_r, proj, _overlap_matrix(S))
    return o_gla, o_nsa


def _moe_ln(h2, h_t, w_router, router_bias, w_gate, w_up, w_down, ws_gate, ws_up, ws_down, ln_g, ln_b):
    T, D = h2.shape
    P = T * TOP_K
    w_hi = w_router.astype(BF16)
    w_lo = (w_router - w_hi.astype(F32)).astype(BF16)
    eidx, wts, rank, counts = _router(h2, w_hi, w_lo, router_bias.reshape(1, -1))

    counts = counts.reshape(-1).astype(jnp.int32)
    padded = (counts + MOE_BLOCK - 1) // MOE_BLOCK * MOE_BLOCK
    pad_end = jnp.cumsum(padded)
    pad_start = pad_end - padded
    nb = -(-P // MOE_BLOCK) + N_EXPERTS
    n_used = (pad_end[-1] // MOE_BLOCK).astype(jnp.int32).reshape(1)
    blk_start = jnp.arange(nb, dtype=jnp.int32) * MOE_BLOCK
    blk_exp = jnp.minimum(jnp.sum((pad_end[None, :] <= blk_start[:, None]).astype(jnp.int32), axis=1),
                          N_EXPERTS - 1)

    NP = nb * MOE_BLOCK
    dest = _dest(eidx, rank, pad_start.astype(F32).reshape(1, -1))
    col = (jnp.arange(SUB, dtype=jnp.int32) * ROWS8)

    n_pad = NP - P
    assert n_pad % T == 0
    pad_cnt = padded - counts
    pad_hi = jnp.cumsum(pad_cnt)
    pad_lo = pad_hi - pad_cnt
    j = jnp.arange(n_pad, dtype=jnp.int32)[:, None]
    owner = (pad_lo[None, :] <= j) & (j < pad_hi[None, :])
    in_expert = jnp.sum(jnp.where(owner, (pad_start + counts - pad_lo)[None, :] + j, 0), axis=1)
    pad_rows = jnp.where(j[:, 0] < pad_hi[-1], in_expert, pad_end[-1] + j[:, 0] - pad_hi[-1])

    rows_all = jnp.concatenate([dest.T, pad_rows.reshape(n_pad // T, T)], axis=0)
    dst = _tile_rows(rows_all).reshape(-1, T // ROWS8, 1, ROWS8) + col[None, None, :, None]
    xs_t = _sc_scatter(h_t.reshape(T * SUB, 128), dst.reshape(-1, T * SUB), NP * SUB)
    xs_t = xs_t.reshape(NP // ROWS8, SUB, ROWS8, 128)
    ys_t = _experts(blk_exp, n_used, xs_t, w_gate.astype(BF16), w_up.astype(BF16), w_down.astype(BF16))
    src = _tile_rows(dest.T).reshape(TOP_K, T // ROWS8, 1, ROWS8) + col[None, None, :, None]
    yg_t = _sc_gather(ys_t.reshape(NP * SUB, 128), src.reshape(-1)).reshape(TOP_K, T // ROWS8, SUB, ROWS8, 128)
    return _combine(h2, wts, yg_t, ws_gate.astype(BF16), ws_up.astype(BF16), ws_down.astype(BF16),
                    ln_g.reshape(1, -1), ln_b.reshape(1, -1))


def kernel(x, positions, w_in, w_alpha2, b_alpha, gla_norm_g, cmp_pos_k, cmp_w1_k, cmp_b1_k, cmp_w2_k, cmp_pos_v, cmp_w1_v, cmp_b1_v, cmp_w2_v, w_out, ln1_g, ln1_b, w_router, router_bias, w_exp_gate, w_exp_up, w_exp_down, w_sh_gate, w_sh_up, w_sh_down, ln2_g, ln2_b):
    def layer(h, pos, l):
        B, S, D = h.shape
        o_gla, o_nsa = _mixers(h, pos, w_in[l], w_alpha2[l], b_alpha[l], gla_norm_g[l],
                               cmp_pos_k[l], cmp_w1_k[l], cmp_b1_k[l], cmp_w2_k[l],
                               cmp_pos_v[l], cmp_w1_v[l], cmp_b1_v[l], cmp_w2_v[l])
        h1, h1_t = _outproj(h.reshape(B * S, D), o_gla.reshape(B * S, -1), o_nsa.reshape(B * S, -1),
                            w_out[l].astype(BF16), ln1_g[l].reshape(1, -1), ln1_b[l].reshape(1, -1))
        h2 = _moe_ln(h1, h1_t, w_router[l], router_bias[l], w_exp_gate[l], w_exp_up[l], w_exp_down[l],
                     w_sh_gate[l], w_sh_up[l], w_sh_down[l], ln2_g[l], ln2_b[l])
        return h2.reshape(B, S, D)

    n_groups = BATCH_GROUPS if x.shape[0] % BATCH_GROUPS == 0 else 1
    hs = jnp.split(x, n_groups, axis=0)
    ps = jnp.split(positions, n_groups, axis=0)
    for l in range(w_in.shape[0]):
        hs = [layer(h, p, l) for h, p in zip(hs, ps)]
    return jnp.concatenate(hs, axis=0)
```

```python
import functools

import numpy as np
import jax
import jax.numpy as jnp
from jax import lax
from jax.experimental import pallas as pl
from jax.experimental.pallas import tpu as pltpu
from jax.experimental.pallas import tpu_sc as plsc

D_MODEL = 1024
GLA_HEADS = 4
GLA_DV = 128
GLA_DK = 64
GLA_LOWRANK = 16
GLA_TAU = 16.0
GLA_CHUNK = 64
NSA_HEADS = 8
NSA_KV_GROUPS = 2
NSA_HPG = 4
NSA_DH = 64
CMP_LEN = 32
CMP_STRIDE = 16
CMP_HIDDEN = 256
SEL_LEN = 64
SEL_TOPK = 16
WINDOW = 512
ROPE_THETA = 10000.0
N_EXPERTS = 256
TOP_K = 8
N_GROUPS = 8
TOPK_GROUPS = 4
D_EXPERT = 256
ROUTED_SCALE = 2.5
DEPTH = 1
DN_ALPHA = (2.0 * DEPTH) ** 0.25
LN_EPS = 1e-5

MOE_BLOCK = 512
BATCH_GROUPS = 2
NEG = -1e30
F32 = jnp.float32
BF16 = jnp.bfloat16

_COLS = {}
_off = 0
for _name, _w in (("gq", 256), ("gk", 256), ("gv", 512), ("gr", 512), ("nq", 512),
                  ("nkc", 128), ("nvc", 128), ("nks", 128), ("nvs", 128), ("nkw", 128), ("nvw", 128),
                  ("ga", 128), ("ng", 128)):
    _COLS[_name] = (_off, _w)
    _off += _w
D_PROJ = _off
_SPLITS = (("gq", 256), ("gk", 256), ("gv", 512), ("ga", 16), ("gr", 512), ("nq", 512),
           ("nkc", 128), ("nvc", 128), ("nks", 128), ("nvs", 128), ("nkw", 128), ("nvw", 128), ("ng", 24))

VMEM_LIMIT = 56 * 1024 * 1024


def _cparams(*sem):
    return pltpu.CompilerParams(dimension_semantics=sem, vmem_limit_bytes=VMEM_LIMIT)


def _dot(a, b):
    return jnp.dot(a, b, preferred_element_type=F32)


def _dot_nt(a, b):
    return lax.dot_general(a, b, (((1,), (1,)), ((), ())), preferred_element_type=F32)


def _split3(x):
    hi = x.astype(BF16)
    r1 = x - hi.astype(F32)
    mid = r1.astype(BF16)
    lo = (r1 - mid.astype(F32)).astype(BF16)
    return hi, mid, lo


def _proj_kernel(x_ref, w_ref, o_ref):
    o_ref[...] = _dot(x_ref[...].astype(BF16), w_ref[...])


def _proj(x2, row0, rows, w):
    D = x2.shape[1]
    N = w.shape[1]
    tm = 512
    off = row0 // tm
    return pl.pallas_call(
        _proj_kernel,
        grid=(rows // tm,),
        in_specs=[pl.BlockSpec((tm, D), lambda i: (i + off, 0)),
                  pl.BlockSpec((D, N), lambda i: (0, 0))],
        out_specs=pl.BlockSpec((tm, N), lambda i: (i, 0)),
        out_shape=jax.ShapeDtypeStruct((rows, N), F32),
        compiler_params=_cparams("parallel"),
        name="proj",
    )(x2, w)


def _gla_kernel(q_ref, k_ref, v_ref, r_ref, a_ref, wa_ref, ba_ref, ng_ref, o_ref, st_ref, *, nchunk):
    C = GLA_CHUNK
    HK = GLA_HEADS * GLA_DK

    @pl.when(pl.program_id(1) == 0)
    def _():
        st_ref[...] = jnp.zeros_like(st_ref)

    ri = lax.broadcasted_iota(jnp.int32, (C, C), 0)
    ci = lax.broadcasted_iota(jnp.int32, (C, C), 1)
    causal = ri >= ci
    tri = jnp.where(causal, 1.0, 0.0).astype(BF16)
    lane_head = lax.broadcasted_iota(jnp.int32, (1, HK), 1) // GLA_DK
    wa = wa_ref[...]
    ba = ba_ref[...]
    ng = ng_ref[...]

    def chunk(c, carry):
        for bb in range(q_ref.shape[0]):
            chunk_one(c, bb)
        return carry

    def chunk_one(c, bb):
        rows = pl.ds(pl.multiple_of(c * C, C), C)
        q = q_ref[bb, rows, :]
        k = k_ref[bb, rows, :]
        v = v_ref[bb, rows, :]
        r = r_ref[bb, rows, :]
        a = a_ref[bb, rows, :]
        z = _dot(a.astype(BF16), wa) + ba
        g = (jnp.minimum(z, 0.0) - jnp.log1p(jnp.exp(-jnp.abs(z)))) * (1.0 / GLA_TAU)
        g_hi, g_mid, g_lo = _split3(g)
        b = _dot(tri, g_hi) + _dot(tri, g_mid) + _dot(tri, g_lo)
        b_last = b[C - 1:C, :]
        qt = q * jnp.exp(b) * (GLA_DK ** -0.5)
        kt = (k * jnp.exp(-b)).astype(BF16)
        ks = (k * jnp.exp(b_last - b)).astype(BF16)
        st = st_ref[bb]
        st_b = st.astype(BF16)
        new_st = st * jnp.exp(b_last)
        for h in range(GLA_HEADS):
            hm = lane_head == h
            qh = jnp.where(hm, qt, 0.0).astype(BF16)
            att = jnp.where(causal, _dot_nt(qh, kt), 0.0)
            vh = v[:, h * GLA_DV:(h + 1) * GLA_DV].astype(BF16)
            o = _dot_nt(qh, st_b) + _dot(att.astype(BF16), vh)
            new_st = new_st + jnp.where(hm, _dot(vh.T, ks), 0.0)
            o = o * lax.rsqrt(jnp.mean(o * o, axis=-1, keepdims=True) + 1e-6) * ng
            rh = r[:, h * GLA_DV:(h + 1) * GLA_DV]
            o = o * (rh * jax.nn.sigmoid(rh))
            o_ref[bb, rows, h * GLA_DV:(h + 1) * GLA_DV] = o.astype(o_ref.dtype)
        st_ref[bb] = new_st

    lax.fori_loop(0, nchunk, chunk, 0)


def _gla(proj3, wa, ba, ng):
    B, S, _ = proj3.shape
    L = 512 if S % 512 == 0 else S
    nchunk = L // GLA_CHUNK

    def col(name, width):
        off = _COLS[name][0]
        assert off % width == 0
        return pl.BlockSpec((nbat, L, width), lambda b, j, o=off // width: (b, j, o))

    nbat = 2 if B % 2 == 0 else 1
    full2 = lambda shape: pl.BlockSpec(shape, lambda b, j: (0, 0))
    return pl.pallas_call(
        functools.partial(_gla_kernel, nchunk=nchunk),
        grid=(B // nbat, S // L),
        in_specs=[col("gq", 256), col("gk", 256), col("gv", 512), col("gr", 512), col("ga", 128),
                  full2((128, 256)), full2((1, 256)), full2((1, 128))],
        out_specs=pl.BlockSpec((nbat, L, 512), lambda b, j: (b, j, 0)),
        out_shape=jax.ShapeDtypeStruct((B, S, 512), BF16),
        scratch_shapes=[pltpu.VMEM((nbat, GLA_DV, GLA_HEADS * GLA_DK), F32)],
        compiler_params=_cparams("parallel", "arbitrary"),
        name="gla",
    )(proj3, proj3, proj3, proj3, proj3, wa, ba, ng)


def _rot_half(x):
    n = x.shape[-1]
    lane = lax.broadcasted_iota(jnp.int32, (1, n), 1)
    first = (lane % NSA_DH) < (NSA_DH // 2)
    return jnp.where(first, -pltpu.roll(x, n - NSA_DH // 2, 1), pltpu.roll(x, NSA_DH // 2, 1))


def _nsa_prep_kernel(pos_ref, inv_ref, q_ref, kc_ref, ks_ref, vs_ref, kw_ref, vw_ref,
                     qo_ref, kco_ref, kso_ref, vso_ref, kwo_ref, vwo_ref):
    ang = pos_ref[0] * inv_ref[...]
    cos = jnp.cos(ang)
    sin = jnp.sin(ang)
    cos4 = jnp.concatenate([cos] * 4, axis=1)
    sin4 = jnp.concatenate([sin] * 4, axis=1)

    q = q_ref[0]
    qr = q * cos4 + _rot_half(q) * sin4
    for h in range(NSA_HEADS):
        qo_ref[0, h] = (qr[:, h * NSA_DH:(h + 1) * NSA_DH] * (NSA_DH ** -0.5)).astype(qo_ref.dtype)

    def rope128(x):
        return x * cos + _rot_half(x) * sin

    kco_ref[0] = rope128(kc_ref[0])
    ksr = rope128(ks_ref[0])
    kwr = rope128(kw_ref[0])
    vs = vs_ref[0]
    vw = vw_ref[0]
    ones_col = jnp.where(lax.broadcasted_iota(jnp.int32, (vs.shape[0], NSA_DH), 1) == 0, 1.0, 0.0)
    for g in range(NSA_KV_GROUPS):
        sl = slice(g * NSA_DH, (g + 1) * NSA_DH)
        kso_ref[0, g] = ksr[:, sl].astype(kso_ref.dtype)
        kwo_ref[0, g] = kwr[:, sl].astype(kwo_ref.dtype)
        vso_ref[0, g] = jnp.concatenate([vs[:, sl], ones_col], axis=1).astype(vso_ref.dtype)
        vwo_ref[0, g] = jnp.concatenate([vw[:, sl], ones_col], axis=1).astype(vwo_ref.dtype)


def _nsa_prep(proj3, pos3, inv_row):
    B, S, _ = proj3.shape
    ts = 512 if S % 512 == 0 else S

    def col(name, width):
        off = _COLS[name][0]
        assert off % width == 0
        return pl.BlockSpec((1, ts, width), lambda b, j, o=off // width: (b, j, o))

    kv_spec = pl.BlockSpec((1, NSA_KV_GROUPS, ts, NSA_DH), lambda b, j: (b, 0, j, 0))
    kv_shape = jax.ShapeDtypeStruct((B, NSA_KV_GROUPS, S, NSA_DH), BF16)
    vx_spec = pl.BlockSpec((1, NSA_KV_GROUPS, ts, 2 * NSA_DH), lambda b, j: (b, 0, j, 0))
    vx_shape = jax.ShapeDtypeStruct((B, NSA_KV_GROUPS, S, 2 * NSA_DH), BF16)
    return pl.pallas_call(
        _nsa_prep_kernel,
        grid=(B, S // ts),
        in_specs=[pl.BlockSpec((1, ts, 1), lambda b, j: (b, j, 0)),
                  pl.BlockSpec((1, 128), lambda b, j: (0, 0)),
                  col("nq", 512), col("nkc", 128), col("nks", 128), col("nvs", 128),
                  col("nkw", 128), col("nvw", 128)],
        out_specs=[pl.BlockSpec((1, NSA_HEADS, ts, NSA_DH), lambda b, j: (b, 0, j, 0)),
                   pl.BlockSpec((1, ts, 128), lambda b, j: (b, j, 0)),
                   kv_spec, vx_spec, kv_spec, vx_spec],
        out_shape=[jax.ShapeDtypeStruct((B, NSA_HEADS, S, NSA_DH), BF16),
                   jax.ShapeDtypeStruct((B, S, 128), F32),
                   kv_shape, vx_shape, kv_shape, vx_shape],
        compiler_params=_cparams("parallel", "parallel"),
        name="nsa_prep",
    )(pos3, inv_row, proj3, proj3, proj3, proj3, proj3, proj3)


def _compress_kernel(k_ref, v_ref, pk_ref, w1k_ref, b1k_ref, w2k_ref, pv_ref, w1v_ref, b1v_ref, w2v_ref,
                     ko_ref, vo_ref):
    nr = k_ref.shape[1] // CMP_STRIDE
    hidden = w2k_ref.shape[0]

    def run(x_ref, p_ref, w1_ref, b1_ref, w2_ref, o_ref):
        first = jnp.zeros((nr, NSA_KV_GROUPS * hidden), F32)
        second = jnp.zeros((nr, NSA_KV_GROUPS * hidden), F32)
        for m in range(CMP_STRIDE):
            xm = x_ref[0, pl.ds(m, nr, stride=CMP_STRIDE), :]
            first = first + _dot((xm + p_ref[m:m + 1, :]).astype(BF16), w1_ref[m])
            second = second + _dot((xm + p_ref[CMP_STRIDE + m:CMP_STRIDE + m + 1, :]).astype(BF16),
                                   w1_ref[CMP_STRIDE + m])
        pre = first + pltpu.roll(second, nr - 1, 0) + b1_ref[...]
        hid = jax.nn.gelu(pre)
        for g in range(NSA_KV_GROUPS):
            out = _dot(hid[:, g * hidden:(g + 1) * hidden].astype(BF16), w2_ref[...])
            row = lax.broadcasted_iota(jnp.int32, out.shape, 0)
            o_ref[0, g] = jnp.where(row < nr - 1, out, 0.0)

    run(k_ref, pk_ref, w1k_ref, b1k_ref, w2k_ref, ko_ref)
    run(v_ref, pv_ref, w1v_ref, b1v_ref, w2v_ref, vo_ref)


def _compress_params(pos, w1, b1, w2):
    L, dh = pos.shape
    hidden = w1.shape[1]
    w = w1.reshape(L, dh, hidden).astype(BF16)
    z = jnp.zeros_like(w)
    w_bd = jnp.concatenate([jnp.concatenate([w, z], axis=2), jnp.concatenate([z, w], axis=2)], axis=1)
    return (jnp.tile(pos, (1, NSA_KV_GROUPS)), w_bd, jnp.tile(b1.reshape(1, -1), (1, NSA_KV_GROUPS)),
            w2.astype(BF16))


def _compress(kc_r, proj3, k_params, v_params):
    B, S, W = kc_r.shape
    NR = S // CMP_STRIDE
    full = lambda a: pl.BlockSpec(a.shape, lambda b: (0,) * a.ndim)
    o_spec = pl.BlockSpec((1, NSA_KV_GROUPS, NR, NSA_DH), lambda b: (b, 0, 0, 0))
    o_shape = jax.ShapeDtypeStruct((B, NSA_KV_GROUPS, NR, NSA_DH), F32)
    params = tuple(k_params) + tuple(v_params)
    return pl.pallas_call(
        _compress_kernel,
        grid=(B,),
        in_specs=[pl.BlockSpec((1, S, W), lambda b: (b, 0, 0)),
                  pl.BlockSpec((1, S, W), lambda b, o=_COLS["nvc"][0] // W: (b, 0, o))]
                 + [full(a) for a in params],
        out_specs=[o_spec, o_spec],
        out_shape=[o_shape, o_shape],
        compiler_params=_cparams("parallel"),
        name="compress",
    )(kc_r, proj3, *params)


def _nsa_attn_kernel(q_ref, kc_ref, vc_ref, ks_ref, vs_ref, kw_ref, vw_ref, gate_ref, ovt_ref, o_ref,
                     *, tq, S, n_sel, n_top):
    g = pl.program_id(1)
    qi = pl.program_id(2)
    start = qi * tq
    H = NSA_HPG
    qs = q_ref[0].reshape(H * tq, NSA_DH)
    t_col = start + lax.broadcasted_iota(jnp.int32, (tq, 1), 0)
    t_row = start + lax.broadcasted_iota(jnp.int32, (1, tq), 1)

    kc = kc_ref[0, 0].astype(BF16)
    vc = vc_ref[0, 0].astype(BF16)
    NR = kc.shape[0]
    s = _dot_nt(qs, kc)
    ncol = lax.broadcasted_iota(jnp.int32, (1, NR), 1)
    cmask = (ncol < NR - 1) & (ncol * CMP_STRIDE + (CMP_LEN - 1) <= t_col)
    s3 = jnp.where(cmask[None], s.reshape(H, tq, NR), NEG)
    m = jnp.max(s3, axis=-1, keepdims=True)
    e = jnp.exp(s3 - m)
    p3 = e / jnp.sum(e, axis=-1, keepdims=True)
    p3 = jnp.where(cmask[None], p3, 0.0)
    o_cmp = _dot(p3.reshape(H * tq, NR).astype(BF16), vc)

    psum = jnp.sum(p3, axis=0)
    p_hi = psum.astype(BF16)
    p_lo = (psum - p_hi.astype(F32)).astype(BF16)
    ovt = ovt_ref[...]
    imp = _dot_nt(ovt, p_hi) + _dot_nt(ovt, p_lo)
    blk = lax.broadcasted_iota(jnp.int32, (n_sel, 1), 0)
    cur = t_row // SEL_LEN
    causal_blk = blk <= cur
    forced = (blk == 0) | (blk == cur) | (blk == cur - 1)
    val = jnp.where(causal_blk, jnp.where(forced, jnp.inf, imp), -jnp.inf)
    rank = jnp.zeros((n_sel, tq), jnp.int32)
    for j in range(n_sel):
        vj = val[j:j + 1, :]
        beats = (vj > val) | ((vj == val) & (blk > j))
        rank = rank + beats.astype(jnp.int32)
    sel_t = jnp.where((rank < n_top) & causal_blk, 1.0, 0.0)
    sel_t = jnp.concatenate([sel_t, jnp.zeros((128 - n_sel, tq), F32)], axis=0)
    sel = sel_t.T.astype(BF16)

    tk = 256
    nsub = 2 if S % (2 * tk) == 0 else 1
    bpc = tk // SEL_LEN
    n_trips = (start + tq + nsub * tk - 1) // (nsub * tk)
    erow = lax.broadcasted_iota(jnp.int32, (128, tk), 0)
    ecol = lax.broadcasted_iota(jnp.int32, (128, tk), 1) // SEL_LEN
    kk = lax.broadcasted_iota(jnp.int32, (1, tk), 1)

    def sel_trip(c, carry):
        m_i, acc = carry
        scores, vals = [], []
        m_new = m_i
        for u in range(nsub):
            cu = c * nsub + u
            k0 = pl.multiple_of(cu * tk, tk)
            kb = ks_ref[0, 0, pl.ds(k0, tk), :]
            vals.append(vs_ref[0, 0, pl.ds(k0, tk), :])
            expand = jnp.where(erow == ecol + cu * bpc, 1.0, 0.0).astype(BF16)
            allowed = (_dot(sel, expand) > 0.5) & (k0 + kk <= t_col)
            bias = jnp.where(allowed, 0.0, NEG)
            sc3 = _dot_nt(qs, kb).reshape(H, tq, tk) + bias[None]
            scores.append(sc3)
            m_new = jnp.maximum(m_new, jnp.max(sc3, axis=-1, keepdims=True))
        acc = jnp.exp(m_i - m_new).reshape(H * tq, 1) * acc
        for u in range(nsub):
            pe = jnp.exp((scores[u] - m_new).astype(BF16))
            acc = acc + _dot(pe.reshape(H * tq, tk), vals[u])
        return m_new, acc

    m0 = jnp.full((H, tq, 1), NEG, F32)
    a0 = jnp.zeros((H * tq, 2 * NSA_DH), F32)
    _, acc_f = lax.fori_loop(0, n_trips, sel_trip, (m0, a0))
    o_sel = acc_f[:, :NSA_DH] / acc_f[:, NSA_DH:NSA_DH + 1]

    span = min(WINDOW + tq, S)
    ws = jnp.clip(start - WINDOW, 0, S - span)
    ws = pl.multiple_of(ws, tq)
    kwb = kw_ref[0, 0, pl.ds(ws, span), :]
    vwb = vw_ref[0, 0, pl.ds(ws, span), :]
    kp = ws + lax.broadcasted_iota(jnp.int32, (1, span), 1)
    wbias = jnp.where((kp <= t_col) & (kp > t_col - WINDOW), 0.0, NEG)
    sw3 = _dot_nt(qs, kwb).reshape(H, tq, span) + wbias[None]
    mw = jnp.max(sw3, axis=-1, keepdims=True)
    ew = jnp.exp((sw3 - mw).astype(BF16))
    aw = _dot(ew.reshape(H * tq, span), vwb)
    o_win = aw[:, :NSA_DH] / aw[:, NSA_DH:NSA_DH + 1]

    gs = jax.nn.sigmoid(gate_ref[0])
    for g_static in range(NSA_KV_GROUPS):
        @pl.when(g == g_static)
        def _(g_static=g_static):
            for h in range(H):
                c0 = g_static * H * 3 + h * 3
                rs = slice(h * tq, (h + 1) * tq)
                o = (gs[:, c0:c0 + 1] * o_cmp[rs] + gs[:, c0 + 1:c0 + 2] * o_sel[rs]
                     + gs[:, c0 + 2:c0 + 3] * o_win[rs])
                o_ref[0, :, h * NSA_DH:(h + 1) * NSA_DH] = o.astype(o_ref.dtype)


def _nsa_attn(q_r, kcmp, vcmp, ks_r, vs_r, kw_r, vw_r, proj3, ov):
    B, _, S, _ = q_r.shape
    G, H = NSA_KV_GROUPS, NSA_HPG
    NR = kcmp.shape[2]
    tq = 256
    n_sel = S // SEL_LEN
    n_top = min(SEL_TOPK, n_sel)
    cmp_spec = pl.BlockSpec((1, 1, NR, NSA_DH), lambda b, g, i: (b, g, 0, 0))
    kv_spec = pl.BlockSpec((1, 1, S, NSA_DH), lambda b, g, i: (b, g, 0, 0))
    vx_spec = pl.BlockSpec((1, 1, S, 2 * NSA_DH), lambda b, g, i: (b, g, 0, 0))
    goff = _COLS["ng"][0] // 128
    return pl.pallas_call(
        functools.partial(_nsa_attn_kernel, tq=tq, S=S, n_sel=n_sel, n_top=n_top),
        grid=(B, G, S // tq),
        in_specs=[pl.BlockSpec((1, H, tq, NSA_DH), lambda b, g, i: (b, g, i, 0)),
                  cmp_spec, cmp_spec, kv_spec, vx_spec, kv_spec, vx_spec,
                  pl.BlockSpec((1, tq, 128), lambda b, g, i: (b, i, goff)),
                  pl.BlockSpec(ov.shape, lambda b, g, i: (0, 0))],
        out_specs=pl.BlockSpec((1, tq, H * NSA_DH), lambda b, g, i: (b, i, g)),
        out_shape=jax.ShapeDtypeStruct((B, S, NSA_HEADS * NSA_DH), BF16),
        compiler_params=_cparams("parallel", "parallel", "arbitrary"),
        name="nsa_attn",
    )(q_r, kcmp, vcmp, ks_r, vs_r, kw_r, vw_r, proj3, ov)


def _layer_norm(x, g, b):
    mu = jnp.mean(x, axis=-1, keepdims=True)
    xc = x - mu
    var = jnp.mean(xc * xc, axis=-1, keepdims=True)
    return xc * lax.rsqrt(var + LN_EPS) * g + b


SUB = D_MODEL // 2 // 128


def _pack_bf16_pairs(v):
    half = v.shape[1] // 2
    bits = pltpu.bitcast(v.astype(BF16).astype(F32), jnp.uint32)
    return pltpu.bitcast((bits[:, :half] >> 16) | bits[:, half:], jnp.int32)


def _unpack_bf16_pairs(p):
    u = pltpu.bitcast(p, jnp.uint32)
    return pltpu.bitcast(u << 16, F32), pltpu.bitcast(u & jnp.uint32(0xFFFF0000), F32)


def _to_tiles(ref, val, lead=()):
    for c in range(SUB):
        ref[lead + (c,)] = val[:, c * 128:(c + 1) * 128]


def _from_tiles(ref, lead=()):
    return jnp.concatenate([ref[lead + (c,)] for c in range(SUB)], axis=1)


def _outproj_kernel(x_ref, og_ref, on_ref, w_ref, g_ref, b_ref, o_ref, ot_ref):
    half = og_ref.shape[1]
    mix = _dot(og_ref[...], w_ref[0:half, :]) + _dot(on_ref[...], w_ref[half:, :])
    h = _layer_norm(DN_ALPHA * x_ref[...] + mix, g_ref[...], b_ref[...])
    o_ref[...] = h
    _to_tiles(ot_ref, _pack_bf16_pairs(h))


def _outproj(x2, row0, og2, on2, w, g, b):
    T = og2.shape[0]
    D = x2.shape[1]
    tm = 512
    off = row0 // tm
    row = lambda width: pl.BlockSpec((tm, width), lambda i: (i, 0))
    full = lambda a: pl.BlockSpec(a.shape, lambda i: (0, 0))
    return pl.pallas_call(
        _outproj_kernel,
        grid=(T // tm,),
        in_specs=[pl.BlockSpec((tm, D), lambda i: (i + off, 0)), row(og2.shape[1]), row(on2.shape[1]),
                  full(w), full(g), full(b)],
        out_specs=[row(D), pl.BlockSpec((SUB, tm, 128), lambda i: (0, i, 0))],
        out_shape=[jax.ShapeDtypeStruct((T, D), F32),
                   jax.ShapeDtypeStruct((SUB, T, 128), jnp.int32)],
        compiler_params=_cparams("parallel"),
        name="outproj_ln",
    )(x2, og2, on2, w, g, b)


def _router_kernel(h_ref, wh_ref, wl_ref, bias_ref, eidx_ref, wts_ref, rank_ref, cnt_ref, carry_ref):
    @pl.when(pl.program_id(0) == 0)
    def _():
        carry_ref[...] = jnp.zeros_like(carry_ref)

    h = h_ref[...]
    tm = h.shape[0]
    E = N_EXPERTS
    h_hi = h.astype(BF16)
    h_lo = (h - h_hi.astype(F32)).astype(BF16)
    wh = wh_ref[...]
    logits = _dot(h_hi, wh) + _dot(h_lo, wh) + _dot(h_hi, wl_ref[...])
    scores = jax.nn.sigmoid(logits)
    biased = scores + bias_ref[...]
    lane_i = lax.broadcasted_iota(jnp.int32, (tm, E), 1)
    gid = lane_i // (E // N_GROUPS)
    lane = lane_i.astype(F32)
    ninf = -jnp.inf

    def row_max(x):
        return jnp.max(x, axis=-1, keepdims=True)

    def first_idx(x, mx):
        return jnp.min(jnp.where(x == mx, lane, float(E)), axis=-1, keepdims=True)

    gscore = []
    for gi in range(N_GROUPS):
        mg = jnp.where(gid == gi, biased, ninf)
        m1 = row_max(mg)
        i1 = first_idx(mg, m1)
        m2 = row_max(jnp.where(lane == i1, ninf, mg))
        gscore.append(m1 + m2)
    emask = jnp.zeros((tm, E), jnp.bool_)
    for gi in range(N_GROUPS):
        rk = jnp.zeros((tm, 1), jnp.int32)
        for gj in range(N_GROUPS):
            if gj == gi:
                continue
            beats = (gscore[gj] > gscore[gi]) | ((gscore[gj] == gscore[gi]) & (gj < gi))
            rk = rk + beats.astype(jnp.int32)
        emask = emask | ((gid == gi) & (rk < TOPK_GROUPS))
    masked = jnp.where(emask, biased, ninf)

    onehots, wsel = [], []
    selm = jnp.zeros((tm, E), F32)
    for k in range(TOP_K):
        mx = row_max(masked)
        idx = first_idx(masked, mx)
        oh = lane == idx
        onehots.append(oh)
        wsel.append(jnp.sum(jnp.where(oh, scores, 0.0), axis=-1, keepdims=True))
        masked = jnp.where(oh, ninf, masked)
        selm = jnp.where(oh, 1.0, selm)
        eidx_ref[:, k:k + 1] = idx.astype(jnp.int32)
    wsum = wsel[0]
    for k in range(1, TOP_K):
        wsum = wsum + wsel[k]
    for k in range(TOP_K):
        wts_ref[:, k:k + 1] = wsel[k] / wsum * ROUTED_SCALE

    ri = lax.broadcasted_iota(jnp.int32, (tm, tm), 0)
    ci = lax.broadcasted_iota(jnp.int32, (tm, tm), 1)
    ltri = jnp.where(ri > ci, 1.0, 0.0).astype(BF16)
    cum = _dot(ltri, selm.astype(BF16)) + carry_ref[...]
    for k in range(TOP_K):
        rk = jnp.sum(jnp.where(onehots[k], cum, 0.0), axis=-1, keepdims=True)
        rank_ref[:, k:k + 1] = rk.astype(jnp.int32)
    total = carry_ref[...] + jnp.sum(selm, axis=0, keepdims=True)
    carry_ref[...] = total
    cnt_ref[...] = total


def _router(h2, w_hi, w_lo, bias):
    T, D = h2.shape
    tm = 256
    full = lambda a: pl.BlockSpec(a.shape, lambda i: (0, 0))
    o8 = pl.BlockSpec((tm, TOP_K), lambda i: (i, 0))
    return pl.pallas_call(
        _router_kernel,
        grid=(T // tm,),
        in_specs=[pl.BlockSpec((tm, D), lambda i: (i, 0)), full(w_hi), full(w_lo), full(bias)],
        out_specs=[o8, o8, o8, pl.BlockSpec((1, N_EXPERTS), lambda i: (0, 0))],
        out_shape=[jax.ShapeDtypeStruct((T, TOP_K), jnp.int32),
                   jax.ShapeDtypeStruct((T, TOP_K), F32),
                   jax.ShapeDtypeStruct((T, TOP_K), jnp.int32),
                   jax.ShapeDtypeStruct((1, N_EXPERTS), F32)],
        scratch_shapes=[pltpu.VMEM((1, N_EXPERTS), F32)],
        compiler_params=_cparams("arbitrary"),
        name="router",
    )(h2, w_hi, w_lo, bias)


def _dest_kernel(eidx_ref, rank_ref, ps_ref, dest_ref):
    tm = eidx_ref.shape[0]
    lane = lax.broadcasted_iota(jnp.int32, (tm, N_EXPERTS), 1)
    ps = ps_ref[...]
    for k in range(TOP_K):
        start = jnp.sum(jnp.where(lane == eidx_ref[:, k:k + 1], ps, 0.0), axis=-1, keepdims=True)
        dest_ref[:, k:k + 1] = start.astype(jnp.int32) + rank_ref[:, k:k + 1]


def _dest(eidx, rank, pad_start_f):
    T = eidx.shape[0]
    tm = 1024 if T % 1024 == 0 else T
    o8 = pl.BlockSpec((tm, TOP_K), lambda i: (i, 0))
    return pl.pallas_call(
        _dest_kernel,
        grid=(T // tm,),
        in_specs=[o8, o8, pl.BlockSpec((1, N_EXPERTS), lambda i: (0, 0))],
        out_specs=o8,
        out_shape=jax.ShapeDtypeStruct((T, TOP_K), jnp.int32),
        compiler_params=_cparams("parallel"),
        name="dest",
    )(eidx, rank, pad_start_f)


SC_WINDOW = 128


def _sc_gather(table, idx):
    _, lanes = table.shape
    n = idx.shape[0]
    mesh = plsc.VectorSubcoreMesh(core_axis_name="core", subcore_axis_name="subcore")

    @functools.partial(pl.kernel, out_type=jax.ShapeDtypeStruct((n, lanes), table.dtype), mesh=mesh,
                       name="sc_row_gather")
    def gather(x_hbm, i_hbm, o_hbm):
        def body(i_vmem, o_vmem):
            pltpu.sync_copy(x_hbm.at[i_vmem.at[0]], o_vmem)

        pltpu.emit_pipeline(
            body,
            grid=(n // SC_WINDOW,),
            in_specs=[pl.BlockSpec((1, SC_WINDOW), lambda i: (0, i))],
            out_specs=[pl.BlockSpec((SC_WINDOW, lanes), lambda i: (i, 0))],
            core_axis_name=("core", "subcore"),
            dimension_semantics=(pltpu.PARALLEL,),
            trace_scopes=False,
        )(i_hbm, o_hbm)

    return gather(table, idx.reshape(1, n))


def _sc_scatter(src, idx, n_out):
    rows, lanes = src.shape
    n_idx = idx.shape[0]
    mesh = plsc.VectorSubcoreMesh(core_axis_name="core", subcore_axis_name="subcore")

    @functools.partial(pl.kernel, out_type=jax.ShapeDtypeStruct((n_out, lanes), src.dtype), mesh=mesh,
                       name="sc_row_scatter")
    def scatter(x_hbm, i_hbm, o_hbm):
        def body(x_vmem, *i_vmems):
            for i_vmem in i_vmems:
                pltpu.sync_copy(x_vmem, o_hbm.at[i_vmem.at[0]])

        pltpu.emit_pipeline(
            body,
            grid=(rows // SC_WINDOW,),
            in_specs=[pl.BlockSpec((SC_WINDOW, lanes), lambda i: (i, 0))]
                     + [pl.BlockSpec((1, SC_WINDOW), lambda i, j=j: (j, i)) for j in range(n_idx)],
            out_specs=[],
            core_axis_name=("core", "subcore"),
            dimension_semantics=(pltpu.PARALLEL,),
            trace_scopes=False,
        )(x_hbm, *([i_hbm] * n_idx))

    return scatter(src, idx)


def _expert_kernel(bexp_ref, nused_ref, x_ref, wg_ref, wu_ref, wd_ref, y_ref):
    del bexp_ref

    @pl.when(pl.program_id(0) < nused_ref[0])
    def _():
        x = jnp.concatenate(_unpack_bf16_pairs(_from_tiles(x_ref)), axis=1).astype(BF16)
        gate = _dot(x, wg_ref[0])
        up = _dot(x, wu_ref[0])
        act = (gate * jax.nn.sigmoid(gate) * up).astype(BF16)
        _to_tiles(y_ref, _pack_bf16_pairs(_dot(act, wd_ref[0])))


def _experts(blk_exp, n_used, xs_t, wg, wu, wd):
    NP = xs_t.shape[1]
    D = D_MODEL
    nb = NP // MOE_BLOCK
    blk = (SUB, MOE_BLOCK, 128)

    def xmap(i, bexp, nused):
        return (0, jnp.minimum(i, nused[0] - 1), 0)

    def wmap(i, bexp, nused):
        return (bexp[jnp.minimum(i, nused[0] - 1)], 0, 0)

    grid_spec = pltpu.PrefetchScalarGridSpec(
        num_scalar_prefetch=2,
        grid=(nb,),
        in_specs=[pl.BlockSpec(blk, xmap),
                  pl.BlockSpec((1, D, D_EXPERT), wmap),
                  pl.BlockSpec((1, D, D_EXPERT), wmap),
                  pl.BlockSpec((1, D_EXPERT, D), wmap)],
        out_specs=pl.BlockSpec(blk, xmap),
    )
    return pl.pallas_call(
        _expert_kernel,
        grid_spec=grid_spec,
        out_shape=jax.ShapeDtypeStruct(xs_t.shape, jnp.int32),
        compiler_params=_cparams("arbitrary"),
        name="experts",
    )(blk_exp, n_used, xs_t, wg, wu, wd)


def _combine_kernel(h_ref, wts_ref, yg_ref, wsg_ref, wsu_ref, wsd_ref, g_ref, b_ref, *rest):
    o_ref = rest[-1]
    h = h_ref[...]
    hb = h.astype(BF16)
    gate = _dot(hb, wsg_ref[...])
    up = _dot(hb, wsu_ref[...])
    shared = _dot((gate * jax.nn.sigmoid(gate) * up).astype(BF16), wsd_ref[...])
    wts = wts_ref[...]
    lo, hi = _unpack_bf16_pairs(_from_tiles(yg_ref, (0,)))
    r_lo, r_hi = wts[:, 0:1] * lo, wts[:, 0:1] * hi
    for k in range(1, TOP_K):
        lo, hi = _unpack_bf16_pairs(_from_tiles(yg_ref, (k,)))
        r_lo, r_hi = r_lo + wts[:, k:k + 1] * lo, r_hi + wts[:, k:k + 1] * hi
    routed = jnp.concatenate([r_lo, r_hi], axis=1)
    o_ref[...] = _layer_norm(DN_ALPHA * h + (routed + shared), g_ref[...], b_ref[...])


def _combine(h2, wts, yg_t, wsg, wsu, wsd, g, b, out_rows, row0, out_prev):
    T, D = h2.shape
    tm = 128
    off = row0 // tm
    full = lambda a: pl.BlockSpec(a.shape, lambda i: (0, 0))
    in_specs = [pl.BlockSpec((tm, D), lambda i: (i, 0)),
                pl.BlockSpec((tm, TOP_K), lambda i: (i, 0)),
                pl.BlockSpec((TOP_K, SUB, tm, 128), lambda i: (0, 0, i, 0)),
                full(wsg), full(wsu), full(wsd), full(g), full(b)]
    args = [h2, wts, yg_t, wsg, wsu, wsd, g, b]
    aliases = {}
    if out_prev is not None:
        in_specs.append(pl.BlockSpec(memory_space=pl.ANY))
        args.append(out_prev)
        aliases = {len(args) - 1: 0}
    return pl.pallas_call(
        _combine_kernel,
        grid=(T // tm,),
        in_specs=in_specs,
        out_specs=pl.BlockSpec((tm, D), lambda i: (i + off, 0)),
        out_shape=jax.ShapeDtypeStruct((out_rows, D), F32),
        input_output_aliases=aliases,
        compiler_params=_cparams("parallel"),
        name="combine_ln",
    )(*args)


def _regroup_w_in(w_in):
    parts, off = {}, 0
    for name, width in _SPLITS:
        parts[name] = w_in[:, off:off + width]
        off += width
    cols = []
    for name, (_, width) in _COLS.items():
        p = parts[name]
        if p.shape[1] < width:
            p = jnp.pad(p, ((0, 0), (0, width - p.shape[1])))
        cols.append(p)
    return jnp.concatenate(cols, axis=1).astype(BF16)


def _overlap_matrix(S):
    nr = S // CMP_STRIDE
    n_sel = S // SEL_LEN
    ci = np.arange(nr)[:, None] * CMP_STRIDE
    sj = np.arange(n_sel)[None, :] * SEL_LEN
    ov = np.clip(np.minimum(ci + CMP_LEN, sj + SEL_LEN) - np.maximum(ci, sj), 0, None) / CMP_LEN
    ov[nr - 1] = 0.0
    return jnp.asarray(ov.T, BF16)


def _mixers(x2, row0, positions, w_in, w_alpha2, b_alpha, gla_norm_g,
            cmp_pos_k, cmp_w1_k, cmp_b1_k, cmp_w2_k, cmp_pos_v, cmp_w1_v, cmp_b1_v, cmp_w2_v):
    B, S = positions.shape
    proj = _proj(x2, row0, B * S, _regroup_w_in(w_in)).reshape(B, S, D_PROJ)

    wa = jnp.pad(w_alpha2, ((0, 128 - GLA_LOWRANK), (0, 0))).astype(BF16)
    o_gla = _gla(proj, wa, b_alpha.reshape(1, -1), gla_norm_g.reshape(1, -1))

    half = NSA_DH // 2
    inv = ROPE_THETA ** (-np.arange(half, dtype=np.float32) / half)
    inv_row = jnp.asarray(np.tile(inv, 128 // half).reshape(1, 128), F32)
    pos3 = positions.astype(F32).reshape(B, S, 1)
    q_r, kc_r, ks_r, vs_r, kw_r, vw_r = _nsa_prep(proj, pos3, inv_row)

    kcmp, vcmp = _compress(kc_r, proj,
                           _compress_params(cmp_pos_k, cmp_w1_k, cmp_b1_k, cmp_w2_k),
                           _compress_params(cmp_pos_v, cmp_w1_v, cmp_b1_v, cmp_w2_v))
    o_nsa = _nsa_attn(q_r, kcmp, vcmp, ks_r, vs_r, kw_r, vw_r, proj, _overlap_matrix(S))
    return o_gla, o_nsa


def _moe_ln(h2, h_t, w_router, router_bias, w_gate, w_up, w_down, ws_gate, ws_up, ws_down, ln_g, ln_b,
            out_rows, row0, out_prev):
    T, D = h2.shape
    P = T * TOP_K
    w_hi = w_router.astype(BF16)
    w_lo = (w_router - w_hi.astype(F32)).astype(BF16)
    eidx, wts, rank, counts = _router(h2, w_hi, w_lo, router_bias.reshape(1, -1))

    counts = counts.reshape(-1).astype(jnp.int32)
    padded = (counts + MOE_BLOCK - 1) // MOE_BLOCK * MOE_BLOCK
    pad_end = jnp.cumsum(padded)
    pad_start = pad_end - padded
    nb = -(-P // MOE_BLOCK) + N_EXPERTS
    n_used = (pad_end[-1] // MOE_BLOCK).astype(jnp.int32).reshape(1)
    blk_start = jnp.arange(nb, dtype=jnp.int32) * MOE_BLOCK
    blk_exp = jnp.minimum(jnp.sum((pad_end[None, :] <= blk_start[:, None]).astype(jnp.int32), axis=1),
                          N_EXPERTS - 1)

    NP = nb * MOE_BLOCK
    dest = _dest(eidx, rank, pad_start.astype(F32).reshape(1, -1))
    col = jnp.arange(SUB, dtype=jnp.int32) * NP

    n_pad = NP - P
    assert n_pad % T == 0
    pad_cnt = padded - counts
    pad_hi = jnp.cumsum(pad_cnt)
    pad_lo = pad_hi - pad_cnt
    j = jnp.arange(n_pad, dtype=jnp.int32)[:, None]
    owner = (pad_lo[None, :] <= j) & (j < pad_hi[None, :])
    in_expert = jnp.sum(jnp.where(owner, (pad_start + counts - pad_lo)[None, :] + j, 0), axis=1)
    pad_rows = jnp.where(j[:, 0] < pad_hi[-1], in_expert, pad_end[-1] + j[:, 0] - pad_hi[-1])

    rows_all = jnp.concatenate([dest.T, pad_rows.reshape(n_pad // T, T)], axis=0)
    dst = (rows_all[:, None, :] + col[None, :, None]).reshape(-1, SUB * T)
    xs_t = _sc_scatter(h_t.reshape(SUB * T, 128), dst, SUB * NP).reshape(SUB, NP, 128)
    ys_t = _experts(blk_exp, n_used, xs_t, w_gate.astype(BF16), w_up.astype(BF16), w_down.astype(BF16))
    src = (dest.T[:, None, :] + col[None, :, None]).reshape(-1)
    yg_t = _sc_gather(ys_t.reshape(SUB * NP, 128), src).reshape(TOP_K, SUB, T, 128)
    return _combine(h2, wts, yg_t, ws_gate.astype(BF16), ws_up.astype(BF16), ws_down.astype(BF16),
                    ln_g.reshape(1, -1), ln_b.reshape(1, -1), out_rows, row0, out_prev)


def kernel(x, positions, w_in, w_alpha2, b_alpha, gla_norm_g, cmp_pos_k, cmp_w1_k, cmp_b1_k, cmp_w2_k, cmp_pos_v, cmp_w1_v, cmp_b1_v, cmp_w2_v, w_out, ln1_g, ln1_b, w_router, router_bias, w_exp_gate, w_exp_up, w_exp_down, w_sh_gate, w_sh_up, w_sh_down, ln2_g, ln2_b):
    B, S, D = x.shape
    n_groups = BATCH_GROUPS if B % BATCH_GROUPS == 0 else 1
    bg = B // n_groups
    rows = bg * S
    h2d = x.reshape(B * S, D)
    for l in range(w_in.shape[0]):
        out = None
        for gi in range(n_groups):
            row0 = gi * rows
            o_gla, o_nsa = _mixers(h2d, row0, positions[gi * bg:(gi + 1) * bg], w_in[l], w_alpha2[l], b_alpha[l],
                                   gla_norm_g[l], cmp_pos_k[l], cmp_w1_k[l], cmp_b1_k[l], cmp_w2_k[l],
                                   cmp_pos_v[l], cmp_w1_v[l], cmp_b1_v[l], cmp_w2_v[l])
            h1, h1_t = _outproj(h2d, row0, o_gla.reshape(rows, -1), o_nsa.reshape(rows, -1),
                                w_out[l].astype(BF16), ln1_g[l].reshape(1, -1), ln1_b[l].reshape(1, -1))
            out = _moe_ln(h1, h1_t, w_router[l], router_bias[l], w_exp_gate[l], w_exp_up[l], w_exp_down[l],
                          w_sh_gate[l], w_sh_up[l], w_sh_down[l], ln2_g[l], ln2_b[l], B * S, row0, out)
        h2d = out
    return h2d.reshape(B, S, D)
```

```python
import functools

import numpy as np
import jax
import jax.numpy as jnp
from jax import lax
from jax.experimental import pallas as pl
from jax.experimental.pallas import tpu as pltpu
from jax.experimental.pallas import tpu_sc as plsc

D_MODEL = 1024
GLA_HEADS = 4
GLA_DV = 128
GLA_DK = 64
GLA_LOWRANK = 16
GLA_TAU = 16.0
GLA_CHUNK = 64
NSA_HEADS = 8
NSA_KV_GROUPS = 2
NSA_HPG = 4
NSA_DH = 64
CMP_LEN = 32
CMP_STRIDE = 16
CMP_HIDDEN = 256
SEL_LEN = 64
SEL_TOPK = 16
WINDOW = 512
ROPE_THETA = 10000.0
N_EXPERTS = 256
TOP_K = 8
N_GROUPS = 8
TOPK_GROUPS = 4
D_EXPERT = 256
ROUTED_SCALE = 2.5
DEPTH = 1
DN_ALPHA = (2.0 * DEPTH) ** 0.25
LN_EPS = 1e-5

MOE_BLOCK = 512
BATCH_GROUPS = 2
NEG = -1e30
F32 = jnp.float32
BF16 = jnp.bfloat16

_COLS = {}
_off = 0
for _name, _w in (("gq", 256), ("gk", 256), ("gv", 512), ("gr", 512), ("nq", 512),
                  ("nkc", 128), ("nvc", 128), ("nks", 128), ("nvs", 128), ("nkw", 128), ("nvw", 128),
                  ("ga", 128), ("ng", 128)):
    _COLS[_name] = (_off, _w)
    _off += _w
D_PROJ = _off
_SPLITS = (("gq", 256), ("gk", 256), ("gv", 512), ("ga", 16), ("gr", 512), ("nq", 512),
           ("nkc", 128), ("nvc", 128), ("nks", 128), ("nvs", 128), ("nkw", 128), ("nvw", 128), ("ng", 24))

VMEM_LIMIT = 56 * 1024 * 1024


def _cparams(*sem):
    return pltpu.CompilerParams(dimension_semantics=sem, vmem_limit_bytes=VMEM_LIMIT)


def _dot(a, b):
    return jnp.dot(a, b, preferred_element_type=F32)


def _dot_nt(a, b):
    return lax.dot_general(a, b, (((1,), (1,)), ((), ())), preferred_element_type=F32)


def _split3(x):
    hi = x.astype(BF16)
    r1 = x - hi.astype(F32)
    mid = r1.astype(BF16)
    lo = (r1 - mid.astype(F32)).astype(BF16)
    return hi, mid, lo


def _proj_kernel(x_ref, w_ref, o_ref):
    o_ref[...] = _dot(x_ref[...].astype(BF16), w_ref[...])


def _proj(x2, row0, rows, w):
    D = x2.shape[1]
    N = w.shape[1]
    tm = 512
    off = row0 // tm
    return pl.pallas_call(
        _proj_kernel,
        grid=(rows // tm,),
        in_specs=[pl.BlockSpec((tm, D), lambda i: (i + off, 0)),
                  pl.BlockSpec((D, N), lambda i: (0, 0))],
        out_specs=pl.BlockSpec((tm, N), lambda i: (i, 0)),
        out_shape=jax.ShapeDtypeStruct((rows, N), F32),
        compiler_params=_cparams("parallel"),
        name="proj",
    )(x2, w)


def _gla_kernel(q_ref, k_ref, v_ref, r_ref, a_ref, wa_ref, ba_ref, ng_ref, o_ref, st_ref, *, nchunk):
    C = GLA_CHUNK
    HK = GLA_HEADS * GLA_DK

    @pl.when(pl.program_id(1) == 0)
    def _():
        st_ref[...] = jnp.zeros_like(st_ref)

    ri = lax.broadcasted_iota(jnp.int32, (C, C), 0)
    ci = lax.broadcasted_iota(jnp.int32, (C, C), 1)
    causal = ri >= ci
    causal4 = jnp.concatenate([causal] * GLA_HEADS, axis=0)
    tri = jnp.where(causal, 1.0, 0.0).astype(BF16)
    lane_head = lax.broadcasted_iota(jnp.int32, (1, HK), 1) // GLA_DK
    wa = wa_ref[...]
    ba = ba_ref[...]
    ng = ng_ref[...]

    def chunk(c, states):
        rows = pl.ds(pl.multiple_of(c * C, C), C)
        results = [chunk_one(rows, bb, states[bb]) for bb in range(len(states))]
        for bb, (outs, _) in enumerate(results):
            for h, o in enumerate(outs):
                o_ref[bb, rows, h * GLA_DV:(h + 1) * GLA_DV] = o
        return tuple(new_st for _, new_st in results)

    def chunk_one(rows, bb, st):
        q = q_ref[bb, rows, :]
        k = k_ref[bb, rows, :]
        v = v_ref[bb, rows, :]
        r = r_ref[bb, rows, :]
        a = a_ref[bb, rows, :]
        z = _dot(a.astype(BF16), wa) + ba
        g = (jnp.minimum(z, 0.0) - jnp.log1p(jnp.exp(-jnp.abs(z)))) * (1.0 / GLA_TAU)
        b3 = _dot(tri, jnp.concatenate(_split3(g), axis=1))
        b = b3[:, :HK] + b3[:, HK:2 * HK] + b3[:, 2 * HK:]
        b_last = b[C - 1:C, :]
        qt = q * jnp.exp(b) * (GLA_DK ** -0.5)
        kt = (k * jnp.exp(-b)).astype(BF16)
        ks = (k * jnp.exp(b_last - b)).astype(BF16)
        vb = v.astype(BF16)
        q_all = jnp.concatenate([jnp.where(lane_head == h, qt, 0.0) for h in range(GLA_HEADS)],
                                axis=0).astype(BF16)
        att_all = jnp.where(causal4, _dot_nt(q_all, kt), 0.0)
        inter_all = _dot_nt(q_all, st.astype(BF16))
        intra_all = _dot(att_all.astype(BF16), vb)
        upd_all = _dot(vb.T, ks)
        new_st = st * jnp.exp(b_last)
        outs = []
        for h in range(GLA_HEADS):
            new_st = new_st + jnp.where(lane_head == h, upd_all[h * GLA_DV:(h + 1) * GLA_DV, :], 0.0)
            o = inter_all[h * C:(h + 1) * C, :] + intra_all[h * C:(h + 1) * C, h * GLA_DV:(h + 1) * GLA_DV]
            o = o * lax.rsqrt(jnp.mean(o * o, axis=-1, keepdims=True) + 1e-6) * ng
            rh = r[:, h * GLA_DV:(h + 1) * GLA_DV]
            outs.append((o * (rh * jax.nn.sigmoid(rh))).astype(o_ref.dtype))
        return outs, new_st

    nbat = q_ref.shape[0]
    states = lax.fori_loop(0, nchunk, chunk, tuple(st_ref[bb] for bb in range(nbat)))
    for bb in range(nbat):
        st_ref[bb] = states[bb]


def _gla(proj3, wa, ba, ng):
    B, S, _ = proj3.shape
    L = 512 if S % 512 == 0 else S
    nchunk = L // GLA_CHUNK

    def col(name, width):
        off = _COLS[name][0]
        assert off % width == 0
        return pl.BlockSpec((nbat, L, width), lambda b, j, o=off // width: (b, j, o))

    nbat = 2 if B % 2 == 0 else 1
    full2 = lambda shape: pl.BlockSpec(shape, lambda b, j: (0, 0))
    return pl.pallas_call(
        functools.partial(_gla_kernel, nchunk=nchunk),
        grid=(B // nbat, S // L),
        in_specs=[col("gq", 256), col("gk", 256), col("gv", 512), col("gr", 512), col("ga", 128),
                  full2((128, 256)), full2((1, 256)), full2((1, 128))],
        out_specs=pl.BlockSpec((nbat, L, 512), lambda b, j: (b, j, 0)),
        out_shape=jax.ShapeDtypeStruct((B, S, 512), BF16),
        scratch_shapes=[pltpu.VMEM((nbat, GLA_DV, GLA_HEADS * GLA_DK), F32)],
        compiler_params=_cparams("parallel", "arbitrary"),
        name="gla",
    )(proj3, proj3, proj3, proj3, proj3, wa, ba, ng)


def _rot_half(x):
    n = x.shape[-1]
    lane = lax.broadcasted_iota(jnp.int32, (1, n), 1)
    first = (lane % NSA_DH) < (NSA_DH // 2)
    return jnp.where(first, -pltpu.roll(x, n - NSA_DH // 2, 1), pltpu.roll(x, NSA_DH // 2, 1))


def _nsa_prep_kernel(pos_ref, inv_ref, q_ref, kc_ref, ks_ref, vs_ref, kw_ref, vw_ref,
                     qo_ref, kco_ref, kso_ref, vso_ref, kwo_ref, vwo_ref):
    ang = pos_ref[0] * inv_ref[...]
    cos = jnp.cos(ang)
    sin = jnp.sin(ang)
    cos4 = jnp.concatenate([cos] * 4, axis=1)
    sin4 = jnp.concatenate([sin] * 4, axis=1)

    q = q_ref[0]
    qr = q * cos4 + _rot_half(q) * sin4
    for h in range(NSA_HEADS):
        qo_ref[0, h] = (qr[:, h * NSA_DH:(h + 1) * NSA_DH] * (NSA_DH ** -0.5)).astype(qo_ref.dtype)

    def rope128(x):
        return x * cos + _rot_half(x) * sin

    kco_ref[0] = rope128(kc_ref[0])
    ksr = rope128(ks_ref[0])
    kwr = rope128(kw_ref[0])
    vs = vs_ref[0]
    vw = vw_ref[0]
    ones_col = jnp.where(lax.broadcasted_iota(jnp.int32, (vs.shape[0], NSA_DH), 1) == 0, 1.0, 0.0)
    for g in range(NSA_KV_GROUPS):
        sl = slice(g * NSA_DH, (g + 1) * NSA_DH)
        kso_ref[0, g] = ksr[:, sl].astype(kso_ref.dtype)
        kwo_ref[0, g] = kwr[:, sl].astype(kwo_ref.dtype)
        vso_ref[0, g] = jnp.concatenate([vs[:, sl], ones_col], axis=1).astype(vso_ref.dtype)
        vwo_ref[0, g] = jnp.concatenate([vw[:, sl], ones_col], axis=1).astype(vwo_ref.dtype)


def _nsa_prep(proj3, pos3, inv_row):
    B, S, _ = proj3.shape
    ts = 512 if S % 512 == 0 else S

    def col(name, width):
        off = _COLS[name][0]
        assert off % width == 0
        return pl.BlockSpec((1, ts, width), lambda b, j, o=off // width: (b, j, o))

    kv_spec = pl.BlockSpec((1, NSA_KV_GROUPS, ts, NSA_DH), lambda b, j: (b, 0, j, 0))
    kv_shape = jax.ShapeDtypeStruct((B, NSA_KV_GROUPS, S, NSA_DH), BF16)
    vx_spec = pl.BlockSpec((1, NSA_KV_GROUPS, ts, 2 * NSA_DH), lambda b, j: (b, 0, j, 0))
    vx_shape = jax.ShapeDtypeStruct((B, NSA_KV_GROUPS, S, 2 * NSA_DH), BF16)
    return pl.pallas_call(
        _nsa_prep_kernel,
        grid=(B, S // ts),
        in_specs=[pl.BlockSpec((1, ts, 1), lambda b, j: (b, j, 0)),
                  pl.BlockSpec((1, 128), lambda b, j: (0, 0)),
                  col("nq", 512), col("nkc", 128), col("nks", 128), col("nvs", 128),
                  col("nkw", 128), col("nvw", 128)],
        out_specs=[pl.BlockSpec((1, NSA_HEADS, ts, NSA_DH), lambda b, j: (b, 0, j, 0)),
                   pl.BlockSpec((1, ts, 128), lambda b, j: (b, j, 0)),
                   kv_spec, vx_spec, kv_spec, vx_spec],
        out_shape=[jax.ShapeDtypeStruct((B, NSA_HEADS, S, NSA_DH), BF16),
                   jax.ShapeDtypeStruct((B, S, 128), F32),
                   kv_shape, vx_shape, kv_shape, vx_shape],
        compiler_params=_cparams("parallel", "parallel"),
        name="nsa_prep",
    )(pos3, inv_row, proj3, proj3, proj3, proj3, proj3, proj3)


def _compress_kernel(k_ref, v_ref, pk_ref, w1k_ref, b1k_ref, w2k_ref, pv_ref, w1v_ref, b1v_ref, w2v_ref,
                     ko_ref, vo_ref):
    nr = k_ref.shape[1] // CMP_STRIDE
    hidden = w2k_ref.shape[0]

    def run(x_ref, p_ref, w1_ref, b1_ref, w2_ref, o_ref):
        first = jnp.zeros((nr, NSA_KV_GROUPS * hidden), F32)
        second = jnp.zeros((nr, NSA_KV_GROUPS * hidden), F32)
        for m in range(CMP_STRIDE):
            xm = x_ref[0, pl.ds(m, nr, stride=CMP_STRIDE), :]
            first = first + _dot((xm + p_ref[m:m + 1, :]).astype(BF16), w1_ref[m])
            second = second + _dot((xm + p_ref[CMP_STRIDE + m:CMP_STRIDE + m + 1, :]).astype(BF16),
                                   w1_ref[CMP_STRIDE + m])
        pre = first + pltpu.roll(second, nr - 1, 0) + b1_ref[...]
        hid = jax.nn.gelu(pre)
        for g in range(NSA_KV_GROUPS):
            out = _dot(hid[:, g * hidden:(g + 1) * hidden].astype(BF16), w2_ref[...])
            row = lax.broadcasted_iota(jnp.int32, out.shape, 0)
            o_ref[0, g] = jnp.where(row < nr - 1, out, 0.0)

    run(k_ref, pk_ref, w1k_ref, b1k_ref, w2k_ref, ko_ref)
    run(v_ref, pv_ref, w1v_ref, b1v_ref, w2v_ref, vo_ref)


def _compress_params(pos, w1, b1, w2):
    L, dh = pos.shape
    hidden = w1.shape[1]
    w = w1.reshape(L, dh, hidden).astype(BF16)
    z = jnp.zeros_like(w)
    w_bd = jnp.concatenate([jnp.concatenate([w, z], axis=2), jnp.concatenate([z, w], axis=2)], axis=1)
    return (jnp.tile(pos, (1, NSA_KV_GROUPS)), w_bd, jnp.tile(b1.reshape(1, -1), (1, NSA_KV_GROUPS)),
            w2.astype(BF16))


def _compress(kc_r, proj3, k_params, v_params):
    B, S, W = kc_r.shape
    NR = S // CMP_STRIDE
    full = lambda a: pl.BlockSpec(a.shape, lambda b: (0,) * a.ndim)
    o_spec = pl.BlockSpec((1, NSA_KV_GROUPS, NR, NSA_DH), lambda b: (b, 0, 0, 0))
    o_shape = jax.ShapeDtypeStruct((B, NSA_KV_GROUPS, NR, NSA_DH), F32)
    params = tuple(k_params) + tuple(v_params)
    return pl.pallas_call(
        _compress_kernel,
        grid=(B,),
        in_specs=[pl.BlockSpec((1, S, W), lambda b: (b, 0, 0)),
                  pl.BlockSpec((1, S, W), lambda b, o=_COLS["nvc"][0] // W: (b, 0, o))]
                 + [full(a) for a in params],
        out_specs=[o_spec, o_spec],
        out_shape=[o_shape, o_shape],
        compiler_params=_cparams("parallel"),
        name="compress",
    )(kc_r, proj3, *params)


def _nsa_attn_kernel(q_ref, kc_ref, vc_ref, ks_ref, vs_ref, kw_ref, vw_ref, gate_ref, ovt_ref, o_ref,
                     *, tq, S, n_sel, n_top):
    g = pl.program_id(1)
    qi = pl.program_id(2)
    start = qi * tq
    H = NSA_HPG
    qs = q_ref[0].reshape(H * tq, NSA_DH)
    t_col = start + lax.broadcasted_iota(jnp.int32, (tq, 1), 0)
    t_row = start + lax.broadcasted_iota(jnp.int32, (1, tq), 1)

    kc = kc_ref[0, 0].astype(BF16)
    vc = vc_ref[0, 0].astype(BF16)
    NR = kc.shape[0]
    s = _dot_nt(qs, kc)
    ncol = lax.broadcasted_iota(jnp.int32, (1, NR), 1)
    cmask = (ncol < NR - 1) & (ncol * CMP_STRIDE + (CMP_LEN - 1) <= t_col)
    s3 = jnp.where(cmask[None], s.reshape(H, tq, NR), NEG)
    m = jnp.max(s3, axis=-1, keepdims=True)
    e = jnp.exp(s3 - m)
    p3 = e / jnp.sum(e, axis=-1, keepdims=True)
    p3 = jnp.where(cmask[None], p3, 0.0)
    o_cmp = _dot(p3.reshape(H * tq, NR).astype(BF16), vc)

    psum = jnp.sum(p3, axis=0)
    p_hi = psum.astype(BF16)
    p_lo = (psum - p_hi.astype(F32)).astype(BF16)
    ovt = ovt_ref[...]
    imp = _dot_nt(ovt, p_hi) + _dot_nt(ovt, p_lo)
    blk = lax.broadcasted_iota(jnp.int32, (n_sel, 1), 0)
    cur = t_row // SEL_LEN
    causal_blk = blk <= cur
    forced = (blk == 0) | (blk == cur) | (blk == cur - 1)
    val = jnp.where(causal_blk, jnp.where(forced, jnp.inf, imp), -jnp.inf)
    rank = jnp.zeros((n_sel, tq), jnp.int32)
    for j in range(n_sel):
        vj = val[j:j + 1, :]
        beats = (vj > val) | ((vj == val) & (blk > j))
        rank = rank + beats.astype(jnp.int32)
    sel_t = jnp.where((rank < n_top) & causal_blk, 1.0, 0.0)
    sel_t = jnp.concatenate([sel_t, jnp.zeros((128 - n_sel, tq), F32)], axis=0)
    sel = sel_t.T.astype(BF16)

    tk = 256
    nsub = 2 if S % (2 * tk) == 0 else 1
    bpc = tk // SEL_LEN
    n_trips = (start + tq + nsub * tk - 1) // (nsub * tk)
    erow = lax.broadcasted_iota(jnp.int32, (128, tk), 0)
    ecol = lax.broadcasted_iota(jnp.int32, (128, tk), 1) // SEL_LEN
    kk = lax.broadcasted_iota(jnp.int32, (1, tk), 1)

    def sel_trip(c, carry):
        m_i, acc = carry
        scores, vals = [], []
        m_new = m_i
        for u in range(nsub):
            cu = c * nsub + u
            k0 = pl.multiple_of(cu * tk, tk)
            kb = ks_ref[0, 0, pl.ds(k0, tk), :]
            vals.append(vs_ref[0, 0, pl.ds(k0, tk), :])
            expand = jnp.where(erow == ecol + cu * bpc, 1.0, 0.0).astype(BF16)
            allowed = (_dot(sel, expand) > 0.5) & (k0 + kk <= t_col)
            bias = jnp.where(allowed, 0.0, NEG)
            sc3 = _dot_nt(qs, kb).reshape(H, tq, tk) + bias[None]
            scores.append(sc3)
            m_new = jnp.maximum(m_new, jnp.max(sc3, axis=-1, keepdims=True))
        acc = jnp.exp(m_i - m_new).reshape(H * tq, 1) * acc
        for u in range(nsub):
            pe = jnp.exp((scores[u] - m_new).astype(BF16))
            acc = acc + _dot(pe.reshape(H * tq, tk), vals[u])
        return m_new, acc

    m0 = jnp.full((H, tq, 1), NEG, F32)
    a0 = jnp.zeros((H * tq, 2 * NSA_DH), F32)
    _, acc_f = lax.fori_loop(0, n_trips, sel_trip, (m0, a0))
    o_sel = acc_f[:, :NSA_DH] / acc_f[:, NSA_DH:NSA_DH + 1]

    span = min(WINDOW + tq, S)
    ws = jnp.clip(start - WINDOW, 0, S - span)
    ws = pl.multiple_of(ws, tq)
    kwb = kw_ref[0, 0, pl.ds(ws, span), :]
    vwb = vw_ref[0, 0, pl.ds(ws, span), :]
    kp = ws + lax.broadcasted_iota(jnp.int32, (1, span), 1)
    wbias = jnp.where((kp <= t_col) & (kp > t_col - WINDOW), 0.0, NEG)
    sw3 = _dot_nt(qs, kwb).reshape(H, tq, span) + wbias[None]
    mw = jnp.max(sw3, axis=-1, keepdims=True)
    ew = jnp.exp((sw3 - mw).astype(BF16))
    aw = _dot(ew.reshape(H * tq, span), vwb)
    o_win = aw[:, :NSA_DH] / aw[:, NSA_DH:NSA_DH + 1]

    gs = jax.nn.sigmoid(gate_ref[0])
    for g_static in range(NSA_KV_GROUPS):
        @pl.when(g == g_static)
        def _(g_static=g_static):
            for h in range(H):
                c0 = g_static * H * 3 + h * 3
                rs = slice(h * tq, (h + 1) * tq)
                o = (gs[:, c0:c0 + 1] * o_cmp[rs] + gs[:, c0 + 1:c0 + 2] * o_sel[rs]
                     + gs[:, c0 + 2:c0 + 3] * o_win[rs])
                o_ref[0, :, h * NSA_DH:(h + 1) * NSA_DH] = o.astype(o_ref.dtype)


def _nsa_attn(q_r, kcmp, vcmp, ks_r, vs_r, kw_r, vw_r, proj3, ov):
    B, _, S, _ = q_r.shape
    G, H = NSA_KV_GROUPS, NSA_HPG
    NR = kcmp.shape[2]
    tq = 256
    n_sel = S // SEL_LEN
    n_top = min(SEL_TOPK, n_sel)
    cmp_spec = pl.BlockSpec((1, 1, NR, NSA_DH), lambda b, g, i: (b, g, 0, 0))
    kv_spec = pl.BlockSpec((1, 1, S, NSA_DH), lambda b, g, i: (b, g, 0, 0))
    vx_spec = pl.BlockSpec((1, 1, S, 2 * NSA_DH), lambda b, g, i: (b, g, 0, 0))
    goff = _COLS["ng"][0] // 128
    return pl.pallas_call(
        functools.partial(_nsa_attn_kernel, tq=tq, S=S, n_sel=n_sel, n_top=n_top),
        grid=(B, G, S // tq),
        in_specs=[pl.BlockSpec((1, H, tq, NSA_DH), lambda b, g, i: (b, g, i, 0)),
                  cmp_spec, cmp_spec, kv_spec, vx_spec, kv_spec, vx_spec,
                  pl.BlockSpec((1, tq, 128), lambda b, g, i: (b, i, goff)),
                  pl.BlockSpec(ov.shape, lambda b, g, i: (0, 0))],
        out_specs=pl.BlockSpec((1, tq, H * NSA_DH), lambda b, g, i: (b, i, g)),
        out_shape=jax.ShapeDtypeStruct((B, S, NSA_HEADS * NSA_DH), BF16),
        compiler_params=_cparams("parallel", "parallel", "arbitrary"),
        name="nsa_attn",
    )(q_r, kcmp, vcmp, ks_r, vs_r, kw_r, vw_r, proj3, ov)


def _layer_norm(x, g, b):
    mu = jnp.mean(x, axis=-1, keepdims=True)
    xc = x - mu
    var = jnp.mean(xc * xc, axis=-1, keepdims=True)
    return xc * lax.rsqrt(var + LN_EPS) * g + b


SUB = D_MODEL // 2 // 128


def _pack_bf16_pairs(v):
    half = v.shape[1] // 2
    bits = pltpu.bitcast(v.astype(BF16).astype(F32), jnp.uint32)
    return pltpu.bitcast((bits[:, :half] >> 16) | bits[:, half:], jnp.int32)


def _unpack_bf16_pairs(p):
    u = pltpu.bitcast(p, jnp.uint32)
    return pltpu.bitcast(u << 16, F32), pltpu.bitcast(u & jnp.uint32(0xFFFF0000), F32)


def _to_tiles(ref, val, lead=()):
    for c in range(SUB):
        ref[lead + (c,)] = val[:, c * 128:(c + 1) * 128]


def _from_tiles(ref, lead=()):
    return jnp.concatenate([ref[lead + (c,)] for c in range(SUB)], axis=1)


def _outproj_kernel(x_ref, og_ref, on_ref, w_ref, g_ref, b_ref, o_ref, ot_ref):
    half = og_ref.shape[1]
    mix = _dot(og_ref[...], w_ref[0:half, :]) + _dot(on_ref[...], w_ref[half:, :])
    h = _layer_norm(DN_ALPHA * x_ref[...] + mix, g_ref[...], b_ref[...])
    o_ref[...] = h
    _to_tiles(ot_ref, _pack_bf16_pairs(h))


def _outproj(x2, row0, og2, on2, w, g, b):
    T = og2.shape[0]
    D = x2.shape[1]
    tm = 512
    off = row0 // tm
    row = lambda width: pl.BlockSpec((tm, width), lambda i: (i, 0))
    full = lambda a: pl.BlockSpec(a.shape, lambda i: (0, 0))
    return pl.pallas_call(
        _outproj_kernel,
        grid=(T // tm,),
        in_specs=[pl.BlockSpec((tm, D), lambda i: (i + off, 0)), row(og2.shape[1]), row(on2.shape[1]),
                  full(w), full(g), full(b)],
        out_specs=[row(D), pl.BlockSpec((SUB, tm, 128), lambda i: (0, i, 0))],
        out_shape=[jax.ShapeDtypeStruct((T, D), F32),
                   jax.ShapeDtypeStruct((SUB, T, 128), jnp.int32)],
        compiler_params=_cparams("parallel"),
        name="outproj_ln",
    )(x2, og2, on2, w, g, b)


def _router_kernel(h_ref, wh_ref, wl_ref, bias_ref, eidx_ref, wts_ref, rank_ref, cnt_ref, carry_ref):
    @pl.when(pl.program_id(0) == 0)
    def _():
        carry_ref[...] = jnp.zeros_like(carry_ref)

    h = h_ref[...]
    tm = h.shape[0]
    E = N_EXPERTS
    h_hi = h.astype(BF16)
    h_lo = (h - h_hi.astype(F32)).astype(BF16)
    wh = wh_ref[...]
    logits = _dot(h_hi, wh) + _dot(h_lo, wh) + _dot(h_hi, wl_ref[...])
    scores = jax.nn.sigmoid(logits)
    biased = scores + bias_ref[...]
    lane_i = lax.broadcasted_iota(jnp.int32, (tm, E), 1)
    gid = lane_i // (E // N_GROUPS)
    lane = lane_i.astype(F32)
    ninf = -jnp.inf

    def row_max(x):
        return jnp.max(x, axis=-1, keepdims=True)

    def first_idx(x, mx):
        return jnp.min(jnp.where(x == mx, lane, float(E)), axis=-1, keepdims=True)

    gscore = []
    for gi in range(N_GROUPS):
        mg = jnp.where(gid == gi, biased, ninf)
        m1 = row_max(mg)
        i1 = first_idx(mg, m1)
        m2 = row_max(jnp.where(lane == i1, ninf, mg))
        gscore.append(m1 + m2)
    emask = jnp.zeros((tm, E), jnp.bool_)
    for gi in range(N_GROUPS):
        rk = jnp.zeros((tm, 1), jnp.int32)
        for gj in range(N_GROUPS):
            if gj == gi:
                continue
            beats = (gscore[gj] > gscore[gi]) | ((gscore[gj] == gscore[gi]) & (gj < gi))
            rk = rk + beats.astype(jnp.int32)
        emask = emask | ((gid == gi) & (rk < TOPK_GROUPS))
    masked = jnp.where(emask, biased, ninf)

    onehots, wsel = [], []
    selm = jnp.zeros((tm, E), F32)
    for k in range(TOP_K):
        mx = row_max(masked)
        idx = first_idx(masked, mx)
        oh = lane == idx
        onehots.append(oh)
        wsel.append(jnp.sum(jnp.where(oh, scores, 0.0), axis=-1, keepdims=True))
        masked = jnp.where(oh, ninf, masked)
        selm = jnp.where(oh, 1.0, selm)
        eidx_ref[:, k:k + 1] = idx.astype(jnp.int32)
    wsum = wsel[0]
    for k in range(1, TOP_K):
        wsum = wsum + wsel[k]
    for k in range(TOP_K):
        wts_ref[:, k:k + 1] = wsel[k] / wsum * ROUTED_SCALE

    ri = lax.broadcasted_iota(jnp.int32, (tm, tm), 0)
    ci = lax.broadcasted_iota(jnp.int32, (tm, tm), 1)
    ltri = jnp.where(ri > ci, 1.0, 0.0).astype(BF16)
    cum = _dot(ltri, selm.astype(BF16)) + carry_ref[...]
    for k in range(TOP_K):
        rk = jnp.sum(jnp.where(onehots[k], cum, 0.0), axis=-1, keepdims=True)
        rank_ref[:, k:k + 1] = rk.astype(jnp.int32)
    total = carry_ref[...] + jnp.sum(selm, axis=0, keepdims=True)
    carry_ref[...] = total
    cnt_ref[...] = total


def _router(h2, w_hi, w_lo, bias):
    T, D = h2.shape
    tm = 256
    full = lambda a: pl.BlockSpec(a.shape, lambda i: (0, 0))
    o8 = pl.BlockSpec((tm, TOP_K), lambda i: (i, 0))
    return pl.pallas_call(
        _router_kernel,
        grid=(T // tm,),
        in_specs=[pl.BlockSpec((tm, D), lambda i: (i, 0)), full(w_hi), full(w_lo), full(bias)],
        out_specs=[o8, o8, o8, pl.BlockSpec((1, N_EXPERTS), lambda i: (0, 0))],
        out_shape=[jax.ShapeDtypeStruct((T, TOP_K), jnp.int32),
                   jax.ShapeDtypeStruct((T, TOP_K), F32),
                   jax.ShapeDtypeStruct((T, TOP_K), jnp.int32),
                   jax.ShapeDtypeStruct((1, N_EXPERTS), F32)],
        scratch_shapes=[pltpu.VMEM((1, N_EXPERTS), F32)],
        compiler_params=_cparams("arbitrary"),
        name="router",
    )(h2, w_hi, w_lo, bias)


def _dest_kernel(eidx_ref, rank_ref, ps_ref, dest_ref):
    tm = eidx_ref.shape[0]
    lane = lax.broadcasted_iota(jnp.int32, (tm, N_EXPERTS), 1)
    ps = ps_ref[...]
    for k in range(TOP_K):
        start = jnp.sum(jnp.where(lane == eidx_ref[:, k:k + 1], ps, 0.0), axis=-1, keepdims=True)
        dest_ref[:, k:k + 1] = start.astype(jnp.int32) + rank_ref[:, k:k + 1]


def _dest(eidx, rank, pad_start_f):
    T = eidx.shape[0]
    tm = 1024 if T % 1024 == 0 else T
    o8 = pl.BlockSpec((tm, TOP_K), lambda i: (i, 0))
    return pl.pallas_call(
        _dest_kernel,
        grid=(T // tm,),
        in_specs=[o8, o8, pl.BlockSpec((1, N_EXPERTS), lambda i: (0, 0))],
        out_specs=o8,
        out_shape=jax.ShapeDtypeStruct((T, TOP_K), jnp.int32),
        compiler_params=_cparams("parallel"),
        name="dest",
    )(eidx, rank, pad_start_f)


SC_WINDOW = 128


def _sc_gather(table, idx):
    _, lanes = table.shape
    n = idx.shape[0]
    mesh = plsc.VectorSubcoreMesh(core_axis_name="core", subcore_axis_name="subcore")

    @functools.partial(pl.kernel, out_type=jax.ShapeDtypeStruct((n, lanes), table.dtype), mesh=mesh,
                       name="sc_row_gather")
    def gather(x_hbm, i_hbm, o_hbm):
        def body(i_vmem, o_vmem):
            pltpu.sync_copy(x_hbm.at[i_vmem.at[0]], o_vmem)

        pltpu.emit_pipeline(
            body,
            grid=(n // SC_WINDOW,),
            in_specs=[pl.BlockSpec((1, SC_WINDOW), lambda i: (0, i))],
            out_specs=[pl.BlockSpec((SC_WINDOW, lanes), lambda i: (i, 0))],
            core_axis_name=("core", "subcore"),
            dimension_semantics=(pltpu.PARALLEL,),
            trace_scopes=False,
        )(i_hbm, o_hbm)

    return gather(table, idx.reshape(1, n))


def _sc_scatter(src, idx, n_out):
    rows, lanes = src.shape
    n_idx = idx.shape[0]
    mesh = plsc.VectorSubcoreMesh(core_axis_name="core", subcore_axis_name="subcore")

    @functools.partial(pl.kernel, out_type=jax.ShapeDtypeStruct((n_out, lanes), src.dtype), mesh=mesh,
                       name="sc_row_scatter")
    def scatter(x_hbm, i_hbm, o_hbm):
        def body(x_vmem, *i_vmems):
            for i_vmem in i_vmems:
                pltpu.sync_copy(x_vmem, o_hbm.at[i_vmem.at[0]])

        pltpu.emit_pipeline(
            body,
            grid=(rows // SC_WINDOW,),
            in_specs=[pl.BlockSpec((SC_WINDOW, lanes), lambda i: (i, 0))]
                     + [pl.BlockSpec((1, SC_WINDOW), lambda i, j=j: (j, i)) for j in range(n_idx)],
            out_specs=[],
            core_axis_name=("core", "subcore"),
            dimension_semantics=(pltpu.PARALLEL,),
            trace_scopes=False,
        )(x_hbm, *([i_hbm] * n_idx))

    return scatter(src, idx)


def _expert_kernel(bexp_ref, nused_ref, x_ref, wg_ref, wu_ref, wd_ref, y_ref, wg_b, wu_b, wd_b):
    i = pl.program_id(0)

    @pl.when(i < nused_ref[0])
    def _():
        @pl.when((i == 0) | (bexp_ref[i] != bexp_ref[jnp.maximum(i - 1, 0)]))
        def _():
            wg_b[...] = wg_ref[0].astype(BF16)
            wu_b[...] = wu_ref[0].astype(BF16)
            wd_b[...] = wd_ref[0].astype(BF16)

        x = jnp.concatenate(_unpack_bf16_pairs(_from_tiles(x_ref)), axis=1).astype(BF16)
        gate = _dot(x, wg_b[...])
        up = _dot(x, wu_b[...])
        act = (gate * jax.nn.sigmoid(gate) * up).astype(BF16)
        _to_tiles(y_ref, _pack_bf16_pairs(_dot(act, wd_b[...])))


def _experts(blk_exp, n_used, xs_t, wg, wu, wd):
    NP = xs_t.shape[1]
    D = D_MODEL
    nb = NP // MOE_BLOCK
    blk = (SUB, MOE_BLOCK, 128)

    def xmap(i, bexp, nused):
        return (0, jnp.minimum(i, nused[0] - 1), 0)

    def wmap(i, bexp, nused):
        return (bexp[jnp.minimum(i, nused[0] - 1)], 0, 0)

    grid_spec = pltpu.PrefetchScalarGridSpec(
        num_scalar_prefetch=2,
        grid=(nb,),
        in_specs=[pl.BlockSpec(blk, xmap),
                  pl.BlockSpec((1, D, D_EXPERT), wmap),
                  pl.BlockSpec((1, D, D_EXPERT), wmap),
                  pl.BlockSpec((1, D_EXPERT, D), wmap)],
        out_specs=pl.BlockSpec(blk, xmap),
        scratch_shapes=[pltpu.VMEM((D, D_EXPERT), BF16), pltpu.VMEM((D, D_EXPERT), BF16),
                        pltpu.VMEM((D_EXPERT, D), BF16)],
    )
    return pl.pallas_call(
        _expert_kernel,
        grid_spec=grid_spec,
        out_shape=jax.ShapeDtypeStruct(xs_t.shape, jnp.int32),
        compiler_params=_cparams("arbitrary"),
        name="experts",
    )(blk_exp, n_used, xs_t, wg, wu, wd)


def _combine_kernel(h_ref, wts_ref, yg_ref, wsg_ref, wsu_ref, wsd_ref, g_ref, b_ref, *rest):
    o_ref = rest[-1]
    h = h_ref[...]
    hb = h.astype(BF16)
    gate = _dot(hb, wsg_ref[...])
    up = _dot(hb, wsu_ref[...])
    shared = _dot((gate * jax.nn.sigmoid(gate) * up).astype(BF16), wsd_ref[...])
    wts = wts_ref[...]
    lo, hi = _unpack_bf16_pairs(_from_tiles(yg_ref, (0,)))
    r_lo, r_hi = wts[:, 0:1] * lo, wts[:, 0:1] * hi
    for k in range(1, TOP_K):
        lo, hi = _unpack_bf16_pairs(_from_tiles(yg_ref, (k,)))
        r_lo, r_hi = r_lo + wts[:, k:k + 1] * lo, r_hi + wts[:, k:k + 1] * hi
    routed = jnp.concatenate([r_lo, r_hi], axis=1)
    o_ref[...] = _layer_norm(DN_ALPHA * h + (routed + shared), g_ref[...], b_ref[...])


def _combine(h2, wts, yg_t, wsg, wsu, wsd, g, b, out_rows, row0, out_prev):
    T, D = h2.shape
    tm = 256
    off = row0 // tm
    full = lambda a: pl.BlockSpec(a.shape, lambda i: (0, 0))
    in_specs = [pl.BlockSpec((tm, D), lambda i: (i, 0)),
                pl.BlockSpec((tm, TOP_K), lambda i: (i, 0)),
                pl.BlockSpec((TOP_K, SUB, tm, 128), lambda i: (0, 0, i, 0)),
                full(wsg), full(wsu), full(wsd), full(g), full(b)]
    args = [h2, wts, yg_t, wsg, wsu, wsd, g, b]
    aliases = {}
    if out_prev is not None:
        in_specs.append(pl.BlockSpec(memory_space=pl.ANY))
        args.append(out_prev)
        aliases = {len(args) - 1: 0}
    return pl.pallas_call(
        _combine_kernel,
        grid=(T // tm,),
        in_specs=in_specs,
        out_specs=pl.BlockSpec((tm, D), lambda i: (i + off, 0)),
        out_shape=jax.ShapeDtypeStruct((out_rows, D), F32),
        input_output_aliases=aliases,
        compiler_params=_cparams("parallel"),
        name="combine_ln",
    )(*args)


def _regroup_w_in(w_in):
    parts, off = {}, 0
    for name, width in _SPLITS:
        parts[name] = w_in[:, off:off + width]
        off += width
    cols = []
    for name, (_, width) in _COLS.items():
        p = parts[name]
        if p.shape[1] < width:
            p = jnp.pad(p, ((0, 0), (0, width - p.shape[1])))
        cols.append(p)
    return jnp.concatenate(cols, axis=1).astype(BF16)


def _overlap_matrix(S):
    nr = S // CMP_STRIDE
    n_sel = S // SEL_LEN
    ci = np.arange(nr)[:, None] * CMP_STRIDE
    sj = np.arange(n_sel)[None, :] * SEL_LEN
    ov = np.clip(np.minimum(ci + CMP_LEN, sj + SEL_LEN) - np.maximum(ci, sj), 0, None) / CMP_LEN
    ov[nr - 1] = 0.0
    return jnp.asarray(ov.T, BF16)


def _mixers(x2, row0, positions, w_in, w_alpha2, b_alpha, gla_norm_g,
            cmp_pos_k, cmp_w1_k, cmp_b1_k, cmp_w2_k, cmp_pos_v, cmp_w1_v, cmp_b1_v, cmp_w2_v):
    B, S = positions.shape
    proj = _proj(x2, row0, B * S, _regroup_w_in(w_in)).reshape(B, S, D_PROJ)

    wa = jnp.pad(w_alpha2, ((0, 128 - GLA_LOWRANK), (0, 0))).astype(BF16)
    o_gla = _gla(proj, wa, b_alpha.reshape(1, -1), gla_norm_g.reshape(1, -1))

    half = NSA_DH // 2
    inv = ROPE_THETA ** (-np.arange(half, dtype=np.float32) / half)
    inv_row = jnp.asarray(np.tile(inv, 128 // half).reshape(1, 128), F32)
    pos3 = positions.astype(F32).reshape(B, S, 1)
    q_r, kc_r, ks_r, vs_r, kw_r, vw_r = _nsa_prep(proj, pos3, inv_row)

    kcmp, vcmp = _compress(kc_r, proj,
                           _compress_params(cmp_pos_k, cmp_w1_k, cmp_b1_k, cmp_w2_k),
                           _compress_params(cmp_pos_v, cmp_w1_v, cmp_b1_v, cmp_w2_v))
    o_nsa = _nsa_attn(q_r, kcmp, vcmp, ks_r, vs_r, kw_r, vw_r, proj, _overlap_matrix(S))
    return o_gla, o_nsa


def _moe_ln(h2, h_t, w_router, router_bias, w_gate, w_up, w_down, ws_gate, ws_up, ws_down, ln_g, ln_b,
            out_rows, row0, out_prev):
    T, D = h2.shape
    P = T * TOP_K
    w_hi = w_router.astype(BF16)
    w_lo = (w_router - w_hi.astype(F32)).astype(BF16)
    eidx, wts, rank, counts = _router(h2, w_hi, w_lo, router_bias.reshape(1, -1))

    counts = counts.reshape(-1).astype(jnp.int32)
    padded = (counts + MOE_BLOCK - 1) // MOE_BLOCK * MOE_BLOCK
    pad_end = jnp.cumsum(padded)
    pad_start = pad_end - padded
    nb = -(-P // MOE_BLOCK) + N_EXPERTS
    n_used = (pad_end[-1] // MOE_BLOCK).astype(jnp.int32).reshape(1)
    blk_start = jnp.arange(nb, dtype=jnp.int32) * MOE_BLOCK
    blk_exp = jnp.minimum(jnp.sum((pad_end[None, :] <= blk_start[:, None]).astype(jnp.int32), axis=1),
                          N_EXPERTS - 1)

    NP = nb * MOE_BLOCK
    dest = _dest(eidx, rank, pad_start.astype(F32).reshape(1, -1))
    col = jnp.arange(SUB, dtype=jnp.int32) * NP

    n_pad = NP - P
    assert n_pad % T == 0
    pad_cnt = padded - counts
    pad_hi = jnp.cumsum(pad_cnt)
    pad_lo = pad_hi - pad_cnt
    j = jnp.arange(n_pad, dtype=jnp.int32)[:, None]
    owner = (pad_lo[None, :] <= j) & (j < pad_hi[None, :])
    in_expert = jnp.sum(jnp.where(owner, (pad_start + counts - pad_lo)[None, :] + j, 0), axis=1)
    pad_rows = jnp.where(j[:, 0] < pad_hi[-1], in_expert, pad_end[-1] + j[:, 0] - pad_hi[-1])

    rows_all = jnp.concatenate([dest.T, pad_rows.reshape(n_pad // T, T)], axis=0)
    dst = (rows_all[:, None, :] + col[None, :, None]).reshape(-1, SUB * T)
    xs_t = _sc_scatter(h_t.reshape(SUB * T, 128), dst, SUB * NP).reshape(SUB, NP, 128)
    ys_t = _experts(blk_exp, n_used, xs_t, w_gate, w_up, w_down)
    src = (dest.T[:, None, :] + col[None, :, None]).reshape(-1)
    yg_t = _sc_gather(ys_t.reshape(SUB * NP, 128), src).reshape(TOP_K, SUB, T, 128)
    return _combine(h2, wts, yg_t, ws_gate.astype(BF16), ws_up.astype(BF16), ws_down.astype(BF16),
                    ln_g.reshape(1, -1), ln_b.reshape(1, -1), out_rows, row0, out_prev)


def kernel(x, positions, w_in, w_alpha2, b_alpha, gla_norm_g, cmp_pos_k, cmp_w1_k, cmp_b1_k, cmp_w2_k, cmp_pos_v, cmp_w1_v, cmp_b1_v, cmp_w2_v, w_out, ln1_g, ln1_b, w_router, router_bias, w_exp_gate, w_exp_up, w_exp_down, w_sh_gate, w_sh_up, w_sh_down, ln2_g, ln2_b):
    B, S, D = x.shape
    n_groups = BATCH_GROUPS if B % BATCH_GROUPS == 0 else 1
    bg = B // n_groups
    rows = bg * S
    h2d = x.reshape(B * S, D)
    for l in range(w_in.shape[0]):
        out = None
        for gi in range(n_groups):
            row0 = gi * rows
            o_gla, o_nsa = _mixers(h2d, row0, positions[gi * bg:(gi + 1) * bg], w_in[l], w_alpha2[l], b_alpha[l],
                                   gla_norm_g[l], cmp_pos_k[l], cmp_w1_k[l], cmp_b1_k[l], cmp_w2_k[l],
                                   cmp_pos_v[l], cmp_w1_v[l], cmp_b1_v[l], cmp_w2_v[l])
            h1, h1_t = _outproj(h2d, row0, o_gla.reshape(rows, -1), o_nsa.reshape(rows, -1),
                                w_out[l].astype(BF16), ln1_g[l].reshape(1, -1), ln1_b[l].reshape(1, -1))
            out = _moe_ln(h1, h1_t, w_router[l], router_bias[l], w_exp_gate[l], w_exp_up[l], w_exp_down[l],
                          w_sh_gate[l], w_sh_up[l], w_sh_down[l], ln2_g[l], ln2_b[l], B * S, row0, out)
        h2d = out
    return h2d.reshape(B, S, D)
```

```python
import functools

import numpy as np
import jax
import jax.numpy as jnp
from jax import lax
from jax.experimental import pallas as pl
from jax.experimental.pallas import tpu as pltpu
from jax.experimental.pallas import tpu_sc as plsc

D_MODEL = 1024
GLA_HEADS = 4
GLA_DV = 128
GLA_DK = 64
GLA_LOWRANK = 16
GLA_TAU = 16.0
GLA_CHUNK = 64
NSA_HEADS = 8
NSA_KV_GROUPS = 2
NSA_HPG = 4
NSA_DH = 64
CMP_LEN = 32
CMP_STRIDE = 16
CMP_HIDDEN = 256
SEL_LEN = 64
SEL_TOPK = 16
WINDOW = 512
ROPE_THETA = 10000.0
N_EXPERTS = 256
TOP_K = 8
N_GROUPS = 8
TOPK_GROUPS = 4
D_EXPERT = 256
ROUTED_SCALE = 2.5
DEPTH = 1
DN_ALPHA = (2.0 * DEPTH) ** 0.25
LN_EPS = 1e-5

MOE_BLOCK = 512
BATCH_GROUPS = 2
NEG = -1e30
F32 = jnp.float32
BF16 = jnp.bfloat16

_COLS = {}
_off = 0
for _name, _w in (("gq", 256), ("gk", 256), ("gv", 512), ("gr", 512), ("nq", 512),
                  ("nkc", 128), ("nvc", 128), ("nks", 128), ("nvs", 128), ("nkw", 128), ("nvw", 128),
                  ("ga", 128), ("ng", 128)):
    _COLS[_name] = (_off, _w)
    _off += _w
D_PROJ = _off
_SPLITS = (("gq", 256), ("gk", 256), ("gv", 512), ("ga", 16), ("gr", 512), ("nq", 512),
           ("nkc", 128), ("nvc", 128), ("nks", 128), ("nvs", 128), ("nkw", 128), ("nvw", 128), ("ng", 24))

VMEM_LIMIT = 56 * 1024 * 1024


def _cparams(*sem):
    return pltpu.CompilerParams(dimension_semantics=sem, vmem_limit_bytes=VMEM_LIMIT)


def _dot(a, b):
    return jnp.dot(a, b, preferred_element_type=F32)


def _dot_nt(a, b):
    return lax.dot_general(a, b, (((1,), (1,)), ((), ())), preferred_element_type=F32)


def _split3(x):
    hi = x.astype(BF16)
    r1 = x - hi.astype(F32)
    mid = r1.astype(BF16)
    lo = (r1 - mid.astype(F32)).astype(BF16)
    return hi, mid, lo


def _proj_kernel(x_ref, w_ref, o_ref):
    o_ref[...] = _dot(x_ref[...].astype(BF16), w_ref[...])


def _proj(x2, row0, rows, w):
    D = x2.shape[1]
    N = w.shape[1]
    tm = 512
    off = row0 // tm
    return pl.pallas_call(
        _proj_kernel,
        grid=(rows // tm,),
        in_specs=[pl.BlockSpec((tm, D), lambda i: (i + off, 0)),
                  pl.BlockSpec((D, N), lambda i: (0, 0))],
        out_specs=pl.BlockSpec((tm, N), lambda i: (i, 0)),
        out_shape=jax.ShapeDtypeStruct((rows, N), F32),
        compiler_params=_cparams("parallel"),
        name="proj",
    )(x2, w)


def _gla_kernel(q_ref, k_ref, v_ref, r_ref, a_ref, wa_ref, ba_ref, ng_ref, o_ref, st_ref, *, nchunk):
    C = GLA_CHUNK
    HK = GLA_HEADS * GLA_DK

    @pl.when(pl.program_id(1) == 0)
    def _():
        st_ref[...] = jnp.zeros_like(st_ref)

    ri = lax.broadcasted_iota(jnp.int32, (C, C), 0)
    ci = lax.broadcasted_iota(jnp.int32, (C, C), 1)
    causal = ri >= ci
    causal4 = jnp.concatenate([causal] * GLA_HEADS, axis=0)
    tri = jnp.where(causal, 1.0, 0.0).astype(BF16)
    lane_head = lax.broadcasted_iota(jnp.int32, (1, HK), 1) // GLA_DK
    wa = wa_ref[...]
    ba = ba_ref[...]
    ng = ng_ref[...]

    def chunk(c, states):
        rows = pl.ds(pl.multiple_of(c * C, C), C)
        results = [chunk_one(rows, bb, states[bb]) for bb in range(len(states))]
        for bb, (outs, _) in enumerate(results):
            for h, o in enumerate(outs):
                o_ref[bb, rows, h * GLA_DV:(h + 1) * GLA_DV] = o
        return tuple(new_st for _, new_st in results)

    def chunk_one(rows, bb, st):
        q = q_ref[bb, rows, :]
        k = k_ref[bb, rows, :]
        v = v_ref[bb, rows, :]
        r = r_ref[bb, rows, :]
        a = a_ref[bb, rows, :]
        z = _dot(a.astype(BF16), wa) + ba
        g = (jnp.minimum(z, 0.0) - jnp.log1p(jnp.exp(-jnp.abs(z)))) * (1.0 / GLA_TAU)
        b3 = _dot(tri, jnp.concatenate(_split3(g), axis=1))
        b = b3[:, :HK] + b3[:, HK:2 * HK] + b3[:, 2 * HK:]
        b_last = b[C - 1:C, :]
        qt = q * jnp.exp(b) * (GLA_DK ** -0.5)
        kt = (k * jnp.exp(-b)).astype(BF16)
        ks = (k * jnp.exp(b_last - b)).astype(BF16)
        vb = v.astype(BF16)
        q_all = jnp.concatenate([jnp.where(lane_head == h, qt, 0.0) for h in range(GLA_HEADS)],
                                axis=0).astype(BF16)
        att_all = jnp.where(causal4, _dot_nt(q_all, kt), 0.0)
        inter_all = _dot_nt(q_all, st.astype(BF16))
        intra_all = _dot(att_all.astype(BF16), vb)
        upd_all = _dot(vb.T, ks)
        new_st = st * jnp.exp(b_last)
        outs = []
        for h in range(GLA_HEADS):
            new_st = new_st + jnp.where(lane_head == h, upd_all[h * GLA_DV:(h + 1) * GLA_DV, :], 0.0)
            o = inter_all[h * C:(h + 1) * C, :] + intra_all[h * C:(h + 1) * C, h * GLA_DV:(h + 1) * GLA_DV]
            o = o * lax.rsqrt(jnp.mean(o * o, axis=-1, keepdims=True) + 1e-6) * ng
            rh = r[:, h * GLA_DV:(h + 1) * GLA_DV]
            outs.append((o * (rh * jax.nn.sigmoid(rh))).astype(o_ref.dtype))
        return outs, new_st

    nbat = q_ref.shape[0]
    states = lax.fori_loop(0, nchunk, chunk, tuple(st_ref[bb] for bb in range(nbat)))
    for bb in range(nbat):
        st_ref[bb] = states[bb]


def _gla(proj3, wa, ba, ng):
    B, S, _ = proj3.shape
    L = 512 if S % 512 == 0 else S
    nchunk = L // GLA_CHUNK

    def col(name, width):
        off = _COLS[name][0]
        assert off % width == 0
        return pl.BlockSpec((nbat, L, width), lambda b, j, o=off // width: (b, j, o))

    nbat = 2 if B % 2 == 0 else 1
    full2 = lambda shape: pl.BlockSpec(shape, lambda b, j: (0, 0))
    return pl.pallas_call(
        functools.partial(_gla_kernel, nchunk=nchunk),
        grid=(B // nbat, S // L),
        in_specs=[col("gq", 256), col("gk", 256), col("gv", 512), col("gr", 512), col("ga", 128),
                  full2((128, 256)), full2((1, 256)), full2((1, 128))],
        out_specs=pl.BlockSpec((nbat, L, 512), lambda b, j: (b, j, 0)),
        out_shape=jax.ShapeDtypeStruct((B, S, 512), BF16),
        scratch_shapes=[pltpu.VMEM((nbat, GLA_DV, GLA_HEADS * GLA_DK), F32)],
        compiler_params=_cparams("parallel", "arbitrary"),
        name="gla",
    )(proj3, proj3, proj3, proj3, proj3, wa, ba, ng)


def _rot_half(x):
    n = x.shape[-1]
    lane = lax.broadcasted_iota(jnp.int32, (1, n), 1)
    first = (lane % NSA_DH) < (NSA_DH // 2)
    return jnp.where(first, -pltpu.roll(x, n - NSA_DH // 2, 1), pltpu.roll(x, NSA_DH // 2, 1))


def _nsa_prep_kernel(pos_ref, inv_ref, q_ref, kc_ref, ks_ref, vs_ref, kw_ref, vw_ref,
                     qo_ref, kco_ref, kso_ref, vso_ref, kwo_ref, vwo_ref):
    ang = pos_ref[0] * inv_ref[...]
    cos = jnp.cos(ang)
    sin = jnp.sin(ang)
    cos4 = jnp.concatenate([cos] * 4, axis=1)
    sin4 = jnp.concatenate([sin] * 4, axis=1)

    q = q_ref[0]
    qr = q * cos4 + _rot_half(q) * sin4
    for h in range(NSA_HEADS):
        qo_ref[0, h] = (qr[:, h * NSA_DH:(h + 1) * NSA_DH] * (NSA_DH ** -0.5)).astype(qo_ref.dtype)

    def rope128(x):
        return x * cos + _rot_half(x) * sin

    kco_ref[0] = rope128(kc_ref[0])
    ksr = rope128(ks_ref[0])
    kwr = rope128(kw_ref[0])
    vs = vs_ref[0]
    vw = vw_ref[0]
    ones_col = jnp.where(lax.broadcasted_iota(jnp.int32, (vs.shape[0], NSA_DH), 1) == 0, 1.0, 0.0)
    for g in range(NSA_KV_GROUPS):
        sl = slice(g * NSA_DH, (g + 1) * NSA_DH)
        kso_ref[0, g] = ksr[:, sl].astype(kso_ref.dtype)
        kwo_ref[0, g] = kwr[:, sl].astype(kwo_ref.dtype)
        vso_ref[0, g] = jnp.concatenate([vs[:, sl], ones_col], axis=1).astype(vso_ref.dtype)
        vwo_ref[0, g] = jnp.concatenate([vw[:, sl], ones_col], axis=1).astype(vwo_ref.dtype)


def _nsa_prep(proj3, pos3, inv_row):
    B, S, _ = proj3.shape
    ts = 512 if S % 512 == 0 else S

    def col(name, width):
        off = _COLS[name][0]
        assert off % width == 0
        return pl.BlockSpec((1, ts, width), lambda b, j, o=off // width: (b, j, o))

    kv_spec = pl.BlockSpec((1, NSA_KV_GROUPS, ts, NSA_DH), lambda b, j: (b, 0, j, 0))
    kv_shape = jax.ShapeDtypeStruct((B, NSA_KV_GROUPS, S, NSA_DH), BF16)
    vx_spec = pl.BlockSpec((1, NSA_KV_GROUPS, ts, 2 * NSA_DH), lambda b, j: (b, 0, j, 0))
    vx_shape = jax.ShapeDtypeStruct((B, NSA_KV_GROUPS, S, 2 * NSA_DH), BF16)
    return pl.pallas_call(
        _nsa_prep_kernel,
        grid=(B, S // ts),
        in_specs=[pl.BlockSpec((1, ts, 1), lambda b, j: (b, j, 0)),
                  pl.BlockSpec((1, 128), lambda b, j: (0, 0)),
                  col("nq", 512), col("nkc", 128), col("nks", 128), col("nvs", 128),
                  col("nkw", 128), col("nvw", 128)],
        out_specs=[pl.BlockSpec((1, NSA_HEADS, ts, NSA_DH), lambda b, j: (b, 0, j, 0)),
                   pl.BlockSpec((1, ts, 128), lambda b, j: (b, j, 0)),
                   kv_spec, vx_spec, kv_spec, vx_spec],
        out_shape=[jax.ShapeDtypeStruct((B, NSA_HEADS, S, NSA_DH), BF16),
                   jax.ShapeDtypeStruct((B, S, 128), F32),
                   kv_shape, vx_shape, kv_shape, vx_shape],
        compiler_params=_cparams("parallel", "parallel"),
        name="nsa_prep",
    )(pos3, inv_row, proj3, proj3, proj3, proj3, proj3, proj3)


def _compress_kernel(k_ref, v_ref, pk_ref, w1k_ref, b1k_ref, w2k_ref, pv_ref, w1v_ref, b1v_ref, w2v_ref,
                     ko_ref, vo_ref):
    nr = k_ref.shape[1] // CMP_STRIDE
    hidden = w2k_ref.shape[0]

    def run(x_ref, p_ref, w1_ref, b1_ref, w2_ref, o_ref):
        first = jnp.zeros((nr, NSA_KV_GROUPS * hidden), F32)
        second = jnp.zeros((nr, NSA_KV_GROUPS * hidden), F32)
        for m in range(CMP_STRIDE):
            xm = x_ref[0, pl.ds(m, nr, stride=CMP_STRIDE), :]
            first = first + _dot((xm + p_ref[m:m + 1, :]).astype(BF16), w1_ref[m])
            second = second + _dot((xm + p_ref[CMP_STRIDE + m:CMP_STRIDE + m + 1, :]).astype(BF16),
                                   w1_ref[CMP_STRIDE + m])
        pre = first + pltpu.roll(second, nr - 1, 0) + b1_ref[...]
        hid = jax.nn.gelu(pre)
        for g in range(NSA_KV_GROUPS):
            out = _dot(hid[:, g * hidden:(g + 1) * hidden].astype(BF16), w2_ref[...])
            row = lax.broadcasted_iota(jnp.int32, out.shape, 0)
            o_ref[0, g] = jnp.where(row < nr - 1, out, 0.0)

    run(k_ref, pk_ref, w1k_ref, b1k_ref, w2k_ref, ko_ref)
    run(v_ref, pv_ref, w1v_ref, b1v_ref, w2v_ref, vo_ref)


def _compress_params(pos, w1, b1, w2):
    L, dh = pos.shape
    hidden = w1.shape[1]
    w = w1.reshape(L, dh, hidden).astype(BF16)
    z = jnp.zeros_like(w)
    w_bd = jnp.concatenate([jnp.concatenate([w, z], axis=2), jnp.concatenate([z, w], axis=2)], axis=1)
    return (jnp.tile(pos, (1, NSA_KV_GROUPS)), w_bd, jnp.tile(b1.reshape(1, -1), (1, NSA_KV_GROUPS)),
            w2.astype(BF16))


def _compress(kc_r, proj3, k_params, v_params):
    B, S, W = kc_r.shape
    NR = S // CMP_STRIDE
    full = lambda a: pl.BlockSpec(a.shape, lambda b: (0,) * a.ndim)
    o_spec = pl.BlockSpec((1, NSA_KV_GROUPS, NR, NSA_DH), lambda b: (b, 0, 0, 0))
    o_shape = jax.ShapeDtypeStruct((B, NSA_KV_GROUPS, NR, NSA_DH), F32)
    params = tuple(k_params) + tuple(v_params)
    return pl.pallas_call(
        _compress_kernel,
        grid=(B,),
        in_specs=[pl.BlockSpec((1, S, W), lambda b: (b, 0, 0)),
                  pl.BlockSpec((1, S, W), lambda b, o=_COLS["nvc"][0] // W: (b, 0, o))]
                 + [full(a) for a in params],
        out_specs=[o_spec, o_spec],
        out_shape=[o_shape, o_shape],
        compiler_params=_cparams("parallel"),
        name="compress",
    )(kc_r, proj3, *params)


def _nsa_attn_kernel(q_ref, kc_ref, vc_ref, ks_ref, vs_ref, kw_ref, vw_ref, gate_ref, ovt_ref, o_ref,
                     *, tq, S, n_sel, n_top):
    g = pl.program_id(1)
    qi = pl.program_id(2)
    start = qi * tq
    H = NSA_HPG
    qs = q_ref[0].reshape(H * tq, NSA_DH)
    t_col = start + lax.broadcasted_iota(jnp.int32, (tq, 1), 0)
    t_row = start + lax.broadcasted_iota(jnp.int32, (1, tq), 1)

    kc = kc_ref[0, 0].astype(BF16)
    vc = vc_ref[0, 0].astype(BF16)
    NR = kc.shape[0]
    s = _dot_nt(qs, kc)
    ncol = lax.broadcasted_iota(jnp.int32, (1, NR), 1)
    cmask = (ncol < NR - 1) & (ncol * CMP_STRIDE + (CMP_LEN - 1) <= t_col)
    s3 = jnp.where(cmask[None], s.reshape(H, tq, NR), NEG)
    m = jnp.max(s3, axis=-1, keepdims=True)
    e = jnp.exp(s3 - m)
    p3 = e / jnp.sum(e, axis=-1, keepdims=True)
    p3 = jnp.where(cmask[None], p3, 0.0)
    o_cmp = _dot(p3.reshape(H * tq, NR).astype(BF16), vc)

    psum = jnp.sum(p3, axis=0)
    p_hi = psum.astype(BF16)
    p_lo = (psum - p_hi.astype(F32)).astype(BF16)
    ovt = ovt_ref[...]
    imp = _dot_nt(ovt, p_hi) + _dot_nt(ovt, p_lo)
    blk = lax.broadcasted_iota(jnp.int32, (n_sel, 1), 0)
    cur = t_row // SEL_LEN
    causal_blk = blk <= cur
    forced = (blk == 0) | (blk == cur) | (blk == cur - 1)
    val = jnp.where(causal_blk, jnp.where(forced, jnp.inf, imp), -jnp.inf)
    rank = jnp.zeros((n_sel, tq), jnp.int32)
    for j in range(n_sel):
        vj = val[j:j + 1, :]
        beats = (vj > val) | ((vj == val) & (blk > j))
        rank = rank + beats.astype(jnp.int32)
    sel_t = jnp.where((rank < n_top) & causal_blk, 1.0, 0.0)
    sel_t = jnp.concatenate([sel_t, jnp.zeros((128 - n_sel, tq), F32)], axis=0)
    sel = sel_t.T.astype(BF16)

    tk = 256
    nsub = 2 if S % (2 * tk) == 0 else 1
    bpc = tk // SEL_LEN
    n_trips = (start + tq + nsub * tk - 1) // (nsub * tk)
    erow = lax.broadcasted_iota(jnp.int32, (128, tk), 0)
    ecol = lax.broadcasted_iota(jnp.int32, (128, tk), 1) // SEL_LEN
    kk = lax.broadcasted_iota(jnp.int32, (1, tk), 1)

    def sel_trip(c, carry):
        m_i, acc = carry
        scores, vals = [], []
        m_new = m_i
        for u in range(nsub):
            cu = c * nsub + u
            k0 = pl.multiple_of(cu * tk, tk)
            kb = ks_ref[0, 0, pl.ds(k0, tk), :]
            vals.append(vs_ref[0, 0, pl.ds(k0, tk), :])
            expand = jnp.where(erow == ecol + cu * bpc, 1.0, 0.0).astype(BF16)
            allowed = (_dot(sel, expand) > 0.5) & (k0 + kk <= t_col)
            bias = jnp.where(allowed, 0.0, NEG)
            sc3 = _dot_nt(qs, kb).reshape(H, tq, tk) + bias[None]
            scores.append(sc3)
            m_new = jnp.maximum(m_new, jnp.max(sc3, axis=-1, keepdims=True))
        acc = jnp.exp(m_i - m_new).reshape(H * tq, 1) * acc
        for u in range(nsub):
            pe = jnp.exp((scores[u] - m_new).astype(BF16))
            acc = acc + _dot(pe.reshape(H * tq, tk), vals[u])
        return m_new, acc

    m0 = jnp.full((H, tq, 1), NEG, F32)
    a0 = jnp.zeros((H * tq, 2 * NSA_DH), F32)
    _, acc_f = lax.fori_loop(0, n_trips, sel_trip, (m0, a0))
    o_sel = acc_f[:, :NSA_DH] / acc_f[:, NSA_DH:NSA_DH + 1]

    span = min(WINDOW + tq, S)
    ws = jnp.clip(start - WINDOW, 0, S - span)
    ws = pl.multiple_of(ws, tq)
    kwb = kw_ref[0, 0, pl.ds(ws, span), :]
    vwb = vw_ref[0, 0, pl.ds(ws, span), :]
    kp = ws + lax.broadcasted_iota(jnp.int32, (1, span), 1)
    wbias = jnp.where((kp <= t_col) & (kp > t_col - WINDOW), 0.0, NEG)
    sw3 = _dot_nt(qs, kwb).reshape(H, tq, span) + wbias[None]
    mw = jnp.max(sw3, axis=-1, keepdims=True)
    ew = jnp.exp((sw3 - mw).astype(BF16))
    aw = _dot(ew.reshape(H * tq, span), vwb)
    o_win = aw[:, :NSA_DH] / aw[:, NSA_DH:NSA_DH + 1]

    gs = jax.nn.sigmoid(gate_ref[0])
    for g_static in range(NSA_KV_GROUPS):
        @pl.when(g == g_static)
        def _(g_static=g_static):
            for h in range(H):
                c0 = g_static * H * 3 + h * 3
                rs = slice(h * tq, (h + 1) * tq)
                o = (gs[:, c0:c0 + 1] * o_cmp[rs] + gs[:, c0 + 1:c0 + 2] * o_sel[rs]
                     + gs[:, c0 + 2:c0 + 3] * o_win[rs])
                o_ref[0, :, h * NSA_DH:(h + 1) * NSA_DH] = o.astype(o_ref.dtype)


def _nsa_attn(q_r, kcmp, vcmp, ks_r, vs_r, kw_r, vw_r, proj3, ov):
    B, _, S, _ = q_r.shape
    G, H = NSA_KV_GROUPS, NSA_HPG
    NR = kcmp.shape[2]
    tq = 256
    n_sel = S // SEL_LEN
    n_top = min(SEL_TOPK, n_sel)
    cmp_spec = pl.BlockSpec((1, 1, NR, NSA_DH), lambda b, g, i: (b, g, 0, 0))
    kv_spec = pl.BlockSpec((1, 1, S, NSA_DH), lambda b, g, i: (b, g, 0, 0))
    vx_spec = pl.BlockSpec((1, 1, S, 2 * NSA_DH), lambda b, g, i: (b, g, 0, 0))
    goff = _COLS["ng"][0] // 128
    return pl.pallas_call(
        functools.partial(_nsa_attn_kernel, tq=tq, S=S, n_sel=n_sel, n_top=n_top),
        grid=(B, G, S // tq),
        in_specs=[pl.BlockSpec((1, H, tq, NSA_DH), lambda b, g, i: (b, g, i, 0)),
                  cmp_spec, cmp_spec, kv_spec, vx_spec, kv_spec, vx_spec,
                  pl.BlockSpec((1, tq, 128), lambda b, g, i: (b, i, goff)),
                  pl.BlockSpec(ov.shape, lambda b, g, i: (0, 0))],
        out_specs=pl.BlockSpec((1, tq, H * NSA_DH), lambda b, g, i: (b, i, g)),
        out_shape=jax.ShapeDtypeStruct((B, S, NSA_HEADS * NSA_DH), BF16),
        compiler_params=_cparams("parallel", "parallel", "arbitrary"),
        name="nsa_attn",
    )(q_r, kcmp, vcmp, ks_r, vs_r, kw_r, vw_r, proj3, ov)


def _layer_norm(x, g, b):
    mu = jnp.mean(x, axis=-1, keepdims=True)
    xc = x - mu
    var = jnp.mean(xc * xc, axis=-1, keepdims=True)
    return xc * lax.rsqrt(var + LN_EPS) * g + b


SUB = D_MODEL // 2 // 128


def _pack_bf16_pairs(v):
    half = v.shape[1] // 2
    bits = pltpu.bitcast(v.astype(BF16).astype(F32), jnp.uint32)
    return pltpu.bitcast((bits[:, :half] >> 16) | bits[:, half:], jnp.int32)


def _unpack_bf16_pairs(p):
    u = pltpu.bitcast(p, jnp.uint32)
    return pltpu.bitcast(u << 16, F32), pltpu.bitcast(u & jnp.uint32(0xFFFF0000), F32)


def _to_tiles(ref, val, lead=()):
    for c in range(SUB):
        ref[lead + (c,)] = val[:, c * 128:(c + 1) * 128]


def _from_tiles(ref, lead=()):
    return jnp.concatenate([ref[lead + (c,)] for c in range(SUB)], axis=1)


def _outproj_kernel(x_ref, og_ref, on_ref, w_ref, g_ref, b_ref, o_ref, ot_ref):
    half = og_ref.shape[1]
    mix = _dot(og_ref[...], w_ref[0:half, :]) + _dot(on_ref[...], w_ref[half:, :])
    h = _layer_norm(DN_ALPHA * x_ref[...] + mix, g_ref[...], b_ref[...])
    o_ref[...] = h
    _to_tiles(ot_ref, _pack_bf16_pairs(h))


def _outproj(x2, row0, og2, on2, w, g, b):
    T = og2.shape[0]
    D = x2.shape[1]
    tm = 512
    off = row0 // tm
    row = lambda width: pl.BlockSpec((tm, width), lambda i: (i, 0))
    full = lambda a: pl.BlockSpec(a.shape, lambda i: (0, 0))
    return pl.pallas_call(
        _outproj_kernel,
        grid=(T // tm,),
        in_specs=[pl.BlockSpec((tm, D), lambda i: (i + off, 0)), row(og2.shape[1]), row(on2.shape[1]),
                  full(w), full(g), full(b)],
        out_specs=[row(D), pl.BlockSpec((SUB, tm, 128), lambda i: (0, i, 0))],
        out_shape=[jax.ShapeDtypeStruct((T, D), F32),
                   jax.ShapeDtypeStruct((SUB, T, 128), jnp.int32)],
        compiler_params=_cparams("parallel"),
        name="outproj_ln",
    )(x2, og2, on2, w, g, b)


def _router_kernel(h_ref, wh_ref, wl_ref, bias_ref, eidx_ref, wts_ref, rank_ref, cnt_ref, carry_ref):
    @pl.when(pl.program_id(0) == 0)
    def _():
        carry_ref[...] = jnp.zeros_like(carry_ref)

    h = h_ref[...]
    tm = h.shape[0]
    E = N_EXPERTS
    GS = E // N_GROUPS
    h_hi = h.astype(BF16)
    h_lo = (h - h_hi.astype(F32)).astype(BF16)
    wh = wh_ref[...]
    logits = _dot_nt(wh, h_hi) + _dot_nt(wh, h_lo) + _dot_nt(wl_ref[...], h_hi)
    scores = jax.nn.sigmoid(logits)
    biased = scores + bias_ref[...]
    eid = lax.broadcasted_iota(jnp.int32, (E, tm), 0).astype(F32)
    ninf = -jnp.inf

    def col_max(x):
        return jnp.max(x, axis=0, keepdims=True)

    def first_idx(x, mx, ids):
        return jnp.min(jnp.where(x == mx, ids, float(E)), axis=0, keepdims=True)

    ids0 = lax.broadcasted_iota(jnp.int32, (GS, tm), 0).astype(F32)
    gscore = []
    for gi in range(N_GROUPS):
        mg = biased[gi * GS:(gi + 1) * GS, :]
        ids = ids0 + float(gi * GS)
        m1 = col_max(mg)
        i1 = first_idx(mg, m1, ids)
        m2 = col_max(jnp.where(ids == i1, ninf, mg))
        gscore.append(m1 + m2)
    parts = []
    for gi in range(N_GROUPS):
        rk = jnp.zeros((1, tm), jnp.int32)
        for gj in range(N_GROUPS):
            if gj == gi:
                continue
            beats = (gscore[gj] > gscore[gi]) | ((gscore[gj] == gscore[gi]) & (gj < gi))
            rk = rk + beats.astype(jnp.int32)
        parts.append(jnp.where(rk < TOPK_GROUPS, biased[gi * GS:(gi + 1) * GS, :], ninf))
    masked = jnp.concatenate(parts, axis=0)

    onehots, wsel = [], []
    selm = jnp.zeros((E, tm), F32)
    for k in range(TOP_K):
        mx = col_max(masked)
        idx = first_idx(masked, mx, eid)
        oh = eid == idx
        onehots.append(oh)
        wsel.append(jnp.sum(jnp.where(oh, scores, 0.0), axis=0, keepdims=True))
        masked = jnp.where(oh, ninf, masked)
        selm = jnp.where(oh, 1.0, selm)
        eidx_ref[k:k + 1, :] = idx.astype(jnp.int32)
    wsum = wsel[0]
    for k in range(1, TOP_K):
        wsum = wsum + wsel[k]
    for k in range(TOP_K):
        wts_ref[k:k + 1, :] = wsel[k] / wsum * ROUTED_SCALE

    ri = lax.broadcasted_iota(jnp.int32, (tm, tm), 0)
    ci = lax.broadcasted_iota(jnp.int32, (tm, tm), 1)
    before = jnp.where(ri < ci, 1.0, 0.0).astype(BF16)
    cum = _dot(selm.astype(BF16), before) + carry_ref[...]
    for k in range(TOP_K):
        rk = jnp.sum(jnp.where(onehots[k], cum, 0.0), axis=0, keepdims=True)
        rank_ref[k:k + 1, :] = rk.astype(jnp.int32)
    total = carry_ref[...] + jnp.sum(selm, axis=1, keepdims=True)
    carry_ref[...] = total
    cnt_ref[...] = total


def _router(h2, w_hi_t, w_lo_t, bias_col):
    T, D = h2.shape
    tm = 256
    full = lambda a: pl.BlockSpec(a.shape, lambda i: (0, 0))
    o8 = pl.BlockSpec((TOP_K, tm), lambda i: (0, i))
    return pl.pallas_call(
        _router_kernel,
        grid=(T // tm,),
        in_specs=[pl.BlockSpec((tm, D), lambda i: (i, 0)), full(w_hi_t), full(w_lo_t), full(bias_col)],
        out_specs=[o8, o8, o8, pl.BlockSpec((N_EXPERTS, 1), lambda i: (0, 0))],
        out_shape=[jax.ShapeDtypeStruct((TOP_K, T), jnp.int32),
                   jax.ShapeDtypeStruct((TOP_K, T), F32),
                   jax.ShapeDtypeStruct((TOP_K, T), jnp.int32),
                   jax.ShapeDtypeStruct((N_EXPERTS, 1), F32)],
        scratch_shapes=[pltpu.VMEM((N_EXPERTS, 1), F32)],
        compiler_params=_cparams("arbitrary"),
        name="router",
    )(h2, w_hi_t, w_lo_t, bias_col)


def _dest_kernel(eidx_ref, rank_ref, ps_ref, dest_ref):
    tm = eidx_ref.shape[1]
    eid = lax.broadcasted_iota(jnp.int32, (N_EXPERTS, tm), 0)
    ps = ps_ref[...]
    for k in range(TOP_K):
        start = jnp.sum(jnp.where(eid == eidx_ref[k:k + 1, :], ps, 0.0), axis=0, keepdims=True)
        dest_ref[k:k + 1, :] = start.astype(jnp.int32) + rank_ref[k:k + 1, :]


def _dest(eidx_t, rank_t, pad_start_col):
    T = eidx_t.shape[1]
    tm = 1024 if T % 1024 == 0 else T
    o8 = pl.BlockSpec((TOP_K, tm), lambda i: (0, i))
    return pl.pallas_call(
        _dest_kernel,
        grid=(T // tm,),
        in_specs=[o8, o8, pl.BlockSpec((N_EXPERTS, 1), lambda i: (0, 0))],
        out_specs=o8,
        out_shape=jax.ShapeDtypeStruct((TOP_K, T), jnp.int32),
        compiler_params=_cparams("parallel"),
        name="dest",
    )(eidx_t, rank_t, pad_start_col)


SC_WINDOW = 128


def _sc_gather(table, idx):
    _, lanes = table.shape
    n = idx.shape[0]
    mesh = plsc.VectorSubcoreMesh(core_axis_name="core", subcore_axis_name="subcore")

    @functools.partial(pl.kernel, out_type=jax.ShapeDtypeStruct((n, lanes), table.dtype), mesh=mesh,
                       name="sc_row_gather")
    def gather(x_hbm, i_hbm, o_hbm):
        def body(i_vmem, o_vmem):
            pltpu.sync_copy(x_hbm.at[i_vmem.at[0]], o_vmem)

        pltpu.emit_pipeline(
            body,
            grid=(n // SC_WINDOW,),
            in_specs=[pl.BlockSpec((1, SC_WINDOW), lambda i: (0, i))],
            out_specs=[pl.BlockSpec((SC_WINDOW, lanes), lambda i: (i, 0))],
            core_axis_name=("core", "subcore"),
            dimension_semantics=(pltpu.PARALLEL,),
            trace_scopes=False,
        )(i_hbm, o_hbm)

    return gather(table, idx.reshape(1, n))


def _sc_scatter(src, idx, n_out):
    rows, lanes = src.shape
    n_idx = idx.shape[0]
    mesh = plsc.VectorSubcoreMesh(core_axis_name="core", subcore_axis_name="subcore")

    @functools.partial(pl.kernel, out_type=jax.ShapeDtypeStruct((n_out, lanes), src.dtype), mesh=mesh,
                       name="sc_row_scatter")
    def scatter(x_hbm, i_hbm, o_hbm):
        def body(x_vmem, *i_vmems):
            for i_vmem in i_vmems:
                pltpu.sync_copy(x_vmem, o_hbm.at[i_vmem.at[0]])

        pltpu.emit_pipeline(
            body,
            grid=(rows // SC_WINDOW,),
            in_specs=[pl.BlockSpec((SC_WINDOW, lanes), lambda i: (i, 0))]
                     + [pl.BlockSpec((1, SC_WINDOW), lambda i, j=j: (j, i)) for j in range(n_idx)],
            out_specs=[],
            core_axis_name=("core", "subcore"),
            dimension_semantics=(pltpu.PARALLEL,),
            trace_scopes=False,
        )(x_hbm, *([i_hbm] * n_idx))

    return scatter(src, idx)


def _expert_kernel(bexp_ref, nused_ref, x_ref, wg_ref, wu_ref, wd_ref, y_ref, wg_b, wu_b, wd_b):
    i = pl.program_id(0)

    @pl.when(i < nused_ref[0])
    def _():
        @pl.when((i == 0) | (bexp_ref[i] != bexp_ref[jnp.maximum(i - 1, 0)]))
        def _():
            wg_b[...] = wg_ref[0].astype(BF16)
            wu_b[...] = wu_ref[0].astype(BF16)
            wd_b[...] = wd_ref[0].astype(BF16)

        x = jnp.concatenate(_unpack_bf16_pairs(_from_tiles(x_ref)), axis=1).astype(BF16)
        gate = _dot(x, wg_b[...])
        up = _dot(x, wu_b[...])
        act = (gate * jax.nn.sigmoid(gate) * up).astype(BF16)
        _to_tiles(y_ref, _pack_bf16_pairs(_dot(act, wd_b[...])))


def _experts(blk_exp, n_used, xs_t, wg, wu, wd):
    NP = xs_t.shape[1]
    D = D_MODEL
    nb = NP // MOE_BLOCK
    blk = (SUB, MOE_BLOCK, 128)

    def xmap(i, bexp, nused):
        return (0, jnp.minimum(i, nused[0] - 1), 0)

    def wmap(i, bexp, nused):
        return (bexp[jnp.minimum(i, nused[0] - 1)], 0, 0)

    grid_spec = pltpu.PrefetchScalarGridSpec(
        num_scalar_prefetch=2,
        grid=(nb,),
        in_specs=[pl.BlockSpec(blk, xmap),
                  pl.BlockSpec((1, D, D_EXPERT), wmap),
                  pl.BlockSpec((1, D, D_EXPERT), wmap),
                  pl.BlockSpec((1, D_EXPERT, D), wmap)],
        out_specs=pl.BlockSpec(blk, xmap),
        scratch_shapes=[pltpu.VMEM((D, D_EXPERT), BF16), pltpu.VMEM((D, D_EXPERT), BF16),
                        pltpu.VMEM((D_EXPERT, D), BF16)],
    )
    return pl.pallas_call(
        _expert_kernel,
        grid_spec=grid_spec,
        out_shape=jax.ShapeDtypeStruct(xs_t.shape, jnp.int32),
        compiler_params=_cparams("arbitrary"),
        name="experts",
    )(blk_exp, n_used, xs_t, wg, wu, wd)


def _combine_kernel(h_ref, wts_ref, yg_ref, wsg_ref, wsu_ref, wsd_ref, g_ref, b_ref, *rest):
    o_ref = rest[-1]
    h = h_ref[...]
    hb = h.astype(BF16)
    gate = _dot(hb, wsg_ref[...])
    up = _dot(hb, wsu_ref[...])
    shared = _dot((gate * jax.nn.sigmoid(gate) * up).astype(BF16), wsd_ref[...])
    wts = wts_ref[...]
    lo, hi = _unpack_bf16_pairs(_from_tiles(yg_ref, (0,)))
    r_lo, r_hi = wts[:, 0:1] * lo, wts[:, 0:1] * hi
    for k in range(1, TOP_K):
        lo, hi = _unpack_bf16_pairs(_from_tiles(yg_ref, (k,)))
        r_lo, r_hi = r_lo + wts[:, k:k + 1] * lo, r_hi + wts[:, k:k + 1] * hi
    routed = jnp.concatenate([r_lo, r_hi], axis=1)
    o_ref[...] = _layer_norm(DN_ALPHA * h + (routed + shared), g_ref[...], b_ref[...])


def _combine(h2, wts, yg_t, wsg, wsu, wsd, g, b, out_rows, row0, out_prev):
    T, D = h2.shape
    tm = 256
    off = row0 // tm
    full = lambda a: pl.BlockSpec(a.shape, lambda i: (0, 0))
    in_specs = [pl.BlockSpec((tm, D), lambda i: (i, 0)),
                pl.BlockSpec((tm, TOP_K), lambda i: (i, 0)),
                pl.BlockSpec((TOP_K, SUB, tm, 128), lambda i: (0, 0, i, 0)),
                full(wsg), full(wsu), full(wsd), full(g), full(b)]
    args = [h2, wts, yg_t, wsg, wsu, wsd, g, b]
    aliases = {}
    if out_prev is not None:
        in_specs.append(pl.BlockSpec(memory_space=pl.ANY))
        args.append(out_prev)
        aliases = {len(args) - 1: 0}
    return pl.pallas_call(
        _combine_kernel,
        grid=(T // tm,),
        in_specs=in_specs,
        out_specs=pl.BlockSpec((tm, D), lambda i: (i + off, 0)),
        out_shape=jax.ShapeDtypeStruct((out_rows, D), F32),
        input_output_aliases=aliases,
        compiler_params=_cparams("parallel"),
        name="combine_ln",
    )(*args)


def _regroup_w_in(w_in):
    parts, off = {}, 0
    for name, width in _SPLITS:
        parts[name] = w_in[:, off:off + width]
        off += width
    cols = []
    for name, (_, width) in _COLS.items():
        p = parts[name]
        if p.shape[1] < width:
            p = jnp.pad(p, ((0, 0), (0, width - p.shape[1])))
        cols.append(p)
    return jnp.concatenate(cols, axis=1).astype(BF16)


def _overlap_matrix(S):
    nr = S // CMP_STRIDE
    n_sel = S // SEL_LEN
    ci = np.arange(nr)[:, None] * CMP_STRIDE
    sj = np.arange(n_sel)[None, :] * SEL_LEN
    ov = np.clip(np.minimum(ci + CMP_LEN, sj + SEL_LEN) - np.maximum(ci, sj), 0, None) / CMP_LEN
    ov[nr - 1] = 0.0
    return jnp.asarray(ov.T, BF16)


def _mixers(x2, row0, positions, w_in, w_alpha2, b_alpha, gla_norm_g,
            cmp_pos_k, cmp_w1_k, cmp_b1_k, cmp_w2_k, cmp_pos_v, cmp_w1_v, cmp_b1_v, cmp_w2_v):
    B, S = positions.shape
    proj = _proj(x2, row0, B * S, _regroup_w_in(w_in)).reshape(B, S, D_PROJ)

    wa = jnp.pad(w_alpha2, ((0, 128 - GLA_LOWRANK), (0, 0))).astype(BF16)
    o_gla = _gla(proj, wa, b_alpha.reshape(1, -1), gla_norm_g.reshape(1, -1))

    half = NSA_DH // 2
    inv = ROPE_THETA ** (-np.arange(half, dtype=np.float32) / half)
    inv_row = jnp.asarray(np.tile(inv, 128 // half).reshape(1, 128), F32)
    pos3 = positions.astype(F32).reshape(B, S, 1)
    q_r, kc_r, ks_r, vs_r, kw_r, vw_r = _nsa_prep(proj, pos3, inv_row)

    kcmp, vcmp = _compress(kc_r, proj,
                           _compress_params(cmp_pos_k, cmp_w1_k, cmp_b1_k, cmp_w2_k),
                           _compress_params(cmp_pos_v, cmp_w1_v, cmp_b1_v, cmp_w2_v))
    o_nsa = _nsa_attn(q_r, kcmp, vcmp, ks_r, vs_r, kw_r, vw_r, proj, _overlap_matrix(S))
    return o_gla, o_nsa


def _moe_ln(h2, h_t, w_router, router_bias, w_gate, w_up, w_down, ws_gate, ws_up, ws_down, ln_g, ln_b,
            out_rows, row0, out_prev):
    T, D = h2.shape
    P = T * TOP_K
    w_hi = w_router.astype(BF16)
    w_lo = (w_router - w_hi.astype(F32)).astype(BF16)
    eidx_t, wts_t, rank_t, counts = _router(h2, w_hi.T, w_lo.T, router_bias.reshape(-1, 1))

    counts = counts.reshape(-1).astype(jnp.int32)
    padded = (counts + MOE_BLOCK - 1) // MOE_BLOCK * MOE_BLOCK
    pad_end = jnp.cumsum(padded)
    pad_start = pad_end - padded
    nb = -(-P // MOE_BLOCK) + N_EXPERTS
    n_used = (pad_end[-1] // MOE_BLOCK).astype(jnp.int32).reshape(1)
    blk_start = jnp.arange(nb, dtype=jnp.int32) * MOE_BLOCK
    blk_exp = jnp.minimum(jnp.sum((pad_end[None, :] <= blk_start[:, None]).astype(jnp.int32), axis=1),
                          N_EXPERTS - 1)

    NP = nb * MOE_BLOCK
    dest_t = _dest(eidx_t, rank_t, pad_start.astype(F32).reshape(-1, 1))
    col = jnp.arange(SUB, dtype=jnp.int32) * NP

    n_pad = NP - P
    assert n_pad % T == 0
    pad_cnt = padded - counts
    pad_hi = jnp.cumsum(pad_cnt)
    pad_lo = pad_hi - pad_cnt
    j = jnp.arange(n_pad, dtype=jnp.int32)[:, None]
    owner = (pad_lo[None, :] <= j) & (j < pad_hi[None, :])
    in_expert = jnp.sum(jnp.where(owner, (pad_start + counts - pad_lo)[None, :] + j, 0), axis=1)
    pad_rows = jnp.where(j[:, 0] < pad_hi[-1], in_expert, pad_end[-1] + j[:, 0] - pad_hi[-1])

    rows_all = jnp.concatenate([dest_t, pad_rows.reshape(n_pad // T, T)], axis=0)
    dst = (rows_all[:, None, :] + col[None, :, None]).reshape(-1, SUB * T)
    xs_t = _sc_scatter(h_t.reshape(SUB * T, 128), dst, SUB * NP).reshape(SUB, NP, 128)
    ys_t = _experts(blk_exp, n_used, xs_t, w_gate, w_up, w_down)
    src = (dest_t[:, None, :] + col[None, :, None]).reshape(-1)
    yg_t = _sc_gather(ys_t.reshape(SUB * NP, 128), src).reshape(TOP_K, SUB, T, 128)
    return _combine(h2, wts_t.T, yg_t, ws_gate.astype(BF16), ws_up.astype(BF16), ws_down.astype(BF16),
                    ln_g.reshape(1, -1), ln_b.reshape(1, -1), out_rows, row0, out_prev)


def kernel(x, positions, w_in, w_alpha2, b_alpha, gla_norm_g, cmp_pos_k, cmp_w1_k, cmp_b1_k, cmp_w2_k, cmp_pos_v, cmp_w1_v, cmp_b1_v, cmp_w2_v, w_out, ln1_g, ln1_b, w_router, router_bias, w_exp_gate, w_exp_up, w_exp_down, w_sh_gate, w_sh_up, w_sh_down, ln2_g, ln2_b):
    B, S, D = x.shape
    n_groups = BATCH_GROUPS if B % BATCH_GROUPS == 0 else 1
    bg = B // n_groups
    rows = bg * S
    h2d = x.reshape(B * S, D)
    for l in range(w_in.shape[0]):
        out = None
        for gi in range(n_groups):
            row0 = gi * rows
            o_gla, o_nsa = _mixers(h2d, row0, positions[gi * bg:(gi + 1) * bg], w_in[l], w_alpha2[l], b_alpha[l],
                                   gla_norm_g[l], cmp_pos_k[l], cmp_w1_k[l], cmp_b1_k[l], cmp_w2_k[l],
                                   cmp_pos_v[l], cmp_w1_v[l], cmp_b1_v[l], cmp_w2_v[l])
            h1, h1_t = _outproj(h2d, row0, o_gla.reshape(rows, -1), o_nsa.reshape(rows, -1),
                                w_out[l].astype(BF16), ln1_g[l].reshape(1, -1), ln1_b[l].reshape(1, -1))
            out = _moe_ln(h1, h1_t, w_router[l], router_bias[l], w_exp_gate[l], w_exp_up[l], w_exp_down[l],
                          w_sh_gate[l], w_sh_up[l], w_sh_down[l], ln2_g[l], ln2_b[l], B * S, row0, out)
        h2d = out
    return h2d.reshape(B, S, D)
```

```python
import functools

import numpy as np
import jax
import jax.numpy as jnp
from jax import lax
from jax.experimental import pallas as pl
from jax.experimental.pallas import tpu as pltpu
from jax.experimental.pallas import tpu_sc as plsc

D_MODEL = 1024
GLA_HEADS = 4
GLA_DV = 128
GLA_DK = 64
GLA_LOWRANK = 16
GLA_TAU = 16.0
GLA_CHUNK = 64
NSA_HEADS = 8
NSA_KV_GROUPS = 2
NSA_HPG = 4
NSA_DH = 64
CMP_LEN = 32
CMP_STRIDE = 16
CMP_HIDDEN = 256
SEL_LEN = 64
SEL_TOPK = 16
WINDOW = 512
ROPE_THETA = 10000.0
N_EXPERTS = 256
TOP_K = 8
N_GROUPS = 8
TOPK_GROUPS = 4
D_EXPERT = 256
ROUTED_SCALE = 2.5
DEPTH = 1
DN_ALPHA = (2.0 * DEPTH) ** 0.25
LN_EPS = 1e-5

MOE_BLOCK = 512
BATCH_GROUPS = 2
NEG = -1e30
F32 = jnp.float32
BF16 = jnp.bfloat16

_COLS = {}
_off = 0
for _name, _w in (("gq", 256), ("gk", 256), ("gv", 512), ("gr", 512), ("nq", 512),
                  ("nkc", 128), ("nvc", 128), ("nks", 128), ("nvs", 128), ("nkw", 128), ("nvw", 128),
                  ("ga", 128), ("ng", 128)):
    _COLS[_name] = (_off, _w)
    _off += _w
D_PROJ = _off
_SPLITS = (("gq", 256), ("gk", 256), ("gv", 512), ("ga", 16), ("gr", 512), ("nq", 512),
           ("nkc", 128), ("nvc", 128), ("nks", 128), ("nvs", 128), ("nkw", 128), ("nvw", 128), ("ng", 24))

VMEM_LIMIT = 56 * 1024 * 1024


def _cparams(*sem):
    return pltpu.CompilerParams(dimension_semantics=sem, vmem_limit_bytes=VMEM_LIMIT)


def _dot(a, b):
    return jnp.dot(a, b, preferred_element_type=F32)


def _dot_nt(a, b):
    return lax.dot_general(a, b, (((1,), (1,)), ((), ())), preferred_element_type=F32)


def _split3(x):
    hi = x.astype(BF16)
    r1 = x - hi.astype(F32)
    mid = r1.astype(BF16)
    lo = (r1 - mid.astype(F32)).astype(BF16)
    return hi, mid, lo


def _proj_kernel(x_ref, w_ref, o_ref):
    o_ref[...] = _dot(x_ref[...].astype(BF16), w_ref[...])


def _proj(x2, row0, rows, w):
    D = x2.shape[1]
    N = w.shape[1]
    tm = 512
    off = row0 // tm
    return pl.pallas_call(
        _proj_kernel,
        grid=(rows // tm,),
        in_specs=[pl.BlockSpec((tm, D), lambda i: (i + off, 0)),
                  pl.BlockSpec((D, N), lambda i: (0, 0))],
        out_specs=pl.BlockSpec((tm, N), lambda i: (i, 0)),
        out_shape=jax.ShapeDtypeStruct((rows, N), F32),
        compiler_params=_cparams("parallel"),
        name="proj",
    )(x2, w)


def _gla_kernel(q_ref, k_ref, v_ref, r_ref, a_ref, wa_ref, ba_ref, ng_ref, o_ref, st_ref, *, nchunk):
    C = GLA_CHUNK
    HK = GLA_HEADS * GLA_DK

    @pl.when(pl.program_id(1) == 0)
    def _():
        st_ref[...] = jnp.zeros_like(st_ref)

    ri = lax.broadcasted_iota(jnp.int32, (C, C), 0)
    ci = lax.broadcasted_iota(jnp.int32, (C, C), 1)
    causal = ri >= ci
    causal4 = jnp.concatenate([causal] * GLA_HEADS, axis=0)
    tri = jnp.where(causal, 1.0, 0.0).astype(BF16)
    lane_head = lax.broadcasted_iota(jnp.int32, (1, HK), 1) // GLA_DK
    wa = wa_ref[...]
    ba = ba_ref[...]
    ng = ng_ref[...]

    def chunk(c, states):
        rows = pl.ds(pl.multiple_of(c * C, C), C)
        results = [chunk_one(rows, bb, states[bb]) for bb in range(len(states))]
        for bb, (outs, _) in enumerate(results):
            for h, o in enumerate(outs):
                o_ref[bb, rows, h * GLA_DV:(h + 1) * GLA_DV] = o
        return tuple(new_st for _, new_st in results)

    def chunk_one(rows, bb, st):
        q = q_ref[bb, rows, :]
        k = k_ref[bb, rows, :]
        v = v_ref[bb, rows, :]
        r = r_ref[bb, rows, :]
        a = a_ref[bb, rows, :]
        z = _dot(a.astype(BF16), wa) + ba
        g = (jnp.minimum(z, 0.0) - jnp.log1p(jnp.exp(-jnp.abs(z)))) * (1.0 / GLA_TAU)
        b3 = _dot(tri, jnp.concatenate(_split3(g), axis=1))
        b = b3[:, :HK] + b3[:, HK:2 * HK] + b3[:, 2 * HK:]
        b_last = b[C - 1:C, :]
        qt = q * jnp.exp(b) * (GLA_DK ** -0.5)
        kt = (k * jnp.exp(-b)).astype(BF16)
        ks = (k * jnp.exp(b_last - b)).astype(BF16)
        vb = v.astype(BF16)
        q_all = jnp.concatenate([jnp.where(lane_head == h, qt, 0.0) for h in range(GLA_HEADS)],
                                axis=0).astype(BF16)
        att_all = jnp.where(causal4, _dot_nt(q_all, kt), 0.0)
        inter_all = _dot_nt(q_all, st.astype(BF16))
        intra_all = _dot(att_all.astype(BF16), vb)
        upd_all = _dot(vb.T, ks)
        new_st = st * jnp.exp(b_last)
        outs = []
        for h in range(GLA_HEADS):
            new_st = new_st + jnp.where(lane_head == h, upd_all[h * GLA_DV:(h + 1) * GLA_DV, :], 0.0)
            o = inter_all[h * C:(h + 1) * C, :] + intra_all[h * C:(h + 1) * C, h * GLA_DV:(h + 1) * GLA_DV]
            o = o * lax.rsqrt(jnp.mean(o * o, axis=-1, keepdims=True) + 1e-6) * ng
            rh = r[:, h * GLA_DV:(h + 1) * GLA_DV]
            outs.append((o * (rh * jax.nn.sigmoid(rh))).astype(o_ref.dtype))
        return outs, new_st

    nbat = q_ref.shape[0]
    states = lax.fori_loop(0, nchunk, chunk, tuple(st_ref[bb] for bb in range(nbat)))
    for bb in range(nbat):
        st_ref[bb] = states[bb]


def _gla(proj3, wa, ba, ng):
    B, S, _ = proj3.shape
    L = 512 if S % 512 == 0 else S
    nchunk = L // GLA_CHUNK

    def col(name, width):
        off = _COLS[name][0]
        assert off % width == 0
        return pl.BlockSpec((nbat, L, width), lambda b, j, o=off // width: (b, j, o))

    nbat = 2 if B % 2 == 0 else 1
    full2 = lambda shape: pl.BlockSpec(shape, lambda b, j: (0, 0))
    return pl.pallas_call(
        functools.partial(_gla_kernel, nchunk=nchunk),
        grid=(B // nbat, S // L),
        in_specs=[col("gq", 256), col("gk", 256), col("gv", 512), col("gr", 512), col("ga", 128),
                  full2((128, 256)), full2((1, 256)), full2((1, 128))],
        out_specs=pl.BlockSpec((nbat, L, 512), lambda b, j: (b, j, 0)),
        out_shape=jax.ShapeDtypeStruct((B, S, 512), BF16),
        scratch_shapes=[pltpu.VMEM((nbat, GLA_DV, GLA_HEADS * GLA_DK), F32)],
        compiler_params=_cparams("parallel", "arbitrary"),
        name="gla",
    )(proj3, proj3, proj3, proj3, proj3, wa, ba, ng)


def _rot_half(x):
    n = x.shape[-1]
    lane = lax.broadcasted_iota(jnp.int32, (1, n), 1)
    first = (lane % NSA_DH) < (NSA_DH // 2)
    return jnp.where(first, -pltpu.roll(x, n - NSA_DH // 2, 1), pltpu.roll(x, NSA_DH // 2, 1))


def _nsa_prep_kernel(pos_ref, inv_ref, q_ref, kc_ref, ks_ref, vs_ref, kw_ref, vw_ref,
                     qo_ref, kco_ref, kso_ref, vso_ref, kwo_ref, vwo_ref):
    ang = pos_ref[0] * inv_ref[...]
    cos = jnp.cos(ang)
    sin = jnp.sin(ang)
    cos4 = jnp.concatenate([cos] * 4, axis=1)
    sin4 = jnp.concatenate([sin] * 4, axis=1)

    q = q_ref[0]
    qr = q * cos4 + _rot_half(q) * sin4
    for h in range(NSA_HEADS):
        qo_ref[0, h] = (qr[:, h * NSA_DH:(h + 1) * NSA_DH] * (NSA_DH ** -0.5)).astype(qo_ref.dtype)

    def rope128(x):
        return x * cos + _rot_half(x) * sin

    kco_ref[0] = rope128(kc_ref[0])
    ksr = rope128(ks_ref[0])
    kwr = rope128(kw_ref[0])
    vs = vs_ref[0]
    vw = vw_ref[0]
    ones_col = jnp.where(lax.broadcasted_iota(jnp.int32, (vs.shape[0], NSA_DH), 1) == 0, 1.0, 0.0)
    for g in range(NSA_KV_GROUPS):
        sl = slice(g * NSA_DH, (g + 1) * NSA_DH)
        kso_ref[0, g] = ksr[:, sl].astype(kso_ref.dtype)
        kwo_ref[0, g] = kwr[:, sl].astype(kwo_ref.dtype)
        vso_ref[0, g] = jnp.concatenate([vs[:, sl], ones_col], axis=1).astype(vso_ref.dtype)
        vwo_ref[0, g] = jnp.concatenate([vw[:, sl], ones_col], axis=1).astype(vwo_ref.dtype)


def _nsa_prep(proj3, pos3, inv_row):
    B, S, _ = proj3.shape
    ts = 512 if S % 512 == 0 else S

    def col(name, width):
        off = _COLS[name][0]
        assert off % width == 0
        return pl.BlockSpec((1, ts, width), lambda b, j, o=off // width: (b, j, o))

    kv_spec = pl.BlockSpec((1, NSA_KV_GROUPS, ts, NSA_DH), lambda b, j: (b, 0, j, 0))
    kv_shape = jax.ShapeDtypeStruct((B, NSA_KV_GROUPS, S, NSA_DH), BF16)
    vx_spec = pl.BlockSpec((1, NSA_KV_GROUPS, ts, 2 * NSA_DH), lambda b, j: (b, 0, j, 0))
    vx_shape = jax.ShapeDtypeStruct((B, NSA_KV_GROUPS, S, 2 * NSA_DH), BF16)
    return pl.pallas_call(
        _nsa_prep_kernel,
        grid=(B, S // ts),
        in_specs=[pl.BlockSpec((1, ts, 1), lambda b, j: (b, j, 0)),
                  pl.BlockSpec((1, 128), lambda b, j: (0, 0)),
                  col("nq", 512), col("nkc", 128), col("nks", 128), col("nvs", 128),
                  col("nkw", 128), col("nvw", 128)],
        out_specs=[pl.BlockSpec((1, NSA_HEADS, ts, NSA_DH), lambda b, j: (b, 0, j, 0)),
                   pl.BlockSpec((1, ts, 128), lambda b, j: (b, j, 0)),
                   kv_spec, vx_spec, kv_spec, vx_spec],
        out_shape=[jax.ShapeDtypeStruct((B, NSA_HEADS, S, NSA_DH), BF16),
                   jax.ShapeDtypeStruct((B, S, 128), F32),
                   kv_shape, vx_shape, kv_shape, vx_shape],
        compiler_params=_cparams("parallel", "parallel"),
        name="nsa_prep",
    )(pos3, inv_row, proj3, proj3, proj3, proj3, proj3, proj3)


def _compress_kernel(k_ref, v_ref, pk_ref, w1k_ref, b1k_ref, w2k_ref, pv_ref, w1v_ref, b1v_ref, w2v_ref,
                     ko_ref, vo_ref):
    nr = k_ref.shape[1] // CMP_STRIDE
    hidden = w2k_ref.shape[0]

    def run(x_ref, p_ref, w1_ref, b1_ref, w2_ref, o_ref):
        first = jnp.zeros((nr, NSA_KV_GROUPS * hidden), F32)
        second = jnp.zeros((nr, NSA_KV_GROUPS * hidden), F32)
        for m in range(CMP_STRIDE):
            xm = x_ref[0, pl.ds(m, nr, stride=CMP_STRIDE), :]
            first = first + _dot((xm + p_ref[m:m + 1, :]).astype(BF16), w1_ref[m])
            second = second + _dot((xm + p_ref[CMP_STRIDE + m:CMP_STRIDE + m + 1, :]).astype(BF16),
                                   w1_ref[CMP_STRIDE + m])
        pre = first + pltpu.roll(second, nr - 1, 0) + b1_ref[...]
        hid = jax.nn.gelu(pre)
        for g in range(NSA_KV_GROUPS):
            out = _dot(hid[:, g * hidden:(g + 1) * hidden].astype(BF16), w2_ref[...])
            row = lax.broadcasted_iota(jnp.int32, out.shape, 0)
            o_ref[0, g] = jnp.where(row < nr - 1, out, 0.0)

    run(k_ref, pk_ref, w1k_ref, b1k_ref, w2k_ref, ko_ref)
    run(v_ref, pv_ref, w1v_ref, b1v_ref, w2v_ref, vo_ref)


def _compress_params(pos, w1, b1, w2):
    L, dh = pos.shape
    hidden = w1.shape[1]
    w = w1.reshape(L, dh, hidden).astype(BF16)
    z = jnp.zeros_like(w)
    w_bd = jnp.concatenate([jnp.concatenate([w, z], axis=2), jnp.concatenate([z, w], axis=2)], axis=1)
    return (jnp.tile(pos, (1, NSA_KV_GROUPS)), w_bd, jnp.tile(b1.reshape(1, -1), (1, NSA_KV_GROUPS)),
            w2.astype(BF16))


def _compress(kc_r, proj3, k_params, v_params):
    B, S, W = kc_r.shape
    NR = S // CMP_STRIDE
    full = lambda a: pl.BlockSpec(a.shape, lambda b: (0,) * a.ndim)
    o_spec = pl.BlockSpec((1, NSA_KV_GROUPS, NR, NSA_DH), lambda b: (b, 0, 0, 0))
    o_shape = jax.ShapeDtypeStruct((B, NSA_KV_GROUPS, NR, NSA_DH), F32)
    params = tuple(k_params) + tuple(v_params)
    return pl.pallas_call(
        _compress_kernel,
        grid=(B,),
        in_specs=[pl.BlockSpec((1, S, W), lambda b: (b, 0, 0)),
                  pl.BlockSpec((1, S, W), lambda b, o=_COLS["nvc"][0] // W: (b, 0, o))]
                 + [full(a) for a in params],
        out_specs=[o_spec, o_spec],
        out_shape=[o_shape, o_shape],
        compiler_params=_cparams("parallel"),
        name="compress",
    )(kc_r, proj3, *params)


def _nsa_attn_kernel(q_ref, kc_ref, vc_ref, ks_ref, vs_ref, kw_ref, vw_ref, gate_ref, ovt_ref, gsel_ref, o_ref,
                     *, tq, S, n_sel, n_top):
    qi = pl.program_id(2)
    start = qi * tq
    H = NSA_HPG
    qs = q_ref[0].reshape(H * tq, NSA_DH)
    t_col = start + lax.broadcasted_iota(jnp.int32, (tq, 1), 0)
    t_row = start + lax.broadcasted_iota(jnp.int32, (1, tq), 1)

    kc = kc_ref[0, 0].astype(BF16)
    vc = vc_ref[0, 0].astype(BF16)
    NR = kc.shape[0]
    s = _dot_nt(qs, kc)
    ncol = lax.broadcasted_iota(jnp.int32, (1, NR), 1)
    cmask = (ncol < NR - 1) & (ncol * CMP_STRIDE + (CMP_LEN - 1) <= t_col)
    s3 = jnp.where(cmask[None], s.reshape(H, tq, NR), NEG)
    m = jnp.max(s3, axis=-1, keepdims=True)
    e = jnp.exp(s3 - m)
    p3 = e / jnp.sum(e, axis=-1, keepdims=True)
    p3 = jnp.where(cmask[None], p3, 0.0)
    o_cmp = _dot(p3.reshape(H * tq, NR).astype(BF16), vc)

    psum = jnp.sum(p3, axis=0)
    p_hi = psum.astype(BF16)
    p_lo = (psum - p_hi.astype(F32)).astype(BF16)
    ovt = ovt_ref[...]
    imp = _dot_nt(ovt, p_hi) + _dot_nt(ovt, p_lo)
    blk = lax.broadcasted_iota(jnp.int32, (n_sel, 1), 0)
    cur = t_row // SEL_LEN
    causal_blk = blk <= cur
    forced = (blk == 0) | (blk == cur) | (blk == cur - 1)
    val = jnp.where(causal_blk, jnp.where(forced, jnp.inf, imp), -jnp.inf)
    rank = jnp.zeros((n_sel, tq), jnp.int32)
    for j in range(n_sel):
        vj = val[j:j + 1, :]
        beats = (vj > val) | ((vj == val) & (blk > j))
        rank = rank + beats.astype(jnp.int32)
    sel_t = jnp.where((rank < n_top) & causal_blk, 1.0, 0.0)
    sel_t = jnp.concatenate([sel_t, jnp.zeros((128 - n_sel, tq), F32)], axis=0)
    sel = sel_t.T.astype(BF16)

    tk = 256
    nsub = 2 if S % (2 * tk) == 0 else 1
    bpc = tk // SEL_LEN
    n_trips = (start + tq + nsub * tk - 1) // (nsub * tk)
    erow = lax.broadcasted_iota(jnp.int32, (128, tk), 0)
    ecol = lax.broadcasted_iota(jnp.int32, (128, tk), 1) // SEL_LEN
    kk = lax.broadcasted_iota(jnp.int32, (1, tk), 1)

    def sel_trip(c, carry):
        m_i, acc = carry
        scores, vals = [], []
        m_new = m_i
        for u in range(nsub):
            cu = c * nsub + u
            k0 = pl.multiple_of(cu * tk, tk)
            kb = ks_ref[0, 0, pl.ds(k0, tk), :]
            vals.append(vs_ref[0, 0, pl.ds(k0, tk), :])
            expand = jnp.where(erow == ecol + cu * bpc, 1.0, 0.0).astype(BF16)
            allowed = (_dot(sel, expand) > 0.5) & (k0 + kk <= t_col)
            bias = jnp.where(allowed, 0.0, NEG)
            sc3 = _dot_nt(qs, kb).reshape(H, tq, tk) + bias[None]
            scores.append(sc3)
            m_new = jnp.maximum(m_new, jnp.max(sc3, axis=-1, keepdims=True))
        acc = jnp.exp(m_i - m_new).reshape(H * tq, 1) * acc
        for u in range(nsub):
            pe = jnp.exp((scores[u] - m_new).astype(BF16))
            acc = acc + _dot(pe.reshape(H * tq, tk), vals[u])
        return m_new, acc

    m0 = jnp.full((H, tq, 1), NEG, F32)
    a0 = jnp.zeros((H * tq, 2 * NSA_DH), F32)
    _, acc_f = lax.fori_loop(0, n_trips, sel_trip, (m0, a0))
    o_sel = acc_f[:, :NSA_DH] / acc_f[:, NSA_DH:NSA_DH + 1]

    span = min(WINDOW + tq, S)
    ws = jnp.clip(start - WINDOW, 0, S - span)
    ws = pl.multiple_of(ws, tq)
    kwb = kw_ref[0, 0, pl.ds(ws, span), :]
    vwb = vw_ref[0, 0, pl.ds(ws, span), :]
    kp = ws + lax.broadcasted_iota(jnp.int32, (1, span), 1)
    wbias = jnp.where((kp <= t_col) & (kp > t_col - WINDOW), 0.0, NEG)
    sw3 = _dot_nt(qs, kwb).reshape(H, tq, span) + wbias[None]
    mw = jnp.max(sw3, axis=-1, keepdims=True)
    ew = jnp.exp((sw3 - mw).astype(BF16))
    aw = _dot(ew.reshape(H * tq, span), vwb)
    o_win = aw[:, :NSA_DH] / aw[:, NSA_DH:NSA_DH + 1]

    gs = jax.nn.sigmoid(gate_ref[0])
    gs_hi = gs.astype(BF16)
    gs_lo = (gs - gs_hi.astype(F32)).astype(BF16)
    spread = gsel_ref[0]
    gw = _dot(gs_hi, spread) + _dot(gs_lo, spread)
    for h in range(H):
        rs = slice(h * tq, (h + 1) * tq)
        gate = [gw[:, (h * 3 + br) * 128:(h * 3 + br) * 128 + NSA_DH] for br in range(3)]
        o = gate[0] * o_cmp[rs] + gate[1] * o_sel[rs] + gate[2] * o_win[rs]
        o_ref[0, :, h * NSA_DH:(h + 1) * NSA_DH] = o.astype(o_ref.dtype)


def _nsa_attn(q_r, kcmp, vcmp, ks_r, vs_r, kw_r, vw_r, proj3, ov):
    B, _, S, _ = q_r.shape
    G, H = NSA_KV_GROUPS, NSA_HPG
    NR = kcmp.shape[2]
    tq = 256
    n_sel = S // SEL_LEN
    n_top = min(SEL_TOPK, n_sel)
    cmp_spec = pl.BlockSpec((1, 1, NR, NSA_DH), lambda b, g, i: (b, g, 0, 0))
    kv_spec = pl.BlockSpec((1, 1, S, NSA_DH), lambda b, g, i: (b, g, 0, 0))
    vx_spec = pl.BlockSpec((1, 1, S, 2 * NSA_DH), lambda b, g, i: (b, g, 0, 0))
    goff = _COLS["ng"][0] // 128
    sel_np = np.zeros((G, 128, H * 3 * 128), np.float32)
    for gg in range(G):
        for hb in range(H * 3):
            sel_np[gg, gg * H * 3 + hb, hb * 128:(hb + 1) * 128] = 1.0
    gsel = jnp.asarray(sel_np, BF16)
    return pl.pallas_call(
        functools.partial(_nsa_attn_kernel, tq=tq, S=S, n_sel=n_sel, n_top=n_top),
        grid=(B, G, S // tq),
        in_specs=[pl.BlockSpec((1, H, tq, NSA_DH), lambda b, g, i: (b, g, i, 0)),
                  cmp_spec, cmp_spec, kv_spec, vx_spec, kv_spec, vx_spec,
                  pl.BlockSpec((1, tq, 128), lambda b, g, i: (b, i, goff)),
                  pl.BlockSpec(ov.shape, lambda b, g, i: (0, 0)),
                  pl.BlockSpec((1,) + gsel.shape[1:], lambda b, g, i: (g, 0, 0))],
        out_specs=pl.BlockSpec((1, tq, H * NSA_DH), lambda b, g, i: (b, i, g)),
        out_shape=jax.ShapeDtypeStruct((B, S, NSA_HEADS * NSA_DH), BF16),
        compiler_params=_cparams("parallel", "parallel", "arbitrary"),
        name="nsa_attn",
    )(q_r, kcmp, vcmp, ks_r, vs_r, kw_r, vw_r, proj3, ov, gsel)


def _layer_norm(x, g, b):
    mu = jnp.mean(x, axis=-1, keepdims=True)
    xc = x - mu
    var = jnp.mean(xc * xc, axis=-1, keepdims=True)
    return xc * lax.rsqrt(var + LN_EPS) * g + b


SUB = D_MODEL // 2 // 128


def _pack_bf16_pairs(v):
    half = v.shape[1] // 2
    bits = pltpu.bitcast(v.astype(BF16).astype(F32), jnp.uint32)
    return pltpu.bitcast((bits[:, :half] >> 16) | bits[:, half:], jnp.int32)


def _unpack_bf16_pairs(p):
    u = pltpu.bitcast(p, jnp.uint32)
    return pltpu.bitcast(u << 16, F32), pltpu.bitcast(u & jnp.uint32(0xFFFF0000), F32)


def _to_tiles(ref, val, lead=()):
    for c in range(SUB):
        ref[lead + (c,)] = val[:, c * 128:(c + 1) * 128]


def _from_tiles(ref, lead=()):
    return jnp.concatenate([ref[lead + (c,)] for c in range(SUB)], axis=1)


def _outproj_kernel(x_ref, og_ref, on_ref, w_ref, g_ref, b_ref, o_ref, ot_ref):
    half = og_ref.shape[1]
    mix = _dot(og_ref[...], w_ref[0:half, :]) + _dot(on_ref[...], w_ref[half:, :])
    h = _layer_norm(DN_ALPHA * x_ref[...] + mix, g_ref[...], b_ref[...])
    o_ref[...] = h
    _to_tiles(ot_ref, _pack_bf16_pairs(h))


def _outproj(x2, row0, og2, on2, w, g, b):
    T = og2.shape[0]
    D = x2.shape[1]
    tm = 512
    off = row0 // tm
    row = lambda width: pl.BlockSpec((tm, width), lambda i: (i, 0))
    full = lambda a: pl.BlockSpec(a.shape, lambda i: (0, 0))
    return pl.pallas_call(
        _outproj_kernel,
        grid=(T // tm,),
        in_specs=[pl.BlockSpec((tm, D), lambda i: (i + off, 0)), row(og2.shape[1]), row(on2.shape[1]),
                  full(w), full(g), full(b)],
        out_specs=[row(D), pl.BlockSpec((SUB, tm, 128), lambda i: (0, i, 0))],
        out_shape=[jax.ShapeDtypeStruct((T, D), F32),
                   jax.ShapeDtypeStruct((SUB, T, 128), jnp.int32)],
        compiler_params=_cparams("parallel"),
        name="outproj_ln",
    )(x2, og2, on2, w, g, b)


def _router_kernel(h_ref, wh_ref, wl_ref, bias_ref, eidx_ref, wts_ref, rank_ref, cnt_ref, carry_ref):
    @pl.when(pl.program_id(0) == 0)
    def _():
        carry_ref[...] = jnp.zeros_like(carry_ref)

    h = h_ref[...]
    tm = h.shape[0]
    E = N_EXPERTS
    GS = E // N_GROUPS
    h_hi = h.astype(BF16)
    h_lo = (h - h_hi.astype(F32)).astype(BF16)
    wh = wh_ref[...]
    logits = _dot_nt(wh, h_hi) + _dot_nt(wh, h_lo) + _dot_nt(wl_ref[...], h_hi)
    scores = jax.nn.sigmoid(logits)
    biased = scores + bias_ref[...]
    eid = lax.broadcasted_iota(jnp.int32, (E, tm), 0).astype(F32)
    ninf = -jnp.inf

    def col_max(x):
        return jnp.max(x, axis=0, keepdims=True)

    def first_idx(x, mx, ids):
        return jnp.min(jnp.where(x == mx, ids, float(E)), axis=0, keepdims=True)

    ids0 = lax.broadcasted_iota(jnp.int32, (GS, tm), 0).astype(F32)
    gscore = []
    for gi in range(N_GROUPS):
        mg = biased[gi * GS:(gi + 1) * GS, :]
        ids = ids0 + float(gi * GS)
        m1 = col_max(mg)
        i1 = first_idx(mg, m1, ids)
        m2 = col_max(jnp.where(ids == i1, ninf, mg))
        gscore.append(m1 + m2)
    parts = []
    for gi in range(N_GROUPS):
        rk = jnp.zeros((1, tm), jnp.int32)
        for gj in range(N_GROUPS):
            if gj == gi:
                continue
            beats = (gscore[gj] > gscore[gi]) | ((gscore[gj] == gscore[gi]) & (gj < gi))
            rk = rk + beats.astype(jnp.int32)
        parts.append(jnp.where(rk < TOPK_GROUPS, biased[gi * GS:(gi + 1) * GS, :], ninf))
    masked = jnp.concatenate(parts, axis=0)

    onehots, wsel = [], []
    selm = jnp.zeros((E, tm), F32)
    for k in range(TOP_K):
        mx = col_max(masked)
        idx = first_idx(masked, mx, eid)
        oh = eid == idx
        onehots.append(oh)
        wsel.append(jnp.sum(jnp.where(oh, scores, 0.0), axis=0, keepdims=True))
        masked = jnp.where(oh, ninf, masked)
        selm = jnp.where(oh, 1.0, selm)
        eidx_ref[k:k + 1, :] = idx.astype(jnp.int32)
    wsum = wsel[0]
    for k in range(1, TOP_K):
        wsum = wsum + wsel[k]
    for k in range(TOP_K):
        wts_ref[k:k + 1, :] = wsel[k] / wsum * ROUTED_SCALE

    ri = lax.broadcasted_iota(jnp.int32, (tm, tm), 0)
    ci = lax.broadcasted_iota(jnp.int32, (tm, tm), 1)
    before = jnp.where(ri < ci, 1.0, 0.0).astype(BF16)
    cum = _dot(selm.astype(BF16), before) + carry_ref[...]
    for k in range(TOP_K):
        rk = jnp.sum(jnp.where(onehots[k], cum, 0.0), axis=0, keepdims=True)
        rank_ref[k:k + 1, :] = rk.astype(jnp.int32)
    total = carry_ref[...] + jnp.sum(selm, axis=1, keepdims=True)
    carry_ref[...] = total
    cnt_ref[...] = total


def _router(h2, w_hi_t, w_lo_t, bias_col):
    T, D = h2.shape
    tm = 256
    full = lambda a: pl.BlockSpec(a.shape, lambda i: (0, 0))
    o8 = pl.BlockSpec((TOP_K, tm), lambda i: (0, i))
    return pl.pallas_call(
        _router_kernel,
        grid=(T // tm,),
        in_specs=[pl.BlockSpec((tm, D), lambda i: (i, 0)), full(w_hi_t), full(w_lo_t), full(bias_col)],
        out_specs=[o8, o8, o8, pl.BlockSpec((N_EXPERTS, 1), lambda i: (0, 0))],
        out_shape=[jax.ShapeDtypeStruct((TOP_K, T), jnp.int32),
                   jax.ShapeDtypeStruct((TOP_K, T), F32),
                   jax.ShapeDtypeStruct((TOP_K, T), jnp.int32),
                   jax.ShapeDtypeStruct((N_EXPERTS, 1), F32)],
        scratch_shapes=[pltpu.VMEM((N_EXPERTS, 1), F32)],
        compiler_params=_cparams("arbitrary"),
        name="router",
    )(h2, w_hi_t, w_lo_t, bias_col)


def _dest_kernel(eidx_ref, rank_ref, ps_ref, dest_ref):
    tm = eidx_ref.shape[1]
    eid = lax.broadcasted_iota(jnp.int32, (N_EXPERTS, tm), 0)
    ps = ps_ref[...]
    for k in range(TOP_K):
        start = jnp.sum(jnp.where(eid == eidx_ref[k:k + 1, :], ps, 0.0), axis=0, keepdims=True)
        dest_ref[k:k + 1, :] = start.astype(jnp.int32) + rank_ref[k:k + 1, :]


def _dest(eidx_t, rank_t, pad_start_col):
    T = eidx_t.shape[1]
    tm = 1024 if T % 1024 == 0 else T
    o8 = pl.BlockSpec((TOP_K, tm), lambda i: (0, i))
    return pl.pallas_call(
        _dest_kernel,
        grid=(T // tm,),
        in_specs=[o8, o8, pl.BlockSpec((N_EXPERTS, 1), lambda i: (0, 0))],
        out_specs=o8,
        out_shape=jax.ShapeDtypeStruct((TOP_K, T), jnp.int32),
        compiler_params=_cparams("parallel"),
        name="dest",
    )(eidx_t, rank_t, pad_start_col)


SC_WINDOW = 128


def _sc_gather(table, idx):
    _, lanes = table.shape
    n = idx.shape[0]
    mesh = plsc.VectorSubcoreMesh(core_axis_name="core", subcore_axis_name="subcore")

    @functools.partial(pl.kernel, out_type=jax.ShapeDtypeStruct((n, lanes), table.dtype), mesh=mesh,
                       name="sc_row_gather")
    def gather(x_hbm, i_hbm, o_hbm):
        def body(i_vmem, o_vmem):
            pltpu.sync_copy(x_hbm.at[i_vmem.at[0]], o_vmem)

        pltpu.emit_pipeline(
            body,
            grid=(n // SC_WINDOW,),
            in_specs=[pl.BlockSpec((1, SC_WINDOW), lambda i: (0, i))],
            out_specs=[pl.BlockSpec((SC_WINDOW, lanes), lambda i: (i, 0))],
            core_axis_name=("core", "subcore"),
            dimension_semantics=(pltpu.PARALLEL,),
            trace_scopes=False,
        )(i_hbm, o_hbm)

    return gather(table, idx.reshape(1, n))


def _sc_scatter(src, idx, n_out):
    rows, lanes = src.shape
    n_idx = idx.shape[0]
    mesh = plsc.VectorSubcoreMesh(core_axis_name="core", subcore_axis_name="subcore")

    @functools.partial(pl.kernel, out_type=jax.ShapeDtypeStruct((n_out, lanes), src.dtype), mesh=mesh,
                       name="sc_row_scatter")
    def scatter(x_hbm, i_hbm, o_hbm):
        def body(x_vmem, *i_vmems):
            for i_vmem in i_vmems:
                pltpu.sync_copy(x_vmem, o_hbm.at[i_vmem.at[0]])

        pltpu.emit_pipeline(
            body,
            grid=(rows // SC_WINDOW,),
            in_specs=[pl.BlockSpec((SC_WINDOW, lanes), lambda i: (i, 0))]
                     + [pl.BlockSpec((1, SC_WINDOW), lambda i, j=j: (j, i)) for j in range(n_idx)],
            out_specs=[],
            core_axis_name=("core", "subcore"),
            dimension_semantics=(pltpu.PARALLEL,),
            trace_scopes=False,
        )(x_hbm, *([i_hbm] * n_idx))

    return scatter(src, idx)


def _expert_kernel(bexp_ref, nused_ref, x_ref, wg_ref, wu_ref, wd_ref, y_ref, wg_b, wu_b, wd_b):
    i = pl.program_id(0)

    @pl.when(i < nused_ref[0])
    def _():
        @pl.when((i == 0) | (bexp_ref[i] != bexp_ref[jnp.maximum(i - 1, 0)]))
        def _():
            wg_b[...] = wg_ref[0].astype(BF16)
            wu_b[...] = wu_ref[0].astype(BF16)
            wd_b[...] = wd_ref[0].astype(BF16)

        x = jnp.concatenate(_unpack_bf16_pairs(_from_tiles(x_ref)), axis=1).astype(BF16)
        gate = _dot(x, wg_b[...])
        up = _dot(x, wu_b[...])
        act = (gate * jax.nn.sigmoid(gate) * up).astype(BF16)
        _to_tiles(y_ref, _pack_bf16_pairs(_dot(act, wd_b[...])))


def _experts(blk_exp, n_used, xs_t, wg, wu, wd):
    NP = xs_t.shape[1]
    D = D_MODEL
    nb = NP // MOE_BLOCK
    blk = (SUB, MOE_BLOCK, 128)

    def xmap(i, bexp, nused):
        return (0, jnp.minimum(i, nused[0] - 1), 0)

    def wmap(i, bexp, nused):
        return (bexp[jnp.minimum(i, nused[0] - 1)], 0, 0)

    grid_spec = pltpu.PrefetchScalarGridSpec(
        num_scalar_prefetch=2,
        grid=(nb,),
        in_specs=[pl.BlockSpec(blk, xmap),
                  pl.BlockSpec((1, D, D_EXPERT), wmap),
                  pl.BlockSpec((1, D, D_EXPERT), wmap),
                  pl.BlockSpec((1, D_EXPERT, D), wmap)],
        out_specs=pl.BlockSpec(blk, xmap),
        scratch_shapes=[pltpu.VMEM((D, D_EXPERT), BF16), pltpu.VMEM((D, D_EXPERT), BF16),
                        pltpu.VMEM((D_EXPERT, D), BF16)],
    )
    return pl.pallas_call(
        _expert_kernel,
        grid_spec=grid_spec,
        out_shape=jax.ShapeDtypeStruct(xs_t.shape, jnp.int32),
        compiler_params=_cparams("arbitrary"),
        name="experts",
    )(blk_exp, n_used, xs_t, wg, wu, wd)


def _combine_kernel(h_ref, wts_ref, yg_ref, wsg_ref, wsu_ref, wsd_ref, g_ref, b_ref, *rest):
    o_ref = rest[-1]
    h = h_ref[...]
    hb = h.astype(BF16)
    gate = _dot(hb, wsg_ref[...])
    up = _dot(hb, wsu_ref[...])
    shared = _dot((gate * jax.nn.sigmoid(gate) * up).astype(BF16), wsd_ref[...])
    wts = wts_ref[...]
    lo, hi = _unpack_bf16_pairs(_from_tiles(yg_ref, (0,)))
    r_lo, r_hi = wts[:, 0:1] * lo, wts[:, 0:1] * hi
    for k in range(1, TOP_K):
        lo, hi = _unpack_bf16_pairs(_from_tiles(yg_ref, (k,)))
        r_lo, r_hi = r_lo + wts[:, k:k + 1] * lo, r_hi + wts[:, k:k + 1] * hi
    routed = jnp.concatenate([r_lo, r_hi], axis=1)
    o_ref[...] = _layer_norm(DN_ALPHA * h + (routed + shared), g_ref[...], b_ref[...])


def _combine(h2, wts, yg_t, wsg, wsu, wsd, g, b, out_rows, row0, out_prev):
    T, D = h2.shape
    tm = 256
    off = row0 // tm
    full = lambda a: pl.BlockSpec(a.shape, lambda i: (0, 0))
    in_specs = [pl.BlockSpec((tm, D), lambda i: (i, 0)),
                pl.BlockSpec((tm, TOP_K), lambda i: (i, 0)),
                pl.BlockSpec((TOP_K, SUB, tm, 128), lambda i: (0, 0, i, 0)),
                full(wsg), full(wsu), full(wsd), full(g), full(b)]
    args = [h2, wts, yg_t, wsg, wsu, wsd, g, b]
    aliases = {}
    if out_prev is not None:
        in_specs.append(pl.BlockSpec(memory_space=pl.ANY))
        args.append(out_prev)
        aliases = {len(args) - 1: 0}
    return pl.pallas_call(
        _combine_kernel,
        grid=(T // tm,),
        in_specs=in_specs,
        out_specs=pl.BlockSpec((tm, D), lambda i: (i + off, 0)),
        out_shape=jax.ShapeDtypeStruct((out_rows, D), F32),
        input_output_aliases=aliases,
        compiler_params=_cparams("parallel"),
        name="combine_ln",
    )(*args)


def _regroup_w_in(w_in):
    parts, off = {}, 0
    for name, width in _SPLITS:
        parts[name] = w_in[:, off:off + width]
        off += width
    cols = []
    for name, (_, width) in _COLS.items():
        p = parts[name]
        if p.shape[1] < width:
            p = jnp.pad(p, ((0, 0), (0, width - p.shape[1])))
        cols.append(p)
    return jnp.concatenate(cols, axis=1).astype(BF16)


def _overlap_matrix(S):
    nr = S // CMP_STRIDE
    n_sel = S // SEL_LEN
    ci = np.arange(nr)[:, None] * CMP_STRIDE
    sj = np.arange(n_sel)[None, :] * SEL_LEN
    ov = np.clip(np.minimum(ci + CMP_LEN, sj + SEL_LEN) - np.maximum(ci, sj), 0, None) / CMP_LEN
    ov[nr - 1] = 0.0
    return jnp.asarray(ov.T, BF16)


def _mixers(x2, row0, positions, w_in, w_alpha2, b_alpha, gla_norm_g,
            cmp_pos_k, cmp_w1_k, cmp_b1_k, cmp_w2_k, cmp_pos_v, cmp_w1_v, cmp_b1_v, cmp_w2_v):
    B, S = positions.shape
    proj = _proj(x2, row0, B * S, _regroup_w_in(w_in)).reshape(B, S, D_PROJ)

    wa = jnp.pad(w_alpha2, ((0, 128 - GLA_LOWRANK), (0, 0))).astype(BF16)
    o_gla = _gla(proj, wa, b_alpha.reshape(1, -1), gla_norm_g.reshape(1, -1))

    half = NSA_DH // 2
    inv = ROPE_THETA ** (-np.arange(half, dtype=np.float32) / half)
    inv_row = jnp.asarray(np.tile(inv, 128 // half).reshape(1, 128), F32)
    pos3 = positions.astype(F32).reshape(B, S, 1)
    q_r, kc_r, ks_r, vs_r, kw_r, vw_r = _nsa_prep(proj, pos3, inv_row)

    kcmp, vcmp = _compress(kc_r, proj,
                           _compress_params(cmp_pos_k, cmp_w1_k, cmp_b1_k, cmp_w2_k),
                           _compress_params(cmp_pos_v, cmp_w1_v, cmp_b1_v, cmp_w2_v))
    o_nsa = _nsa_attn(q_r, kcmp, vcmp, ks_r, vs_r, kw_r, vw_r, proj, _overlap_matrix(S))
    return o_gla, o_nsa


def _moe_ln(h2, h_t, w_router, router_bias, w_gate, w_up, w_down, ws_gate, ws_up, ws_down, ln_g, ln_b,
            out_rows, row0, out_prev):
    T, D = h2.shape
    P = T * TOP_K
    w_hi = w_router.astype(BF16)
    w_lo = (w_router - w_hi.astype(F32)).astype(BF16)
    eidx_t, wts_t, rank_t, counts = _router(h2, w_hi.T, w_lo.T, router_bias.reshape(-1, 1))

    counts = counts.reshape(-1).astype(jnp.int32)
    padded = (counts + MOE_BLOCK - 1) // MOE_BLOCK * MOE_BLOCK
    pad_end = jnp.cumsum(padded)
    pad_start = pad_end - padded
    nb = -(-P // MOE_BLOCK) + N_EXPERTS
    n_used = (pad_end[-1] // MOE_BLOCK).astype(jnp.int32).reshape(1)
    blk_start = jnp.arange(nb, dtype=jnp.int32) * MOE_BLOCK
    blk_exp = jnp.minimum(jnp.sum((pad_end[None, :] <= blk_start[:, None]).astype(jnp.int32), axis=1),
                          N_EXPERTS - 1)

    NP = nb * MOE_BLOCK
    dest_t = _dest(eidx_t, rank_t, pad_start.astype(F32).reshape(-1, 1))
    col = jnp.arange(SUB, dtype=jnp.int32) * NP

    n_pad = NP - P
    assert n_pad % T == 0
    pad_cnt = padded - counts
    pad_hi = jnp.cumsum(pad_cnt)
    pad_lo = pad_hi - pad_cnt
    j = jnp.arange(n_pad, dtype=jnp.int32)[:, None]
    owner = (pad_lo[None, :] <= j) & (j < pad_hi[None, :])
    in_expert = jnp.sum(jnp.where(owner, (pad_start + counts - pad_lo)[None, :] + j, 0), axis=1)
    pad_rows = jnp.where(j[:, 0] < pad_hi[-1], in_expert, pad_end[-1] + j[:, 0] - pad_hi[-1])

    rows_all = jnp.concatenate([dest_t, pad_rows.reshape(n_pad // T, T)], axis=0)
    dst = (rows_all[:, None, :] + col[None, :, None]).reshape(-1, SUB * T)
    xs_t = _sc_scatter(h_t.reshape(SUB * T, 128), dst, SUB * NP).reshape(SUB, NP, 128)
    ys_t = _experts(blk_exp, n_used, xs_t, w_gate, w_up, w_down)
    src = (dest_t[:, None, :] + col[None, :, None]).reshape(-1)
    yg_t = _sc_gather(ys_t.reshape(SUB * NP, 128), src).reshape(TOP_K, SUB, T, 128)
    return _combine(h2, wts_t.T, yg_t, ws_gate.astype(BF16), ws_up.astype(BF16), ws_down.astype(BF16),
                    ln_g.reshape(1, -1), ln_b.reshape(1, -1), out_rows, row0, out_prev)


def kernel(x, positions, w_in, w_alpha2, b_alpha, gla_norm_g, cmp_pos_k, cmp_w1_k, cmp_b1_k, cmp_w2_k, cmp_pos_v, cmp_w1_v, cmp_b1_v, cmp_w2_v, w_out, ln1_g, ln1_b, w_router, router_bias, w_exp_gate, w_exp_up, w_exp_down, w_sh_gate, w_sh_up, w_sh_down, ln2_g, ln2_b):
    B, S, D = x.shape
    n_groups = BATCH_GROUPS if B % BATCH_GROUPS == 0 else 1
    bg = B // n_groups
    rows = bg * S
    h2d = x.reshape(B * S, D)
    for l in range(w_in.shape[0]):
        out = None
        for gi in range(n_groups):
            row0 = gi * rows
            o_gla, o_nsa = _mixers(h2d, row0, positions[gi * bg:(gi + 1) * bg], w_in[l], w_alpha2[l], b_alpha[l],
                                   gla_norm_g[l], cmp_pos_k[l], cmp_w1_k[l], cmp_b1_k[l], cmp_w2_k[l],
                                   cmp_pos_v[l], cmp_w1_v[l], cmp_b1_v[l], cmp_w2_v[l])
            h1, h1_t = _outproj(h2d, row0, o_gla.reshape(rows, -1), o_nsa.reshape(rows, -1),
                                w_out[l].astype(BF16), ln1_g[l].reshape(1, -1), ln1_b[l].reshape(1, -1))
            out = _moe_ln(h1, h1_t, w_router[l], router_bias[l], w_exp_gate[l], w_exp_up[l], w_exp_down[l],
                          w_sh_gate[l], w_sh_up[l], w_sh_down[l], ln2_g[l], ln2_b[l], B * S, row0, out)
        h2d = out
    return h2d.reshape(B, S, D)
```

```python
import functools

import numpy as np
import jax
import jax.numpy as jnp
from jax import lax
from jax.experimental import pallas as pl
from jax.experimental.pallas import tpu as pltpu
from jax.experimental.pallas import tpu_sc as plsc

D_MODEL = 1024
GLA_HEADS = 4
GLA_DV = 128
GLA_DK = 64
GLA_LOWRANK = 16
GLA_TAU = 16.0
GLA_CHUNK = 64
NSA_HEADS = 8
NSA_KV_GROUPS = 2
NSA_HPG = 4
NSA_DH = 64
CMP_LEN = 32
CMP_STRIDE = 16
CMP_HIDDEN = 256
SEL_LEN = 64
SEL_TOPK = 16
WINDOW = 512
ROPE_THETA = 10000.0
N_EXPERTS = 256
TOP_K = 8
N_GROUPS = 8
TOPK_GROUPS = 4
D_EXPERT = 256
ROUTED_SCALE = 2.5
DEPTH = 1
DN_ALPHA = (2.0 * DEPTH) ** 0.25
LN_EPS = 1e-5

MOE_BLOCK = 512
BATCH_GROUPS = 2
NEG = -1e30
F32 = jnp.float32
BF16 = jnp.bfloat16

_COLS = {}
_off = 0
for _name, _w in (("gq", 256), ("gk", 256), ("gv", 512), ("gr", 512), ("nq", 512),
                  ("nkc", 128), ("nvc", 128), ("nks", 128), ("nvs", 128), ("nkw", 128), ("nvw", 128),
                  ("ga", 128), ("ng", 128)):
    _COLS[_name] = (_off, _w)
    _off += _w
D_PROJ = _off
_SPLITS = (("gq", 256), ("gk", 256), ("gv", 512), ("ga", 16), ("gr", 512), ("nq", 512),
           ("nkc", 128), ("nvc", 128), ("nks", 128), ("nvs", 128), ("nkw", 128), ("nvw", 128), ("ng", 24))

VMEM_LIMIT = 56 * 1024 * 1024


def _cparams(*sem):
    return pltpu.CompilerParams(dimension_semantics=sem, vmem_limit_bytes=VMEM_LIMIT)


def _dot(a, b):
    return jnp.dot(a, b, preferred_element_type=F32)


def _dot_nt(a, b):
    return lax.dot_general(a, b, (((1,), (1,)), ((), ())), preferred_element_type=F32)


def _split3(x):
    hi = x.astype(BF16)
    r1 = x - hi.astype(F32)
    mid = r1.astype(BF16)
    lo = (r1 - mid.astype(F32)).astype(BF16)
    return hi, mid, lo


def _proj_kernel(x_ref, w_ref, o_ref):
    o_ref[...] = _dot(x_ref[...].astype(BF16), w_ref[...])


def _proj(x2, row0, rows, w):
    D = x2.shape[1]
    N = w.shape[1]
    tm = 512
    off = row0 // tm
    return pl.pallas_call(
        _proj_kernel,
        grid=(rows // tm,),
        in_specs=[pl.BlockSpec((tm, D), lambda i: (i + off, 0)),
                  pl.BlockSpec((D, N), lambda i: (0, 0))],
        out_specs=pl.BlockSpec((tm, N), lambda i: (i, 0)),
        out_shape=jax.ShapeDtypeStruct((rows, N), F32),
        compiler_params=_cparams("parallel"),
        name="proj",
    )(x2, w)


def _gla_kernel(q_ref, k_ref, v_ref, r_ref, a_ref, wa_ref, ba_ref, ng_ref, o_ref, st_ref, *, nchunk):
    C = GLA_CHUNK
    HK = GLA_HEADS * GLA_DK

    @pl.when(pl.program_id(1) == 0)
    def _():
        st_ref[...] = jnp.zeros_like(st_ref)

    ri = lax.broadcasted_iota(jnp.int32, (C, C), 0)
    ci = lax.broadcasted_iota(jnp.int32, (C, C), 1)
    causal = ri >= ci
    causal4 = jnp.concatenate([causal] * GLA_HEADS, axis=0)
    tri = jnp.where(causal, 1.0, 0.0).astype(BF16)
    lane_head = lax.broadcasted_iota(jnp.int32, (1, HK), 1) // GLA_DK
    wa = wa_ref[...]
    ba = ba_ref[...]
    ng = ng_ref[...]

    def chunk(c, states):
        rows = pl.ds(pl.multiple_of(c * C, C), C)
        results = [chunk_one(rows, bb, states[bb]) for bb in range(len(states))]
        for bb, (outs, _) in enumerate(results):
            for h, o in enumerate(outs):
                o_ref[bb, rows, h * GLA_DV:(h + 1) * GLA_DV] = o
        return tuple(new_st for _, new_st in results)

    def chunk_one(rows, bb, st):
        q = q_ref[bb, rows, :]
        k = k_ref[bb, rows, :]
        v = v_ref[bb, rows, :]
        r = r_ref[bb, rows, :]
        a = a_ref[bb, rows, :]
        z = _dot(a.astype(BF16), wa) + ba
        g = (jnp.minimum(z, 0.0) - jnp.log1p(jnp.exp(-jnp.abs(z)))) * (1.0 / GLA_TAU)
        b3 = _dot(tri, jnp.concatenate(_split3(g), axis=1))
        b = b3[:, :HK] + b3[:, HK:2 * HK] + b3[:, 2 * HK:]
        b_last = b[C - 1:C, :]
        qt = q * jnp.exp(b) * (GLA_DK ** -0.5)
        kt = (k * jnp.exp(-b)).astype(BF16)
        ks = (k * jnp.exp(b_last - b)).astype(BF16)
        vb = v.astype(BF16)
        q_all = jnp.concatenate([jnp.where(lane_head == h, qt, 0.0) for h in range(GLA_HEADS)],
                                axis=0).astype(BF16)
        att_all = jnp.where(causal4, _dot_nt(q_all, kt), 0.0)
        inter_all = _dot_nt(q_all, st.astype(BF16))
        intra_all = _dot(att_all.astype(BF16), vb)
        upd_all = _dot(vb.T, ks)
        new_st = st * jnp.exp(b_last)
        outs = []
        for h in range(GLA_HEADS):
            new_st = new_st + jnp.where(lane_head == h, upd_all[h * GLA_DV:(h + 1) * GLA_DV, :], 0.0)
            o = inter_all[h * C:(h + 1) * C, :] + intra_all[h * C:(h + 1) * C, h * GLA_DV:(h + 1) * GLA_DV]
            o = o * lax.rsqrt(jnp.mean(o * o, axis=-1, keepdims=True) + 1e-6) * ng
            rh = r[:, h * GLA_DV:(h + 1) * GLA_DV]
            outs.append((o * (rh * jax.nn.sigmoid(rh))).astype(o_ref.dtype))
        return outs, new_st

    nbat = q_ref.shape[0]
    states = lax.fori_loop(0, nchunk, chunk, tuple(st_ref[bb] for bb in range(nbat)))
    for bb in range(nbat):
        st_ref[bb] = states[bb]


def _gla(proj3, wa, ba, ng):
    B, S, _ = proj3.shape
    L = 512 if S % 512 == 0 else S
    nchunk = L // GLA_CHUNK

    def col(name, width):
        off = _COLS[name][0]
        assert off % width == 0
        return pl.BlockSpec((nbat, L, width), lambda b, j, o=off // width: (b, j, o))

    nbat = 4 if B % 4 == 0 else (2 if B % 2 == 0 else 1)
    full2 = lambda shape: pl.BlockSpec(shape, lambda b, j: (0, 0))
    return pl.pallas_call(
        functools.partial(_gla_kernel, nchunk=nchunk),
        grid=(B // nbat, S // L),
        in_specs=[col("gq", 256), col("gk", 256), col("gv", 512), col("gr", 512), col("ga", 128),
                  full2((128, 256)), full2((1, 256)), full2((1, 128))],
        out_specs=pl.BlockSpec((nbat, L, 512), lambda b, j: (b, j, 0)),
        out_shape=jax.ShapeDtypeStruct((B, S, 512), BF16),
        scratch_shapes=[pltpu.VMEM((nbat, GLA_DV, GLA_HEADS * GLA_DK), F32)],
        compiler_params=_cparams("parallel", "arbitrary"),
        name="gla",
    )(proj3, proj3, proj3, proj3, proj3, wa, ba, ng)


def _rot_half(x):
    n = x.shape[-1]
    lane = lax.broadcasted_iota(jnp.int32, (1, n), 1)
    first = (lane % NSA_DH) < (NSA_DH // 2)
    return jnp.where(first, -pltpu.roll(x, n - NSA_DH // 2, 1), pltpu.roll(x, NSA_DH // 2, 1))


def _nsa_prep_kernel(pos_ref, inv_ref, q_ref, kc_ref, ks_ref, vs_ref, kw_ref, vw_ref,
                     qo_ref, kco_ref, kso_ref, vso_ref, kwo_ref, vwo_ref):
    ang = pos_ref[0] * inv_ref[...]
    cos = jnp.cos(ang)
    sin = jnp.sin(ang)
    cos4 = jnp.concatenate([cos] * 4, axis=1)
    sin4 = jnp.concatenate([sin] * 4, axis=1)

    q = q_ref[0]
    qr = q * cos4 + _rot_half(q) * sin4
    for h in range(NSA_HEADS):
        qo_ref[0, h] = (qr[:, h * NSA_DH:(h + 1) * NSA_DH] * (NSA_DH ** -0.5)).astype(qo_ref.dtype)

    def rope128(x):
        return x * cos + _rot_half(x) * sin

    kco_ref[0] = rope128(kc_ref[0])
    ksr = rope128(ks_ref[0])
    kwr = rope128(kw_ref[0])
    vs = vs_ref[0]
    vw = vw_ref[0]
    ones_col = jnp.where(lax.broadcasted_iota(jnp.int32, (vs.shape[0], NSA_DH), 1) == 0, 1.0, 0.0)
    for g in range(NSA_KV_GROUPS):
        sl = slice(g * NSA_DH, (g + 1) * NSA_DH)
        kso_ref[0, g] = ksr[:, sl].astype(kso_ref.dtype)
        kwo_ref[0, g] = kwr[:, sl].astype(kwo_ref.dtype)
        vso_ref[0, g] = jnp.concatenate([vs[:, sl], ones_col], axis=1).astype(vso_ref.dtype)
        vwo_ref[0, g] = jnp.concatenate([vw[:, sl], ones_col], axis=1).astype(vwo_ref.dtype)


def _nsa_prep(proj3, pos3, inv_row):
    B, S, _ = proj3.shape
    ts = 512 if S % 512 == 0 else S

    def col(name, width):
        off = _COLS[name][0]
        assert off % width == 0
        return pl.BlockSpec((1, ts, width), lambda b, j, o=off // width: (b, j, o))

    kv_spec = pl.BlockSpec((1, NSA_KV_GROUPS, ts, NSA_DH), lambda b, j: (b, 0, j, 0))
    kv_shape = jax.ShapeDtypeStruct((B, NSA_KV_GROUPS, S, NSA_DH), BF16)
    vx_spec = pl.BlockSpec((1, NSA_KV_GROUPS, ts, 2 * NSA_DH), lambda b, j: (b, 0, j, 0))
    vx_shape = jax.ShapeDtypeStruct((B, NSA_KV_GROUPS, S, 2 * NSA_DH), BF16)
    return pl.pallas_call(
        _nsa_prep_kernel,
        grid=(B, S // ts),
        in_specs=[pl.BlockSpec((1, ts, 1), lambda b, j: (b, j, 0)),
                  pl.BlockSpec((1, 128), lambda b, j: (0, 0)),
                  col("nq", 512), col("nkc", 128), col("nks", 128), col("nvs", 128),
                  col("nkw", 128), col("nvw", 128)],
        out_specs=[pl.BlockSpec((1, NSA_HEADS, ts, NSA_DH), lambda b, j: (b, 0, j, 0)),
                   pl.BlockSpec((1, ts, 128), lambda b, j: (b, j, 0)),
                   kv_spec, vx_spec, kv_spec, vx_spec],
        out_shape=[jax.ShapeDtypeStruct((B, NSA_HEADS, S, NSA_DH), BF16),
                   jax.ShapeDtypeStruct((B, S, 128), F32),
                   kv_shape, vx_shape, kv_shape, vx_shape],
        compiler_params=_cparams("parallel", "parallel"),
        name="nsa_prep",
    )(pos3, inv_row, proj3, proj3, proj3, proj3, proj3, proj3)


def _compress_kernel(k_ref, v_ref, pk_ref, w1k_ref, b1k_ref, w2k_ref, pv_ref, w1v_ref, b1v_ref, w2v_ref,
                     ko_ref, vo_ref):
    nr = k_ref.shape[1] // CMP_STRIDE
    hidden = w2k_ref.shape[0]

    def run(x_ref, p_ref, w1_ref, b1_ref, w2_ref, o_ref):
        first = jnp.zeros((nr, NSA_KV_GROUPS * hidden), F32)
        second = jnp.zeros((nr, NSA_KV_GROUPS * hidden), F32)
        for m in range(CMP_STRIDE):
            xm = x_ref[0, pl.ds(m, nr, stride=CMP_STRIDE), :]
            first = first + _dot((xm + p_ref[m:m + 1, :]).astype(BF16), w1_ref[m])
            second = second + _dot((xm + p_ref[CMP_STRIDE + m:CMP_STRIDE + m + 1, :]).astype(BF16),
                                   w1_ref[CMP_STRIDE + m])
        pre = first + pltpu.roll(second, nr - 1, 0) + b1_ref[...]
        hid = jax.nn.gelu(pre)
        for g in range(NSA_KV_GROUPS):
            out = _dot(hid[:, g * hidden:(g + 1) * hidden].astype(BF16), w2_ref[...])
            row = lax.broadcasted_iota(jnp.int32, out.shape, 0)
            o_ref[0, g] = jnp.where(row < nr - 1, out, 0.0)

    run(k_ref, pk_ref, w1k_ref, b1k_ref, w2k_ref, ko_ref)
    run(v_ref, pv_ref, w1v_ref, b1v_ref, w2v_ref, vo_ref)


def _compress_params(pos, w1, b1, w2):
    L, dh = pos.shape
    hidden = w1.shape[1]
    w = w1.reshape(L, dh, hidden).astype(BF16)
    z = jnp.zeros_like(w)
    w_bd = jnp.concatenate([jnp.concatenate([w, z], axis=2), jnp.concatenate([z, w], axis=2)], axis=1)
    return (jnp.tile(pos, (1, NSA_KV_GROUPS)), w_bd, jnp.tile(b1.reshape(1, -1), (1, NSA_KV_GROUPS)),
            w2.astype(BF16))


def _compress(kc_r, proj3, k_params, v_params):
    B, S, W = kc_r.shape
    NR = S // CMP_STRIDE
    full = lambda a: pl.BlockSpec(a.shape, lambda b: (0,) * a.ndim)
    o_spec = pl.BlockSpec((1, NSA_KV_GROUPS, NR, NSA_DH), lambda b: (b, 0, 0, 0))
    o_shape = jax.ShapeDtypeStruct((B, NSA_KV_GROUPS, NR, NSA_DH), F32)
    params = tuple(k_params) + tuple(v_params)
    return pl.pallas_call(
        _compress_kernel,
        grid=(B,),
        in_specs=[pl.BlockSpec((1, S, W), lambda b: (b, 0, 0)),
                  pl.BlockSpec((1, S, W), lambda b, o=_COLS["nvc"][0] // W: (b, 0, o))]
                 + [full(a) for a in params],
        out_specs=[o_spec, o_spec],
        out_shape=[o_shape, o_shape],
        compiler_params=_cparams("parallel"),
        name="compress",
    )(kc_r, proj3, *params)


def _nsa_attn_kernel(q_ref, kc_ref, vc_ref, ks_ref, vs_ref, kw_ref, vw_ref, gate_ref, ovt_ref, gsel_ref, o_ref,
                     *, tq, S, n_sel, n_top):
    qi = pl.program_id(2)
    start = qi * tq
    H = NSA_HPG
    qs = q_ref[0].reshape(H * tq, NSA_DH)
    t_col = start + lax.broadcasted_iota(jnp.int32, (tq, 1), 0)
    t_row = start + lax.broadcasted_iota(jnp.int32, (1, tq), 1)

    kc = kc_ref[0, 0].astype(BF16)
    vc = vc_ref[0, 0].astype(BF16)
    NR = kc.shape[0]
    s = _dot_nt(qs, kc)
    ncol = lax.broadcasted_iota(jnp.int32, (1, NR), 1)
    cmask = (ncol < NR - 1) & (ncol * CMP_STRIDE + (CMP_LEN - 1) <= t_col)
    s3 = jnp.where(cmask[None], s.reshape(H, tq, NR), NEG)
    m = jnp.max(s3, axis=-1, keepdims=True)
    e = jnp.exp(s3 - m)
    p3 = e / jnp.sum(e, axis=-1, keepdims=True)
    p3 = jnp.where(cmask[None], p3, 0.0)
    o_cmp = _dot(p3.reshape(H * tq, NR).astype(BF16), vc)

    psum = jnp.sum(p3, axis=0)
    p_hi = psum.astype(BF16)
    p_lo = (psum - p_hi.astype(F32)).astype(BF16)
    ovt = ovt_ref[...]
    imp = _dot_nt(ovt, p_hi) + _dot_nt(ovt, p_lo)
    blk = lax.broadcasted_iota(jnp.int32, (n_sel, 1), 0)
    cur = t_row // SEL_LEN
    causal_blk = blk <= cur
    forced = (blk == 0) | (blk == cur) | (blk == cur - 1)
    val = jnp.where(causal_blk, jnp.where(forced, jnp.inf, imp), -jnp.inf)
    rank = jnp.zeros((n_sel, tq), jnp.int32)
    for j in range(n_sel):
        vj = val[j:j + 1, :]
        beats = (vj > val) | ((vj == val) & (blk > j))
        rank = rank + beats.astype(jnp.int32)
    sel_t = jnp.where((rank < n_top) & causal_blk, 1.0, 0.0)
    sel_t = jnp.concatenate([sel_t, jnp.zeros((128 - n_sel, tq), F32)], axis=0)
    sel = sel_t.T.astype(BF16)

    tk = 256
    nsub = 2 if S % (2 * tk) == 0 else 1
    bpc = tk // SEL_LEN
    n_trips = (start + tq + nsub * tk - 1) // (nsub * tk)
    erow = lax.broadcasted_iota(jnp.int32, (128, tk), 0)
    ecol = lax.broadcasted_iota(jnp.int32, (128, tk), 1) // SEL_LEN
    kk = lax.broadcasted_iota(jnp.int32, (1, tk), 1)

    def sel_trip(c, carry):
        m_i, acc = carry
        scores, vals = [], []
        m_new = m_i
        for u in range(nsub):
            cu = c * nsub + u
            k0 = pl.multiple_of(cu * tk, tk)
            kb = ks_ref[0, 0, pl.ds(k0, tk), :]
            vals.append(vs_ref[0, 0, pl.ds(k0, tk), :])
            expand = jnp.where(erow == ecol + cu * bpc, 1.0, 0.0).astype(BF16)
            allowed = (_dot(sel, expand) > 0.5) & (k0 + kk <= t_col)
            bias = jnp.where(allowed, 0.0, NEG)
            sc3 = _dot_nt(qs, kb).reshape(H, tq, tk) + bias[None]
            scores.append(sc3)
            m_new = jnp.maximum(m_new, jnp.max(sc3, axis=-1, keepdims=True))
        acc = jnp.exp(m_i - m_new).reshape(H * tq, 1) * acc
        for u in range(nsub):
            pe = jnp.exp((scores[u] - m_new).astype(BF16))
            acc = acc + _dot(pe.reshape(H * tq, tk), vals[u])
        return m_new, acc

    m0 = jnp.full((H, tq, 1), NEG, F32)
    a0 = jnp.zeros((H * tq, 2 * NSA_DH), F32)
    _, acc_f = lax.fori_loop(0, n_trips, sel_trip, (m0, a0))
    o_sel = acc_f[:, :NSA_DH] / acc_f[:, NSA_DH:NSA_DH + 1]

    span = min(WINDOW + tq, S)
    ws = jnp.clip(start - WINDOW, 0, S - span)
    ws = pl.multiple_of(ws, tq)
    kwb = kw_ref[0, 0, pl.ds(ws, span), :]
    vwb = vw_ref[0, 0, pl.ds(ws, span), :]
    kp = ws + lax.broadcasted_iota(jnp.int32, (1, span), 1)
    wbias = jnp.where((kp <= t_col) & (kp > t_col - WINDOW), 0.0, NEG)
    sw3 = _dot_nt(qs, kwb).reshape(H, tq, span) + wbias[None]
    mw = jnp.max(sw3, axis=-1, keepdims=True)
    ew = jnp.exp((sw3 - mw).astype(BF16))
    aw = _dot(ew.reshape(H * tq, span), vwb)
    o_win = aw[:, :NSA_DH] / aw[:, NSA_DH:NSA_DH + 1]

    gs = jax.nn.sigmoid(gate_ref[0])
    gs_hi = gs.astype(BF16)
    gs_lo = (gs - gs_hi.astype(F32)).astype(BF16)
    spread = gsel_ref[0]
    gw = _dot(gs_hi, spread) + _dot(gs_lo, spread)
    for h in range(H):
        rs = slice(h * tq, (h + 1) * tq)
        gate = [gw[:, (h * 3 + br) * 128:(h * 3 + br) * 128 + NSA_DH] for br in range(3)]
        o = gate[0] * o_cmp[rs] + gate[1] * o_sel[rs] + gate[2] * o_win[rs]
        o_ref[0, :, h * NSA_DH:(h + 1) * NSA_DH] = o.astype(o_ref.dtype)


def _nsa_attn(q_r, kcmp, vcmp, ks_r, vs_r, kw_r, vw_r, proj3, ov):
    B, _, S, _ = q_r.shape
    G, H = NSA_KV_GROUPS, NSA_HPG
    NR = kcmp.shape[2]
    tq = 256
    n_sel = S // SEL_LEN
    n_top = min(SEL_TOPK, n_sel)
    cmp_spec = pl.BlockSpec((1, 1, NR, NSA_DH), lambda b, g, i: (b, g, 0, 0))
    kv_spec = pl.BlockSpec((1, 1, S, NSA_DH), lambda b, g, i: (b, g, 0, 0))
    vx_spec = pl.BlockSpec((1, 1, S, 2 * NSA_DH), lambda b, g, i: (b, g, 0, 0))
    goff = _COLS["ng"][0] // 128
    sel_np = np.zeros((G, 128, H * 3 * 128), np.float32)
    for gg in range(G):
        for hb in range(H * 3):
            sel_np[gg, gg * H * 3 + hb, hb * 128:(hb + 1) * 128] = 1.0
    gsel = jnp.asarray(sel_np, BF16)
    return pl.pallas_call(
        functools.partial(_nsa_attn_kernel, tq=tq, S=S, n_sel=n_sel, n_top=n_top),
        grid=(B, G, S // tq),
        in_specs=[pl.BlockSpec((1, H, tq, NSA_DH), lambda b, g, i: (b, g, i, 0)),
                  cmp_spec, cmp_spec, kv_spec, vx_spec, kv_spec, vx_spec,
                  pl.BlockSpec((1, tq, 128), lambda b, g, i: (b, i, goff)),
                  pl.BlockSpec(ov.shape, lambda b, g, i: (0, 0)),
                  pl.BlockSpec((1,) + gsel.shape[1:], lambda b, g, i: (g, 0, 0))],
        out_specs=pl.BlockSpec((1, tq, H * NSA_DH), lambda b, g, i: (b, i, g)),
        out_shape=jax.ShapeDtypeStruct((B, S, NSA_HEADS * NSA_DH), BF16),
        compiler_params=_cparams("parallel", "parallel", "arbitrary"),
        name="nsa_attn",
    )(q_r, kcmp, vcmp, ks_r, vs_r, kw_r, vw_r, proj3, ov, gsel)


def _layer_norm(x, g, b):
    mu = jnp.mean(x, axis=-1, keepdims=True)
    xc = x - mu
    var = jnp.mean(xc * xc, axis=-1, keepdims=True)
    return xc * lax.rsqrt(var + LN_EPS) * g + b


SUB = D_MODEL // 2 // 128


def _pack_bf16_pairs(v):
    half = v.shape[1] // 2
    bits = pltpu.bitcast(v.astype(BF16).astype(F32), jnp.uint32)
    return pltpu.bitcast((bits[:, :half] >> 16) | bits[:, half:], jnp.int32)


def _unpack_bf16_pairs(p):
    u = pltpu.bitcast(p, jnp.uint32)
    return pltpu.bitcast(u << 16, F32), pltpu.bitcast(u & jnp.uint32(0xFFFF0000), F32)


def _to_tiles(ref, val, lead=()):
    for c in range(SUB):
        ref[lead + (c,)] = val[:, c * 128:(c + 1) * 128]


def _from_tiles(ref, lead=()):
    return jnp.concatenate([ref[lead + (c,)] for c in range(SUB)], axis=1)


def _outproj_kernel(x_ref, og_ref, on_ref, w_ref, g_ref, b_ref, o_ref, ot_ref):
    half = og_ref.shape[1]
    mix = _dot(og_ref[...], w_ref[0:half, :]) + _dot(on_ref[...], w_ref[half:, :])
    h = _layer_norm(DN_ALPHA * x_ref[...] + mix, g_ref[...], b_ref[...])
    o_ref[...] = h
    _to_tiles(ot_ref, _pack_bf16_pairs(h))


def _outproj(x2, row0, og2, on2, w, g, b):
    T = og2.shape[0]
    D = x2.shape[1]
    tm = 512
    off = row0 // tm
    row = lambda width: pl.BlockSpec((tm, width), lambda i: (i, 0))
    full = lambda a: pl.BlockSpec(a.shape, lambda i: (0, 0))
    return pl.pallas_call(
        _outproj_kernel,
        grid=(T // tm,),
        in_specs=[pl.BlockSpec((tm, D), lambda i: (i + off, 0)), row(og2.shape[1]), row(on2.shape[1]),
                  full(w), full(g), full(b)],
        out_specs=[row(D), pl.BlockSpec((SUB, tm, 128), lambda i: (0, i, 0))],
        out_shape=[jax.ShapeDtypeStruct((T, D), F32),
                   jax.ShapeDtypeStruct((SUB, T, 128), jnp.int32)],
        compiler_params=_cparams("parallel"),
        name="outproj_ln",
    )(x2, og2, on2, w, g, b)


def _router_kernel(h_ref, wh_ref, wl_ref, bias_ref, eidx_ref, wts_ref, rank_ref, cnt_ref, carry_ref):
    @pl.when(pl.program_id(0) == 0)
    def _():
        carry_ref[...] = jnp.zeros_like(carry_ref)

    h = h_ref[...]
    tm = h.shape[0]
    E = N_EXPERTS
    GS = E // N_GROUPS
    h_hi = h.astype(BF16)
    h_lo = (h - h_hi.astype(F32)).astype(BF16)
    wh = wh_ref[...]
    logits = _dot_nt(wh, h_hi) + _dot_nt(wh, h_lo) + _dot_nt(wl_ref[...], h_hi)
    scores = jax.nn.sigmoid(logits)
    biased = scores + bias_ref[...]
    eid = lax.broadcasted_iota(jnp.int32, (E, tm), 0).astype(F32)
    ninf = -jnp.inf

    def col_max(x):
        return jnp.max(x, axis=0, keepdims=True)

    def first_idx(x, mx, ids):
        return jnp.min(jnp.where(x == mx, ids, float(E)), axis=0, keepdims=True)

    ids0 = lax.broadcasted_iota(jnp.int32, (GS, tm), 0).astype(F32)
    gscore = []
    for gi in range(N_GROUPS):
        mg = biased[gi * GS:(gi + 1) * GS, :]
        ids = ids0 + float(gi * GS)
        m1 = col_max(mg)
        i1 = first_idx(mg, m1, ids)
        m2 = col_max(jnp.where(ids == i1, ninf, mg))
        gscore.append(m1 + m2)
    parts = []
    for gi in range(N_GROUPS):
        rk = jnp.zeros((1, tm), jnp.int32)
        for gj in range(N_GROUPS):
            if gj == gi:
                continue
            beats = (gscore[gj] > gscore[gi]) | ((gscore[gj] == gscore[gi]) & (gj < gi))
            rk = rk + beats.astype(jnp.int32)
        parts.append(jnp.where(rk < TOPK_GROUPS, biased[gi * GS:(gi + 1) * GS, :], ninf))
    masked = jnp.concatenate(parts, axis=0)

    onehots, wsel = [], []
    selm = jnp.zeros((E, tm), F32)
    for k in range(TOP_K):
        mx = col_max(masked)
        idx = first_idx(masked, mx, eid)
        oh = eid == idx
        onehots.append(oh)
        wsel.append(jnp.sum(jnp.where(oh, scores, 0.0), axis=0, keepdims=True))
        masked = jnp.where(oh, ninf, masked)
        selm = jnp.where(oh, 1.0, selm)
        eidx_ref[k:k + 1, :] = idx.astype(jnp.int32)
    wsum = wsel[0]
    for k in range(1, TOP_K):
        wsum = wsum + wsel[k]
    for k in range(TOP_K):
        wts_ref[k:k + 1, :] = wsel[k] / wsum * ROUTED_SCALE

    ri = lax.broadcasted_iota(jnp.int32, (tm, tm), 0)
    ci = lax.broadcasted_iota(jnp.int32, (tm, tm), 1)
    before = jnp.where(ri < ci, 1.0, 0.0).astype(BF16)
    cum = _dot(selm.astype(BF16), before) + carry_ref[...]
    for k in range(TOP_K):
        rk = jnp.sum(jnp.where(onehots[k], cum, 0.0), axis=0, keepdims=True)
        rank_ref[k:k + 1, :] = rk.astype(jnp.int32)
    total = carry_ref[...] + jnp.sum(selm, axis=1, keepdims=True)
    carry_ref[...] = total
    cnt_ref[...] = total


def _router(h2, w_hi_t, w_lo_t, bias_col):
    T, D = h2.shape
    tm = 256
    full = lambda a: pl.BlockSpec(a.shape, lambda i: (0, 0))
    o8 = pl.BlockSpec((TOP_K, tm), lambda i: (0, i))
    return pl.pallas_call(
        _router_kernel,
        grid=(T // tm,),
        in_specs=[pl.BlockSpec((tm, D), lambda i: (i, 0)), full(w_hi_t), full(w_lo_t), full(bias_col)],
        out_specs=[o8, o8, o8, pl.BlockSpec((N_EXPERTS, 1), lambda i: (0, 0))],
        out_shape=[jax.ShapeDtypeStruct((TOP_K, T), jnp.int32),
                   jax.ShapeDtypeStruct((TOP_K, T), F32),
                   jax.ShapeDtypeStruct((TOP_K, T), jnp.int32),
                   jax.ShapeDtypeStruct((N_EXPERTS, 1), F32)],
        scratch_shapes=[pltpu.VMEM((N_EXPERTS, 1), F32)],
        compiler_params=_cparams("arbitrary"),
        name="router",
    )(h2, w_hi_t, w_lo_t, bias_col)


def _dest_kernel(eidx_ref, rank_ref, ps_ref, dest_ref):
    tm = eidx_ref.shape[1]
    eid = lax.broadcasted_iota(jnp.int32, (N_EXPERTS, tm), 0)
    ps = ps_ref[...]
    for k in range(TOP_K):
        start = jnp.sum(jnp.where(eid == eidx_ref[k:k + 1, :], ps, 0.0), axis=0, keepdims=True)
        dest_ref[k:k + 1, :] = start.astype(jnp.int32) + rank_ref[k:k + 1, :]


def _dest(eidx_t, rank_t, pad_start_col):
    T = eidx_t.shape[1]
    tm = 1024 if T % 1024 == 0 else T
    o8 = pl.BlockSpec((TOP_K, tm), lambda i: (0, i))
    return pl.pallas_call(
        _dest_kernel,
        grid=(T // tm,),
        in_specs=[o8, o8, pl.BlockSpec((N_EXPERTS, 1), lambda i: (0, 0))],
        out_specs=o8,
        out_shape=jax.ShapeDtypeStruct((TOP_K, T), jnp.int32),
        compiler_params=_cparams("parallel"),
        name="dest",
    )(eidx_t, rank_t, pad_start_col)


SC_WINDOW = 128


def _sc_gather(table, idx):
    _, lanes = table.shape
    n = idx.shape[0]
    mesh = plsc.VectorSubcoreMesh(core_axis_name="core", subcore_axis_name="subcore")

    @functools.partial(pl.kernel, out_type=jax.ShapeDtypeStruct((n, lanes), table.dtype), mesh=mesh,
                       name="sc_row_gather")
    def gather(x_hbm, i_hbm, o_hbm):
        def body(i_vmem, o_vmem):
            pltpu.sync_copy(x_hbm.at[i_vmem.at[0]], o_vmem)

        pltpu.emit_pipeline(
            body,
            grid=(n // SC_WINDOW,),
            in_specs=[pl.BlockSpec((1, SC_WINDOW), lambda i: (0, i))],
            out_specs=[pl.BlockSpec((SC_WINDOW, lanes), lambda i: (i, 0))],
            core_axis_name=("core", "subcore"),
            dimension_semantics=(pltpu.PARALLEL,),
            trace_scopes=False,
        )(i_hbm, o_hbm)

    return gather(table, idx.reshape(1, n))


def _sc_scatter(src, idx, n_out):
    rows, lanes = src.shape
    n_idx = idx.shape[0]
    mesh = plsc.VectorSubcoreMesh(core_axis_name="core", subcore_axis_name="subcore")

    @functools.partial(pl.kernel, out_type=jax.ShapeDtypeStruct((n_out, lanes), src.dtype), mesh=mesh,
                       name="sc_row_scatter")
    def scatter(x_hbm, i_hbm, o_hbm):
        def body(x_vmem, *i_vmems):
            for i_vmem in i_vmems:
                pltpu.sync_copy(x_vmem, o_hbm.at[i_vmem.at[0]])

        pltpu.emit_pipeline(
            body,
            grid=(rows // SC_WINDOW,),
            in_specs=[pl.BlockSpec((SC_WINDOW, lanes), lambda i: (i, 0))]
                     + [pl.BlockSpec((1, SC_WINDOW), lambda i, j=j: (j, i)) for j in range(n_idx)],
            out_specs=[],
            core_axis_name=("core", "subcore"),
            dimension_semantics=(pltpu.PARALLEL,),
            trace_scopes=False,
        )(x_hbm, *([i_hbm] * n_idx))

    return scatter(src, idx)


def _expert_kernel(bexp_ref, nused_ref, x_ref, wg_ref, wu_ref, wd_ref, y_ref, wg_b, wu_b, wd_b):
    i = pl.program_id(0)

    @pl.when(i < nused_ref[0])
    def _():
        @pl.when((i == 0) | (bexp_ref[i] != bexp_ref[jnp.maximum(i - 1, 0)]))
        def _():
            wg_b[...] = wg_ref[0].astype(BF16)
            wu_b[...] = wu_ref[0].astype(BF16)
            wd_b[...] = wd_ref[0].astype(BF16)

        x = jnp.concatenate(_unpack_bf16_pairs(_from_tiles(x_ref)), axis=1).astype(BF16)
        gate = _dot(x, wg_b[...])
        up = _dot(x, wu_b[...])
        act = (gate * jax.nn.sigmoid(gate) * up).astype(BF16)
        _to_tiles(y_ref, _pack_bf16_pairs(_dot(act, wd_b[...])))


def _experts(blk_exp, n_used, xs_t, wg, wu, wd):
    NP = xs_t.shape[1]
    D = D_MODEL
    nb = NP // MOE_BLOCK
    blk = (SUB, MOE_BLOCK, 128)

    def xmap(i, bexp, nused):
        return (0, jnp.minimum(i, nused[0] - 1), 0)

    def wmap(i, bexp, nused):
        return (bexp[jnp.minimum(i, nused[0] - 1)], 0, 0)

    grid_spec = pltpu.PrefetchScalarGridSpec(
        num_scalar_prefetch=2,
        grid=(nb,),
        in_specs=[pl.BlockSpec(blk, xmap),
                  pl.BlockSpec((1, D, D_EXPERT), wmap),
                  pl.BlockSpec((1, D, D_EXPERT), wmap),
                  pl.BlockSpec((1, D_EXPERT, D), wmap)],
        out_specs=pl.BlockSpec(blk, xmap),
        scratch_shapes=[pltpu.VMEM((D, D_EXPERT), BF16), pltpu.VMEM((D, D_EXPERT), BF16),
                        pltpu.VMEM((D_EXPERT, D), BF16)],
    )
    return pl.pallas_call(
        _expert_kernel,
        grid_spec=grid_spec,
        out_shape=jax.ShapeDtypeStruct(xs_t.shape, jnp.int32),
        compiler_params=_cparams("arbitrary"),
        name="experts",
    )(blk_exp, n_used, xs_t, wg, wu, wd)


def _combine_kernel(h_ref, wts_ref, yg_ref, wsg_ref, wsu_ref, wsd_ref, g_ref, b_ref, *rest):
    o_ref = rest[-1]
    h = h_ref[...]
    hb = h.astype(BF16)
    gate = _dot(hb, wsg_ref[...])
    up = _dot(hb, wsu_ref[...])
    shared = _dot((gate * jax.nn.sigmoid(gate) * up).astype(BF16), wsd_ref[...])
    wts = wts_ref[...]
    lo, hi = _unpack_bf16_pairs(_from_tiles(yg_ref, (0,)))
    r_lo, r_hi = wts[:, 0:1] * lo, wts[:, 0:1] * hi
    for k in range(1, TOP_K):
        lo, hi = _unpack_bf16_pairs(_from_tiles(yg_ref, (k,)))
        r_lo, r_hi = r_lo + wts[:, k:k + 1] * lo, r_hi + wts[:, k:k + 1] * hi
    routed = jnp.concatenate([r_lo, r_hi], axis=1)
    o_ref[...] = _layer_norm(DN_ALPHA * h + (routed + shared), g_ref[...], b_ref[...])


def _combine(h2, wts, yg_t, wsg, wsu, wsd, g, b, out_rows, row0, out_prev):
    T, D = h2.shape
    tm = 256
    off = row0 // tm
    full = lambda a: pl.BlockSpec(a.shape, lambda i: (0, 0))
    in_specs = [pl.BlockSpec((tm, D), lambda i: (i, 0)),
                pl.BlockSpec((tm, TOP_K), lambda i: (i, 0)),
                pl.BlockSpec((TOP_K, SUB, tm, 128), lambda i: (0, 0, i, 0)),
                full(wsg), full(wsu), full(wsd), full(g), full(b)]
    args = [h2, wts, yg_t, wsg, wsu, wsd, g, b]
    aliases = {}
    if out_prev is not None:
        in_specs.append(pl.BlockSpec(memory_space=pl.ANY))
        args.append(out_prev)
        aliases = {len(args) - 1: 0}
    return pl.pallas_call(
        _combine_kernel,
        grid=(T // tm,),
        in_specs=in_specs,
        out_specs=pl.BlockSpec((tm, D), lambda i: (i + off, 0)),
        out_shape=jax.ShapeDtypeStruct((out_rows, D), F32),
        input_output_aliases=aliases,
        compiler_params=_cparams("parallel"),
        name="combine_ln",
    )(*args)


def _regroup_w_in(w_in):
    parts, off = {}, 0
    for name, width in _SPLITS:
        parts[name] = w_in[:, off:off + width]
        off += width
    cols = []
    for name, (_, width) in _COLS.items():
        p = parts[name]
        if p.shape[1] < width:
            p = jnp.pad(p, ((0, 0), (0, width - p.shape[1])))
        cols.append(p)
    return jnp.concatenate(cols, axis=1).astype(BF16)


def _overlap_matrix(S):
    nr = S // CMP_STRIDE
    n_sel = S // SEL_LEN
    ci = np.arange(nr)[:, None] * CMP_STRIDE
    sj = np.arange(n_sel)[None, :] * SEL_LEN
    ov = np.clip(np.minimum(ci + CMP_LEN, sj + SEL_LEN) - np.maximum(ci, sj), 0, None) / CMP_LEN
    ov[nr - 1] = 0.0
    return jnp.asarray(ov.T, BF16)


def _mixers(x2, row0, positions, w_in, w_alpha2, b_alpha, gla_norm_g,
            cmp_pos_k, cmp_w1_k, cmp_b1_k, cmp_w2_k, cmp_pos_v, cmp_w1_v, cmp_b1_v, cmp_w2_v):
    B, S = positions.shape
    proj = _proj(x2, row0, B * S, _regroup_w_in(w_in)).reshape(B, S, D_PROJ)

    wa = jnp.pad(w_alpha2, ((0, 128 - GLA_LOWRANK), (0, 0))).astype(BF16)
    o_gla = _gla(proj, wa, b_alpha.reshape(1, -1), gla_norm_g.reshape(1, -1))

    half = NSA_DH // 2
    inv = ROPE_THETA ** (-np.arange(half, dtype=np.float32) / half)
    inv_row = jnp.asarray(np.tile(inv, 128 // half).reshape(1, 128), F32)
    pos3 = positions.astype(F32).reshape(B, S, 1)
    q_r, kc_r, ks_r, vs_r, kw_r, vw_r = _nsa_prep(proj, pos3, inv_row)

    kcmp, vcmp = _compress(kc_r, proj,
                           _compress_params(cmp_pos_k, cmp_w1_k, cmp_b1_k, cmp_w2_k),
                           _compress_params(cmp_pos_v, cmp_w1_v, cmp_b1_v, cmp_w2_v))
    o_nsa = _nsa_attn(q_r, kcmp, vcmp, ks_r, vs_r, kw_r, vw_r, proj, _overlap_matrix(S))
    return o_gla, o_nsa


def _moe_ln(h2, h_t, w_router, router_bias, w_gate, w_up, w_down, ws_gate, ws_up, ws_down, ln_g, ln_b,
            out_rows, row0, out_prev):
    T, D = h2.shape
    P = T * TOP_K
    w_hi = w_router.astype(BF16)
    w_lo = (w_router - w_hi.astype(F32)).astype(BF16)
    eidx_t, wts_t, rank_t, counts = _router(h2, w_hi.T, w_lo.T, router_bias.reshape(-1, 1))

    counts = counts.reshape(-1).astype(jnp.int32)
    padded = (counts + MOE_BLOCK - 1) // MOE_BLOCK * MOE_BLOCK
    pad_end = jnp.cumsum(padded)
    pad_start = pad_end - padded
    nb = -(-P // MOE_BLOCK) + N_EXPERTS
    n_used = (pad_end[-1] // MOE_BLOCK).astype(jnp.int32).reshape(1)
    blk_start = jnp.arange(nb, dtype=jnp.int32) * MOE_BLOCK
    blk_exp = jnp.minimum(jnp.sum((pad_end[None, :] <= blk_start[:, None]).astype(jnp.int32), axis=1),
                          N_EXPERTS - 1)

    NP = nb * MOE_BLOCK
    dest_t = _dest(eidx_t, rank_t, pad_start.astype(F32).reshape(-1, 1))
    col = jnp.arange(SUB, dtype=jnp.int32) * NP

    n_pad = NP - P
    assert n_pad % T == 0
    pad_cnt = padded - counts
    pad_hi = jnp.cumsum(pad_cnt)
    pad_lo = pad_hi - pad_cnt
    j = jnp.arange(n_pad, dtype=jnp.int32)[:, None]
    owner = (pad_lo[None, :] <= j) & (j < pad_hi[None, :])
    in_expert = jnp.sum(jnp.where(owner, (pad_start + counts - pad_lo)[None, :] + j, 0), axis=1)
    pad_rows = jnp.where(j[:, 0] < pad_hi[-1], in_expert, pad_end[-1] + j[:, 0] - pad_hi[-1])

    rows_all = jnp.concatenate([dest_t, pad_rows.reshape(n_pad // T, T)], axis=0)
    dst = (rows_all[:, None, :] + col[None, :, None]).reshape(-1, SUB * T)
    xs_t = _sc_scatter(h_t.reshape(SUB * T, 128), dst, SUB * NP).reshape(SUB, NP, 128)
    ys_t = _experts(blk_exp, n_used, xs_t, w_gate, w_up, w_down)
    src = (dest_t[:, None, :] + col[None, :, None]).reshape(-1)
    yg_t = _sc_gather(ys_t.reshape(SUB * NP, 128), src).reshape(TOP_K, SUB, T, 128)
    return _combine(h2, wts_t.T, yg_t, ws_gate.astype(BF16), ws_up.astype(BF16), ws_down.astype(BF16),
                    ln_g.reshape(1, -1), ln_b.reshape(1, -1), out_rows, row0, out_prev)


def kernel(x, positions, w_in, w_alpha2, b_alpha, gla_norm_g, cmp_pos_k, cmp_w1_k, cmp_b1_k, cmp_w2_k, cmp_pos_v, cmp_w1_v, cmp_b1_v, cmp_w2_v, w_out, ln1_g, ln1_b, w_router, router_bias, w_exp_gate, w_exp_up, w_exp_down, w_sh_gate, w_sh_up, w_sh_down, ln2_g, ln2_b):
    B, S, D = x.shape
    n_groups = BATCH_GROUPS if B % BATCH_GROUPS == 0 else 1
    bg = B // n_groups
    rows = bg * S
    h2d = x.reshape(B * S, D)
    for l in range(w_in.shape[0]):
        out = None
        for gi in range(n_groups):
            row0 = gi * rows
            o_gla, o_nsa = _mixers(h2d, row0, positions[gi * bg:(gi + 1) * bg], w_in[l], w_alpha2[l], b_alpha[l],
                                   gla_norm_g[l], cmp_pos_k[l], cmp_w1_k[l], cmp_b1_k[l], cmp_w2_k[l],
                                   cmp_pos_v[l], cmp_w1_v[l], cmp_b1_v[l], cmp_w2_v[l])
            h1, h1_t = _outproj(h2d, row0, o_gla.reshape(rows, -1), o_nsa.reshape(rows, -1),
                                w_out[l].astype(BF16), ln1_g[l].reshape(1, -1), ln1_b[l].reshape(1, -1))
            out = _moe_ln(h1, h1_t, w_router[l], router_bias[l], w_exp_gate[l], w_exp_up[l], w_exp_down[l],
                          w_sh_gate[l], w_sh_up[l], w_sh_down[l], ln2_g[l], ln2_b[l], B * S, row0, out)
        h2d = out
    return h2d.reshape(B, S, D)
```

```python
import functools

import numpy as np
import jax
import jax.numpy as jnp
from jax import lax
from jax.experimental import pallas as pl
from jax.experimental.pallas import tpu as pltpu
from jax.experimental.pallas import tpu_sc as plsc

D_MODEL = 1024
GLA_HEADS = 4
GLA_DV = 128
GLA_DK = 64
GLA_LOWRANK = 16
GLA_TAU = 16.0
GLA_CHUNK = 64
NSA_HEADS = 8
NSA_KV_GROUPS = 2
NSA_HPG = 4
NSA_DH = 64
CMP_LEN = 32
CMP_STRIDE = 16
CMP_HIDDEN = 256
SEL_LEN = 64
SEL_TOPK = 16
WINDOW = 512
ROPE_THETA = 10000.0
N_EXPERTS = 256
TOP_K = 8
N_GROUPS = 8
TOPK_GROUPS = 4
D_EXPERT = 256
ROUTED_SCALE = 2.5
DEPTH = 1
DN_ALPHA = (2.0 * DEPTH) ** 0.25
LN_EPS = 1e-5

MOE_BLOCK = 512
BATCH_GROUPS = 2
NEG = -1e30
F32 = jnp.float32
BF16 = jnp.bfloat16

_COLS = {}
_off = 0
for _name, _w in (("gq", 256), ("gk", 256), ("gv", 512), ("gr", 512), ("nq", 512),
                  ("nkc", 128), ("nvc", 128), ("nks", 128), ("nvs", 128), ("nkw", 128), ("nvw", 128),
                  ("ga", 128), ("ng", 128)):
    _COLS[_name] = (_off, _w)
    _off += _w
D_PROJ = _off
_SPLITS = (("gq", 256), ("gk", 256), ("gv", 512), ("ga", 16), ("gr", 512), ("nq", 512),
           ("nkc", 128), ("nvc", 128), ("nks", 128), ("nvs", 128), ("nkw", 128), ("nvw", 128), ("ng", 24))

VMEM_LIMIT = 56 * 1024 * 1024


def _cparams(*sem):
    return pltpu.CompilerParams(dimension_semantics=sem, vmem_limit_bytes=VMEM_LIMIT)


def _dot(a, b):
    return jnp.dot(a, b, preferred_element_type=F32)


def _dot_nt(a, b):
    return lax.dot_general(a, b, (((1,), (1,)), ((), ())), preferred_element_type=F32)


def _split3(x):
    hi = x.astype(BF16)
    r1 = x - hi.astype(F32)
    mid = r1.astype(BF16)
    lo = (r1 - mid.astype(F32)).astype(BF16)
    return hi, mid, lo


def _proj_kernel(x_ref, w_ref, o_ref):
    o_ref[...] = _dot(x_ref[...].astype(BF16), w_ref[...])


def _proj(x2, row0, rows, w):
    D = x2.shape[1]
    N = w.shape[1]
    tm = 512
    off = row0 // tm
    return pl.pallas_call(
        _proj_kernel,
        grid=(rows // tm,),
        in_specs=[pl.BlockSpec((tm, D), lambda i: (i + off, 0)),
                  pl.BlockSpec((D, N), lambda i: (0, 0))],
        out_specs=pl.BlockSpec((tm, N), lambda i: (i, 0)),
        out_shape=jax.ShapeDtypeStruct((rows, N), F32),
        compiler_params=_cparams("parallel"),
        name="proj",
    )(x2, w)


def _gla_kernel(q_ref, k_ref, v_ref, r_ref, a_ref, wa_ref, ba_ref, ng_ref, o_ref, st_ref, *, nchunk):
    C = GLA_CHUNK
    HK = GLA_HEADS * GLA_DK

    @pl.when(pl.program_id(1) == 0)
    def _():
        st_ref[...] = jnp.zeros_like(st_ref)

    ri = lax.broadcasted_iota(jnp.int32, (C, C), 0)
    ci = lax.broadcasted_iota(jnp.int32, (C, C), 1)
    causal = ri >= ci
    causal4 = jnp.concatenate([causal] * GLA_HEADS, axis=0)
    tri = jnp.where(causal, 1.0, 0.0).astype(BF16)
    lane_head = lax.broadcasted_iota(jnp.int32, (1, HK), 1) // GLA_DK
    wa = wa_ref[...]
    ba = ba_ref[...]
    ng = ng_ref[...]

    def chunk(c, states):
        rows = pl.ds(pl.multiple_of(c * C, C), C)
        results = [chunk_one(rows, bb, states[bb]) for bb in range(len(states))]
        for bb, (outs, _) in enumerate(results):
            for h, o in enumerate(outs):
                o_ref[bb, rows, h * GLA_DV:(h + 1) * GLA_DV] = o
        return tuple(new_st for _, new_st in results)

    def chunk_one(rows, bb, st):
        q = q_ref[bb, rows, :]
        k = k_ref[bb, rows, :]
        v = v_ref[bb, rows, :]
        r = r_ref[bb, rows, :]
        a = a_ref[bb, rows, :]
        z = _dot(a.astype(BF16), wa) + ba
        g = (jnp.minimum(z, 0.0) - jnp.log1p(jnp.exp(-jnp.abs(z)))) * (1.0 / GLA_TAU)
        b3 = _dot(tri, jnp.concatenate(_split3(g), axis=1))
        b = b3[:, :HK] + b3[:, HK:2 * HK] + b3[:, 2 * HK:]
        b_last = b[C - 1:C, :]
        qt = q * jnp.exp(b) * (GLA_DK ** -0.5)
        kt = (k * jnp.exp(-b)).astype(BF16)
        ks = (k * jnp.exp(b_last - b)).astype(BF16)
        vb = v.astype(BF16)
        q_all = jnp.concatenate([jnp.where(lane_head == h, qt, 0.0) for h in range(GLA_HEADS)],
                                axis=0).astype(BF16)
        att_all = jnp.where(causal4, _dot_nt(q_all, kt), 0.0)
        inter_all = _dot_nt(q_all, st.astype(BF16))
        intra_all = _dot(att_all.astype(BF16), vb)
        upd_all = _dot(vb.T, ks)
        new_st = st * jnp.exp(b_last)
        outs = []
        for h in range(GLA_HEADS):
            new_st = new_st + jnp.where(lane_head == h, upd_all[h * GLA_DV:(h + 1) * GLA_DV, :], 0.0)
            o = inter_all[h * C:(h + 1) * C, :] + intra_all[h * C:(h + 1) * C, h * GLA_DV:(h + 1) * GLA_DV]
            o = o * lax.rsqrt(jnp.mean(o * o, axis=-1, keepdims=True) + 1e-6) * ng
            rh = r[:, h * GLA_DV:(h + 1) * GLA_DV]
            outs.append((o * (rh * jax.nn.sigmoid(rh))).astype(o_ref.dtype))
        return outs, new_st

    nbat = q_ref.shape[0]
    states = lax.fori_loop(0, nchunk, chunk, tuple(st_ref[bb] for bb in range(nbat)))
    for bb in range(nbat):
        st_ref[bb] = states[bb]


def _gla(proj3, wa, ba, ng):
    B, S, _ = proj3.shape
    L = 512 if S % 512 == 0 else S
    nchunk = L // GLA_CHUNK

    def col(name, width):
        off = _COLS[name][0]
        assert off % width == 0
        return pl.BlockSpec((nbat, L, width), lambda b, j, o=off // width: (b, j, o))

    nbat = 4 if B % 4 == 0 else (2 if B % 2 == 0 else 1)
    full2 = lambda shape: pl.BlockSpec(shape, lambda b, j: (0, 0))
    return pl.pallas_call(
        functools.partial(_gla_kernel, nchunk=nchunk),
        grid=(B // nbat, S // L),
        in_specs=[col("gq", 256), col("gk", 256), col("gv", 512), col("gr", 512), col("ga", 128),
                  full2((128, 256)), full2((1, 256)), full2((1, 128))],
        out_specs=pl.BlockSpec((nbat, L, 512), lambda b, j: (b, j, 0)),
        out_shape=jax.ShapeDtypeStruct((B, S, 512), BF16),
        scratch_shapes=[pltpu.VMEM((nbat, GLA_DV, GLA_HEADS * GLA_DK), F32)],
        compiler_params=_cparams("parallel", "arbitrary"),
        name="gla",
    )(proj3, proj3, proj3, proj3, proj3, wa, ba, ng)


def _rot_half(x):
    n = x.shape[-1]
    lane = lax.broadcasted_iota(jnp.int32, (1, n), 1)
    first = (lane % NSA_DH) < (NSA_DH // 2)
    return jnp.where(first, -pltpu.roll(x, n - NSA_DH // 2, 1), pltpu.roll(x, NSA_DH // 2, 1))


def _nsa_prep_kernel(pos_ref, inv_ref, q_ref, kc_ref, ks_ref, vs_ref, kw_ref, vw_ref,
                     qo_ref, kco_ref, kso_ref, vso_ref, kwo_ref, vwo_ref):
    ang = pos_ref[0] * inv_ref[...]
    cos = jnp.cos(ang)
    sin = jnp.sin(ang)
    cos4 = jnp.concatenate([cos] * 4, axis=1)
    sin4 = jnp.concatenate([sin] * 4, axis=1)

    q = q_ref[0]
    qr = q * cos4 + _rot_half(q) * sin4
    for h in range(NSA_HEADS):
        qo_ref[0, h] = (qr[:, h * NSA_DH:(h + 1) * NSA_DH] * (NSA_DH ** -0.5)).astype(qo_ref.dtype)

    def rope128(x):
        return x * cos + _rot_half(x) * sin

    kco_ref[0] = rope128(kc_ref[0])
    ksr = rope128(ks_ref[0])
    kwr = rope128(kw_ref[0])
    vs = vs_ref[0]
    vw = vw_ref[0]
    ones_col = jnp.where(lax.broadcasted_iota(jnp.int32, (vs.shape[0], NSA_DH), 1) == 0, 1.0, 0.0)
    for g in range(NSA_KV_GROUPS):
        sl = slice(g * NSA_DH, (g + 1) * NSA_DH)
        kso_ref[0, g] = ksr[:, sl].astype(kso_ref.dtype)
        kwo_ref[0, g] = kwr[:, sl].astype(kwo_ref.dtype)
        vso_ref[0, g] = jnp.concatenate([vs[:, sl], ones_col], axis=1).astype(vso_ref.dtype)
        vwo_ref[0, g] = jnp.concatenate([vw[:, sl], ones_col], axis=1).astype(vwo_ref.dtype)


def _nsa_prep(proj3, pos3, inv_row):
    B, S, _ = proj3.shape
    ts = 512 if S % 512 == 0 else S

    def col(name, width):
        off = _COLS[name][0]
        assert off % width == 0
        return pl.BlockSpec((1, ts, width), lambda b, j, o=off // width: (b, j, o))

    kv_spec = pl.BlockSpec((1, NSA_KV_GROUPS, ts, NSA_DH), lambda b, j: (b, 0, j, 0))
    kv_shape = jax.ShapeDtypeStruct((B, NSA_KV_GROUPS, S, NSA_DH), BF16)
    vx_spec = pl.BlockSpec((1, NSA_KV_GROUPS, ts, 2 * NSA_DH), lambda b, j: (b, 0, j, 0))
    vx_shape = jax.ShapeDtypeStruct((B, NSA_KV_GROUPS, S, 2 * NSA_DH), BF16)
    return pl.pallas_call(
        _nsa_prep_kernel,
        grid=(B, S // ts),
        in_specs=[pl.BlockSpec((1, ts, 1), lambda b, j: (b, j, 0)),
                  pl.BlockSpec((1, 128), lambda b, j: (0, 0)),
                  col("nq", 512), col("nkc", 128), col("nks", 128), col("nvs", 128),
                  col("nkw", 128), col("nvw", 128)],
        out_specs=[pl.BlockSpec((1, NSA_HEADS, ts, NSA_DH), lambda b, j: (b, 0, j, 0)),
                   pl.BlockSpec((1, ts, 128), lambda b, j: (b, j, 0)),
                   kv_spec, vx_spec, kv_spec, vx_spec],
        out_shape=[jax.ShapeDtypeStruct((B, NSA_HEADS, S, NSA_DH), BF16),
                   jax.ShapeDtypeStruct((B, S, 128), F32),
                   kv_shape, vx_shape, kv_shape, vx_shape],
        compiler_params=_cparams("parallel", "parallel"),
        name="nsa_prep",
    )(pos3, inv_row, proj3, proj3, proj3, proj3, proj3, proj3)


def _compress_kernel(k_ref, v_ref, pk_ref, w1k_ref, b1k_ref, w2k_ref, pv_ref, w1v_ref, b1v_ref, w2v_ref,
                     ko_ref, vo_ref):
    nr = k_ref.shape[1] // CMP_STRIDE
    hidden = w2k_ref.shape[0]

    def run(x_ref, p_ref, w1_ref, b1_ref, w2_ref, o_ref):
        first = jnp.zeros((nr, NSA_KV_GROUPS * hidden), F32)
        second = jnp.zeros((nr, NSA_KV_GROUPS * hidden), F32)
        for m in range(CMP_STRIDE):
            xm = x_ref[0, pl.ds(m, nr, stride=CMP_STRIDE), :]
            first = first + _dot((xm + p_ref[m:m + 1, :]).astype(BF16), w1_ref[m])
            second = second + _dot((xm + p_ref[CMP_STRIDE + m:CMP_STRIDE + m + 1, :]).astype(BF16),
                                   w1_ref[CMP_STRIDE + m])
        pre = first + pltpu.roll(second, nr - 1, 0) + b1_ref[...]
        hid = jax.nn.gelu(pre)
        for g in range(NSA_KV_GROUPS):
            out = _dot(hid[:, g * hidden:(g + 1) * hidden].astype(BF16), w2_ref[...])
            row = lax.broadcasted_iota(jnp.int32, out.shape, 0)
            o_ref[0, g] = jnp.where(row < nr - 1, out, 0.0)

    run(k_ref, pk_ref, w1k_ref, b1k_ref, w2k_ref, ko_ref)
    run(v_ref, pv_ref, w1v_ref, b1v_ref, w2v_ref, vo_ref)


def _compress_params(pos, w1, b1, w2):
    L, dh = pos.shape
    hidden = w1.shape[1]
    w = w1.reshape(L, dh, hidden).astype(BF16)
    z = jnp.zeros_like(w)
    w_bd = jnp.concatenate([jnp.concatenate([w, z], axis=2), jnp.concatenate([z, w], axis=2)], axis=1)
    return (jnp.tile(pos, (1, NSA_KV_GROUPS)), w_bd, jnp.tile(b1.reshape(1, -1), (1, NSA_KV_GROUPS)),
            w2.astype(BF16))


def _compress(kc_r, proj3, k_params, v_params):
    B, S, W = kc_r.shape
    NR = S // CMP_STRIDE
    full = lambda a: pl.BlockSpec(a.shape, lambda b: (0,) * a.ndim)
    o_spec = pl.BlockSpec((1, NSA_KV_GROUPS, NR, NSA_DH), lambda b: (b, 0, 0, 0))
    o_shape = jax.ShapeDtypeStruct((B, NSA_KV_GROUPS, NR, NSA_DH), F32)
    params = tuple(k_params) + tuple(v_params)
    return pl.pallas_call(
        _compress_kernel,
        grid=(B,),
        in_specs=[pl.BlockSpec((1, S, W), lambda b: (b, 0, 0)),
                  pl.BlockSpec((1, S, W), lambda b, o=_COLS["nvc"][0] // W: (b, 0, o))]
                 + [full(a) for a in params],
        out_specs=[o_spec, o_spec],
        out_shape=[o_shape, o_shape],
        compiler_params=_cparams("parallel"),
        name="compress",
    )(kc_r, proj3, *params)


def _nsa_attn_kernel(q_ref, kc_ref, vc_ref, ks_ref, vs_ref, kw_ref, vw_ref, gate_ref, ovt_ref, gsel_ref, o_ref,
                     *, tq, S, n_sel, n_top):
    qi = pl.program_id(2)
    start = qi * tq
    H = NSA_HPG
    qs = q_ref[0].reshape(H * tq, NSA_DH)
    t_col = start + lax.broadcasted_iota(jnp.int32, (tq, 1), 0)
    t_row = start + lax.broadcasted_iota(jnp.int32, (1, tq), 1)

    kc = kc_ref[0, 0].astype(BF16)
    vc = vc_ref[0, 0].astype(BF16)
    NR = kc.shape[0]
    s = _dot_nt(qs, kc)
    ncol = lax.broadcasted_iota(jnp.int32, (1, NR), 1)
    cmask = (ncol < NR - 1) & (ncol * CMP_STRIDE + (CMP_LEN - 1) <= t_col)
    s3 = jnp.where(cmask[None], s.reshape(H, tq, NR), NEG)
    m = jnp.max(s3, axis=-1, keepdims=True)
    e = jnp.exp(s3 - m)
    p3 = e / jnp.sum(e, axis=-1, keepdims=True)
    p3 = jnp.where(cmask[None], p3, 0.0)
    o_cmp = _dot(p3.reshape(H * tq, NR).astype(BF16), vc)

    psum = jnp.sum(p3, axis=0)
    p_hi = psum.astype(BF16)
    p_lo = (psum - p_hi.astype(F32)).astype(BF16)
    ovt = ovt_ref[...]
    imp = _dot_nt(ovt, p_hi) + _dot_nt(ovt, p_lo)
    blk = lax.broadcasted_iota(jnp.int32, (n_sel, 1), 0)
    cur = t_row // SEL_LEN
    causal_blk = blk <= cur
    forced = (blk == 0) | (blk == cur) | (blk == cur - 1)
    val = jnp.where(causal_blk, jnp.where(forced, jnp.inf, imp), -jnp.inf)
    rank = jnp.zeros((n_sel, tq), jnp.int32)
    for j in range(n_sel):
        vj = val[j:j + 1, :]
        beats = (vj > val) | ((vj == val) & (blk > j))
        rank = rank + beats.astype(jnp.int32)
    sel_t = jnp.where((rank < n_top) & causal_blk, 1.0, 0.0)
    sel_t = jnp.concatenate([sel_t, jnp.zeros((128 - n_sel, tq), F32)], axis=0)
    sel = sel_t.T.astype(BF16)

    tk = 256
    nsub = 2 if S % (2 * tk) == 0 else 1
    bpc = tk // SEL_LEN
    n_trips = (start + tq + nsub * tk - 1) // (nsub * tk)
    erow = lax.broadcasted_iota(jnp.int32, (128, tk), 0)
    ecol = lax.broadcasted_iota(jnp.int32, (128, tk), 1) // SEL_LEN
    kk = lax.broadcasted_iota(jnp.int32, (1, tk), 1)

    def sel_trip(c, carry):
        m_i, acc = carry
        scores, vals = [], []
        m_new = m_i
        for u in range(nsub):
            cu = c * nsub + u
            k0 = pl.multiple_of(cu * tk, tk)
            kb = ks_ref[0, 0, pl.ds(k0, tk), :]
            vals.append(vs_ref[0, 0, pl.ds(k0, tk), :])
            expand = jnp.where(erow == ecol + cu * bpc, 1.0, 0.0).astype(BF16)
            allowed = (_dot(sel, expand) > 0.5) & (k0 + kk <= t_col)
            bias = jnp.where(allowed, 0.0, NEG)
            sc3 = _dot_nt(qs, kb).reshape(H, tq, tk) + bias[None]
            scores.append(sc3)
            m_new = jnp.maximum(m_new, jnp.max(sc3, axis=-1, keepdims=True))
        acc = jnp.exp(m_i - m_new).reshape(H * tq, 1) * acc
        for u in range(nsub):
            pe = jnp.exp((scores[u] - m_new).astype(BF16))
            acc = acc + _dot(pe.reshape(H * tq, tk), vals[u])
        return m_new, acc

    m0 = jnp.full((H, tq, 1), NEG, F32)
    a0 = jnp.zeros((H * tq, 2 * NSA_DH), F32)
    _, acc_f = lax.fori_loop(0, n_trips, sel_trip, (m0, a0))
    o_sel = acc_f[:, :NSA_DH] / acc_f[:, NSA_DH:NSA_DH + 1]

    tw = 128
    span = min(WINDOW + tw, S)
    q3 = qs.reshape(H, tq, NSA_DH)
    parts = []
    for hf in range(tq // tw):
        ws = pl.multiple_of(jnp.clip(start + hf * tw - WINDOW, 0, S - span), tw)
        kwb = kw_ref[0, 0, pl.ds(ws, span), :]
        vwb = vw_ref[0, 0, pl.ds(ws, span), :]
        kp = ws + lax.broadcasted_iota(jnp.int32, (1, span), 1)
        tc = t_col[hf * tw:(hf + 1) * tw]
        wbias = jnp.where((kp <= tc) & (kp > tc - WINDOW), 0.0, NEG)
        qh = q3[:, hf * tw:(hf + 1) * tw].reshape(H * tw, NSA_DH)
        sw3 = _dot_nt(qh, kwb).reshape(H, tw, span) + wbias[None]
        mw = jnp.max(sw3, axis=-1, keepdims=True)
        ew = jnp.exp((sw3 - mw).astype(BF16))
        aw = _dot(ew.reshape(H * tw, span), vwb)
        parts.append((aw[:, :NSA_DH] / aw[:, NSA_DH:NSA_DH + 1]).reshape(H, tw, NSA_DH))
    o_win = jnp.concatenate(parts, axis=1).reshape(H * tq, NSA_DH)

    gs = jax.nn.sigmoid(gate_ref[0])
    gs_hi = gs.astype(BF16)
    gs_lo = (gs - gs_hi.astype(F32)).astype(BF16)
    spread = gsel_ref[0]
    gw = _dot(gs_hi, spread) + _dot(gs_lo, spread)
    for h in range(H):
        rs = slice(h * tq, (h + 1) * tq)
        gate = [gw[:, (h * 3 + br) * 128:(h * 3 + br) * 128 + NSA_DH] for br in range(3)]
        o = gate[0] * o_cmp[rs] + gate[1] * o_sel[rs] + gate[2] * o_win[rs]
        o_ref[0, :, h * NSA_DH:(h + 1) * NSA_DH] = o.astype(o_ref.dtype)


def _nsa_attn(q_r, kcmp, vcmp, ks_r, vs_r, kw_r, vw_r, proj3, ov):
    B, _, S, _ = q_r.shape
    G, H = NSA_KV_GROUPS, NSA_HPG
    NR = kcmp.shape[2]
    tq = 256
    n_sel = S // SEL_LEN
    n_top = min(SEL_TOPK, n_sel)
    cmp_spec = pl.BlockSpec((1, 1, NR, NSA_DH), lambda b, g, i: (b, g, 0, 0))
    kv_spec = pl.BlockSpec((1, 1, S, NSA_DH), lambda b, g, i: (b, g, 0, 0))
    vx_spec = pl.BlockSpec((1, 1, S, 2 * NSA_DH), lambda b, g, i: (b, g, 0, 0))
    goff = _COLS["ng"][0] // 128
    sel_np = np.zeros((G, 128, H * 3 * 128), np.float32)
    for gg in range(G):
        for hb in range(H * 3):
            sel_np[gg, gg * H * 3 + hb, hb * 128:(hb + 1) * 128] = 1.0
    gsel = jnp.asarray(sel_np, BF16)
    return pl.pallas_call(
        functools.partial(_nsa_attn_kernel, tq=tq, S=S, n_sel=n_sel, n_top=n_top),
        grid=(B, G, S // tq),
        in_specs=[pl.BlockSpec((1, H, tq, NSA_DH), lambda b, g, i: (b, g, i, 0)),
                  cmp_spec, cmp_spec, kv_spec, vx_spec, kv_spec, vx_spec,
                  pl.BlockSpec((1, tq, 128), lambda b, g, i: (b, i, goff)),
                  pl.BlockSpec(ov.shape, lambda b, g, i: (0, 0)),
                  pl.BlockSpec((1,) + gsel.shape[1:], lambda b, g, i: (g, 0, 0))],
        out_specs=pl.BlockSpec((1, tq, H * NSA_DH), lambda b, g, i: (b, i, g)),
        out_shape=jax.ShapeDtypeStruct((B, S, NSA_HEADS * NSA_DH), BF16),
        compiler_params=_cparams("parallel", "parallel", "arbitrary"),
        name="nsa_attn",
    )(q_r, kcmp, vcmp, ks_r, vs_r, kw_r, vw_r, proj3, ov, gsel)


def _layer_norm(x, g, b):
    mu = jnp.mean(x, axis=-1, keepdims=True)
    xc = x - mu
    var = jnp.mean(xc * xc, axis=-1, keepdims=True)
    return xc * lax.rsqrt(var + LN_EPS) * g + b


SUB = D_MODEL // 2 // 128


def _pack_bf16_pairs(v):
    half = v.shape[1] // 2
    bits = pltpu.bitcast(v.astype(BF16).astype(F32), jnp.uint32)
    return pltpu.bitcast((bits[:, :half] >> 16) | bits[:, half:], jnp.int32)


def _unpack_bf16_pairs(p):
    u = pltpu.bitcast(p, jnp.uint32)
    return pltpu.bitcast(u << 16, F32), pltpu.bitcast(u & jnp.uint32(0xFFFF0000), F32)


def _to_tiles(ref, val, lead=()):
    for c in range(SUB):
        ref[lead + (c,)] = val[:, c * 128:(c + 1) * 128]


def _from_tiles(ref, lead=()):
    return jnp.concatenate([ref[lead + (c,)] for c in range(SUB)], axis=1)


def _outproj_kernel(x_ref, og_ref, on_ref, w_ref, g_ref, b_ref, o_ref, ot_ref):
    half = og_ref.shape[1]
    mix = _dot(og_ref[...], w_ref[0:half, :]) + _dot(on_ref[...], w_ref[half:, :])
    h = _layer_norm(DN_ALPHA * x_ref[...] + mix, g_ref[...], b_ref[...])
    o_ref[...] = h
    _to_tiles(ot_ref, _pack_bf16_pairs(h))


def _outproj(x2, row0, og2, on2, w, g, b):
    T = og2.shape[0]
    D = x2.shape[1]
    tm = 512
    off = row0 // tm
    row = lambda width: pl.BlockSpec((tm, width), lambda i: (i, 0))
    full = lambda a: pl.BlockSpec(a.shape, lambda i: (0, 0))
    return pl.pallas_call(
        _outproj_kernel,
        grid=(T // tm,),
        in_specs=[pl.BlockSpec((tm, D), lambda i: (i + off, 0)), row(og2.shape[1]), row(on2.shape[1]),
                  full(w), full(g), full(b)],
        out_specs=[row(D), pl.BlockSpec((SUB, tm, 128), lambda i: (0, i, 0))],
        out_shape=[jax.ShapeDtypeStruct((T, D), F32),
                   jax.ShapeDtypeStruct((SUB, T, 128), jnp.int32)],
        compiler_params=_cparams("parallel"),
        name="outproj_ln",
    )(x2, og2, on2, w, g, b)


def _router_kernel(h_ref, wh_ref, wl_ref, bias_ref, eidx_ref, wts_ref, rank_ref, cnt_ref, carry_ref):
    @pl.when(pl.program_id(0) == 0)
    def _():
        carry_ref[...] = jnp.zeros_like(carry_ref)

    h = h_ref[...]
    tm = h.shape[0]
    E = N_EXPERTS
    GS = E // N_GROUPS
    h_hi = h.astype(BF16)
    h_lo = (h - h_hi.astype(F32)).astype(BF16)
    wh = wh_ref[...]
    logits = _dot_nt(wh, h_hi) + _dot_nt(wh, h_lo) + _dot_nt(wl_ref[...], h_hi)
    scores = jax.nn.sigmoid(logits)
    biased = scores + bias_ref[...]
    eid = lax.broadcasted_iota(jnp.int32, (E, tm), 0).astype(F32)
    ninf = -jnp.inf

    def col_max(x):
        return jnp.max(x, axis=0, keepdims=True)

    def first_idx(x, mx, ids):
        return jnp.min(jnp.where(x == mx, ids, float(E)), axis=0, keepdims=True)

    ids0 = lax.broadcasted_iota(jnp.int32, (GS, tm), 0).astype(F32)
    gscore = []
    for gi in range(N_GROUPS):
        mg = biased[gi * GS:(gi + 1) * GS, :]
        ids = ids0 + float(gi * GS)
        m1 = col_max(mg)
        i1 = first_idx(mg, m1, ids)
        m2 = col_max(jnp.where(ids == i1, ninf, mg))
        gscore.append(m1 + m2)
    parts = []
    for gi in range(N_GROUPS):
        rk = jnp.zeros((1, tm), jnp.int32)
        for gj in range(N_GROUPS):
            if gj == gi:
                continue
            beats = (gscore[gj] > gscore[gi]) | ((gscore[gj] == gscore[gi]) & (gj < gi))
            rk = rk + beats.astype(jnp.int32)
        parts.append(jnp.where(rk < TOPK_GROUPS, biased[gi * GS:(gi + 1) * GS, :], ninf))
    masked = jnp.concatenate(parts, axis=0)

    onehots, wsel = [], []
    selm = jnp.zeros((E, tm), F32)
    for k in range(TOP_K):
        mx = col_max(masked)
        idx = first_idx(masked, mx, eid)
        oh = eid == idx
        onehots.append(oh)
        wsel.append(jnp.sum(jnp.where(oh, scores, 0.0), axis=0, keepdims=True))
        masked = jnp.where(oh, ninf, masked)
        selm = jnp.where(oh, 1.0, selm)
        eidx_ref[k:k + 1, :] = idx.astype(jnp.int32)
    wsum = wsel[0]
    for k in range(1, TOP_K):
        wsum = wsum + wsel[k]
    for k in range(TOP_K):
        wts_ref[k:k + 1, :] = wsel[k] / wsum * ROUTED_SCALE

    ri = lax.broadcasted_iota(jnp.int32, (tm, tm), 0)
    ci = lax.broadcasted_iota(jnp.int32, (tm, tm), 1)
    before = jnp.where(ri < ci, 1.0, 0.0).astype(BF16)
    cum = _dot(selm.astype(BF16), before) + carry_ref[...]
    for k in range(TOP_K):
        rk = jnp.sum(jnp.where(onehots[k], cum, 0.0), axis=0, keepdims=True)
        rank_ref[k:k + 1, :] = rk.astype(jnp.int32)
    total = carry_ref[...] + jnp.sum(selm, axis=1, keepdims=True)
    carry_ref[...] = total
    cnt_ref[...] = total


def _router(h2, w_hi_t, w_lo_t, bias_col):
    T, D = h2.shape
    tm = 256
    full = lambda a: pl.BlockSpec(a.shape, lambda i: (0, 0))
    o8 = pl.BlockSpec((TOP_K, tm), lambda i: (0, i))
    return pl.pallas_call(
        _router_kernel,
        grid=(T // tm,),
        in_specs=[pl.BlockSpec((tm, D), lambda i: (i, 0)), full(w_hi_t), full(w_lo_t), full(bias_col)],
        out_specs=[o8, o8, o8, pl.BlockSpec((N_EXPERTS, 1), lambda i: (0, 0))],
        out_shape=[jax.ShapeDtypeStruct((TOP_K, T), jnp.int32),
                   jax.ShapeDtypeStruct((TOP_K, T), F32),
                   jax.ShapeDtypeStruct((TOP_K, T), jnp.int32),
                   jax.ShapeDtypeStruct((N_EXPERTS, 1), F32)],
        scratch_shapes=[pltpu.VMEM((N_EXPERTS, 1), F32)],
        compiler_params=_cparams("arbitrary"),
        name="router",
    )(h2, w_hi_t, w_lo_t, bias_col)


def _dest_kernel(eidx_ref, rank_ref, ps_ref, dest_ref):
    tm = eidx_ref.shape[1]
    eid = lax.broadcasted_iota(jnp.int32, (N_EXPERTS, tm), 0)
    ps = ps_ref[...]
    for k in range(TOP_K):
        start = jnp.sum(jnp.where(eid == eidx_ref[k:k + 1, :], ps, 0.0), axis=0, keepdims=True)
        dest_ref[k:k + 1, :] = start.astype(jnp.int32) + rank_ref[k:k + 1, :]


def _dest(eidx_t, rank_t, pad_start_col):
    T = eidx_t.shape[1]
    tm = 1024 if T % 1024 == 0 else T
    o8 = pl.BlockSpec((TOP_K, tm), lambda i: (0, i))
    return pl.pallas_call(
        _dest_kernel,
        grid=(T // tm,),
        in_specs=[o8, o8, pl.BlockSpec((N_EXPERTS, 1), lambda i: (0, 0))],
        out_specs=o8,
        out_shape=jax.ShapeDtypeStruct((TOP_K, T), jnp.int32),
        compiler_params=_cparams("parallel"),
        name="dest",
    )(eidx_t, rank_t, pad_start_col)


SC_WINDOW = 128


def _sc_gather(table, idx):
    _, lanes = table.shape
    n = idx.shape[0]
    mesh = plsc.VectorSubcoreMesh(core_axis_name="core", subcore_axis_name="subcore")

    @functools.partial(pl.kernel, out_type=jax.ShapeDtypeStruct((n, lanes), table.dtype), mesh=mesh,
                       name="sc_row_gather")
    def gather(x_hbm, i_hbm, o_hbm):
        def body(i_vmem, o_vmem):
            pltpu.sync_copy(x_hbm.at[i_vmem.at[0]], o_vmem)

        pltpu.emit_pipeline(
            body,
            grid=(n // SC_WINDOW,),
            in_specs=[pl.BlockSpec((1, SC_WINDOW), lambda i: (0, i))],
            out_specs=[pl.BlockSpec((SC_WINDOW, lanes), lambda i: (i, 0))],
            core_axis_name=("core", "subcore"),
            dimension_semantics=(pltpu.PARALLEL,),
            trace_scopes=False,
        )(i_hbm, o_hbm)

    return gather(table, idx.reshape(1, n))


def _sc_scatter(src, idx, n_out):
    rows, lanes = src.shape
    n_idx = idx.shape[0]
    mesh = plsc.VectorSubcoreMesh(core_axis_name="core", subcore_axis_name="subcore")

    @functools.partial(pl.kernel, out_type=jax.ShapeDtypeStruct((n_out, lanes), src.dtype), mesh=mesh,
                       name="sc_row_scatter")
    def scatter(x_hbm, i_hbm, o_hbm):
        def body(x_vmem, *i_vmems):
            for i_vmem in i_vmems:
                pltpu.sync_copy(x_vmem, o_hbm.at[i_vmem.at[0]])

        pltpu.emit_pipeline(
            body,
            grid=(rows // SC_WINDOW,),
            in_specs=[pl.BlockSpec((SC_WINDOW, lanes), lambda i: (i, 0))]
                     + [pl.BlockSpec((1, SC_WINDOW), lambda i, j=j: (j, i)) for j in range(n_idx)],
            out_specs=[],
            core_axis_name=("core", "subcore"),
            dimension_semantics=(pltpu.PARALLEL,),
            trace_scopes=False,
        )(x_hbm, *([i_hbm] * n_idx))

    return scatter(src, idx)


def _expert_kernel(bexp_ref, nused_ref, x_ref, wg_ref, wu_ref, wd_ref, y_ref, wg_b, wu_b, wd_b):
    i = pl.program_id(0)

    @pl.when(i < nused_ref[0])
    def _():
        @pl.when((i == 0) | (bexp_ref[i] != bexp_ref[jnp.maximum(i - 1, 0)]))
        def _():
            wg_b[...] = wg_ref[0].astype(BF16)
            wu_b[...] = wu_ref[0].astype(BF16)
            wd_b[...] = wd_ref[0].astype(BF16)

        x = jnp.concatenate(_unpack_bf16_pairs(_from_tiles(x_ref)), axis=1).astype(BF16)
        gate = _dot(x, wg_b[...])
        up = _dot(x, wu_b[...])
        act = (gate * jax.nn.sigmoid(gate) * up).astype(BF16)
        _to_tiles(y_ref, _pack_bf16_pairs(_dot(act, wd_b[...])))


def _experts(blk_exp, n_used, xs_t, wg, wu, wd):
    NP = xs_t.shape[1]
    D = D_MODEL
    nb = NP // MOE_BLOCK
    blk = (SUB, MOE_BLOCK, 128)

    def xmap(i, bexp, nused):
        return (0, jnp.minimum(i, nused[0] - 1), 0)

    def wmap(i, bexp, nused):
        return (bexp[jnp.minimum(i, nused[0] - 1)], 0, 0)

    grid_spec = pltpu.PrefetchScalarGridSpec(
        num_scalar_prefetch=2,
        grid=(nb,),
        in_specs=[pl.BlockSpec(blk, xmap),
                  pl.BlockSpec((1, D, D_EXPERT), wmap),
                  pl.BlockSpec((1, D, D_EXPERT), wmap),
                  pl.BlockSpec((1, D_EXPERT, D), wmap)],
        out_specs=pl.BlockSpec(blk, xmap),
        scratch_shapes=[pltpu.VMEM((D, D_EXPERT), BF16), pltpu.VMEM((D, D_EXPERT), BF16),
                        pltpu.VMEM((D_EXPERT, D), BF16)],
    )
    return pl.pallas_call(
        _expert_kernel,
        grid_spec=grid_spec,
        out_shape=jax.ShapeDtypeStruct(xs_t.shape, jnp.int32),
        compiler_params=_cparams("arbitrary"),
        name="experts",
    )(blk_exp, n_used, xs_t, wg, wu, wd)


def _combine_kernel(h_ref, wts_ref, yg_ref, wsg_ref, wsu_ref, wsd_ref, g_ref, b_ref, *rest):
    o_ref = rest[-1]
    h = h_ref[...]
    hb = h.astype(BF16)
    gate = _dot(hb, wsg_ref[...])
    up = _dot(hb, wsu_ref[...])
    shared = _dot((gate * jax.nn.sigmoid(gate) * up).astype(BF16), wsd_ref[...])
    wts = wts_ref[...]
    lo, hi = _unpack_bf16_pairs(_from_tiles(yg_ref, (0,)))
    r_lo, r_hi = wts[:, 0:1] * lo, wts[:, 0:1] * hi
    for k in range(1, TOP_K):
        lo, hi = _unpack_bf16_pairs(_from_tiles(yg_ref, (k,)))
        r_lo, r_hi = r_lo + wts[:, k:k + 1] * lo, r_hi + wts[:, k:k + 1] * hi
    routed = jnp.concatenate([r_lo, r_hi], axis=1)
    o_ref[...] = _layer_norm(DN_ALPHA * h + (routed + shared), g_ref[...], b_ref[...])


def _combine(h2, wts, yg_t, wsg, wsu, wsd, g, b, out_rows, row0, out_prev):
    T, D = h2.shape
    tm = 256
    off = row0 // tm
    full = lambda a: pl.BlockSpec(a.shape, lambda i: (0, 0))
    in_specs = [pl.BlockSpec((tm, D), lambda i: (i, 0)),
                pl.BlockSpec((tm, TOP_K), lambda i: (i, 0)),
                pl.BlockSpec((TOP_K, SUB, tm, 128), lambda i: (0, 0, i, 0)),
                full(wsg), full(wsu), full(wsd), full(g), full(b)]
    args = [h2, wts, yg_t, wsg, wsu, wsd, g, b]
    aliases = {}
    if out_prev is not None:
        in_specs.append(pl.BlockSpec(memory_space=pl.ANY))
        args.append(out_prev)
        aliases = {len(args) - 1: 0}
    return pl.pallas_call(
        _combine_kernel,
        grid=(T // tm,),
        in_specs=in_specs,
        out_specs=pl.BlockSpec((tm, D), lambda i: (i + off, 0)),
        out_shape=jax.ShapeDtypeStruct((out_rows, D), F32),
        input_output_aliases=aliases,
        compiler_params=_cparams("parallel"),
        name="combine_ln",
    )(*args)


def _regroup_w_in(w_in):
    parts, off = {}, 0
    for name, width in _SPLITS:
        parts[name] = w_in[:, off:off + width]
        off += width
    cols = []
    for name, (_, width) in _COLS.items():
        p = parts[name]
        if p.shape[1] < width:
            p = jnp.pad(p, ((0, 0), (0, width - p.shape[1])))
        cols.append(p)
    return jnp.concatenate(cols, axis=1).astype(BF16)


def _overlap_matrix(S):
    nr = S // CMP_STRIDE
    n_sel = S // SEL_LEN
    ci = np.arange(nr)[:, None] * CMP_STRIDE
    sj = np.arange(n_sel)[None, :] * SEL_LEN
    ov = np.clip(np.minimum(ci + CMP_LEN, sj + SEL_LEN) - np.maximum(ci, sj), 0, None) / CMP_LEN
    ov[nr - 1] = 0.0
    return jnp.asarray(ov.T, BF16)


def _mixers(x2, row0, positions, w_in, w_alpha2, b_alpha, gla_norm_g,
            cmp_pos_k, cmp_w1_k, cmp_b1_k, cmp_w2_k, cmp_pos_v, cmp_w1_v, cmp_b1_v, cmp_w2_v):
    B, S = positions.shape
    proj = _proj(x2, row0, B * S, _regroup_w_in(w_in)).reshape(B, S, D_PROJ)

    wa = jnp.pad(w_alpha2, ((0, 128 - GLA_LOWRANK), (0, 0))).astype(BF16)
    o_gla = _gla(proj, wa, b_alpha.reshape(1, -1), gla_norm_g.reshape(1, -1))

    half = NSA_DH // 2
    inv = ROPE_THETA ** (-np.arange(half, dtype=np.float32) / half)
    inv_row = jnp.asarray(np.tile(inv, 128 // half).reshape(1, 128), F32)
    pos3 = positions.astype(F32).reshape(B, S, 1)
    q_r, kc_r, ks_r, vs_r, kw_r, vw_r = _nsa_prep(proj, pos3, inv_row)

    kcmp, vcmp = _compress(kc_r, proj,
                           _compress_params(cmp_pos_k, cmp_w1_k, cmp_b1_k, cmp_w2_k),
                           _compress_params(cmp_pos_v, cmp_w1_v, cmp_b1_v, cmp_w2_v))
    o_nsa = _nsa_attn(q_r, kcmp, vcmp, ks_r, vs_r, kw_r, vw_r, proj, _overlap_matrix(S))
    return o_gla, o_nsa


def _moe_ln(h2, h_t, w_router, router_bias, w_gate, w_up, w_down, ws_gate, ws_up, ws_down, ln_g, ln_b,
            out_rows, row0, out_prev):
    T, D = h2.shape
    P = T * TOP_K
    w_hi = w_router.astype(BF16)
    w_lo = (w_router - w_hi.astype(F32)).astype(BF16)
    eidx_t, wts_t, rank_t, counts = _router(h2, w_hi.T, w_lo.T, router_bias.reshape(-1, 1))

    counts = counts.reshape(-1).astype(jnp.int32)
    padded = (counts + MOE_BLOCK - 1) // MOE_BLOCK * MOE_BLOCK
    pad_end = jnp.cumsum(padded)
    pad_start = pad_end - padded
    nb = -(-P // MOE_BLOCK) + N_EXPERTS
    n_used = (pad_end[-1] // MOE_BLOCK).astype(jnp.int32).reshape(1)
    blk_start = jnp.arange(nb, dtype=jnp.int32) * MOE_BLOCK
    blk_exp = jnp.minimum(jnp.sum((pad_end[None, :] <= blk_start[:, None]).astype(jnp.int32), axis=1),
                          N_EXPERTS - 1)

    NP = nb * MOE_BLOCK
    dest_t = _dest(eidx_t, rank_t, pad_start.astype(F32).reshape(-1, 1))
    col = jnp.arange(SUB, dtype=jnp.int32) * NP

    n_pad = NP - P
    assert n_pad % T == 0
    pad_cnt = padded - counts
    pad_hi = jnp.cumsum(pad_cnt)
    pad_lo = pad_hi - pad_cnt
    j = jnp.arange(n_pad, dtype=jnp.int32)[:, None]
    owner = (pad_lo[None, :] <= j) & (j < pad_hi[None, :])
    in_expert = jnp.sum(jnp.where(owner, (pad_start + counts - pad_lo)[None, :] + j, 0), axis=1)
    pad_rows = jnp.where(j[:, 0] < pad_hi[-1], in_expert, pad_end[-1] + j[:, 0] - pad_hi[-1])

    rows_all = jnp.concatenate([dest_t, pad_rows.reshape(n_pad // T, T)], axis=0)
    dst = (rows_all[:, None, :] + col[None, :, None]).reshape(-1, SUB * T)
    xs_t = _sc_scatter(h_t.reshape(SUB * T, 128), dst, SUB * NP).reshape(SUB, NP, 128)
    ys_t = _experts(blk_exp, n_used, xs_t, w_gate, w_up, w_down)
    src = (dest_t[:, None, :] + col[None, :, None]).reshape(-1)
    yg_t = _sc_gather(ys_t.reshape(SUB * NP, 128), src).reshape(TOP_K, SUB, T, 128)
    return _combine(h2, wts_t.T, yg_t, ws_gate.astype(BF16), ws_up.astype(BF16), ws_down.astype(BF16),
                    ln_g.reshape(1, -1), ln_b.reshape(1, -1), out_rows, row0, out_prev)


def kernel(x, positions, w_in, w_alpha2, b_alpha, gla_norm_g, cmp_pos_k, cmp_w1_k, cmp_b1_k, cmp_w2_k, cmp_pos_v, cmp_w1_v, cmp_b1_v, cmp_w2_v, w_out, ln1_g, ln1_b, w_router, router_bias, w_exp_gate, w_exp_up, w_exp_down, w_sh_gate, w_sh_up, w_sh_down, ln2_g, ln2_b):
    B, S, D = x.shape
    n_groups = BATCH_GROUPS if B % BATCH_GROUPS == 0 else 1
    bg = B // n_groups
    rows = bg * S
    h2d = x.reshape(B * S, D)
    for l in range(w_in.shape[0]):
        out = None
        for gi in range(n_groups):
            row0 = gi * rows
            o_gla, o_nsa = _mixers(h2d, row0, positions[gi * bg:(gi + 1) * bg], w_in[l], w_alpha2[l], b_alpha[l],
                                   gla_norm_g[l], cmp_pos_k[l], cmp_w1_k[l], cmp_b1_k[l], cmp_w2_k[l],
                                   cmp_pos_v[l], cmp_w1_v[l], cmp_b1_v[l], cmp_w2_v[l])
            h1, h1_t = _outproj(h2d, row0, o_gla.reshape(rows, -1), o_nsa.reshape(rows, -1),
                                w_out[l].astype(BF16), ln1_g[l].reshape(1, -1), ln1_b[l].reshape(1, -1))
            out = _moe_ln(h1, h1_t, w_router[l], router_bias[l], w_exp_gate[l], w_exp_up[l], w_exp_down[l],
                          w_sh_gate[l], w_sh_up[l], w_sh_down[l], ln2_g[l], ln2_b[l], B * S, row0, out)
        h2d = out
    return h2d.reshape(B, S, D)
```
